```python
import jax, jax.numpy as jnp
from jax import lax
import numpy as np

D_MODEL = 1024
BATCH = 32
SEQ = 256
DEPTH = 2
DEC_BATCH = 2
DEC_SEQ = 2048
PAST_LEN = 512

GRID_W = 64
N_EVEN = (DEPTH + 1) // 2
N_ODD = DEPTH // 2
CONV_DIM = 512
CONV_K = 3
POOL_WINDOWS = (2, 4, 8, 16)
N_POOL = 4
POOL_DIM = 512
POOL_GROUP = POOL_DIM // N_POOL
EVEN_IN = 3 * CONV_DIM + POOL_DIM
EVEN_MIX = CONV_DIM + POOL_DIM
N_HEADS = 8
QK_NOPE = 128
QK_ROPE = 64
QK_HEAD = QK_NOPE + QK_ROPE
V_DIM = 128
Q_RANK = 384
KV_RANK = 256
ROPE_THETA = 10000.0
Q_BLOCK = 128
FNET_GROUPS = 4
FNET_GROUP_DIM = 64
FNET_DIM = FNET_GROUPS * FNET_GROUP_DIM
ODD_IN = Q_RANK + KV_RANK + QK_ROPE + FNET_DIM
ODD_MIX = N_HEADS * V_DIM + FNET_DIM
D_FF = 2816
N_EXPERTS = 8
TOP_K = 2
D_FF_EXPERT = 1792
ALPHA = (2 * DEPTH) ** 0.25
BETA = (8 * DEPTH) ** -0.25
LN_EPS = 1e-5
RMS_EPS = 1e-6

kernel_name = "hybrid_diffusion_prefix_trunk_step"


def layer_norm(x, g, b):
    xf = x.astype(jnp.float32)
    mu = jnp.mean(xf, axis=-1, keepdims=True)
    var = jnp.mean(jnp.square(xf - mu), axis=-1, keepdims=True)
    return ((xf - mu) * lax.rsqrt(var + LN_EPS) * g + b).astype(x.dtype)


def rms_norm(x, g):
    xf = x.astype(jnp.float32)
    return (xf * lax.rsqrt(jnp.mean(jnp.square(xf), axis=-1, keepdims=True) + RMS_EPS) * g).astype(x.dtype)


def modulation(cond, w, b):
    m = jax.nn.silu(cond) @ w + b
    return jnp.split(m[..., None, :], 6, axis=-1)


def modulate(x, shift, scale):
    return x * (1.0 + scale) + shift


def post_norm(x, y, gate, g, b):
    return layer_norm(ALPHA * x + gate * y, g, b)


def axial_rope(L):
    rows = L // GRID_W
    row = jnp.repeat(jnp.arange(rows), GRID_W).astype(jnp.float32)
    col = jnp.tile(jnp.arange(GRID_W), rows).astype(jnp.float32)
    half = QK_ROPE // 2
    inv = ROPE_THETA ** (-jnp.arange(0, half, 2, dtype=jnp.float32) / half)
    ar, ac = row[:, None] * inv, col[:, None] * inv
    ang = jnp.concatenate([ar, ar, ac, ac], axis=-1)
    return jnp.cos(ang), jnp.sin(ang)


def rotate_half_axial(x):
    q = QK_ROPE // 4
    x = x.reshape(x.shape[:-1] + (2, 2, q))
    x = jnp.stack([-x[..., 1, :], x[..., 0, :]], axis=-2)
    return x.reshape(x.shape[:-3] + (QK_ROPE,))


def apply_rope(x, cos, sin):
    return (x * cos + rotate_half_axial(x) * sin).astype(x.dtype)


def centred_conv3(u, w):
    up = jnp.pad(u, ((0, 0), (1, 1), (0, 0)))
    return up[:, :-2] * w[0] + up[:, 1:-1] * w[1] + up[:, 2:] * w[2]


def pool_mixer(u, w_pool, pool_scale):
    B, L, _ = u.shape
    uf = u.astype(jnp.float32).reshape(B, L, N_POOL, POOL_GROUP)
    cs = jnp.pad(jnp.cumsum(uf, axis=1), ((0, 0), (1, 0), (0, 0), (0, 0)))
    t = jnp.arange(L)
    groups = []
    for g, w in enumerate(POOL_WINDOWS):
        lo = jnp.clip(t - w // 2, 0, L)
        hi = jnp.clip(t + w // 2, 0, L)
        cnt = (hi - lo).astype(jnp.float32)[None, :, None]
        groups.append((cs[:, hi, g] - cs[:, lo, g]) / cnt - uf[:, :, g])
    p = jnp.stack(groups, axis=2).astype(u.dtype)
    y = jnp.einsum('blgc,gcd->blgd', p, w_pool).reshape(B, L, POOL_DIM)
    return y * pool_scale


def even_mixer(h, w_in, conv_w, pool_w, pool_scale, w_out):
    u = h @ w_in
    ux, ub, uc, up = jnp.split(u, [CONV_DIM, 2 * CONV_DIM, 3 * CONV_DIM], axis=-1)
    ya = ub * centred_conv3(uc * ux, conv_w)
    yb = pool_mixer(up, pool_w, pool_scale)
    return jnp.concatenate([ya, yb], axis=-1) @ w_out


def block_attention(q, k, v):
    B, Lq, H, Dq = q.shape
    nb = Lq // Q_BLOCK
    scale = Dq ** -0.5
    qb = jnp.moveaxis(q.reshape(B, nb, Q_BLOCK, H, Dq), 1, 0)

    def one(qi):
        s = jnp.einsum('bqhd,bkhd->bhqk', qi, k).astype(jnp.float32) * scale
        p = jax.nn.softmax(s, axis=-1).astype(v.dtype)
        return jnp.einsum('bhqk,bkhd->bqhd', p, v)

    o = lax.map(one, qb)
    return jnp.moveaxis(o, 0, 1).reshape(B, Lq, H * V_DIM)


def mla_keys_values(ckv, kpe, w_kv_b):
    B, L, _ = ckv.shape
    kv = (ckv @ w_kv_b).reshape(B, L, N_HEADS, QK_NOPE + V_DIM)
    k = jnp.concatenate([kv[..., :QK_NOPE], jnp.broadcast_to(kpe[:, :, None, :], (B, L, N_HEADS, QK_ROPE))], axis=-1)
    return k, kv[..., QK_NOPE:]


def fourier_mixer(u, fnet_w):
    B, L, _ = u.shape
    ug = u.astype(jnp.float32).reshape(B, L, FNET_GROUPS, FNET_GROUP_DIM)
    mu = jnp.mean(ug, axis=-1, keepdims=True)
    var = jnp.mean(jnp.square(ug - mu), axis=-1, keepdims=True)
    ug = (ug - mu) * lax.rsqrt(var + LN_EPS)
    f = jnp.fft.fft2(ug, axes=(1, 3), norm='ortho').real
    return f.reshape(B, L, FNET_DIM).astype(u.dtype) @ fnet_w


def odd_project(h, w_in, q_norm, kv_norm, w_q_b):
    B, L, _ = h.shape
    uq, ukv, upe, uf = jnp.split(h @ w_in, [Q_RANK, Q_RANK + KV_RANK, Q_RANK + KV_RANK + QK_ROPE], axis=-1)
    q = (rms_norm(uq, q_norm) @ w_q_b).reshape(B, L, N_HEADS, QK_HEAD)
    return q, rms_norm(ukv, kv_norm), upe, uf


def odd_merge(attn, uf, fnet_w, w_out):
    return jnp.concatenate([attn, fourier_mixer(uf, fnet_w)], axis=-1) @ w_out


def odd_mixer_context(h, w_in, q_norm, kv_norm, w_q_b, w_kv_b, fnet_w, w_out):
    q, ckv, kpe, uf = odd_project(h, w_in, q_norm, kv_norm, w_q_b)
    k, v = mla_keys_values(ckv, kpe, w_kv_b)
    return odd_merge(block_attention(q, k, v), uf, fnet_w, w_out), ckv, kpe


def odd_mixer_latent(h, ctx_ckv, ctx_kpe, cos, sin, w_in, q_norm, kv_norm, w_q_b, w_kv_b, fnet_w, w_out):
    q, ckv, kpe, uf = odd_project(h, w_in, q_norm, kv_norm, w_q_b)
    q = jnp.concatenate([q[..., :QK_NOPE], apply_rope(q[..., QK_NOPE:], cos[:, None], sin[:, None])], axis=-1)
    kpe = apply_rope(kpe, cos, sin)
    k_ctx, v_ctx = mla_keys_values(ctx_ckv, ctx_kpe, w_kv_b)
    k_lat, v_lat = mla_keys_values(ckv, kpe, w_kv_b)
    k = jnp.concatenate([k_ctx, k_lat], axis=1)
    v = jnp.concatenate([v_ctx, v_lat], axis=1)
    return odd_merge(block_attention(q, k, v), uf, fnet_w, w_out)


def swiglu(h, wg, wu, wd):
    return (jax.nn.silu(h @ wg) * (h @ wu)) @ wd


def moe_swiglu(h, w_router, wg, wu, wd):
    logits = (h @ w_router).astype(jnp.float32)
    top_v, top_i = lax.top_k(logits, TOP_K)
    top_w = jax.nn.softmax(top_v, axis=-1)
    gates = jnp.einsum('blk,blke->ble', top_w, jax.nn.one_hot(top_i, N_EXPERTS, dtype=jnp.float32)).astype(h.dtype)
    y = jnp.zeros_like(h)
    for e in range(N_EXPERTS):
        y = y + gates[..., e:e + 1] * swiglu(h, wg[e], wu[e], wd[e])
    return y


def setup_inputs(seed: int = 0) -> dict:
    key = jax.random.key(seed)
    ks = jax.random.split(key, 32)
    D = D_MODEL

    def nrm(k, shape, scale=1.0):
        return scale * jax.random.normal(k, shape, jnp.float32)

    return {
        "x_prompt": nrm(ks[0], (BATCH, SEQ, D)),
        "x_sample": nrm(ks[1], (DEC_BATCH, DEC_SEQ, D)),
        "cache_ckv": nrm(ks[2], (DEC_BATCH, N_ODD, PAST_LEN, KV_RANK)),
        "cache_kpe": nrm(ks[3], (DEC_BATCH, N_ODD, PAST_LEN, QK_ROPE)),
        "c": nrm(ks[4], (DEC_BATCH, D)),
        "c_ctx": nrm(ks[5], (D,)),
        "ada_w": nrm(ks[6], (DEPTH, D, 6 * D), 0.5 * D ** -0.5),
        "ada_b": nrm(ks[7], (DEPTH, 6 * D), 0.02),
        "ln_g": 1.0 + nrm(ks[8], (DEPTH, 2, D), 0.05),
        "ln_b": nrm(ks[9], (DEPTH, 2, D), 0.02),
        "ev_w_in": nrm(ks[10], (N_EVEN, D, EVEN_IN), D ** -0.5),
        "ev_conv_w": nrm(ks[11], (N_EVEN, CONV_K, CONV_DIM), CONV_K ** -0.5),
        "ev_pool_w": nrm(ks[12], (N_EVEN, N_POOL, POOL_GROUP, POOL_GROUP), POOL_GROUP ** -0.5),
        "ev_pool_scale": 1.0 + nrm(ks[13], (N_EVEN, POOL_DIM), 0.1),
        "ev_w_out": nrm(ks[14], (N_EVEN, EVEN_MIX, D), BETA * EVEN_MIX ** -0.5),
        "ffn_w_gate": nrm(ks[15], (N_EVEN, D, D_FF), D ** -0.5),
        "ffn_w_up": nrm(ks[16], (N_EVEN, D, D_FF), D ** -0.5),
        "ffn_w_down": nrm(ks[17], (N_EVEN, D_FF, D), BETA * D_FF ** -0.5),
        "od_w_in": nrm(ks[18], (N_ODD, D, ODD_IN), D ** -0.5),
        "od_q_norm": 1.0 + nrm(ks[19], (N_ODD, Q_RANK), 0.05),
        "od_kv_norm": 1.0 + nrm(ks[20], (N_ODD, KV_RANK), 0.05),
        "od_w_q_b": nrm(ks[21], (N_ODD, Q_RANK, N_HEADS * QK_HEAD), Q_RANK ** -0.5),
        "od_w_kv_b": nrm(ks[22], (N_ODD, KV_RANK, N_HEADS * (QK_NOPE + V_DIM)), KV_RANK ** -0.5),
        "od_fnet_w": nrm(ks[23], (N_ODD, FNET_DIM, FNET_DIM), FNET_DIM ** -0.5),
        "od_w_out": nrm(ks[24], (N_ODD, ODD_MIX, D), BETA * ODD_MIX ** -0.5),
        "moe_router": nrm(ks[25], (N_ODD, D, N_EXPERTS), D ** -0.5),
        "moe_w_gate": nrm(ks[26], (N_ODD, N_EXPERTS, D, D_FF_EXPERT), D ** -0.5),
        "moe_w_up": nrm(ks[27], (N_ODD, N_EXPERTS, D, D_FF_EXPERT), D ** -0.5),
        "moe_w_down": nrm(ks[28], (N_ODD, N_EXPERTS, D_FF_EXPERT, D), BETA * D_FF_EXPERT ** -0.5),
    }


def reference(x_prompt, x_sample, cache_ckv, cache_kpe, c, c_ctx, ada_w, ada_b, ln_g, ln_b,
              ev_w_in, ev_conv_w, ev_pool_w, ev_pool_scale, ev_w_out,
              ffn_w_gate, ffn_w_up, ffn_w_down,
              od_w_in, od_q_norm, od_kv_norm, od_w_q_b, od_w_kv_b, od_fnet_w, od_w_out,
              moe_router, moe_w_gate, moe_w_up, moe_w_down):
    cos, sin = axial_rope(x_sample.shape[1])
    xp, xs = x_prompt, x_sample
    ckv_layers, kpe_layers = [], []
    for l in range(DEPTH):
        i = l // 2
        mp = modulation(c_ctx, ada_w[l], ada_b[l])
        ms = modulation(c, ada_w[l], ada_b[l])
        hp = modulate(xp, mp[0], mp[1])
        hs = modulate(xs, ms[0], ms[1])
        if l % 2 == 0:
            yp = even_mixer(hp, ev_w_in[i], ev_conv_w[i], ev_pool_w[i], ev_pool_scale[i], ev_w_out[i])
            ys = even_mixer(hs, ev_w_in[i], ev_conv_w[i], ev_pool_w[i], ev_pool_scale[i], ev_w_out[i])
        else:
            yp, ckv, kpe = odd_mixer_context(hp, od_w_in[i], od_q_norm[i], od_kv_norm[i], od_w_q_b[i],
                                             od_w_kv_b[i], od_fnet_w[i], od_w_out[i])
            ys = odd_mixer_latent(hs, cache_ckv[:, i], cache_kpe[:, i], cos, sin, od_w_in[i], od_q_norm[i],
                                  od_kv_norm[i], od_w_q_b[i], od_w_kv_b[i], od_fnet_w[i], od_w_out[i])
            ckv_layers.append(ckv)
            kpe_layers.append(kpe)
        xp = post_norm(xp, yp, mp[2], ln_g[l, 0], ln_b[l, 0])
        xs = post_norm(xs, ys, ms[2], ln_g[l, 0], ln_b[l, 0])
        hp = modulate(xp, mp[3], mp[4])
        hs = modulate(xs, ms[3], ms[4])
        if l % 2 == 0:
            fp = swiglu(hp, ffn_w_gate[i], ffn_w_up[i], ffn_w_down[i])
            fs = swiglu(hs, ffn_w_gate[i], ffn_w_up[i], ffn_w_down[i])
        else:
            fp = moe_swiglu(hp, moe_router[i], moe_w_gate[i], moe_w_up[i], moe_w_down[i])
            fs = moe_swiglu(hs, moe_router[i], moe_w_gate[i], moe_w_up[i], moe_w_down[i])
        xp = post_norm(xp, fp, mp[5], ln_g[l, 1], ln_b[l, 1])
        xs = post_norm(xs, fs, ms[5], ln_g[l, 1], ln_b[l, 1])
    new_ckv = jnp.stack(ckv_layers, axis=1)
    new_kpe = jnp.stack(kpe_layers, axis=1)
    return (xp, xs, new_ckv, new_kpe)
```

```python
import functools

import numpy as np
import jax
import jax.numpy as jnp
from jax import lax
from jax.experimental import pallas as pl
from jax.experimental.pallas import tpu as pltpu

F32 = jnp.float32
BF16 = jnp.bfloat16

D = 1024
BATCH, SEQ = 32, 256
DEC_BATCH, DEC_SEQ = 2, 2048
PAST = 512
GRID_W = 64
T_P = BATCH * SEQ
T_S = DEC_BATCH * DEC_SEQ
T = T_P + T_S
N_COND = 1 + DEC_BATCH

CONV_DIM = 512
POOL_WINDOWS = (2, 4, 8, 16)
POOL_GROUP = 128
N_HEADS = 8
QK_NOPE, QK_ROPE, V_DIM = 128, 64, 128
Q_RANK, KV_RANK = 384, 256
FNET_DIM, FNET_GROUP_DIM = 256, 64
D_FF = 2816
N_EXPERTS = 8
D_FF_EXPERT = 1792
DEPTH = 2
ALPHA = (2 * DEPTH) ** 0.25
LN_EPS = 1e-5
RMS_EPS = 1e-6
ROPE_THETA = 10000.0

TM = 256
NP_TILES = T_P // TM
TPS = DEC_SEQ // TM
N_TILES = T // TM
HALO = 8
TM_FFN = 512
TM_KV = 512
LK_S = PAST + DEC_SEQ
KV_ROWS = DEC_BATCH * LK_S + T_P
TM_E = 256
R_MAX = 2 * T + N_EXPERTS * TM_E
N_ETILES = R_MAX // TM_E
VMEM_LIMIT = 56 * 1024 * 1024


def _cond_index(i, tm=TM):
    return jnp.where(i < T_P // tm, 0, 1 + (i - T_P // tm) // (DEC_SEQ // tm))


def _const_spec(shape):
    nd = len(shape)
    return pl.BlockSpec(shape, lambda *_: (0,) * nd, pipeline_mode=pl.Buffered(1))


def _params(n_axes=1, vmem=VMEM_LIMIT):
    return pltpu.CompilerParams(dimension_semantics=("arbitrary",) * n_axes, vmem_limit_bytes=vmem)


def _layer_norm(v, g, b):
    mu = jnp.mean(v, axis=-1, keepdims=True)
    d = v - mu
    var = jnp.mean(d * d, axis=-1, keepdims=True)
    return d * lax.rsqrt(var + LN_EPS) * g + b


def _split_bf16(v):
    hi = v.astype(BF16)
    lo = (v - hi.astype(F32)).astype(BF16)
    return hi, lo


def _dot(a, b):
    return jnp.dot(a, b, preferred_element_type=F32)


def _silu(v):
    return v / (1.0 + jnp.exp(-v))


def _mod_kernel(cond_ref, w_ref, b_ref, o_ref):
    s = _silu(cond_ref[...]).astype(BF16)
    o_ref[...] = _dot(s, w_ref[...].astype(BF16)) + b_ref[...]


def _modulation(cond8, ada_w, ada_b):
    nb = 6 * D // 1024
    return pl.pallas_call(
        _mod_kernel,
        grid=(DEPTH, nb),
        in_specs=[
            pl.BlockSpec((8, D), lambda l, j: (0, 0)),
            pl.BlockSpec((None, D, 1024), lambda l, j: (l, 0, j)),
            pl.BlockSpec((None, 1, 1024), lambda l, j: (l, 0, j)),
        ],
        out_specs=pl.BlockSpec((None, 8, 1024), lambda l, j: (l, 0, j)),
        out_shape=jax.ShapeDtypeStruct((DEPTH, 8, 6 * D), F32),
        compiler_params=_params(2),
        name="adaln_modulation",
    )(cond8, ada_w, ada_b.reshape(DEPTH, 1, 6 * D))


def _even_mixer_kernel(xp_ref, xc_ref, xn_ref, mod_ref, win_ref, convw_ref, poolw_ref,
                       pscale_ref, wout_ref, g_ref, b_ref, o_ref):
    i = pl.program_id(0)
    j = i - NP_TILES
    is_latent = i >= NP_TILES
    left_ok = jnp.logical_and(is_latent, lax.rem(j, TPS) != 0)
    right_ok = jnp.logical_and(is_latent, lax.rem(j, TPS) != TPS - 1)
    row_lo = jnp.where(left_ok, 0, HALO)
    row_hi = jnp.where(right_ok, TM + 2 * HALO, TM + HALO)

    shift, scale, gate = mod_ref[0:1, :], mod_ref[1:2, :], mod_ref[2:3, :]
    xc = xc_ref[...]
    x_all = jnp.concatenate([xp_ref[...], xc, xn_ref[...]], axis=0)
    h = (x_all * (1.0 + scale) + shift).astype(BF16)
    u = _dot(h, win_ref[...])
    rows = lax.broadcasted_iota(jnp.int32, (TM + 2 * HALO, 1), 0)
    in_seq = jnp.logical_and(rows >= row_lo, rows < row_hi)
    u = jnp.where(in_seq, u, 0.0)

    ux, ub = u[:, 0:CONV_DIM], u[:, CONV_DIM:2 * CONV_DIM]
    uc, up = u[:, 2 * CONV_DIM:3 * CONV_DIM], u[:, 3 * CONV_DIM:]
    z = uc * ux
    conv = (z[HALO - 1:HALO - 1 + TM] * convw_ref[0:1, :]
            + z[HALO:HALO + TM] * convw_ref[1:2, :]
            + z[HALO + 1:HALO + 1 + TM] * convw_ref[2:3, :])
    ya = ub[HALO:HALO + TM] * conv

    t_idx = lax.broadcasted_iota(jnp.int32, (TM, TM + 2 * HALO), 0)
    r_idx = lax.broadcasted_iota(jnp.int32, (TM, TM + 2 * HALO), 1)
    col_ok = jnp.logical_and(r_idx >= row_lo, r_idx < row_hi)
    pos = r_idx - HALO
    t_col = lax.broadcasted_iota(jnp.int32, (TM, 1), 0)
    first = jnp.where(left_ok, -HALO, 0)
    last = jnp.where(right_ok, TM + HALO, TM)
    up_hi, up_lo = _split_bf16(up)
    yb_groups = []
    for gi, w in enumerate(POOL_WINDOWS):
        band = jnp.logical_and(jnp.logical_and(pos >= t_idx - w // 2, pos < t_idx + w // 2), col_ok)
        band = jnp.where(band, 1.0, 0.0).astype(BF16)
        sl = slice(gi * POOL_GROUP, (gi + 1) * POOL_GROUP)
        tot = _dot(band, up_hi[:, sl]) + _dot(band, up_lo[:, sl])
        cnt = (jnp.minimum(t_col + w // 2, last) - jnp.maximum(t_col - w // 2, first)).astype(F32)
        p = tot / cnt - up[HALO:HALO + TM, sl]
        yb_groups.append(_dot(p.astype(BF16), poolw_ref[gi]))
    yb = jnp.concatenate(yb_groups, axis=1) * pscale_ref[...]

    mix = jnp.concatenate([ya, yb], axis=1).astype(BF16)
    y = _dot(mix, wout_ref[...])
    o_ref[...] = _layer_norm(ALPHA * xc + gate * y, g_ref[...], b_ref[...])


def _even_mixer(x, mods, w_in, conv_w, pool_w, pool_scale, w_out, g, b):
    hb = TM // HALO
    n8 = T // HALO
    return pl.pallas_call(
        _even_mixer_kernel,
        grid=(N_TILES,),
        in_specs=[
            pl.BlockSpec((HALO, D), lambda i: (jnp.maximum(i * hb - 1, 0), 0)),
            pl.BlockSpec((TM, D), lambda i: (i, 0)),
            pl.BlockSpec((HALO, D), lambda i: (jnp.minimum((i + 1) * hb, n8 - 1), 0)),
            pl.BlockSpec((None, 6, D), lambda i: (_cond_index(i), 0, 0)),
            _const_spec((D, 4 * CONV_DIM)),
            _const_spec((3, CONV_DIM)),
            _const_spec((4, POOL_GROUP, POOL_GROUP)),
            _const_spec((1, 4 * POOL_GROUP)),
            _const_spec((D, D)),
            _const_spec((1, D)),
            _const_spec((1, D)),
        ],
        out_specs=pl.BlockSpec((TM, D), lambda i: (i, 0)),
        out_shape=jax.ShapeDtypeStruct((T, D), F32),
        compiler_params=_params(1),
        name="even_mixer",
    )(x, x, x, mods, w_in, conv_w, pool_w, pool_scale, w_out, g, b)


FF_CHUNK = D_FF // 2


def _ffn_kernel(x_ref, mod_ref, wg_ref, wu_ref, wd_ref, g_ref, b_ref, o_ref):
    shift, scale, gate = mod_ref[3:4, :], mod_ref[4:5, :], mod_ref[5:6, :]
    x = x_ref[...]
    h = (x * (1.0 + scale) + shift).astype(BF16)
    f = jnp.zeros((TM_FFN, D), F32)
    for c in range(D_FF // FF_CHUNK):
        sl = slice(c * FF_CHUNK, (c + 1) * FF_CHUNK)
        a = _silu(_dot(h, wg_ref[:, sl])) * _dot(h, wu_ref[:, sl])
        f = f + _dot(a.astype(BF16), wd_ref[sl, :])
    o_ref[...] = _layer_norm(ALPHA * x + gate * f, g_ref[...], b_ref[...])


def _ffn(x, mods, wg, wu, wd, g, b):
    return pl.pallas_call(
        _ffn_kernel,
        grid=(T // TM_FFN,),
        in_specs=[
            pl.BlockSpec((TM_FFN, D), lambda i: (i, 0)),
            pl.BlockSpec((None, 6, D), lambda i: (_cond_index(i, TM_FFN), 0, 0)),
            _const_spec((D, D_FF)),
            _const_spec((D, D_FF)),
            _const_spec((D_FF, D)),
            _const_spec((1, D)),
            _const_spec((1, D)),
        ],
        out_specs=pl.BlockSpec((TM_FFN, D), lambda i: (i, 0)),
        out_shape=jax.ShapeDtypeStruct((T, D), F32),
        compiler_params=_params(1),
        name="dense_swiglu",
    )(x, mods, wg, wu, wd, g, b)


W_IN_EXT = Q_RANK + KV_RANK + FNET_DIM + 128 + 128
QH = 256
ATT_SCALE = (QK_NOPE + QK_ROPE) ** -0.5


def _odd_proj_kernel(x_ref, mod_ref, rope_ref, win_ref, qn_ref, kvn_ref, wqa_ref, wqb_ref,
                     avg_ref, dfth_ref, dftl_ref,
                     q_ref, ckv_ref, kpe_ref, yh_ref, yl_ref):
    shift, scale = mod_ref[0:1, :], mod_ref[1:2, :]
    h = (x_ref[...] * (1.0 + scale) + shift).astype(BF16)
    u = _dot(h, win_ref[...])
    uq = u[:, 0:Q_RANK]
    ukv = u[:, Q_RANK:Q_RANK + KV_RANK]
    uf = u[:, Q_RANK + KV_RANK:Q_RANK + KV_RANK + FNET_DIM]
    o = Q_RANK + KV_RANK + FNET_DIM
    upe, upe_rot = u[:, o:o + 128], u[:, o + 128:o + 256]
    cos, sin = rope_ref[:, 0:128], rope_ref[:, 128:256]

    ckv_ref[...] = ukv * lax.rsqrt(jnp.mean(ukv * ukv, axis=-1, keepdims=True) + RMS_EPS) * kvn_ref[...]
    kpe_ref[...] = upe * cos + upe_rot * sin

    qlat = (uq * lax.rsqrt(jnp.mean(uq * uq, axis=-1, keepdims=True) + RMS_EPS) * qn_ref[...]).astype(BF16)
    qa = _dot(qlat, wqa_ref[...])
    qb = _dot(qlat, wqb_ref[...])
    for hd in range(N_HEADS):
        nope = qa[:, hd * QH:hd * QH + 128]
        pe = qa[:, hd * QH + 128:(hd + 1) * QH] * cos + qb[:, hd * 128:(hd + 1) * 128] * sin
        q_ref[:, hd * QH:hd * QH + 128] = (nope * ATT_SCALE).astype(BF16)
        q_ref[:, hd * QH + 128:(hd + 1) * QH] = (pe * ATT_SCALE).astype(BF16)

    avg = avg_ref[...]
    uf_hi, uf_lo = _split_bf16(uf)
    mu = _dot(uf_hi, avg) + _dot(uf_lo, avg)
    dlt = uf - mu
    sq_hi, sq_lo = _split_bf16(dlt * dlt)
    var = _dot(sq_hi, avg) + _dot(sq_lo, avg)
    xn = dlt * lax.rsqrt(var + LN_EPS)
    xn_hi, xn_lo = _split_bf16(xn)
    y = _dot(xn_hi, dfth_ref[...]) + _dot(xn_lo, dfth_ref[...]) + _dot(xn_hi, dftl_ref[...])
    y_hi, y_lo = _split_bf16(y)
    yh_ref[...] = y_hi
    yl_ref[...] = y_lo


def _odd_proj(x, mods, rope_tab, w_in_ext, q_norm, kv_norm, wqa, wqb, avg, dft_hi, dft_lo):
    def rope_index(i):
        return (jnp.where(i < NP_TILES, 0, 1 + lax.rem(i - NP_TILES, TPS)), 0)

    return pl.pallas_call(
        _odd_proj_kernel,
        grid=(N_TILES,),
        in_specs=[
            pl.BlockSpec((TM, D), lambda i: (i, 0)),
            pl.BlockSpec((None, 6, D), lambda i: (_cond_index(i), 0, 0)),
            pl.BlockSpec((TM, 256), rope_index),
            _const_spec((D, W_IN_EXT)),
            _const_spec((1, Q_RANK)),
            _const_spec((1, KV_RANK)),
            _const_spec((Q_RANK, N_HEADS * QH)),
            _const_spec((Q_RANK, N_HEADS * 128)),
            _const_spec((FNET_DIM, FNET_DIM)),
            _const_spec((FNET_DIM, 2 * FNET_DIM)),
            _const_spec((FNET_DIM, 2 * FNET_DIM)),
        ],
        out_specs=[
            pl.BlockSpec((TM, N_HEADS * QH), lambda i: (i, 0)),
            pl.BlockSpec((TM, KV_RANK), lambda i: (i, 0)),
            pl.BlockSpec((TM, 128), lambda i: (i, 0)),
            pl.BlockSpec((TM, 2 * FNET_DIM), lambda i: (i, 0)),
            pl.BlockSpec((TM, 2 * FNET_DIM), lambda i: (i, 0)),
        ],
        out_shape=[
            jax.ShapeDtypeStruct((T, N_HEADS * QH), BF16),
            jax.ShapeDtypeStruct((T, KV_RANK), F32),
            jax.ShapeDtypeStruct((T, 128), F32),
            jax.ShapeDtypeStruct((T, 2 * FNET_DIM), BF16),
            jax.ShapeDtypeStruct((T, 2 * FNET_DIM), BF16),
        ],
        compiler_params=_params(1),
        name="odd_projections",
    )(x, mods, rope_tab, w_in_ext, q_norm, kv_norm, wqa, wqb, avg, dft_hi, dft_lo)


def _kv_kernel(c_ref, w_ref, o_ref):
    o_ref[...] = _dot(c_ref[...].astype(BF16), w_ref[...]).astype(BF16)


def _kv_expand(ckv_all, w_kv):
    return pl.pallas_call(
        _kv_kernel,
        grid=(KV_ROWS // TM_KV,),
        in_specs=[pl.BlockSpec((TM_KV, KV_RANK), lambda i: (i, 0)),
                  _const_spec((KV_RANK, 2 * N_HEADS * 128))],
        out_specs=pl.BlockSpec((TM_KV, 2 * N_HEADS * 128), lambda i: (i, 0)),
        out_shape=jax.ShapeDtypeStruct((KV_ROWS, 2 * N_HEADS * 128), BF16),
        compiler_params=_params(1),
        name="kv_expand",
    )(ckv_all, w_kv)


def _attn_body(q_ref, kv_ref, kpe_ref, o_ref):
    kpe = kpe_ref[...].astype(BF16)
    for hd in range(N_HEADS):
        qh = q_ref[:, hd * QH:(hd + 1) * QH]
        kh = jnp.concatenate([kv_ref[:, hd * 128:(hd + 1) * 128], kpe], axis=1)
        s = lax.dot_general(qh, kh, (((1,), (1,)), ((), ())), preferred_element_type=F32)
        p = jnp.exp(s - jnp.max(s, axis=-1, keepdims=True))
        den = jnp.sum(p, axis=-1, keepdims=True)
        vh = kv_ref[:, (N_HEADS + hd) * 128:(N_HEADS + hd + 1) * 128]
        o = _dot(p.astype(BF16), vh)
        o_ref[:, hd * 128:(hd + 1) * 128] = (o / den).astype(BF16)


def _attn_kernel(q_ref, kvp_ref, kpep_ref, kvs_ref, kpes_ref, o_ref):
    @pl.when(pl.program_id(0) < NP_TILES)
    def _():
        _attn_body(q_ref, kvp_ref, kpep_ref, o_ref)

    @pl.when(pl.program_id(0) >= NP_TILES)
    def _():
        _attn_body(q_ref, kvs_ref, kpes_ref, o_ref)


def _attention(q, kv, kpe_all):
    kv_p0 = DEC_BATCH * LK_S // SEQ
    ctx_blk = lambda i: (kv_p0 + jnp.minimum(i, NP_TILES - 1), 0)
    lat_blk = lambda i: (jnp.maximum(i - NP_TILES, 0) // TPS, 0)
    return pl.pallas_call(
        _attn_kernel,
        grid=(N_TILES,),
        in_specs=[
            pl.BlockSpec((TM, N_HEADS * QH), lambda i: (i, 0)),
            pl.BlockSpec((SEQ, 2 * N_HEADS * 128), ctx_blk),
            pl.BlockSpec((SEQ, 128), ctx_blk),
            pl.BlockSpec((LK_S, 2 * N_HEADS * 128), lat_blk),
            pl.BlockSpec((LK_S, 128), lat_blk),
        ],
        out_specs=pl.BlockSpec((TM, N_HEADS * V_DIM), lambda i: (i, 0)),
        out_shape=jax.ShapeDtypeStruct((T, N_HEADS * V_DIM), BF16),
        compiler_params=_params(1),
        name="attention",
    )(q, kv, kpe_all, kv, kpe_all)


def _pos_dft_body(yh_ref, yl_ref, ch_ref, cl_ref, sh_ref, sl_ref, o_ref):
    yc_h, ys_h = yh_ref[:, 0:FNET_DIM], yh_ref[:, FNET_DIM:]
    yc_l, ys_l = yl_ref[:, 0:FNET_DIM], yl_ref[:, FNET_DIM:]
    ch, cl, sh, sl = ch_ref[...], cl_ref[...], sh_ref[...], sl_ref[...]
    f = (_dot(ch, yc_h) + _dot(ch, yc_l) + _dot(cl, yc_h)
         + _dot(sh, ys_h) + _dot(sh, ys_l) + _dot(sl, ys_h))
    o_ref[...] = f.astype(BF16)


def _pos_dft_kernel(yhp_ref, ylp_ref, chp_ref, clp_ref, shp_ref, slp_ref,
                    yhs_ref, yls_ref, chs_ref, cls_ref, shs_ref, sls_ref, o_ref):
    @pl.when(pl.program_id(0) < NP_TILES)
    def _():
        _pos_dft_body(yhp_ref, ylp_ref, chp_ref, clp_ref, shp_ref, slp_ref, o_ref)

    @pl.when(pl.program_id(0) >= NP_TILES)
    def _():
        _pos_dft_body(yhs_ref, yls_ref, chs_ref, cls_ref, shs_ref, sls_ref, o_ref)


def _pos_dft(yh, yl, tabs_p, tabs_s):
    ctx_blk = lambda i: (jnp.minimum(i, NP_TILES - 1), 0)
    lat_seq = lambda i: (T_P // DEC_SEQ + jnp.maximum(i - NP_TILES, 0) // TPS, 0)
    lat_tab = lambda i: (lax.rem(jnp.maximum(i - NP_TILES, 0), TPS), 0)
    return pl.pallas_call(
        _pos_dft_kernel,
        grid=(N_TILES,),
        in_specs=[pl.BlockSpec((SEQ, 2 * FNET_DIM), ctx_blk)] * 2
        + [_const_spec((SEQ, SEQ))] * 4
        + [pl.BlockSpec((DEC_SEQ, 2 * FNET_DIM), lat_seq)] * 2
        + [pl.BlockSpec((TM, DEC_SEQ), lat_tab)] * 4,
        out_specs=pl.BlockSpec((TM, FNET_DIM), lambda i: (i, 0)),
        out_shape=jax.ShapeDtypeStruct((T, FNET_DIM), BF16),
        compiler_params=_params(1),
        name="pos_dft",
    )(yh, yl, *tabs_p, yh, yl, *tabs_s)


def _odd_merge_kernel(x_ref, attn_ref, f_ref, mod_ref, fw_ref, wo_ref, g_ref, b_ref, rh_ref, rl_ref,
                      xo_ref, h_ref, info_ref):
    gate = mod_ref[2:3, :]
    shift2, scale2 = mod_ref[3:4, :], mod_ref[4:5, :]
    fm = _dot(f_ref[...], fw_ref[...]).astype(BF16)
    y = _dot(attn_ref[...], wo_ref[0:N_HEADS * V_DIM, :]) + _dot(fm, wo_ref[N_HEADS * V_DIM:, :])
    x = _layer_norm(ALPHA * x_ref[...] + gate * y, g_ref[...], b_ref[...])
    xo_ref[...] = x
    h = x * (1.0 + scale2) + shift2
    h_ref[...] = h

    h_hi, h_lo = _split_bf16(h)
    logits = _dot(h_hi, rh_ref[...]) + _dot(h_lo, rh_ref[...]) + _dot(h_hi, rl_ref[...])
    lane = lax.broadcasted_iota(jnp.int32, (TM, 128), 1)
    neg = jnp.float32(-jnp.inf)
    logits = jnp.where(lane < N_EXPERTS, logits, neg)
    m1 = jnp.max(logits, axis=-1, keepdims=True)
    i1 = jnp.min(jnp.where(logits == m1, lane, 128), axis=-1, keepdims=True)
    rest = jnp.where(lane == i1, neg, logits)
    m2 = jnp.max(rest, axis=-1, keepdims=True)
    i2 = jnp.min(jnp.where(rest == m2, lane, 128), axis=-1, keepdims=True)
    e2 = jnp.exp(m2 - m1)
    w1 = 1.0 / (1.0 + e2)
    w2 = e2 / (1.0 + e2)
    info = jnp.where(lane == 0, w1, 0.0)
    info = jnp.where(lane == 1, w2, info)
    info = jnp.where(lane == 2, i1.astype(F32), info)
    info = jnp.where(lane == 3, i2.astype(F32), info)
    info_ref[...] = info


def _odd_merge(x, attn, f, mods, fnet_w, w_out, g, b, r_hi, r_lo):
    row = lambda i: (i, 0)
    return pl.pallas_call(
        _odd_merge_kernel,
        grid=(N_TILES,),
        in_specs=[
            pl.BlockSpec((TM, D), row),
            pl.BlockSpec((TM, N_HEADS * V_DIM), row),
            pl.BlockSpec((TM, FNET_DIM), row),
            pl.BlockSpec((None, 6, D), lambda i: (_cond_index(i), 0, 0)),
            _const_spec((FNET_DIM, FNET_DIM)),
            _const_spec((N_HEADS * V_DIM + FNET_DIM, D)),
            _const_spec((1, D)),
            _const_spec((1, D)),
            _const_spec((D, 128)),
            _const_spec((D, 128)),
        ],
        out_specs=[pl.BlockSpec((TM, D), row), pl.BlockSpec((TM, D), row), pl.BlockSpec((TM, 128), row)],
        out_shape=[jax.ShapeDtypeStruct((T, D), F32), jax.ShapeDtypeStruct((T, D), F32),
                   jax.ShapeDtypeStruct((T, 128), F32)],
        compiler_params=_params(1),
        name="odd_merge_router",
    )(x, attn, f, mods, fnet_w, w_out, g, b, r_hi, r_lo)


TM_D = 512


def _row_copy(src, s, dst, d, sem):
    return pltpu.make_async_copy(src.at[pl.ds(s, 1)], dst.at[pl.ds(d, 1)], sem)


def _dispatch_kernel(dest_ref, h_ref, xs_in_ref, xs_ref, sem):
    del xs_in_ref
    base = pl.program_id(0) * TM_D

    def issue(r, carry):
        t = base + r
        _row_copy(h_ref, t, xs_ref, dest_ref[t], sem).start()
        _row_copy(h_ref, t, xs_ref, dest_ref[T + t], sem).start()
        return carry

    lax.fori_loop(0, TM_D, issue, 0)

    def drain(r, carry):
        _row_copy(h_ref, 0, xs_ref, 0, sem).wait()
        _row_copy(h_ref, 0, xs_ref, 0, sem).wait()
        return carry

    lax.fori_loop(0, TM_D, drain, 0)


def _dispatch(dest, h, xs_init):
    return pl.pallas_call(
        _dispatch_kernel,
        grid_spec=pltpu.PrefetchScalarGridSpec(
            num_scalar_prefetch=1,
            grid=(T // TM_D,),
            in_specs=[pl.BlockSpec(memory_space=pl.ANY), pl.BlockSpec(memory_space=pl.ANY)],
            out_specs=pl.BlockSpec(memory_space=pl.ANY),
            scratch_shapes=[pltpu.SemaphoreType.DMA(())],
        ),
        out_shape=jax.ShapeDtypeStruct((R_MAX, D), F32),
        input_output_aliases={2: 0},
        compiler_params=pltpu.CompilerParams(dimension_semantics=("arbitrary",), has_side_effects=True),
        name="expert_dispatch",
    )(dest, h, xs_init)


def _expert_kernel(te_ref, nt_ref, xs_ref, wg_ref, wu_ref, wd_ref, o_ref):
    @pl.when(pl.program_id(0) < nt_ref[0])
    def _():
        h = xs_ref[...].astype(BF16)
        a = _silu(_dot(h, wg_ref[...])) * _dot(h, wu_ref[...])
        o_ref[...] = _dot(a.astype(BF16), wd_ref[...])

    @pl.when(pl.program_id(0) >= nt_ref[0])
    def _():
        o_ref[...] = jnp.zeros_like(o_ref)


def _experts(tile_expert, n_used, xs, wg, wu, wd):
    return pl.pallas_call(
        _expert_kernel,
        grid_spec=pltpu.PrefetchScalarGridSpec(
            num_scalar_prefetch=2,
            grid=(N_ETILES,),
            in_specs=[
                pl.BlockSpec((TM_E, D), lambda i, te, nt: (i, 0)),
                pl.BlockSpec((None, D, D_FF_EXPERT), lambda i, te, nt: (te[i], 0, 0)),
                pl.BlockSpec((None, D, D_FF_EXPERT), lambda i, te, nt: (te[i], 0, 0)),
                pl.BlockSpec((None, D_FF_EXPERT, D), lambda i, te, nt: (te[i], 0, 0)),
            ],
            out_specs=pl.BlockSpec((TM_E, D), lambda i, te, nt: (i, 0)),
        ),
        out_shape=jax.ShapeDtypeStruct((R_MAX, D), F32),
        compiler_params=_params(1),
        name="expert_swiglu",
    )(tile_expert, n_used, xs, wg, wu, wd)


def _combine_kernel(dest_ref, x_ref, info_ref, mod_ref, g_ref, b_ref, ys_ref, o_ref, buf0, buf1, sem):
    base = pl.program_id(0) * TM

    def issue(r, carry):
        t = base + r
        _row_copy(ys_ref, dest_ref[t], buf0, r, sem).start()
        _row_copy(ys_ref, dest_ref[T + t], buf1, r, sem).start()
        return carry

    lax.fori_loop(0, TM, issue, 0)

    def drain(r, carry):
        _row_copy(ys_ref, 0, buf0, 0, sem).wait()
        _row_copy(ys_ref, 0, buf1, 0, sem).wait()
        return carry

    lax.fori_loop(0, TM, drain, 0)

    gate = mod_ref[5:6, :]
    w1, w2 = info_ref[:, 0:1], info_ref[:, 1:2]
    y = w1 * buf0[...] + w2 * buf1[...]
    o_ref[...] = _layer_norm(ALPHA * x_ref[...] + gate * y, g_ref[...], b_ref[...])


def _combine(dest, x, info, mods, g, b, ys):
    return pl.pallas_call(
        _combine_kernel,
        grid_spec=pltpu.PrefetchScalarGridSpec(
            num_scalar_prefetch=1,
            grid=(N_TILES,),
            in_specs=[
                pl.BlockSpec((TM, D), lambda i, d: (i, 0)),
                pl.BlockSpec((TM, 128), lambda i, d: (i, 0)),
                pl.BlockSpec((None, 6, D), lambda i, d: (_cond_index(i), 0, 0)),
                pl.BlockSpec((1, D), lambda i, d: (0, 0)),
                pl.BlockSpec((1, D), lambda i, d: (0, 0)),
                pl.BlockSpec(memory_space=pl.ANY),
            ],
            out_specs=pl.BlockSpec((TM, D), lambda i, d: (i, 0)),
            scratch_shapes=[pltpu.VMEM((TM, D), F32), pltpu.VMEM((TM, D), F32),
                            pltpu.SemaphoreType.DMA(())],
        ),
        out_shape=jax.ShapeDtypeStruct((T, D), F32),
        compiler_params=_params(1),
        name="expert_combine",
    )(dest, x, info, mods, g, b, ys)


def _rot_cols(w):
    w4 = w.reshape(w.shape[:-1] + (2, 2, QK_ROPE // 4))
    return jnp.stack([-w4[..., 1, :], w4[..., 0, :]], axis=-2).reshape(w.shape)


def _rope_table():
    rows = DEC_SEQ // GRID_W
    row = jnp.repeat(jnp.arange(rows), GRID_W).astype(F32)
    col = jnp.tile(jnp.arange(GRID_W), rows).astype(F32)
    half = QK_ROPE // 2
    inv = ROPE_THETA ** (-jnp.arange(0, half, 2, dtype=F32) / half)
    ar, ac = row[:, None] * inv, col[:, None] * inv
    ang = jnp.concatenate([ar, ar, ac, ac], axis=-1)
    cos = jnp.concatenate([jnp.ones((TM, QK_ROPE), F32), jnp.cos(ang)], axis=0)
    sin = jnp.concatenate([jnp.zeros((TM, QK_ROPE), F32), jnp.sin(ang)], axis=0)
    n = cos.shape[0]
    return jnp.concatenate([cos, jnp.ones((n, 64), F32), sin, jnp.zeros((n, 64), F32)], axis=1)


def _dft_tables(n):
    k = jnp.arange(n, dtype=jnp.int32)
    ang = ((k[:, None] * k[None, :]) % n).astype(F32) * (2.0 * np.pi / n)
    sc = n ** -0.5
    return jnp.cos(ang) * sc, jnp.sin(ang) * sc


def _hi_lo(m):
    hi = m.astype(BF16)
    return hi, (m - hi.astype(F32)).astype(BF16)


def _block_diag4(m):
    return jnp.kron(jnp.eye(4, dtype=m.dtype), m)


def kernel(x_prompt, x_sample, cache_ckv, cache_kpe, c, c_ctx, ada_w, ada_b, ln_g, ln_b, ev_w_in, ev_conv_w, ev_pool_w, ev_pool_scale, ev_w_out, ffn_w_gate, ffn_w_up, ffn_w_down, od_w_in, od_q_norm, od_kv_norm, od_w_q_b, od_w_kv_b, od_fnet_w, od_w_out, moe_router, moe_w_gate, moe_w_up, moe_w_down):
    x = jnp.concatenate([x_prompt.reshape(T_P, D), x_sample.reshape(T_S, D)], axis=0)

    cond8 = jnp.concatenate([c_ctx[None, :], c, jnp.zeros((8 - N_COND, D), F32)], axis=0)
    mods = _modulation(cond8, ada_w, ada_b)[:, :N_COND].reshape(DEPTH, N_COND, 6, D)

    x = _even_mixer(x, mods[0], ev_w_in[0].astype(BF16), ev_conv_w[0], ev_pool_w[0].astype(BF16),
                    ev_pool_scale[0][None, :], ev_w_out[0].astype(BF16), ln_g[0, 0][None, :], ln_b[0, 0][None, :])
    x = _ffn(x, mods[0], ffn_w_gate[0].astype(BF16), ffn_w_up[0].astype(BF16), ffn_w_down[0].astype(BF16),
             ln_g[0, 1][None, :], ln_b[0, 1][None, :])

    w_in = od_w_in[0]
    w_pe = w_in[:, Q_RANK + KV_RANK:Q_RANK + KV_RANK + QK_ROPE]
    zpad = jnp.zeros((D, 64), F32)
    w_in_ext = jnp.concatenate([w_in[:, :Q_RANK + KV_RANK], w_in[:, Q_RANK + KV_RANK + QK_ROPE:],
                                w_pe, zpad, _rot_cols(w_pe), zpad], axis=1).astype(BF16)
    wq = od_w_q_b[0].reshape(Q_RANK, N_HEADS, QK_NOPE + QK_ROPE)
    zq = jnp.zeros((Q_RANK, N_HEADS, 64), F32)
    wqa = jnp.concatenate([wq, zq], axis=-1).reshape(Q_RANK, N_HEADS * QH).astype(BF16)
    wqb = jnp.concatenate([_rot_cols(wq[..., QK_NOPE:]), zq], axis=-1).reshape(Q_RANK, N_HEADS * 128).astype(BF16)
    wkv = od_w_kv_b[0].reshape(KV_RANK, N_HEADS, QK_NOPE + V_DIM)
    w_kv = jnp.concatenate([wkv[..., :QK_NOPE].reshape(KV_RANK, -1), wkv[..., QK_NOPE:].reshape(KV_RANK, -1)],
                           axis=1).astype(BF16)

    avg = _block_diag4(jnp.full((FNET_GROUP_DIM, FNET_GROUP_DIM), 1.0 / FNET_GROUP_DIM, F32)).astype(BF16)
    cc, sc = _dft_tables(FNET_GROUP_DIM)
    dft_c = jnp.concatenate([_block_diag4(cc), -_block_diag4(sc)], axis=1)
    dft_hi, dft_lo = _hi_lo(dft_c)

    q, ckv, kpe, yh, yl = _odd_proj(x, mods[1], _rope_table(), w_in_ext, od_q_norm[0][None, :],
                                    od_kv_norm[0][None, :], wqa, wqb, avg, dft_hi, dft_lo)

    cache_kpe128 = jnp.pad(cache_kpe[:, 0], ((0, 0), (0, 0), (0, 128 - QK_ROPE)))
    ckv_parts, kpe_parts = [], []
    for bi in range(DEC_BATCH):
        lat = slice(T_P + bi * DEC_SEQ, T_P + (bi + 1) * DEC_SEQ)
        ckv_parts += [cache_ckv[bi, 0], ckv[lat]]
        kpe_parts += [cache_kpe128[bi], kpe[lat]]
    ckv_all = jnp.concatenate(ckv_parts + [ckv[:T_P]], axis=0)
    kpe_all = jnp.concatenate(kpe_parts + [kpe[:T_P]], axis=0)
    kv = _kv_expand(ckv_all, w_kv)
    attn = _attention(q, kv, kpe_all)

    tabs_p = sum((_hi_lo(m) for m in _dft_tables(SEQ)), ())
    tabs_s = sum((_hi_lo(m) for m in _dft_tables(DEC_SEQ)), ())
    f = _pos_dft(yh, yl, tabs_p, tabs_s)

    router = jnp.pad(moe_router[0], ((0, 0), (0, 128 - N_EXPERTS)))
    r_hi, r_lo = _hi_lo(router)
    x, h2, info = _odd_merge(x, attn, f, mods[1], od_fnet_w[0].astype(BF16), od_w_out[0].astype(BF16),
                             ln_g[1, 0][None, :], ln_b[1, 0][None, :], r_hi, r_lo)

    e_flat = info[:, 2:4].astype(jnp.int32).T.reshape(-1)
    onehot = (e_flat[:, None] == jnp.arange(N_EXPERTS, dtype=jnp.int32)[None, :]).astype(jnp.int32)
    csum = jnp.cumsum(onehot, axis=0)
    counts = csum[-1]
    padded = ((counts + TM_E - 1) // TM_E) * TM_E
    g_end = jnp.cumsum(padded)
    g_start = g_end - padded
    dest = jnp.sum(onehot * (csum - 1 + g_start[None, :]), axis=1).astype(jnp.int32)
    tile_row = jnp.arange(N_ETILES, dtype=jnp.int32) * TM_E
    tile_expert = jnp.minimum(jnp.sum((tile_row[:, None] >= g_end[None, :]).astype(jnp.int32), axis=1),
                              N_EXPERTS - 1).astype(jnp.int32)
    n_used = (g_end[-1:] // TM_E).astype(jnp.int32)

    xs = _dispatch(dest, h2, jnp.zeros((R_MAX, D), F32))
    ys = _experts(tile_expert, n_used, xs, moe_w_gate[0].astype(BF16), moe_w_up[0].astype(BF16),
                  moe_w_down[0].astype(BF16))
    x = _combine(dest, x, info, mods[1], ln_g[1, 1][None, :], ln_b[1, 1][None, :], ys)

    y_prompt = x[:T_P].reshape(BATCH, SEQ, D)
    y_sample = x[T_P:].reshape(DEC_BATCH, DEC_SEQ, D)
    new_ckv = ckv[:T_P].reshape(BATCH, 1, SEQ, KV_RANK)
    new_kpe = kpe[:T_P, :QK_ROPE].reshape(BATCH, 1, SEQ, QK_ROPE)
    return (y_prompt, y_sample, new_ckv, new_kpe)
```

```python
import functools

import numpy as np
import jax
import jax.numpy as jnp
from jax import lax
from jax.experimental import pallas as pl
from jax.experimental.pallas import tpu as pltpu

F32 = jnp.float32
BF16 = jnp.bfloat16

D = 1024
BATCH, SEQ = 32, 256
DEC_BATCH, DEC_SEQ = 2, 2048
PAST = 512
GRID_W = 64
T_P = BATCH * SEQ
T_S = DEC_BATCH * DEC_SEQ
T = T_P + T_S
N_COND = 1 + DEC_BATCH

CONV_DIM = 512
POOL_WINDOWS = (2, 4, 8, 16)
POOL_GROUP = 128
N_HEADS = 8
QK_NOPE, QK_ROPE, V_DIM = 128, 64, 128
Q_RANK, KV_RANK = 384, 256
FNET_DIM, FNET_GROUP_DIM = 256, 64
D_FF = 2816
N_EXPERTS = 8
D_FF_EXPERT = 1792
DEPTH = 2
ALPHA = (2 * DEPTH) ** 0.25
LN_EPS = 1e-5
RMS_EPS = 1e-6
ROPE_THETA = 10000.0

TM = 256
NP_TILES = T_P // TM
TPS = DEC_SEQ // TM
N_TILES = T // TM
HALO = 8
TM_FFN = 512
TM_KV = 512
LK_S = PAST + DEC_SEQ
KV_ROWS = DEC_BATCH * LK_S + T_P
TM_E = 256
R_MAX = 2 * T + N_EXPERTS * TM_E
N_ETILES = R_MAX // TM_E
VMEM_LIMIT = 56 * 1024 * 1024


def _cond_index(i, tm=TM):
    return jnp.where(i < T_P // tm, 0, 1 + (i - T_P // tm) // (DEC_SEQ // tm))


def _const_spec(shape):
    nd = len(shape)
    return pl.BlockSpec(shape, lambda *_: (0,) * nd, pipeline_mode=pl.Buffered(1))


def _params(n_axes=1, vmem=VMEM_LIMIT):
    return pltpu.CompilerParams(dimension_semantics=("arbitrary",) * n_axes, vmem_limit_bytes=vmem)


def _layer_norm(v, g, b):
    mu = jnp.mean(v, axis=-1, keepdims=True)
    d = v - mu
    var = jnp.mean(d * d, axis=-1, keepdims=True)
    return d * lax.rsqrt(var + LN_EPS) * g + b


def _split_bf16(v):
    hi = v.astype(BF16)
    lo = (v - hi.astype(F32)).astype(BF16)
    return hi, lo


def _dot(a, b):
    return jnp.dot(a, b, preferred_element_type=F32)


def _silu(v):
    return v / (1.0 + jnp.exp(-v))


def _mod_kernel(cond_ref, w_ref, b_ref, o_ref):
    s = _silu(cond_ref[...]).astype(BF16)
    o_ref[...] = _dot(s, w_ref[...].astype(BF16)) + b_ref[...]


def _modulation(cond8, ada_w, ada_b):
    nb = 6 * D // 1024
    return pl.pallas_call(
        _mod_kernel,
        grid=(DEPTH, nb),
        in_specs=[
            pl.BlockSpec((8, D), lambda l, j: (0, 0)),
            pl.BlockSpec((None, D, 1024), lambda l, j: (l, 0, j)),
            pl.BlockSpec((None, 1, 1024), lambda l, j: (l, 0, j)),
        ],
        out_specs=pl.BlockSpec((None, 8, 1024), lambda l, j: (l, 0, j)),
        out_shape=jax.ShapeDtypeStruct((DEPTH, 8, 6 * D), F32),
        compiler_params=_params(2),
        name="adaln_modulation",
    )(cond8, ada_w, ada_b.reshape(DEPTH, 1, 6 * D))


def _even_mixer_kernel(xp_ref, xc_ref, xn_ref, mod_ref, win_ref, convw_ref, poolw_ref,
                       pscale_ref, wout_ref, g_ref, b_ref, o_ref):
    i = pl.program_id(0)
    j = i - NP_TILES
    is_latent = i >= NP_TILES
    left_ok = jnp.logical_and(is_latent, lax.rem(j, TPS) != 0)
    right_ok = jnp.logical_and(is_latent, lax.rem(j, TPS) != TPS - 1)
    row_lo = jnp.where(left_ok, 0, HALO)
    row_hi = jnp.where(right_ok, TM + 2 * HALO, TM + HALO)

    shift, scale, gate = mod_ref[0:1, :], mod_ref[1:2, :], mod_ref[2:3, :]
    xc = xc_ref[...]
    x_all = jnp.concatenate([xp_ref[...], xc, xn_ref[...]], axis=0)
    h = (x_all * (1.0 + scale) + shift).astype(BF16)
    u = _dot(h, win_ref[...])
    rows = lax.broadcasted_iota(jnp.int32, (TM + 2 * HALO, 1), 0)
    in_seq = jnp.logical_and(rows >= row_lo, rows < row_hi)
    u = jnp.where(in_seq, u, 0.0)

    ux, ub = u[:, 0:CONV_DIM], u[:, CONV_DIM:2 * CONV_DIM]
    uc, up = u[:, 2 * CONV_DIM:3 * CONV_DIM], u[:, 3 * CONV_DIM:]
    z = uc * ux
    conv = (z[HALO - 1:HALO - 1 + TM] * convw_ref[0:1, :]
            + z[HALO:HALO + TM] * convw_ref[1:2, :]
            + z[HALO + 1:HALO + 1 + TM] * convw_ref[2:3, :])
    ya = ub[HALO:HALO + TM] * conv

    t_idx = lax.broadcasted_iota(jnp.int32, (TM, TM + 2 * HALO), 0)
    r_idx = lax.broadcasted_iota(jnp.int32, (TM, TM + 2 * HALO), 1)
    col_ok = jnp.logical_and(r_idx >= row_lo, r_idx < row_hi)
    pos = r_idx - HALO
    t_col = lax.broadcasted_iota(jnp.int32, (TM, 1), 0)
    first = jnp.where(left_ok, -HALO, 0)
    last = jnp.where(right_ok, TM + HALO, TM)
    up_hi, up_lo = _split_bf16(up)
    yb_groups = []
    for gi, w in enumerate(POOL_WINDOWS):
        band = jnp.logical_and(jnp.logical_and(pos >= t_idx - w // 2, pos < t_idx + w // 2), col_ok)
        band = jnp.where(band, 1.0, 0.0).astype(BF16)
        sl = slice(gi * POOL_GROUP, (gi + 1) * POOL_GROUP)
        tot = _dot(band, up_hi[:, sl]) + _dot(band, up_lo[:, sl])
        cnt = (jnp.minimum(t_col + w // 2, last) - jnp.maximum(t_col - w // 2, first)).astype(F32)
        p = tot / cnt - up[HALO:HALO + TM, sl]
        yb_groups.append(_dot(p.astype(BF16), poolw_ref[gi]))
    yb = jnp.concatenate(yb_groups, axis=1) * pscale_ref[...]

    mix = jnp.concatenate([ya, yb], axis=1).astype(BF16)
    y = _dot(mix, wout_ref[...])
    o_ref[...] = _layer_norm(ALPHA * xc + gate * y, g_ref[...], b_ref[...])


def _even_mixer(x, mods, w_in, conv_w, pool_w, pool_scale, w_out, g, b):
    hb = TM // HALO
    n8 = T // HALO
    return pl.pallas_call(
        _even_mixer_kernel,
        grid=(N_TILES,),
        in_specs=[
            pl.BlockSpec((HALO, D), lambda i: (jnp.maximum(i * hb - 1, 0), 0)),
            pl.BlockSpec((TM, D), lambda i: (i, 0)),
            pl.BlockSpec((HALO, D), lambda i: (jnp.minimum((i + 1) * hb, n8 - 1), 0)),
            pl.BlockSpec((None, 6, D), lambda i: (_cond_index(i), 0, 0)),
            _const_spec((D, 4 * CONV_DIM)),
            _const_spec((3, CONV_DIM)),
            _const_spec((4, POOL_GROUP, POOL_GROUP)),
            _const_spec((1, 4 * POOL_GROUP)),
            _const_spec((D, D)),
            _const_spec((1, D)),
            _const_spec((1, D)),
        ],
        out_specs=pl.BlockSpec((TM, D), lambda i: (i, 0)),
        out_shape=jax.ShapeDtypeStruct((T, D), F32),
        compiler_params=_params(1),
        name="even_mixer",
    )(x, x, x, mods, w_in, conv_w, pool_w, pool_scale, w_out, g, b)


FF_CHUNK = D_FF // 2


def _ffn_kernel(x_ref, mod_ref, wg_ref, wu_ref, wd_ref, g_ref, b_ref, o_ref):
    shift, scale, gate = mod_ref[3:4, :], mod_ref[4:5, :], mod_ref[5:6, :]
    x = x_ref[...]
    h = (x * (1.0 + scale) + shift).astype(BF16)
    f = jnp.zeros((TM_FFN, D), F32)
    for c in range(D_FF // FF_CHUNK):
        sl = slice(c * FF_CHUNK, (c + 1) * FF_CHUNK)
        a = _silu(_dot(h, wg_ref[:, sl])) * _dot(h, wu_ref[:, sl])
        f = f + _dot(a.astype(BF16), wd_ref[sl, :])
    o_ref[...] = _layer_norm(ALPHA * x + gate * f, g_ref[...], b_ref[...])


def _ffn(x, mods, wg, wu, wd, g, b):
    return pl.pallas_call(
        _ffn_kernel,
        grid=(T // TM_FFN,),
        in_specs=[
            pl.BlockSpec((TM_FFN, D), lambda i: (i, 0)),
            pl.BlockSpec((None, 6, D), lambda i: (_cond_index(i, TM_FFN), 0, 0)),
            _const_spec((D, D_FF)),
            _const_spec((D, D_FF)),
            _const_spec((D_FF, D)),
            _const_spec((1, D)),
            _const_spec((1, D)),
        ],
        out_specs=pl.BlockSpec((TM_FFN, D), lambda i: (i, 0)),
        out_shape=jax.ShapeDtypeStruct((T, D), F32),
        compiler_params=_params(1),
        name="dense_swiglu",
    )(x, mods, wg, wu, wd, g, b)


W_IN_EXT = Q_RANK + KV_RANK + FNET_DIM + 128 + 128
QH = 256
ATT_SCALE = (QK_NOPE + QK_ROPE) ** -0.5


def _odd_proj_kernel(x_ref, mod_ref, rope_ref, win_ref, qn_ref, kvn_ref, wqa_ref, wqb_ref,
                     avg_ref, dfth_ref, dftl_ref,
                     q_ref, ckv_ref, kpe_ref, yh_ref, yl_ref):
    shift, scale = mod_ref[0:1, :], mod_ref[1:2, :]
    h = (x_ref[...] * (1.0 + scale) + shift).astype(BF16)
    u = _dot(h, win_ref[...])
    uq = u[:, 0:Q_RANK]
    ukv = u[:, Q_RANK:Q_RANK + KV_RANK]
    uf = u[:, Q_RANK + KV_RANK:Q_RANK + KV_RANK + FNET_DIM]
    o = Q_RANK + KV_RANK + FNET_DIM
    upe, upe_rot = u[:, o:o + 128], u[:, o + 128:o + 256]
    cos, sin = rope_ref[:, 0:128], rope_ref[:, 128:256]

    ckv_ref[...] = ukv * lax.rsqrt(jnp.mean(ukv * ukv, axis=-1, keepdims=True) + RMS_EPS) * kvn_ref[...]
    kpe_ref[...] = upe * cos + upe_rot * sin

    qlat = (uq * lax.rsqrt(jnp.mean(uq * uq, axis=-1, keepdims=True) + RMS_EPS) * qn_ref[...]).astype(BF16)
    qa = _dot(qlat, wqa_ref[...])
    qb = _dot(qlat, wqb_ref[...])
    for hd in range(N_HEADS):
        nope = qa[:, hd * QH:hd * QH + 128]
        pe = qa[:, hd * QH + 128:(hd + 1) * QH] * cos + qb[:, hd * 128:(hd + 1) * 128] * sin
        q_ref[:, hd * QH:hd * QH + 128] = (nope * ATT_SCALE).astype(BF16)
        q_ref[:, hd * QH + 128:(hd + 1) * QH] = (pe * ATT_SCALE).astype(BF16)

    avg = avg_ref[...]
    uf_hi, uf_lo = _split_bf16(uf)
    mu = _dot(uf_hi, avg) + _dot(uf_lo, avg)
    dlt = uf - mu
    sq_hi, sq_lo = _split_bf16(dlt * dlt)
    var = _dot(sq_hi, avg) + _dot(sq_lo, avg)
    xn = dlt * lax.rsqrt(var + LN_EPS)
    xn_hi, xn_lo = _split_bf16(xn)
    y = _dot(xn_hi, dfth_ref[...]) + _dot(xn_lo, dfth_ref[...]) + _dot(xn_hi, dftl_ref[...])
    y_hi, y_lo = _split_bf16(y)
    yh_ref[...] = y_hi
    yl_ref[...] = y_lo


def _odd_proj(x, mods, rope_tab, w_in_ext, q_norm, kv_norm, wqa, wqb, avg, dft_hi, dft_lo):
    def rope_index(i):
        return (jnp.where(i < NP_TILES, 0, 1 + lax.rem(i - NP_TILES, TPS)), 0)

    return pl.pallas_call(
        _odd_proj_kernel,
        grid=(N_TILES,),
        in_specs=[
            pl.BlockSpec((TM, D), lambda i: (i, 0)),
            pl.BlockSpec((None, 6, D), lambda i: (_cond_index(i), 0, 0)),
            pl.BlockSpec((TM, 256), rope_index),
            _const_spec((D, W_IN_EXT)),
            _const_spec((1, Q_RANK)),
            _const_spec((1, KV_RANK)),
            _const_spec((Q_RANK, N_HEADS * QH)),
            _const_spec((Q_RANK, N_HEADS * 128)),
            _const_spec((FNET_DIM, FNET_DIM)),
            _const_spec((FNET_DIM, 2 * FNET_DIM)),
            _const_spec((FNET_DIM, 2 * FNET_DIM)),
        ],
        out_specs=[
            pl.BlockSpec((TM, N_HEADS * QH), lambda i: (i, 0)),
            pl.BlockSpec((TM, KV_RANK), lambda i: (i, 0)),
            pl.BlockSpec((TM, 128), lambda i: (i, 0)),
            pl.BlockSpec((TM, 2 * FNET_DIM), lambda i: (i, 0)),
            pl.BlockSpec((TM, 2 * FNET_DIM), lambda i: (i, 0)),
        ],
        out_shape=[
            jax.ShapeDtypeStruct((T, N_HEADS * QH), BF16),
            jax.ShapeDtypeStruct((T, KV_RANK), F32),
            jax.ShapeDtypeStruct((T, 128), F32),
            jax.ShapeDtypeStruct((T, 2 * FNET_DIM), BF16),
            jax.ShapeDtypeStruct((T, 2 * FNET_DIM), BF16),
        ],
        compiler_params=_params(1),
        name="odd_projections",
    )(x, mods, rope_tab, w_in_ext, q_norm, kv_norm, wqa, wqb, avg, dft_hi, dft_lo)


def _kv_kernel(c_ref, w_ref, o_ref):
    o_ref[...] = _dot(c_ref[...].astype(BF16), w_ref[...]).astype(BF16)


def _kv_expand(ckv_all, w_kv):
    return pl.pallas_call(
        _kv_kernel,
        grid=(KV_ROWS // TM_KV,),
        in_specs=[pl.BlockSpec((TM_KV, KV_RANK), lambda i: (i, 0)),
                  _const_spec((KV_RANK, 2 * N_HEADS * 128))],
        out_specs=pl.BlockSpec((TM_KV, 2 * N_HEADS * 128), lambda i: (i, 0)),
        out_shape=jax.ShapeDtypeStruct((KV_ROWS, 2 * N_HEADS * 128), BF16),
        compiler_params=_params(1),
        name="kv_expand",
    )(ckv_all, w_kv)


def _attn_body(q_ref, kv_ref, kpe_ref, o_ref):
    kpe = kpe_ref[...].astype(BF16)
    for hd in range(N_HEADS):
        qh = q_ref[:, hd * QH:(hd + 1) * QH]
        kh = jnp.concatenate([kv_ref[:, hd * 128:(hd + 1) * 128], kpe], axis=1)
        s = lax.dot_general(qh, kh, (((1,), (1,)), ((), ())), preferred_element_type=F32)
        p = jnp.exp(s - jnp.max(s, axis=-1, keepdims=True))
        den = jnp.sum(p, axis=-1, keepdims=True)
        vh = kv_ref[:, (N_HEADS + hd) * 128:(N_HEADS + hd + 1) * 128]
        o = _dot(p.astype(BF16), vh)
        o_ref[:, hd * 128:(hd + 1) * 128] = (o / den).astype(BF16)


def _attn_kernel(q_ref, kvp_ref, kpep_ref, kvs_ref, kpes_ref, o_ref):
    @pl.when(pl.program_id(0) < NP_TILES)
    def _():
        _attn_body(q_ref, kvp_ref, kpep_ref, o_ref)

    @pl.when(pl.program_id(0) >= NP_TILES)
    def _():
        _attn_body(q_ref, kvs_ref, kpes_ref, o_ref)


def _attention(q, kv, kpe_all):
    kv_p0 = DEC_BATCH * LK_S // SEQ
    ctx_blk = lambda i: (kv_p0 + jnp.minimum(i, NP_TILES - 1), 0)
    lat_blk = lambda i: (jnp.maximum(i - NP_TILES, 0) // TPS, 0)
    return pl.pallas_call(
        _attn_kernel,
        grid=(N_TILES,),
        in_specs=[
            pl.BlockSpec((TM, N_HEADS * QH), lambda i: (i, 0)),
            pl.BlockSpec((SEQ, 2 * N_HEADS * 128), ctx_blk),
            pl.BlockSpec((SEQ, 128), ctx_blk),
            pl.BlockSpec((LK_S, 2 * N_HEADS * 128), lat_blk),
            pl.BlockSpec((LK_S, 128), lat_blk),
        ],
        out_specs=pl.BlockSpec((TM, N_HEADS * V_DIM), lambda i: (i, 0)),
        out_shape=jax.ShapeDtypeStruct((T, N_HEADS * V_DIM), BF16),
        compiler_params=_params(1),
        name="attention",
    )(q, kv, kpe_all, kv, kpe_all)


def _pos_dft_body(yh_ref, yl_ref, ch, cl, sh, sl, o_ref):
    yc_h, ys_h = yh_ref[:, 0:FNET_DIM], yh_ref[:, FNET_DIM:]
    yc_l, ys_l = yl_ref[:, 0:FNET_DIM], yl_ref[:, FNET_DIM:]
    f = (_dot(ch, yc_h) + _dot(ch, yc_l) + _dot(cl, yc_h)
         + _dot(sh, ys_h) + _dot(sh, ys_l) + _dot(sl, ys_h))
    o_ref[...] = f.astype(BF16)


def _pos_dft_kernel(yhp_ref, ylp_ref, chp_ref, clp_ref, shp_ref, slp_ref,
                    yhs_ref, yls_ref, c0_ref, s0_ref, cb_ref, sb_ref, o_ref):
    @pl.when(pl.program_id(0) < NP_TILES)
    def _():
        _pos_dft_body(yhp_ref, ylp_ref, chp_ref[...], clp_ref[...], shp_ref[...], slp_ref[...], o_ref)

    @pl.when(pl.program_id(0) >= NP_TILES)
    def _():
        j = lax.rem(pl.program_id(0) - NP_TILES, TPS)
        cb, sb = cb_ref[pl.ds(j, 1), :], sb_ref[pl.ds(j, 1), :]
        c0, s0 = c0_ref[...], s0_ref[...]
        ch, cl = _split_bf16(c0 * cb - s0 * sb)
        sh, sl = _split_bf16(s0 * cb + c0 * sb)
        _pos_dft_body(yhs_ref, yls_ref, ch, cl, sh, sl, o_ref)


def _pos_dft(yh, yl, tabs_p, base_s, step_s):
    ctx_blk = lambda i: (jnp.minimum(i, NP_TILES - 1), 0)
    lat_seq = lambda i: (T_P // DEC_SEQ + jnp.maximum(i - NP_TILES, 0) // TPS, 0)
    return pl.pallas_call(
        _pos_dft_kernel,
        grid=(N_TILES,),
        in_specs=[pl.BlockSpec((SEQ, 2 * FNET_DIM), ctx_blk)] * 2
        + [_const_spec((SEQ, SEQ))] * 4
        + [pl.BlockSpec((DEC_SEQ, 2 * FNET_DIM), lat_seq)] * 2
        + [_const_spec((TM, DEC_SEQ))] * 2
        + [_const_spec((TPS, DEC_SEQ))] * 2,
        out_specs=pl.BlockSpec((TM, FNET_DIM), lambda i: (i, 0)),
        out_shape=jax.ShapeDtypeStruct((T, FNET_DIM), BF16),
        compiler_params=_params(1),
        name="pos_dft",
    )(yh, yl, *tabs_p, yh, yl, *base_s, *step_s)


def _odd_merge_kernel(x_ref, attn_ref, f_ref, mod_ref, fw_ref, wo_ref, g_ref, b_ref, rh_ref, rl_ref,
                      xo_ref, h_ref, info_ref):
    gate = mod_ref[2:3, :]
    shift2, scale2 = mod_ref[3:4, :], mod_ref[4:5, :]
    fm = _dot(f_ref[...], fw_ref[...]).astype(BF16)
    y = _dot(attn_ref[...], wo_ref[0:N_HEADS * V_DIM, :]) + _dot(fm, wo_ref[N_HEADS * V_DIM:, :])
    x = _layer_norm(ALPHA * x_ref[...] + gate * y, g_ref[...], b_ref[...])
    xo_ref[...] = x
    h = x * (1.0 + scale2) + shift2
    h_ref[...] = h.reshape((TM,) + ROW)

    h_hi, h_lo = _split_bf16(h)
    logits = _dot(h_hi, rh_ref[...]) + _dot(h_lo, rh_ref[...]) + _dot(h_hi, rl_ref[...])
    lane = lax.broadcasted_iota(jnp.int32, (TM, 128), 1)
    neg = jnp.float32(-jnp.inf)
    logits = jnp.where(lane < N_EXPERTS, logits, neg)
    m1 = jnp.max(logits, axis=-1, keepdims=True)
    i1 = jnp.min(jnp.where(logits == m1, lane, 128), axis=-1, keepdims=True)
    rest = jnp.where(lane == i1, neg, logits)
    m2 = jnp.max(rest, axis=-1, keepdims=True)
    i2 = jnp.min(jnp.where(rest == m2, lane, 128), axis=-1, keepdims=True)
    e2 = jnp.exp(m2 - m1)
    w1 = 1.0 / (1.0 + e2)
    w2 = e2 / (1.0 + e2)
    info = jnp.where(lane == 0, w1, 0.0)
    info = jnp.where(lane == 1, w2, info)
    info = jnp.where(lane == 2, i1.astype(F32), info)
    info = jnp.where(lane == 3, i2.astype(F32), info)
    info_ref[...] = info


def _odd_merge(x, attn, f, mods, fnet_w, w_out, g, b, r_hi, r_lo):
    row = lambda i: (i, 0)
    return pl.pallas_call(
        _odd_merge_kernel,
        grid=(N_TILES,),
        in_specs=[
            pl.BlockSpec((TM, D), row),
            pl.BlockSpec((TM, N_HEADS * V_DIM), row),
            pl.BlockSpec((TM, FNET_DIM), row),
            pl.BlockSpec((None, 6, D), lambda i: (_cond_index(i), 0, 0)),
            _const_spec((FNET_DIM, FNET_DIM)),
            _const_spec((N_HEADS * V_DIM + FNET_DIM, D)),
            _const_spec((1, D)),
            _const_spec((1, D)),
            _const_spec((D, 128)),
            _const_spec((D, 128)),
        ],
        out_specs=[pl.BlockSpec((TM, D), row), pl.BlockSpec((TM,) + ROW, lambda i: (i, 0, 0)),
                   pl.BlockSpec((TM, 128), row)],
        out_shape=[jax.ShapeDtypeStruct((T, D), F32), jax.ShapeDtypeStruct((T,) + ROW, F32),
                   jax.ShapeDtypeStruct((T, 128), F32)],
        compiler_params=_params(1),
        name="odd_merge_router",
    )(x, attn, f, mods, fnet_w, w_out, g, b, r_hi, r_lo)


TM_D = 512
ROW = (8, 128)
DMA_UNROLL = 8


def _row_copy(src, s, dst, d, sem):
    return pltpu.make_async_copy(src.at[s], dst.at[d], sem)


def _dispatch_kernel(dest_ref, pad_lo_ref, pad_hi_ref, h_ref, xs_ref, zero_ref, sem):
    i = pl.program_id(0)
    base = i * TM_D

    def issue(r, carry):
        t = base + r
        _row_copy(h_ref, r, xs_ref, dest_ref[t], sem).start()
        _row_copy(h_ref, r, xs_ref, dest_ref[T + t], sem).start()
        return carry

    lax.fori_loop(0, TM_D, issue, 0, unroll=DMA_UNROLL)

    def drain(r, carry):
        _row_copy(h_ref, 0, xs_ref, 0, sem).wait()
        _row_copy(h_ref, 0, xs_ref, 0, sem).wait()
        return carry

    lax.fori_loop(0, TM_D, drain, 0, unroll=DMA_UNROLL)

    @pl.when(i == 0)
    def _():
        zero_ref[...] = jnp.zeros_like(zero_ref)
        for e in range(N_EXPERTS + 1):
            def zissue(r, carry):
                _row_copy(zero_ref, 0, xs_ref, r, sem).start()
                return carry

            def zdrain(r, carry):
                _row_copy(zero_ref, 0, xs_ref, 0, sem).wait()
                return carry

            lax.fori_loop(pad_lo_ref[e], pad_hi_ref[e], zissue, 0)
            lax.fori_loop(pad_lo_ref[e], pad_hi_ref[e], zdrain, 0)


def _dispatch(dest, pad_lo, pad_hi, h3):
    return pl.pallas_call(
        _dispatch_kernel,
        grid_spec=pltpu.PrefetchScalarGridSpec(
            num_scalar_prefetch=3,
            grid=(T // TM_D,),
            in_specs=[pl.BlockSpec((TM_D,) + ROW, lambda i, *_: (i, 0, 0))],
            out_specs=pl.BlockSpec(memory_space=pl.ANY),
            scratch_shapes=[pltpu.VMEM((1,) + ROW, F32), pltpu.SemaphoreType.DMA(())],
        ),
        out_shape=jax.ShapeDtypeStruct((R_MAX,) + ROW, F32),
        compiler_params=pltpu.CompilerParams(dimension_semantics=("arbitrary",), has_side_effects=True),
        name="expert_dispatch",
    )(dest, pad_lo, pad_hi, h3)


def _expert_kernel(te_ref, nt_ref, xs_ref, wg_ref, wu_ref, wd_ref, o_ref):
    @pl.when(pl.program_id(0) < nt_ref[0])
    def _():
        h = xs_ref[...].reshape(TM_E, D).astype(BF16)
        a = _silu(_dot(h, wg_ref[...])) * _dot(h, wu_ref[...])
        o_ref[...] = _dot(a.astype(BF16), wd_ref[...]).reshape((TM_E,) + ROW)

    @pl.when(pl.program_id(0) >= nt_ref[0])
    def _():
        o_ref[...] = jnp.zeros_like(o_ref)


def _experts(tile_expert, n_used, xs, wg, wu, wd):
    return pl.pallas_call(
        _expert_kernel,
        grid_spec=pltpu.PrefetchScalarGridSpec(
            num_scalar_prefetch=2,
            grid=(N_ETILES,),
            in_specs=[
                pl.BlockSpec((TM_E,) + ROW, lambda i, te, nt: (jnp.minimum(i, nt[0] - 1), 0, 0)),
                pl.BlockSpec((None, D, D_FF_EXPERT), lambda i, te, nt: (te[i], 0, 0)),
                pl.BlockSpec((None, D, D_FF_EXPERT), lambda i, te, nt: (te[i], 0, 0)),
                pl.BlockSpec((None, D_FF_EXPERT, D), lambda i, te, nt: (te[i], 0, 0)),
            ],
            out_specs=pl.BlockSpec((TM_E,) + ROW, lambda i, te, nt: (i, 0, 0)),
        ),
        out_shape=jax.ShapeDtypeStruct((R_MAX,) + ROW, F32),
        compiler_params=_params(1),
        name="expert_swiglu",
    )(tile_expert, n_used, xs, wg, wu, wd)


def _combine_kernel(dest_ref, x_ref, info_ref, mod_ref, g_ref, b_ref, ys_ref, op_ref, os_ref, buf0, buf1, sem):
    base = pl.program_id(0) * TM

    def issue(r, carry):
        t = base + r
        _row_copy(ys_ref, dest_ref[t], buf0, r, sem).start()
        _row_copy(ys_ref, dest_ref[T + t], buf1, r, sem).start()
        return carry

    lax.fori_loop(0, TM, issue, 0, unroll=DMA_UNROLL)

    def drain(r, carry):
        _row_copy(ys_ref, 0, buf0, 0, sem).wait()
        _row_copy(ys_ref, 0, buf1, 0, sem).wait()
        return carry

    lax.fori_loop(0, TM, drain, 0, unroll=DMA_UNROLL)

    gate = mod_ref[5:6, :]
    w1, w2 = info_ref[:, 0:1], info_ref[:, 1:2]
    y = w1 * buf0[...].reshape(TM, D) + w2 * buf1[...].reshape(TM, D)
    out = _layer_norm(ALPHA * x_ref[...] + gate * y, g_ref[...], b_ref[...])

    @pl.when(pl.program_id(0) < NP_TILES)
    def _():
        op_ref[...] = out

    @pl.when(pl.program_id(0) >= NP_TILES)
    def _():
        os_ref[...] = out


def _combine(dest, x, info, mods, g, b, ys):
    return pl.pallas_call(
        _combine_kernel,
        grid_spec=pltpu.PrefetchScalarGridSpec(
            num_scalar_prefetch=1,
            grid=(N_TILES,),
            in_specs=[
                pl.BlockSpec((TM, D), lambda i, d: (i, 0)),
                pl.BlockSpec((TM, 128), lambda i, d: (i, 0)),
                pl.BlockSpec((None, 6, D), lambda i, d: (_cond_index(i), 0, 0)),
                pl.BlockSpec((1, D), lambda i, d: (0, 0)),
                pl.BlockSpec((1, D), lambda i, d: (0, 0)),
                pl.BlockSpec(memory_space=pl.ANY),
            ],
            out_specs=[pl.BlockSpec((TM, D), lambda i, d: (jnp.minimum(i, NP_TILES - 1), 0)),
                       pl.BlockSpec((TM, D), lambda i, d: (jnp.maximum(i - NP_TILES, 0), 0))],
            scratch_shapes=[pltpu.VMEM((TM,) + ROW, F32), pltpu.VMEM((TM,) + ROW, F32),
                            pltpu.SemaphoreType.DMA(())],
        ),
        out_shape=[jax.ShapeDtypeStruct((T_P, D), F32), jax.ShapeDtypeStruct((T_S, D), F32)],
        compiler_params=_params(1),
        name="expert_combine",
    )(dest, x, info, mods, g, b, ys)


def _rot_cols(w):
    w4 = w.reshape(w.shape[:-1] + (2, 2, QK_ROPE // 4))
    return jnp.stack([-w4[..., 1, :], w4[..., 0, :]], axis=-2).reshape(w.shape)


def _rope_table():
    rows = DEC_SEQ // GRID_W
    row = jnp.repeat(jnp.arange(rows), GRID_W).astype(F32)
    col = jnp.tile(jnp.arange(GRID_W), rows).astype(F32)
    half = QK_ROPE // 2
    inv = ROPE_THETA ** (-jnp.arange(0, half, 2, dtype=F32) / half)
    ar, ac = row[:, None] * inv, col[:, None] * inv
    ang = jnp.concatenate([ar, ar, ac, ac], axis=-1)
    cos = jnp.concatenate([jnp.ones((TM, QK_ROPE), F32), jnp.cos(ang)], axis=0)
    sin = jnp.concatenate([jnp.zeros((TM, QK_ROPE), F32), jnp.sin(ang)], axis=0)
    n = cos.shape[0]
    return jnp.concatenate([cos, jnp.ones((n, 64), F32), sin, jnp.zeros((n, 64), F32)], axis=1)


def _dft_tables(n):
    k = jnp.arange(n, dtype=jnp.int32)
    ang = ((k[:, None] * k[None, :]) % n).astype(F32) * (2.0 * np.pi / n)
    sc = n ** -0.5
    return jnp.cos(ang) * sc, jnp.sin(ang) * sc


def _hi_lo(m):
    hi = m.astype(BF16)
    return hi, (m - hi.astype(F32)).astype(BF16)


def _block_diag4(m):
    return jnp.kron(jnp.eye(4, dtype=m.dtype), m)


def kernel(x_prompt, x_sample, cache_ckv, cache_kpe, c, c_ctx, ada_w, ada_b, ln_g, ln_b, ev_w_in, ev_conv_w, ev_pool_w, ev_pool_scale, ev_w_out, ffn_w_gate, ffn_w_up, ffn_w_down, od_w_in, od_q_norm, od_kv_norm, od_w_q_b, od_w_kv_b, od_fnet_w, od_w_out, moe_router, moe_w_gate, moe_w_up, moe_w_down):
    x = jnp.concatenate([x_prompt.reshape(T_P, D), x_sample.reshape(T_S, D)], axis=0)

    cond8 = jnp.concatenate([c_ctx[None, :], c, jnp.zeros((8 - N_COND, D), F32)], axis=0)
    mods = _modulation(cond8, ada_w, ada_b)[:, :N_COND].reshape(DEPTH, N_COND, 6, D)

    x = _even_mixer(x, mods[0], ev_w_in[0].astype(BF16), ev_conv_w[0], ev_pool_w[0].astype(BF16),
                    ev_pool_scale[0][None, :], ev_w_out[0].astype(BF16), ln_g[0, 0][None, :], ln_b[0, 0][None, :])
    x = _ffn(x, mods[0], ffn_w_gate[0].astype(BF16), ffn_w_up[0].astype(BF16), ffn_w_down[0].astype(BF16),
             ln_g[0, 1][None, :], ln_b[0, 1][None, :])

    w_in = od_w_in[0]
    w_pe = w_in[:, Q_RANK + KV_RANK:Q_RANK + KV_RANK + QK_ROPE]
    zpad = jnp.zeros((D, 64), F32)
    w_in_ext = jnp.concatenate([w_in[:, :Q_RANK + KV_RANK], w_in[:, Q_RANK + KV_RANK + QK_ROPE:],
                                w_pe, zpad, _rot_cols(w_pe), zpad], axis=1).astype(BF16)
    wq = od_w_q_b[0].reshape(Q_RANK, N_HEADS, QK_NOPE + QK_ROPE)
    zq = jnp.zeros((Q_RANK, N_HEADS, 64), F32)
    wqa = jnp.concatenate([wq, zq], axis=-1).reshape(Q_RANK, N_HEADS * QH).astype(BF16)
    wqb = jnp.concatenate([_rot_cols(wq[..., QK_NOPE:]), zq], axis=-1).reshape(Q_RANK, N_HEADS * 128).astype(BF16)
    wkv = od_w_kv_b[0].reshape(KV_RANK, N_HEADS, QK_NOPE + V_DIM)
    w_kv = jnp.concatenate([wkv[..., :QK_NOPE].reshape(KV_RANK, -1), wkv[..., QK_NOPE:].reshape(KV_RANK, -1)],
                           axis=1).astype(BF16)

    avg = _block_diag4(jnp.full((FNET_GROUP_DIM, FNET_GROUP_DIM), 1.0 / FNET_GROUP_DIM, F32)).astype(BF16)
    cc, sc = _dft_tables(FNET_GROUP_DIM)
    dft_c = jnp.concatenate([_block_diag4(cc), -_block_diag4(sc)], axis=1)
    dft_hi, dft_lo = _hi_lo(dft_c)

    q, ckv, kpe, yh, yl = _odd_proj(x, mods[1], _rope_table(), w_in_ext, od_q_norm[0][None, :],
                                    od_kv_norm[0][None, :], wqa, wqb, avg, dft_hi, dft_lo)

    cache_kpe128 = jnp.pad(cache_kpe[:, 0], ((0, 0), (0, 0), (0, 128 - QK_ROPE)))
    ckv_parts, kpe_parts = [], []
    for bi in range(DEC_BATCH):
        lat = slice(T_P + bi * DEC_SEQ, T_P + (bi + 1) * DEC_SEQ)
        ckv_parts += [cache_ckv[bi, 0], ckv[lat]]
        kpe_parts += [cache_kpe128[bi], kpe[lat]]
    ckv_all = jnp.concatenate(ckv_parts + [ckv[:T_P]], axis=0)
    kpe_all = jnp.concatenate(kpe_parts + [kpe[:T_P]], axis=0)
    kv = _kv_expand(ckv_all, w_kv)
    attn = _attention(q, kv, kpe_all)

    tabs_p = sum((_hi_lo(m) for m in _dft_tables(SEQ)), ())
    k_s = jnp.arange(DEC_SEQ, dtype=jnp.int32)
    ang = lambda rows: ((rows[:, None] * k_s[None, :]) % DEC_SEQ).astype(F32) * (2.0 * np.pi / DEC_SEQ)
    a_base = ang(jnp.arange(TM, dtype=jnp.int32))
    a_step = ang(jnp.arange(TPS, dtype=jnp.int32) * TM)
    base_s = (jnp.cos(a_base) * DEC_SEQ ** -0.5, jnp.sin(a_base) * DEC_SEQ ** -0.5)
    step_s = (jnp.cos(a_step), jnp.sin(a_step))
    f = _pos_dft(yh, yl, tabs_p, base_s, step_s)

    router = jnp.pad(moe_router[0], ((0, 0), (0, 128 - N_EXPERTS)))
    r_hi, r_lo = _hi_lo(router)
    x, h2, info = _odd_merge(x, attn, f, mods[1], od_fnet_w[0].astype(BF16), od_w_out[0].astype(BF16),
                             ln_g[1, 0][None, :], ln_b[1, 0][None, :], r_hi, r_lo)

    e_flat = info[:, 2:4].astype(jnp.int32).T.reshape(-1)
    onehot = (e_flat[:, None] == jnp.arange(N_EXPERTS, dtype=jnp.int32)[None, :]).astype(jnp.int32)
    csum = jnp.cumsum(onehot, axis=0)
    counts = csum[-1]
    padded = ((counts + TM_E - 1) // TM_E) * TM_E
    g_end = jnp.cumsum(padded)
    g_start = g_end - padded
    dest = jnp.sum(onehot * (csum - 1 + g_start[None, :]), axis=1).astype(jnp.int32)
    tile_row = jnp.arange(N_ETILES, dtype=jnp.int32) * TM_E
    tile_expert = jnp.minimum(jnp.sum((tile_row[:, None] >= g_end[None, :]).astype(jnp.int32), axis=1),
                              N_EXPERTS - 1).astype(jnp.int32)
    n_used = (g_end[-1:] // TM_E).astype(jnp.int32)

    pad_lo = jnp.concatenate([g_start + counts, g_end[-1:]]).astype(jnp.int32)
    pad_hi = jnp.concatenate([g_end, jnp.full((1,), R_MAX, jnp.int32)]).astype(jnp.int32)
    xs = _dispatch(dest, pad_lo, pad_hi, h2)
    ys = _experts(tile_expert, n_used, xs, moe_w_gate[0].astype(BF16), moe_w_up[0].astype(BF16),
                  moe_w_down[0].astype(BF16))
    yp, ysm = _combine(dest, x, info, mods[1], ln_g[1, 1][None, :], ln_b[1, 1][None, :], ys)

    y_prompt = yp.reshape(BATCH, SEQ, D)
    y_sample = ysm.reshape(DEC_BATCH, DEC_SEQ, D)
    new_ckv = ckv[:T_P].reshape(BATCH, 1, SEQ, KV_RANK)
    new_kpe = kpe[:T_P, :QK_ROPE].reshape(BATCH, 1, SEQ, QK_ROPE)
    return (y_prompt, y_sample, new_ckv, new_kpe)
```

```python
import functools

import numpy as np
import jax
import jax.numpy as jnp
from jax import lax
from jax.experimental import pallas as pl
from jax.experimental.pallas import tpu as pltpu

F32 = jnp.float32
BF16 = jnp.bfloat16

D = 1024
BATCH, SEQ = 32, 256
DEC_BATCH, DEC_SEQ = 2, 2048
PAST = 512
GRID_W = 64
T_P = BATCH * SEQ
T_S = DEC_BATCH * DEC_SEQ
T = T_P + T_S
N_COND = 1 + DEC_BATCH

CONV_DIM = 512
POOL_WINDOWS = (2, 4, 8, 16)
POOL_GROUP = 128
N_HEADS = 8
QK_NOPE, QK_ROPE, V_DIM = 128, 64, 128
Q_RANK, KV_RANK = 384, 256
FNET_DIM, FNET_GROUP_DIM = 256, 64
D_FF = 2816
N_EXPERTS = 8
D_FF_EXPERT = 1792
DEPTH = 2
ALPHA = (2 * DEPTH) ** 0.25
LN_EPS = 1e-5
RMS_EPS = 1e-6
ROPE_THETA = 10000.0

TM = 256
NP_TILES = T_P // TM
TPS = DEC_SEQ // TM
N_TILES = T // TM
HALO = 8
TM_FFN = 512
TM_KV = 512
LK_S = PAST + DEC_SEQ
KV_ROWS = DEC_BATCH * LK_S + T_P
TM_E = 256
R_MAX = 2 * T + N_EXPERTS * TM_E
N_ETILES = R_MAX // TM_E
VMEM_LIMIT = 56 * 1024 * 1024


def _cond_index(i, tm=TM):
    return jnp.where(i < T_P // tm, 0, 1 + (i - T_P // tm) // (DEC_SEQ // tm))


def _const_spec(shape):
    nd = len(shape)
    return pl.BlockSpec(shape, lambda *_: (0,) * nd, pipeline_mode=pl.Buffered(1))


def _params(n_axes=1, vmem=VMEM_LIMIT):
    return pltpu.CompilerParams(dimension_semantics=("arbitrary",) * n_axes, vmem_limit_bytes=vmem)


def _layer_norm(v, g, b):
    mu = jnp.mean(v, axis=-1, keepdims=True)
    d = v - mu
    var = jnp.mean(d * d, axis=-1, keepdims=True)
    return d * lax.rsqrt(var + LN_EPS) * g + b


def _split_bf16(v):
    hi = v.astype(BF16)
    lo = (v - hi.astype(F32)).astype(BF16)
    return hi, lo


def _dot(a, b):
    return jnp.dot(a, b, preferred_element_type=F32)


def _silu(v):
    return v / (1.0 + jnp.exp(-v))


def _mod_kernel(cond_ref, w_ref, b_ref, o_ref):
    s = _silu(cond_ref[...]).astype(BF16)
    o_ref[...] = _dot(s, w_ref[...].astype(BF16)) + b_ref[...]


def _modulation(cond8, ada_w, ada_b):
    nb = 6 * D // 1024
    return pl.pallas_call(
        _mod_kernel,
        grid=(DEPTH, nb),
        in_specs=[
            pl.BlockSpec((8, D), lambda l, j: (0, 0)),
            pl.BlockSpec((None, D, 1024), lambda l, j: (l, 0, j)),
            pl.BlockSpec((None, 1, 1024), lambda l, j: (l, 0, j)),
        ],
        out_specs=pl.BlockSpec((None, 8, 1024), lambda l, j: (l, 0, j)),
        out_shape=jax.ShapeDtypeStruct((DEPTH, 8, 6 * D), F32),
        compiler_params=_params(2),
        name="adaln_modulation",
    )(cond8, ada_w, ada_b.reshape(DEPTH, 1, 6 * D))


def _even_mixer_kernel(xp_ref, xc_ref, xn_ref, mod_ref, win_ref, convw_ref, poolw_ref,
                       pscale_ref, wout_ref, g_ref, b_ref, o_ref):
    i = pl.program_id(0)
    j = i - NP_TILES
    is_latent = i >= NP_TILES
    left_ok = jnp.logical_and(is_latent, lax.rem(j, TPS) != 0)
    right_ok = jnp.logical_and(is_latent, lax.rem(j, TPS) != TPS - 1)
    row_lo = jnp.where(left_ok, 0, HALO)
    row_hi = jnp.where(right_ok, TM + 2 * HALO, TM + HALO)

    shift, scale, gate = mod_ref[0:1, :], mod_ref[1:2, :], mod_ref[2:3, :]
    xc = xc_ref[...]
    x_all = jnp.concatenate([xp_ref[...], xc, xn_ref[...]], axis=0)
    h = (x_all * (1.0 + scale) + shift).astype(BF16)
    u = _dot(h, win_ref[...])
    rows = lax.broadcasted_iota(jnp.int32, (TM + 2 * HALO, 1), 0)
    in_seq = jnp.logical_and(rows >= row_lo, rows < row_hi)
    u = jnp.where(in_seq, u, 0.0)

    ux, ub = u[:, 0:CONV_DIM], u[:, CONV_DIM:2 * CONV_DIM]
    uc, up = u[:, 2 * CONV_DIM:3 * CONV_DIM], u[:, 3 * CONV_DIM:]
    z = uc * ux
    conv = (z[HALO - 1:HALO - 1 + TM] * convw_ref[0:1, :]
            + z[HALO:HALO + TM] * convw_ref[1:2, :]
            + z[HALO + 1:HALO + 1 + TM] * convw_ref[2:3, :])
    ya = ub[HALO:HALO + TM] * conv

    t_idx = lax.broadcasted_iota(jnp.int32, (TM, TM + 2 * HALO), 0)
    r_idx = lax.broadcasted_iota(jnp.int32, (TM, TM + 2 * HALO), 1)
    col_ok = jnp.logical_and(r_idx >= row_lo, r_idx < row_hi)
    pos = r_idx - HALO
    t_col = lax.broadcasted_iota(jnp.int32, (TM, 1), 0)
    first = jnp.where(left_ok, -HALO, 0)
    last = jnp.where(right_ok, TM + HALO, TM)
    up_hi, up_lo = _split_bf16(up)
    yb_groups = []
    for gi, w in enumerate(POOL_WINDOWS):
        band = jnp.logical_and(jnp.logical_and(pos >= t_idx - w // 2, pos < t_idx + w // 2), col_ok)
        band = jnp.where(band, 1.0, 0.0).astype(BF16)
        sl = slice(gi * POOL_GROUP, (gi + 1) * POOL_GROUP)
        tot = _dot(band, up_hi[:, sl]) + _dot(band, up_lo[:, sl])
        cnt = (jnp.minimum(t_col + w // 2, last) - jnp.maximum(t_col - w // 2, first)).astype(F32)
        p = tot / cnt - up[HALO:HALO + TM, sl]
        yb_groups.append(_dot(p.astype(BF16), poolw_ref[gi]))
    yb = jnp.concatenate(yb_groups, axis=1) * pscale_ref[...]

    mix = jnp.concatenate([ya, yb], axis=1).astype(BF16)
    y = _dot(mix, wout_ref[...])
    o_ref[...] = _layer_norm(ALPHA * xc + gate * y, g_ref[...], b_ref[...])


def _even_mixer(x, mods, w_in, conv_w, pool_w, pool_scale, w_out, g, b):
    hb = TM // HALO
    n8 = T // HALO
    return pl.pallas_call(
        _even_mixer_kernel,
        grid=(N_TILES,),
        in_specs=[
            pl.BlockSpec((HALO, D), lambda i: (jnp.maximum(i * hb - 1, 0), 0)),
            pl.BlockSpec((TM, D), lambda i: (i, 0)),
            pl.BlockSpec((HALO, D), lambda i: (jnp.minimum((i + 1) * hb, n8 - 1), 0)),
            pl.BlockSpec((None, 6, D), lambda i: (_cond_index(i), 0, 0)),
            _const_spec((D, 4 * CONV_DIM)),
            _const_spec((3, CONV_DIM)),
            _const_spec((4, POOL_GROUP, POOL_GROUP)),
            _const_spec((1, 4 * POOL_GROUP)),
            _const_spec((D, D)),
            _const_spec((1, D)),
            _const_spec((1, D)),
        ],
        out_specs=pl.BlockSpec((TM, D), lambda i: (i, 0)),
        out_shape=jax.ShapeDtypeStruct((T, D), F32),
        compiler_params=_params(1),
        name="even_mixer",
    )(x, x, x, mods, w_in, conv_w, pool_w, pool_scale, w_out, g, b)


FF_CHUNK = D_FF // 2


def _ffn_kernel(x_ref, mod_ref, wg_ref, wu_ref, wd_ref, g_ref, b_ref, o_ref):
    shift, scale, gate = mod_ref[3:4, :], mod_ref[4:5, :], mod_ref[5:6, :]
    x = x_ref[...]
    h = (x * (1.0 + scale) + shift).astype(BF16)
    f = jnp.zeros((TM_FFN, D), F32)
    for c in range(D_FF // FF_CHUNK):
        sl = slice(c * FF_CHUNK, (c + 1) * FF_CHUNK)
        a = _silu(_dot(h, wg_ref[:, sl])) * _dot(h, wu_ref[:, sl])
        f = f + _dot(a.astype(BF16), wd_ref[sl, :])
    o_ref[...] = _layer_norm(ALPHA * x + gate * f, g_ref[...], b_ref[...])


def _ffn(x, mods, wg, wu, wd, g, b):
    return pl.pallas_call(
        _ffn_kernel,
        grid=(T // TM_FFN,),
        in_specs=[
            pl.BlockSpec((TM_FFN, D), lambda i: (i, 0)),
            pl.BlockSpec((None, 6, D), lambda i: (_cond_index(i, TM_FFN), 0, 0)),
            _const_spec((D, D_FF)),
            _const_spec((D, D_FF)),
            _const_spec((D_FF, D)),
            _const_spec((1, D)),
            _const_spec((1, D)),
        ],
        out_specs=pl.BlockSpec((TM_FFN, D), lambda i: (i, 0)),
        out_shape=jax.ShapeDtypeStruct((T, D), F32),
        compiler_params=_params(1),
        name="dense_swiglu",
    )(x, mods, wg, wu, wd, g, b)


W_IN_EXT = Q_RANK + KV_RANK + FNET_DIM + 128 + 128
QH = 256
ATT_SCALE = (QK_NOPE + QK_ROPE) ** -0.5


def _odd_proj_kernel(x_ref, mod_ref, rope_ref, win_ref, qn_ref, kvn_ref, wqa_ref, wqb_ref,
                     avg_ref, dfth_ref, dftl_ref,
                     q_ref, ckv_ref, kpe_ref, yh_ref, yl_ref):
    shift, scale = mod_ref[0:1, :], mod_ref[1:2, :]
    h = (x_ref[...] * (1.0 + scale) + shift).astype(BF16)
    u = _dot(h, win_ref[...])
    uq = u[:, 0:Q_RANK]
    ukv = u[:, Q_RANK:Q_RANK + KV_RANK]
    uf = u[:, Q_RANK + KV_RANK:Q_RANK + KV_RANK + FNET_DIM]
    o = Q_RANK + KV_RANK + FNET_DIM
    upe, upe_rot = u[:, o:o + 128], u[:, o + 128:o + 256]
    cos, sin = rope_ref[:, 0:128], rope_ref[:, 128:256]

    ckv_ref[...] = ukv * lax.rsqrt(jnp.mean(ukv * ukv, axis=-1, keepdims=True) + RMS_EPS) * kvn_ref[...]
    kpe_ref[...] = upe * cos + upe_rot * sin

    qlat = (uq * lax.rsqrt(jnp.mean(uq * uq, axis=-1, keepdims=True) + RMS_EPS) * qn_ref[...]).astype(BF16)
    qa = _dot(qlat, wqa_ref[...])
    qb = _dot(qlat, wqb_ref[...])
    for hd in range(N_HEADS):
        nope = qa[:, hd * QH:hd * QH + 128]
        pe = qa[:, hd * QH + 128:(hd + 1) * QH] * cos + qb[:, hd * 128:(hd + 1) * 128] * sin
        q_ref[:, hd * QH:hd * QH + 128] = (nope * ATT_SCALE).astype(BF16)
        q_ref[:, hd * QH + 128:(hd + 1) * QH] = (pe * ATT_SCALE).astype(BF16)

    avg = avg_ref[...]
    uf_hi, uf_lo = _split_bf16(uf)
    mu = _dot(uf_hi, avg) + _dot(uf_lo, avg)
    dlt = uf - mu
    sq_hi, sq_lo = _split_bf16(dlt * dlt)
    var = _dot(sq_hi, avg) + _dot(sq_lo, avg)
    xn = dlt * lax.rsqrt(var + LN_EPS)
    xn_hi, xn_lo = _split_bf16(xn)
    y = _dot(xn_hi, dfth_ref[...]) + _dot(xn_lo, dfth_ref[...]) + _dot(xn_hi, dftl_ref[...])
    y_hi, y_lo = _split_bf16(y)
    yh_ref[...] = y_hi
    yl_ref[...] = y_lo


def _odd_proj(x, mods, rope_tab, w_in_ext, q_norm, kv_norm, wqa, wqb, avg, dft_hi, dft_lo):
    def rope_index(i):
        return (jnp.where(i < NP_TILES, 0, 1 + lax.rem(i - NP_TILES, TPS)), 0)

    return pl.pallas_call(
        _odd_proj_kernel,
        grid=(N_TILES,),
        in_specs=[
            pl.BlockSpec((TM, D), lambda i: (i, 0)),
            pl.BlockSpec((None, 6, D), lambda i: (_cond_index(i), 0, 0)),
            pl.BlockSpec((TM, 256), rope_index),
            _const_spec((D, W_IN_EXT)),
            _const_spec((1, Q_RANK)),
            _const_spec((1, KV_RANK)),
            _const_spec((Q_RANK, N_HEADS * QH)),
            _const_spec((Q_RANK, N_HEADS * 128)),
            _const_spec((FNET_DIM, FNET_DIM)),
            _const_spec((FNET_DIM, 2 * FNET_DIM)),
            _const_spec((FNET_DIM, 2 * FNET_DIM)),
        ],
        out_specs=[
            pl.BlockSpec((TM, N_HEADS * QH), lambda i: (i, 0)),
            pl.BlockSpec((TM, KV_RANK), lambda i: (i, 0)),
            pl.BlockSpec((TM, 128), lambda i: (i, 0)),
            pl.BlockSpec((TM, 2 * FNET_DIM), lambda i: (i, 0)),
            pl.BlockSpec((TM, 2 * FNET_DIM), lambda i: (i, 0)),
        ],
        out_shape=[
            jax.ShapeDtypeStruct((T, N_HEADS * QH), BF16),
            jax.ShapeDtypeStruct((T, KV_RANK), F32),
            jax.ShapeDtypeStruct((T, 128), F32),
            jax.ShapeDtypeStruct((T, 2 * FNET_DIM), BF16),
            jax.ShapeDtypeStruct((T, 2 * FNET_DIM), BF16),
        ],
        compiler_params=_params(1),
        name="odd_projections",
    )(x, mods, rope_tab, w_in_ext, q_norm, kv_norm, wqa, wqb, avg, dft_hi, dft_lo)


def _kv_kernel(c_ref, w_ref, o_ref):
    o_ref[...] = _dot(c_ref[...].astype(BF16), w_ref[...]).astype(BF16)


def _kv_expand(ckv_all, w_kv):
    return pl.pallas_call(
        _kv_kernel,
        grid=(KV_ROWS // TM_KV,),
        in_specs=[pl.BlockSpec((TM_KV, KV_RANK), lambda i: (i, 0)),
                  _const_spec((KV_RANK, 2 * N_HEADS * 128))],
        out_specs=pl.BlockSpec((TM_KV, 2 * N_HEADS * 128), lambda i: (i, 0)),
        out_shape=jax.ShapeDtypeStruct((KV_ROWS, 2 * N_HEADS * 128), BF16),
        compiler_params=_params(1),
        name="kv_expand",
    )(ckv_all, w_kv)


def _attn_body(q_ref, kv_ref, kpe_ref, o_ref):
    kpe = kpe_ref[...].astype(BF16)
    for hd in range(N_HEADS):
        qh = q_ref[:, hd * QH:(hd + 1) * QH]
        kh = jnp.concatenate([kv_ref[:, hd * 128:(hd + 1) * 128], kpe], axis=1)
        s = lax.dot_general(qh, kh, (((1,), (1,)), ((), ())), preferred_element_type=F32)
        p = jnp.exp(s - jnp.max(s, axis=-1, keepdims=True))
        den = jnp.sum(p, axis=-1, keepdims=True)
        vh = kv_ref[:, (N_HEADS + hd) * 128:(N_HEADS + hd + 1) * 128]
        o = _dot(p.astype(BF16), vh)
        o_ref[:, hd * 128:(hd + 1) * 128] = (o / den).astype(BF16)


def _attn_kernel(q_ref, kvp_ref, kpep_ref, kvs_ref, kpes_ref, o_ref):
    @pl.when(pl.program_id(0) < NP_TILES)
    def _():
        _attn_body(q_ref, kvp_ref, kpep_ref, o_ref)

    @pl.when(pl.program_id(0) >= NP_TILES)
    def _():
        _attn_body(q_ref, kvs_ref, kpes_ref, o_ref)


def _attention(q, kv, kpe_all):
    kv_p0 = DEC_BATCH * LK_S // SEQ
    ctx_blk = lambda i: (kv_p0 + jnp.minimum(i, NP_TILES - 1), 0)
    lat_blk = lambda i: (jnp.maximum(i - NP_TILES, 0) // TPS, 0)
    return pl.pallas_call(
        _attn_kernel,
        grid=(N_TILES,),
        in_specs=[
            pl.BlockSpec((TM, N_HEADS * QH), lambda i: (i, 0)),
            pl.BlockSpec((SEQ, 2 * N_HEADS * 128), ctx_blk),
            pl.BlockSpec((SEQ, 128), ctx_blk),
            pl.BlockSpec((LK_S, 2 * N_HEADS * 128), lat_blk),
            pl.BlockSpec((LK_S, 128), lat_blk),
        ],
        out_specs=pl.BlockSpec((TM, N_HEADS * V_DIM), lambda i: (i, 0)),
        out_shape=jax.ShapeDtypeStruct((T, N_HEADS * V_DIM), BF16),
        compiler_params=_params(1),
        name="attention",
    )(q, kv, kpe_all, kv, kpe_all)


def _pos_dft_body(yh_ref, yl_ref, ch, cl, sh, sl, o_ref):
    yc_h, ys_h = yh_ref[:, 0:FNET_DIM], yh_ref[:, FNET_DIM:]
    yc_l, ys_l = yl_ref[:, 0:FNET_DIM], yl_ref[:, FNET_DIM:]
    f = (_dot(ch, yc_h) + _dot(ch, yc_l) + _dot(cl, yc_h)
         + _dot(sh, ys_h) + _dot(sh, ys_l) + _dot(sl, ys_h))
    o_ref[...] = f.astype(BF16)


def _pos_dft_kernel(yhp_ref, ylp_ref, chp_ref, clp_ref, shp_ref, slp_ref,
                    yhs_ref, yls_ref, c0_ref, s0_ref, cb_ref, sb_ref, o_ref):
    @pl.when(pl.program_id(0) < NP_TILES)
    def _():
        _pos_dft_body(yhp_ref, ylp_ref, chp_ref[...], clp_ref[...], shp_ref[...], slp_ref[...], o_ref)

    @pl.when(pl.program_id(0) >= NP_TILES)
    def _():
        j = lax.rem(pl.program_id(0) - NP_TILES, TPS)
        cb, sb = cb_ref[pl.ds(j, 1), :], sb_ref[pl.ds(j, 1), :]
        c0, s0 = c0_ref[...], s0_ref[...]
        ch, cl = _split_bf16(c0 * cb - s0 * sb)
        sh, sl = _split_bf16(s0 * cb + c0 * sb)
        _pos_dft_body(yhs_ref, yls_ref, ch, cl, sh, sl, o_ref)


def _pos_dft(yh, yl, tabs_p, base_s, step_s):
    ctx_blk = lambda i: (jnp.minimum(i, NP_TILES - 1), 0)
    lat_seq = lambda i: (T_P // DEC_SEQ + jnp.maximum(i - NP_TILES, 0) // TPS, 0)
    return pl.pallas_call(
        _pos_dft_kernel,
        grid=(N_TILES,),
        in_specs=[pl.BlockSpec((SEQ, 2 * FNET_DIM), ctx_blk)] * 2
        + [_const_spec((SEQ, SEQ))] * 4
        + [pl.BlockSpec((DEC_SEQ, 2 * FNET_DIM), lat_seq)] * 2
        + [_const_spec((TM, DEC_SEQ))] * 2
        + [_const_spec((TPS, DEC_SEQ))] * 2,
        out_specs=pl.BlockSpec((TM, FNET_DIM), lambda i: (i, 0)),
        out_shape=jax.ShapeDtypeStruct((T, FNET_DIM), BF16),
        compiler_params=_params(1),
        name="pos_dft",
    )(yh, yl, *tabs_p, yh, yl, *base_s, *step_s)


def _odd_merge_kernel(x_ref, attn_ref, f_ref, mod_ref, fw_ref, wo_ref, g_ref, b_ref, rh_ref, rl_ref, tri_ref,
                      xo_ref, info_ref, cnt_ref, carry_ref):
    @pl.when(pl.program_id(0) == 0)
    def _():
        carry_ref[...] = jnp.zeros_like(carry_ref)

    gate = mod_ref[2:3, :]
    shift2, scale2 = mod_ref[3:4, :], mod_ref[4:5, :]
    fm = _dot(f_ref[...], fw_ref[...]).astype(BF16)
    y = _dot(attn_ref[...], wo_ref[0:N_HEADS * V_DIM, :]) + _dot(fm, wo_ref[N_HEADS * V_DIM:, :])
    x = _layer_norm(ALPHA * x_ref[...] + gate * y, g_ref[...], b_ref[...])
    xo_ref[...] = x
    h = x * (1.0 + scale2) + shift2

    h_hi, h_lo = _split_bf16(h)
    logits = _dot(h_hi, rh_ref[...]) + _dot(h_lo, rh_ref[...]) + _dot(h_hi, rl_ref[...])
    lane = lax.broadcasted_iota(jnp.int32, (TM, 128), 1)
    neg = jnp.float32(-jnp.inf)
    logits = jnp.where(lane < N_EXPERTS, logits, neg)
    m1 = jnp.max(logits, axis=-1, keepdims=True)
    i1 = jnp.min(jnp.where(logits == m1, lane, 128), axis=-1, keepdims=True)
    rest = jnp.where(lane == i1, neg, logits)
    m2 = jnp.max(rest, axis=-1, keepdims=True)
    i2 = jnp.min(jnp.where(rest == m2, lane, 128), axis=-1, keepdims=True)
    e2 = jnp.exp(m2 - m1)
    w1 = 1.0 / (1.0 + e2)
    w2 = e2 / (1.0 + e2)
    info = jnp.where(lane == 0, w1, 0.0)
    info = jnp.where(lane == 1, w2, info)
    info = jnp.where(lane == 2, i1.astype(F32), info)
    info = jnp.where(lane == 3, i2.astype(F32), info)

    uses = jnp.logical_or(lane == i1, lane == i2)
    seen = _dot(tri_ref[...], jnp.where(uses, 1.0, 0.0).astype(BF16)) + carry_ref[...]
    r1 = jnp.sum(jnp.where(lane == i1, seen, 0.0), axis=-1, keepdims=True)
    r2 = jnp.sum(jnp.where(lane == i2, seen, 0.0), axis=-1, keepdims=True)
    info = jnp.where(lane == 4, r1, info)
    info = jnp.where(lane == 5, r2, info)
    info_ref[...] = info
    total = carry_ref[...] + jnp.sum(jnp.where(uses, 1.0, 0.0), axis=0, keepdims=True)
    carry_ref[...] = total
    cnt_ref[...] = jnp.broadcast_to(total, cnt_ref.shape)


def _odd_merge(x, attn, f, mods, fnet_w, w_out, g, b, r_hi, r_lo):
    row = lambda i: (i, 0)
    tri = jnp.tril(jnp.ones((TM, TM), F32), -1).astype(BF16)
    return pl.pallas_call(
        _odd_merge_kernel,
        grid=(N_TILES,),
        in_specs=[
            pl.BlockSpec((TM, D), row),
            pl.BlockSpec((TM, N_HEADS * V_DIM), row),
            pl.BlockSpec((TM, FNET_DIM), row),
            pl.BlockSpec((None, 6, D), lambda i: (_cond_index(i), 0, 0)),
            _const_spec((FNET_DIM, FNET_DIM)),
            _const_spec((N_HEADS * V_DIM + FNET_DIM, D)),
            _const_spec((1, D)),
            _const_spec((1, D)),
            _const_spec((D, 128)),
            _const_spec((D, 128)),
            _const_spec((TM, TM)),
        ],
        out_specs=[pl.BlockSpec((TM, D), row), pl.BlockSpec((TM, 128), row),
                   pl.BlockSpec((8, 128), lambda i: (0, 0))],
        out_shape=[jax.ShapeDtypeStruct((T, D), F32), jax.ShapeDtypeStruct((T, 128), F32),
                   jax.ShapeDtypeStruct((8, 128), F32)],
        scratch_shapes=[pltpu.VMEM((1, 128), F32)],
        compiler_params=_params(1),
        name="odd_merge_router",
    )(x, attn, f, mods, fnet_w, w_out, g, b, r_hi, r_lo, tri)


TM_D = 512
ROW = (8, 128)
DMA_UNROLL = 8


def _row_copy(src, s, dst, d, sem):
    return pltpu.make_async_copy(src.at[s], dst.at[d], sem)


def _dispatch_kernel(dest_ref, pad_lo_ref, pad_hi_ref, x_ref, mod_ref, xs_ref, h_ref, zero_ref, sem):
    i = pl.program_id(0)
    base = i * TM_D
    shift2, scale2 = mod_ref[3:4, :], mod_ref[4:5, :]
    h_ref[...] = (x_ref[...] * (1.0 + scale2) + shift2).reshape((TM_D,) + ROW)

    def issue(r, carry):
        t = base + r
        _row_copy(h_ref, r, xs_ref, dest_ref[t], sem).start(priority=0)
        _row_copy(h_ref, r, xs_ref, dest_ref[T + t], sem).start(priority=1)
        return carry

    lax.fori_loop(0, TM_D, issue, 0, unroll=DMA_UNROLL)

    def drain(r, carry):
        _row_copy(h_ref, 0, xs_ref, 0, sem).wait()
        _row_copy(h_ref, 0, xs_ref, 0, sem).wait()
        return carry

    lax.fori_loop(0, TM_D, drain, 0, unroll=DMA_UNROLL)

    @pl.when(i == 0)
    def _():
        zero_ref[...] = jnp.zeros_like(zero_ref)
        for e in range(N_EXPERTS + 1):
            def zissue(r, carry):
                _row_copy(zero_ref, 0, xs_ref, r, sem).start()
                return carry

            def zdrain(r, carry):
                _row_copy(zero_ref, 0, xs_ref, 0, sem).wait()
                return carry

            lax.fori_loop(pad_lo_ref[e], pad_hi_ref[e], zissue, 0)
            lax.fori_loop(pad_lo_ref[e], pad_hi_ref[e], zdrain, 0)


def _dispatch(dest, pad_lo, pad_hi, x, mods):
    return pl.pallas_call(
        _dispatch_kernel,
        grid_spec=pltpu.PrefetchScalarGridSpec(
            num_scalar_prefetch=3,
            grid=(T // TM_D,),
            in_specs=[pl.BlockSpec((TM_D, D), lambda i, *_: (i, 0)),
                      pl.BlockSpec((None, 6, D), lambda i, *_: (_cond_index(i, TM_D), 0, 0))],
            out_specs=pl.BlockSpec(memory_space=pl.ANY),
            scratch_shapes=[pltpu.VMEM((TM_D,) + ROW, F32), pltpu.VMEM((1,) + ROW, F32),
                            pltpu.SemaphoreType.DMA(())],
        ),
        out_shape=jax.ShapeDtypeStruct((R_MAX,) + ROW, F32),
        compiler_params=pltpu.CompilerParams(dimension_semantics=("arbitrary",), has_side_effects=True),
        name="expert_dispatch",
    )(dest, pad_lo, pad_hi, x, mods)


def _expert_kernel(te_ref, nt_ref, xs_ref, wg_ref, wu_ref, wd_ref, o_ref):
    @pl.when(pl.program_id(0) < nt_ref[0])
    def _():
        h = xs_ref[...].reshape(TM_E, D).astype(BF16)
        a = _silu(_dot(h, wg_ref[...])) * _dot(h, wu_ref[...])
        o_ref[...] = _dot(a.astype(BF16), wd_ref[...]).reshape((TM_E,) + ROW)

    @pl.when(pl.program_id(0) >= nt_ref[0])
    def _():
        o_ref[...] = jnp.zeros_like(o_ref)


def _experts(tile_expert, n_used, xs, wg, wu, wd):
    return pl.pallas_call(
        _expert_kernel,
        grid_spec=pltpu.PrefetchScalarGridSpec(
            num_scalar_prefetch=2,
            grid=(N_ETILES,),
            in_specs=[
                pl.BlockSpec((TM_E,) + ROW, lambda i, te, nt: (jnp.minimum(i, nt[0] - 1), 0, 0)),
                pl.BlockSpec((None, D, D_FF_EXPERT), lambda i, te, nt: (te[i], 0, 0)),
                pl.BlockSpec((None, D, D_FF_EXPERT), lambda i, te, nt: (te[i], 0, 0)),
                pl.BlockSpec((None, D_FF_EXPERT, D), lambda i, te, nt: (te[i], 0, 0)),
            ],
            out_specs=pl.BlockSpec((TM_E,) + ROW, lambda i, te, nt: (i, 0, 0)),
        ),
        out_shape=jax.ShapeDtypeStruct((R_MAX,) + ROW, F32),
        compiler_params=_params(1),
        name="expert_swiglu",
    )(tile_expert, n_used, xs, wg, wu, wd)


def _combine_kernel(dest_ref, x_ref, info_ref, mod_ref, g_ref, b_ref, ys_ref, op_ref, os_ref, buf, sem):
    i = pl.program_id(0)
    slot = lax.rem(i, 2)

    def gather(tile, s):
        def issue(r, carry):
            t = tile * TM + r
            _row_copy(ys_ref, dest_ref[t], buf.at[s, 0], r, sem.at[s]).start(priority=0)
            _row_copy(ys_ref, dest_ref[T + t], buf.at[s, 1], r, sem.at[s]).start(priority=1)
            return carry

        lax.fori_loop(0, TM, issue, 0, unroll=DMA_UNROLL)

    @pl.when(i == 0)
    def _():
        gather(0, 0)

    @pl.when(i + 1 < N_TILES)
    def _():
        gather(i + 1, 1 - slot)

    def drain(r, carry):
        _row_copy(ys_ref, 0, buf.at[slot, 0], 0, sem.at[slot]).wait()
        _row_copy(ys_ref, 0, buf.at[slot, 1], 0, sem.at[slot]).wait()
        return carry

    lax.fori_loop(0, TM, drain, 0, unroll=DMA_UNROLL)

    gate = mod_ref[5:6, :]
    w1, w2 = info_ref[:, 0:1], info_ref[:, 1:2]
    y = w1 * buf[slot, 0].reshape(TM, D) + w2 * buf[slot, 1].reshape(TM, D)
    out = _layer_norm(ALPHA * x_ref[...] + gate * y, g_ref[...], b_ref[...])

    @pl.when(pl.program_id(0) < NP_TILES)
    def _():
        op_ref[...] = out

    @pl.when(pl.program_id(0) >= NP_TILES)
    def _():
        os_ref[...] = out


def _combine(dest, x, info, mods, g, b, ys):
    return pl.pallas_call(
        _combine_kernel,
        grid_spec=pltpu.PrefetchScalarGridSpec(
            num_scalar_prefetch=1,
            grid=(N_TILES,),
            in_specs=[
                pl.BlockSpec((TM, D), lambda i, d: (i, 0)),
                pl.BlockSpec((TM, 128), lambda i, d: (i, 0)),
                pl.BlockSpec((None, 6, D), lambda i, d: (_cond_index(i), 0, 0)),
                pl.BlockSpec((1, D), lambda i, d: (0, 0)),
                pl.BlockSpec((1, D), lambda i, d: (0, 0)),
                pl.BlockSpec(memory_space=pl.ANY),
            ],
            out_specs=[pl.BlockSpec((TM, D), lambda i, d: (jnp.minimum(i, NP_TILES - 1), 0)),
                       pl.BlockSpec((TM, D), lambda i, d: (jnp.maximum(i - NP_TILES, 0), 0))],
            scratch_shapes=[pltpu.VMEM((2, 2, TM) + ROW, F32), pltpu.SemaphoreType.DMA((2,))],
        ),
        out_shape=[jax.ShapeDtypeStruct((T_P, D), F32), jax.ShapeDtypeStruct((T_S, D), F32)],
        compiler_params=_params(1),
        name="expert_combine",
    )(dest, x, info, mods, g, b, ys)


def _rot_cols(w):
    w4 = w.reshape(w.shape[:-1] + (2, 2, QK_ROPE // 4))
    return jnp.stack([-w4[..., 1, :], w4[..., 0, :]], axis=-2).reshape(w.shape)


def _rope_table():
    rows = DEC_SEQ // GRID_W
    row = jnp.repeat(jnp.arange(rows), GRID_W).astype(F32)
    col = jnp.tile(jnp.arange(GRID_W), rows).astype(F32)
    half = QK_ROPE // 2
    inv = ROPE_THETA ** (-jnp.arange(0, half, 2, dtype=F32) / half)
    ar, ac = row[:, None] * inv, col[:, None] * inv
    ang = jnp.concatenate([ar, ar, ac, ac], axis=-1)
    cos = jnp.concatenate([jnp.ones((TM, QK_ROPE), F32), jnp.cos(ang)], axis=0)
    sin = jnp.concatenate([jnp.zeros((TM, QK_ROPE), F32), jnp.sin(ang)], axis=0)
    n = cos.shape[0]
    return jnp.concatenate([cos, jnp.ones((n, 64), F32), sin, jnp.zeros((n, 64), F32)], axis=1)


def _dft_tables(n):
    k = jnp.arange(n, dtype=jnp.int32)
    ang = ((k[:, None] * k[None, :]) % n).astype(F32) * (2.0 * np.pi / n)
    sc = n ** -0.5
    return jnp.cos(ang) * sc, jnp.sin(ang) * sc


def _hi_lo(m):
    hi = m.astype(BF16)
    return hi, (m - hi.astype(F32)).astype(BF16)


def _block_diag4(m):
    return jnp.kron(jnp.eye(4, dtype=m.dtype), m)


def kernel(x_prompt, x_sample, cache_ckv, cache_kpe, c, c_ctx, ada_w, ada_b, ln_g, ln_b, ev_w_in, ev_conv_w, ev_pool_w, ev_pool_scale, ev_w_out, ffn_w_gate, ffn_w_up, ffn_w_down, od_w_in, od_q_norm, od_kv_norm, od_w_q_b, od_w_kv_b, od_fnet_w, od_w_out, moe_router, moe_w_gate, moe_w_up, moe_w_down):
    x = jnp.concatenate([x_prompt.reshape(T_P, D), x_sample.reshape(T_S, D)], axis=0)

    cond8 = jnp.concatenate([c_ctx[None, :], c, jnp.zeros((8 - N_COND, D), F32)], axis=0)
    mods = _modulation(cond8, ada_w, ada_b)[:, :N_COND].reshape(DEPTH, N_COND, 6, D)

    x = _even_mixer(x, mods[0], ev_w_in[0].astype(BF16), ev_conv_w[0], ev_pool_w[0].astype(BF16),
                    ev_pool_scale[0][None, :], ev_w_out[0].astype(BF16), ln_g[0, 0][None, :], ln_b[0, 0][None, :])
    x = _ffn(x, mods[0], ffn_w_gate[0].astype(BF16), ffn_w_up[0].astype(BF16), ffn_w_down[0].astype(BF16),
             ln_g[0, 1][None, :], ln_b[0, 1][None, :])

    w_in = od_w_in[0]
    w_pe = w_in[:, Q_RANK + KV_RANK:Q_RANK + KV_RANK + QK_ROPE]
    zpad = jnp.zeros((D, 64), F32)
    w_in_ext = jnp.concatenate([w_in[:, :Q_RANK + KV_RANK], w_in[:, Q_RANK + KV_RANK + QK_ROPE:],
                                w_pe, zpad, _rot_cols(w_pe), zpad], axis=1).astype(BF16)
    wq = od_w_q_b[0].reshape(Q_RANK, N_HEADS, QK_NOPE + QK_ROPE)
    zq = jnp.zeros((Q_RANK, N_HEADS, 64), F32)
    wqa = jnp.concatenate([wq, zq], axis=-1).reshape(Q_RANK, N_HEADS * QH).astype(BF16)
    wqb = jnp.concatenate([_rot_cols(wq[..., QK_NOPE:]), zq], axis=-1).reshape(Q_RANK, N_HEADS * 128).astype(BF16)
    wkv = od_w_kv_b[0].reshape(KV_RANK, N_HEADS, QK_NOPE + V_DIM)
    w_kv = jnp.concatenate([wkv[..., :QK_NOPE].reshape(KV_RANK, -1), wkv[..., QK_NOPE:].reshape(KV_RANK, -1)],
                           axis=1).astype(BF16)

    avg = _block_diag4(jnp.full((FNET_GROUP_DIM, FNET_GROUP_DIM), 1.0 / FNET_GROUP_DIM, F32)).astype(BF16)
    cc, sc = _dft_tables(FNET_GROUP_DIM)
    dft_c = jnp.concatenate([_block_diag4(cc), -_block_diag4(sc)], axis=1)
    dft_hi, dft_lo = _hi_lo(dft_c)

    q, ckv, kpe, yh, yl = _odd_proj(x, mods[1], _rope_table(), w_in_ext, od_q_norm[0][None, :],
                                    od_kv_norm[0][None, :], wqa, wqb, avg, dft_hi, dft_lo)

    cache_kpe128 = jnp.pad(cache_kpe[:, 0], ((0, 0), (0, 0), (0, 128 - QK_ROPE)))
    ckv_parts, kpe_parts = [], []
    for bi in range(DEC_BATCH):
        lat = slice(T_P + bi * DEC_SEQ, T_P + (bi + 1) * DEC_SEQ)
        ckv_parts += [cache_ckv[bi, 0], ckv[lat]]
        kpe_parts += [cache_kpe128[bi], kpe[lat]]
    ckv_all = jnp.concatenate(ckv_parts + [ckv[:T_P]], axis=0)
    kpe_all = jnp.concatenate(kpe_parts + [kpe[:T_P]], axis=0)
    kv = _kv_expand(ckv_all, w_kv)
    attn = _attention(q, kv, kpe_all)

    tabs_p = sum((_hi_lo(m) for m in _dft_tables(SEQ)), ())
    k_s = jnp.arange(DEC_SEQ, dtype=jnp.int32)
    ang = lambda rows: ((rows[:, None] * k_s[None, :]) % DEC_SEQ).astype(F32) * (2.0 * np.pi / DEC_SEQ)
    a_base = ang(jnp.arange(TM, dtype=jnp.int32))
    a_step = ang(jnp.arange(TPS, dtype=jnp.int32) * TM)
    base_s = (jnp.cos(a_base) * DEC_SEQ ** -0.5, jnp.sin(a_base) * DEC_SEQ ** -0.5)
    step_s = (jnp.cos(a_step), jnp.sin(a_step))
    f = _pos_dft(yh, yl, tabs_p, base_s, step_s)

    router = jnp.pad(moe_router[0], ((0, 0), (0, 128 - N_EXPERTS)))
    r_hi, r_lo = _hi_lo(router)
    x, info, cnt = _odd_merge(x, attn, f, mods[1], od_fnet_w[0].astype(BF16), od_w_out[0].astype(BF16),
                              ln_g[1, 0][None, :], ln_b[1, 0][None, :], r_hi, r_lo)

    counts = cnt[0, :N_EXPERTS].astype(jnp.int32)
    padded = ((counts + TM_E - 1) // TM_E) * TM_E
    g_end = jnp.cumsum(padded)
    g_start = g_end - padded
    choice = info[:, 2:4].astype(jnp.int32)
    rank = info[:, 4:6].astype(jnp.int32)
    is_e = choice[..., None] == jnp.arange(N_EXPERTS, dtype=jnp.int32)
    dest = (jnp.sum(jnp.where(is_e, g_start, 0), axis=-1) + rank).T.reshape(-1).astype(jnp.int32)
    tile_row = jnp.arange(N_ETILES, dtype=jnp.int32) * TM_E
    tile_expert = jnp.minimum(jnp.sum((tile_row[:, None] >= g_end[None, :]).astype(jnp.int32), axis=1),
                              N_EXPERTS - 1).astype(jnp.int32)
    n_used = (g_end[-1:] // TM_E).astype(jnp.int32)

    pad_lo = jnp.concatenate([g_start + counts, g_end[-1:]]).astype(jnp.int32)
    pad_hi = jnp.concatenate([g_end, jnp.full((1,), R_MAX, jnp.int32)]).astype(jnp.int32)
    xs = _dispatch(dest, pad_lo, pad_hi, x, mods[1])
    ys = _experts(tile_expert, n_used, xs, moe_w_gate[0].astype(BF16), moe_w_up[0].astype(BF16),
                  moe_w_down[0].astype(BF16))
    yp, ysm = _combine(dest, x, info, mods[1], ln_g[1, 1][None, :], ln_b[1, 1][None, :], ys)

    y_prompt = yp.reshape(BATCH, SEQ, D)
    y_sample = ysm.reshape(DEC_BATCH, DEC_SEQ, D)
    new_ckv = ckv[:T_P].reshape(BATCH, 1, SEQ, KV_RANK)
    new_kpe = kpe[:T_P, :QK_ROPE].reshape(BATCH, 1, SEQ, QK_ROPE)
    return (y_prompt, y_sample, new_ckv, new_kpe)
```

```python
import functools

import numpy as np
import jax
import jax.numpy as jnp
from jax import lax
from jax.experimental import pallas as pl
from jax.experimental.pallas import tpu as pltpu

F32 = jnp.float32
BF16 = jnp.bfloat16

D = 1024
BATCH, SEQ = 32, 256
DEC_BATCH, DEC_SEQ = 2, 2048
PAST = 512
GRID_W = 64
T_P = BATCH * SEQ
T_S = DEC_BATCH * DEC_SEQ
T = T_P + T_S
N_COND = 1 + DEC_BATCH

CONV_DIM = 512
POOL_WINDOWS = (2, 4, 8, 16)
POOL_GROUP = 128
N_HEADS = 8
QK_NOPE, QK_ROPE, V_DIM = 128, 64, 128
Q_RANK, KV_RANK = 384, 256
FNET_DIM, FNET_GROUP_DIM = 256, 64
D_FF = 2816
N_EXPERTS = 8
D_FF_EXPERT = 1792
DEPTH = 2
ALPHA = (2 * DEPTH) ** 0.25
LN_EPS = 1e-5
RMS_EPS = 1e-6
ROPE_THETA = 10000.0

TM = 256
NP_TILES = T_P // TM
TPS = DEC_SEQ // TM
N_TILES = T // TM
HALO = 8
TM_FFN = 512
TM_KV = 512
LK_S = PAST + DEC_SEQ
KV_ROWS = DEC_BATCH * LK_S + T_P
TM_E = 256
R_MAX = 2 * T + N_EXPERTS * TM_E
N_ETILES = R_MAX // TM_E
VMEM_LIMIT = 56 * 1024 * 1024


def _cond_index(i, tm=TM):
    return jnp.where(i < T_P // tm, 0, 1 + (i - T_P // tm) // (DEC_SEQ // tm))


def _const_spec(shape):
    nd = len(shape)
    return pl.BlockSpec(shape, lambda *_: (0,) * nd, pipeline_mode=pl.Buffered(1))


def _params(n_axes=1, vmem=VMEM_LIMIT):
    return pltpu.CompilerParams(dimension_semantics=("arbitrary",) * n_axes, vmem_limit_bytes=vmem)


def _layer_norm(v, g, b):
    mu = jnp.mean(v, axis=-1, keepdims=True)
    d = v - mu
    var = jnp.mean(d * d, axis=-1, keepdims=True)
    return d * lax.rsqrt(var + LN_EPS) * g + b


def _split_bf16(v):
    hi = v.astype(BF16)
    lo = (v - hi.astype(F32)).astype(BF16)
    return hi, lo


def _dot(a, b):
    return jnp.dot(a, b, preferred_element_type=F32)


def _silu(v):
    return v / (1.0 + jnp.exp(-v))


def _mod_kernel(cond_ref, w_ref, b_ref, o_ref):
    s = _silu(cond_ref[...]).astype(BF16)
    o_ref[...] = _dot(s, w_ref[...].astype(BF16)) + b_ref[...]


def _modulation(cond8, ada_w, ada_b):
    nb = 6 * D // 1024
    return pl.pallas_call(
        _mod_kernel,
        grid=(DEPTH, nb),
        in_specs=[
            pl.BlockSpec((8, D), lambda l, j: (0, 0)),
            pl.BlockSpec((None, D, 1024), lambda l, j: (l, 0, j)),
            pl.BlockSpec((None, 1, 1024), lambda l, j: (l, 0, j)),
        ],
        out_specs=pl.BlockSpec((None, 8, 1024), lambda l, j: (l, 0, j)),
        out_shape=jax.ShapeDtypeStruct((DEPTH, 8, 6 * D), F32),
        compiler_params=_params(2),
        name="adaln_modulation",
    )(cond8, ada_w, ada_b.reshape(DEPTH, 1, 6 * D))


CAST_SPECS = (
    (N_EXPERTS * D, D_FF_EXPERT, 32),
    (N_EXPERTS * D, D_FF_EXPERT, 32),
    (N_EXPERTS * D_FF_EXPERT, D, 32),
    (D, D_FF, 32),
    (D, D_FF, 32),
    (D_FF, D, 16),
)
N_CAST = len(CAST_SPECS)


def _even_mixer_kernel(xctx_ref, xprev_ref, xlat_ref, xnext_ref, mod_ref, win_ref, convw_ref, poolw_ref,
                       pscale_ref, wout_ref, g_ref, b_ref, *rest):
    cast_in, o_ref, cast_out, xall_ref = rest[:N_CAST], rest[N_CAST], rest[N_CAST + 1:2 * N_CAST + 1], rest[-1]
    i = pl.program_id(0)
    j = i - NP_TILES
    is_latent = i >= NP_TILES
    left_ok = jnp.logical_and(is_latent, lax.rem(j, TPS) != 0)
    right_ok = jnp.logical_and(is_latent, lax.rem(j, TPS) != TPS - 1)
    row_lo = jnp.where(left_ok, 0, HALO)
    row_hi = jnp.where(right_ok, TM + 2 * HALO, TM + HALO)

    for src, dst, (_, _, n_blocks) in zip(cast_in, cast_out, CAST_SPECS):
        @pl.when(i < n_blocks)
        def _(src=src, dst=dst):
            dst[...] = src[...].astype(BF16)

    @pl.when(jnp.logical_not(is_latent))
    def _():
        xall_ref[0:HALO, :] = jnp.zeros((HALO, D), F32)
        xall_ref[HALO:HALO + TM, :] = xctx_ref[...]
        xall_ref[HALO + TM:, :] = jnp.zeros((HALO, D), F32)

    @pl.when(is_latent)
    def _():
        xall_ref[0:HALO, :] = xprev_ref[...]
        xall_ref[HALO:HALO + TM, :] = xlat_ref[...]
        xall_ref[HALO + TM:, :] = xnext_ref[...]

    shift, scale, gate = mod_ref[0:1, :], mod_ref[1:2, :], mod_ref[2:3, :]
    xc = xall_ref[HALO:HALO + TM, :]
    x_all = xall_ref[...]
    h = (x_all * (1.0 + scale) + shift).astype(BF16)
    u = _dot(h, win_ref[...])
    rows = lax.broadcasted_iota(jnp.int32, (TM + 2 * HALO, 1), 0)
    in_seq = jnp.logical_and(rows >= row_lo, rows < row_hi)
    u = jnp.where(in_seq, u, 0.0)

    ux, ub = u[:, 0:CONV_DIM], u[:, CONV_DIM:2 * CONV_DIM]
    uc, up = u[:, 2 * CONV_DIM:3 * CONV_DIM], u[:, 3 * CONV_DIM:]
    z = uc * ux
    conv = (z[HALO - 1:HALO - 1 + TM] * convw_ref[0:1, :]
            + z[HALO:HALO + TM] * convw_ref[1:2, :]
            + z[HALO + 1:HALO + 1 + TM] * convw_ref[2:3, :])
    ya = ub[HALO:HALO + TM] * conv

    t_idx = lax.broadcasted_iota(jnp.int32, (TM, TM + 2 * HALO), 0)
    r_idx = lax.broadcasted_iota(jnp.int32, (TM, TM + 2 * HALO), 1)
    col_ok = jnp.logical_and(r_idx >= row_lo, r_idx < row_hi)
    pos = r_idx - HALO
    t_col = lax.broadcasted_iota(jnp.int32, (TM, 1), 0)
    first = jnp.where(left_ok, -HALO, 0)
    last = jnp.where(right_ok, TM + HALO, TM)
    up_hi, up_lo = _split_bf16(up)
    yb_groups = []
    for gi, w in enumerate(POOL_WINDOWS):
        band = jnp.logical_and(jnp.logical_and(pos >= t_idx - w // 2, pos < t_idx + w // 2), col_ok)
        band = jnp.where(band, 1.0, 0.0).astype(BF16)
        sl = slice(gi * POOL_GROUP, (gi + 1) * POOL_GROUP)
        tot = _dot(band, up_hi[:, sl]) + _dot(band, up_lo[:, sl])
        cnt = (jnp.minimum(t_col + w // 2, last) - jnp.maximum(t_col - w // 2, first)).astype(F32)
        p = tot / cnt - up[HALO:HALO + TM, sl]
        yb_groups.append(_dot(p.astype(BF16), poolw_ref[gi]))
    yb = jnp.concatenate(yb_groups, axis=1) * pscale_ref[...]

    mix = jnp.concatenate([ya, yb], axis=1).astype(BF16)
    y = _dot(mix, wout_ref[...])
    o_ref[...] = _layer_norm(ALPHA * xc + gate * y, g_ref[...], b_ref[...])


def _even_mixer(x_ctx, x_lat, mods, w_in, conv_w, pool_w, pool_scale, w_out, g, b, cast_srcs):
    hb = TM // HALO
    n8 = T_S // HALO
    lat = lambda i: jnp.maximum(i - NP_TILES, 0)
    cast_specs = [pl.BlockSpec((r // nb, c), lambda i, nb=nb: (jnp.minimum(i, nb - 1), 0))
                  for r, c, nb in CAST_SPECS]
    outs = pl.pallas_call(
        _even_mixer_kernel,
        grid=(N_TILES,),
        in_specs=[
            pl.BlockSpec((TM, D), lambda i: (jnp.minimum(i, NP_TILES - 1), 0)),
            pl.BlockSpec((HALO, D), lambda i: (jnp.maximum(lat(i) * hb - 1, 0), 0)),
            pl.BlockSpec((TM, D), lambda i: (lat(i), 0)),
            pl.BlockSpec((HALO, D), lambda i: (jnp.minimum((lat(i) + 1) * hb, n8 - 1), 0)),
            pl.BlockSpec((None, 6, D), lambda i: (_cond_index(i), 0, 0)),
            _const_spec((D, 4 * CONV_DIM)),
            _const_spec((3, CONV_DIM)),
            _const_spec((4, POOL_GROUP, POOL_GROUP)),
            _const_spec((1, 4 * POOL_GROUP)),
            _const_spec((D, D)),
            _const_spec((1, D)),
            _const_spec((1, D)),
        ] + cast_specs,
        out_specs=[pl.BlockSpec((TM, D), lambda i: (i, 0))] + cast_specs,
        out_shape=[jax.ShapeDtypeStruct((T, D), F32)]
        + [jax.ShapeDtypeStruct((r, c), BF16) for r, c, _ in CAST_SPECS],
        scratch_shapes=[pltpu.VMEM((TM + 2 * HALO, D), F32)],
        compiler_params=_params(1),
        name="even_mixer",
    )(x_ctx, x_lat, x_lat, x_lat, mods, w_in, conv_w, pool_w, pool_scale, w_out, g, b, *cast_srcs)
    return outs[0], outs[1:]


FF_CHUNK = D_FF // 2


def _ffn_kernel(x_ref, mod_ref, wg_ref, wu_ref, wd_ref, g_ref, b_ref, o_ref):
    shift, scale, gate = mod_ref[3:4, :], mod_ref[4:5, :], mod_ref[5:6, :]
    x = x_ref[...]
    h = (x * (1.0 + scale) + shift).astype(BF16)
    f = jnp.zeros((TM_FFN, D), F32)
    for c in range(D_FF // FF_CHUNK):
        sl = slice(c * FF_CHUNK, (c + 1) * FF_CHUNK)
        a = _silu(_dot(h, wg_ref[:, sl])) * _dot(h, wu_ref[:, sl])
        f = f + _dot(a.astype(BF16), wd_ref[sl, :])
    o_ref[...] = _layer_norm(ALPHA * x + gate * f, g_ref[...], b_ref[...])


def _ffn(x, mods, wg, wu, wd, g, b):
    return pl.pallas_call(
        _ffn_kernel,
        grid=(T // TM_FFN,),
        in_specs=[
            pl.BlockSpec((TM_FFN, D), lambda i: (i, 0)),
            pl.BlockSpec((None, 6, D), lambda i: (_cond_index(i, TM_FFN), 0, 0)),
            _const_spec((D, D_FF)),
            _const_spec((D, D_FF)),
            _const_spec((D_FF, D)),
            _const_spec((1, D)),
            _const_spec((1, D)),
        ],
        out_specs=pl.BlockSpec((TM_FFN, D), lambda i: (i, 0)),
        out_shape=jax.ShapeDtypeStruct((T, D), F32),
        compiler_params=_params(1),
        name="dense_swiglu",
    )(x, mods, wg, wu, wd, g, b)


W_IN_EXT = Q_RANK + KV_RANK + FNET_DIM + 128 + 128
QH = 256
ATT_SCALE = (QK_NOPE + QK_ROPE) ** -0.5 * float(np.log2(np.e))


def _odd_proj_kernel(x_ref, mod_ref, rope_ref, win_ref, qn_ref, kvn_ref, wqa_ref, wqb_ref,
                     avg_ref, dfth_ref, dftl_ref,
                     q_ref, ckv_ref, kpe_ref, yh_ref, yl_ref):
    shift, scale = mod_ref[0:1, :], mod_ref[1:2, :]
    h = (x_ref[...] * (1.0 + scale) + shift).astype(BF16)
    u = _dot(h, win_ref[...])
    uq = u[:, 0:Q_RANK]
    ukv = u[:, Q_RANK:Q_RANK + KV_RANK]
    uf = u[:, Q_RANK + KV_RANK:Q_RANK + KV_RANK + FNET_DIM]
    o = Q_RANK + KV_RANK + FNET_DIM
    upe, upe_rot = u[:, o:o + 128], u[:, o + 128:o + 256]
    cos, sin = rope_ref[:, 0:128], rope_ref[:, 128:256]

    ckv_ref[...] = ukv * lax.rsqrt(jnp.mean(ukv * ukv, axis=-1, keepdims=True) + RMS_EPS) * kvn_ref[...]
    kpe_ref[...] = upe * cos + upe_rot * sin

    qlat = (uq * lax.rsqrt(jnp.mean(uq * uq, axis=-1, keepdims=True) + RMS_EPS) * qn_ref[...]).astype(BF16)
    qa = _dot(qlat, wqa_ref[...])
    qb = _dot(qlat, wqb_ref[...])
    for hd in range(N_HEADS):
        nope = qa[:, hd * QH:hd * QH + 128]
        pe = qa[:, hd * QH + 128:(hd + 1) * QH] * cos + qb[:, hd * 128:(hd + 1) * 128] * sin
        q_ref[:, hd * QH:hd * QH + 128] = (nope * ATT_SCALE).astype(BF16)
        q_ref[:, hd * QH + 128:(hd + 1) * QH] = (pe * ATT_SCALE).astype(BF16)

    avg = avg_ref[...]
    uf_hi, uf_lo = _split_bf16(uf)
    mu = _dot(uf_hi, avg) + _dot(uf_lo, avg)
    dlt = uf - mu
    sq_hi, sq_lo = _split_bf16(dlt * dlt)
    var = _dot(sq_hi, avg) + _dot(sq_lo, avg)
    xn = dlt * lax.rsqrt(var + LN_EPS)
    xn_hi, xn_lo = _split_bf16(xn)
    y = _dot(xn_hi, dfth_ref[...]) + _dot(xn_lo, dfth_ref[...]) + _dot(xn_hi, dftl_ref[...])
    y_hi, y_lo = _split_bf16(y)
    yh_ref[...] = y_hi
    yl_ref[...] = y_lo


def _odd_proj(x, mods, rope_tab, w_in_ext, q_norm, kv_norm, wqa, wqb, avg, dft_hi, dft_lo):
    def rope_index(i):
        return (jnp.where(i < NP_TILES, 0, 1 + lax.rem(i - NP_TILES, TPS)), 0)

    return pl.pallas_call(
        _odd_proj_kernel,
        grid=(N_TILES,),
        in_specs=[
            pl.BlockSpec((TM, D), lambda i: (i, 0)),
            pl.BlockSpec((None, 6, D), lambda i: (_cond_index(i), 0, 0)),
            pl.BlockSpec((TM, 256), rope_index),
            _const_spec((D, W_IN_EXT)),
            _const_spec((1, Q_RANK)),
            _const_spec((1, KV_RANK)),
            _const_spec((Q_RANK, N_HEADS * QH)),
            _const_spec((Q_RANK, N_HEADS * 128)),
            _const_spec((FNET_DIM, FNET_DIM)),
            _const_spec((FNET_DIM, 2 * FNET_DIM)),
            _const_spec((FNET_DIM, 2 * FNET_DIM)),
        ],
        out_specs=[
            pl.BlockSpec((TM, N_HEADS * QH), lambda i: (i, 0)),
            pl.BlockSpec((TM, KV_RANK), lambda i: (i, 0)),
            pl.BlockSpec((TM, 128), lambda i: (i, 0)),
            pl.BlockSpec((TM, 2 * FNET_DIM), lambda i: (i, 0)),
            pl.BlockSpec((TM, 2 * FNET_DIM), lambda i: (i, 0)),
        ],
        out_shape=[
            jax.ShapeDtypeStruct((T, N_HEADS * QH), BF16),
            jax.ShapeDtypeStruct((T, KV_RANK), F32),
            jax.ShapeDtypeStruct((T, 128), F32),
            jax.ShapeDtypeStruct((T, 2 * FNET_DIM), BF16),
            jax.ShapeDtypeStruct((T, 2 * FNET_DIM), BF16),
        ],
        compiler_params=_params(1),
        name="odd_projections",
    )(x, mods, rope_tab, w_in_ext, q_norm, kv_norm, wqa, wqb, avg, dft_hi, dft_lo)


def _kv_kernel(c_ref, w_ref, o_ref):
    o_ref[...] = _dot(c_ref[...].astype(BF16), w_ref[...]).astype(BF16)


def _kv_expand(ckv_all, w_kv):
    return pl.pallas_call(
        _kv_kernel,
        grid=(KV_ROWS // TM_KV,),
        in_specs=[pl.BlockSpec((TM_KV, KV_RANK), lambda i: (i, 0)),
                  _const_spec((KV_RANK, 2 * N_HEADS * 128))],
        out_specs=pl.BlockSpec((TM_KV, 2 * N_HEADS * 128), lambda i: (i, 0)),
        out_shape=jax.ShapeDtypeStruct((KV_ROWS, 2 * N_HEADS * 128), BF16),
        compiler_params=_params(1),
        name="kv_expand",
    )(ckv_all, w_kv)


def _attn_body(q_ref, kv_ref, kpe_ref, o_ref):
    kpe = kpe_ref[...].astype(BF16)
    for hd in range(N_HEADS):
        qh = q_ref[:, hd * QH:(hd + 1) * QH]
        kh = jnp.concatenate([kv_ref[:, hd * 128:(hd + 1) * 128], kpe], axis=1)
        s = lax.dot_general(qh, kh, (((1,), (1,)), ((), ())), preferred_element_type=F32)
        p = jnp.exp2(s - jnp.max(s, axis=-1, keepdims=True))
        den = jnp.sum(p, axis=-1, keepdims=True)
        vh = kv_ref[:, (N_HEADS + hd) * 128:(N_HEADS + hd + 1) * 128]
        o = _dot(p.astype(BF16), vh)
        o_ref[:, hd * 128:(hd + 1) * 128] = (o / den).astype(BF16)


def _attn_kernel(q_ref, kvp_ref, kpep_ref, kvs_ref, kpes_ref, o_ref):
    @pl.when(pl.program_id(0) < NP_TILES)
    def _():
        _attn_body(q_ref, kvp_ref, kpep_ref, o_ref)

    @pl.when(pl.program_id(0) >= NP_TILES)
    def _():
        _attn_body(q_ref, kvs_ref, kpes_ref, o_ref)


def _attention(q, kv, kpe_all):
    kv_p0 = DEC_BATCH * LK_S // SEQ
    ctx_blk = lambda i: (kv_p0 + jnp.minimum(i, NP_TILES - 1), 0)
    lat_blk = lambda i: (jnp.maximum(i - NP_TILES, 0) // TPS, 0)
    return pl.pallas_call(
        _attn_kernel,
        grid=(N_TILES,),
        in_specs=[
            pl.BlockSpec((TM, N_HEADS * QH), lambda i: (i, 0)),
            pl.BlockSpec((SEQ, 2 * N_HEADS * 128), ctx_blk),
            pl.BlockSpec((SEQ, 128), ctx_blk),
            pl.BlockSpec((LK_S, 2 * N_HEADS * 128), lat_blk),
            pl.BlockSpec((LK_S, 128), lat_blk),
        ],
        out_specs=pl.BlockSpec((TM, N_HEADS * V_DIM), lambda i: (i, 0)),
        out_shape=jax.ShapeDtypeStruct((T, N_HEADS * V_DIM), BF16),
        compiler_params=_params(1),
        name="attention",
    )(q, kv, kpe_all, kv, kpe_all)


def _pos_dft_body(yh_ref, yl_ref, ch, cl, sh, sl, o_ref):
    yc_h, ys_h = yh_ref[:, 0:FNET_DIM], yh_ref[:, FNET_DIM:]
    yc_l, ys_l = yl_ref[:, 0:FNET_DIM], yl_ref[:, FNET_DIM:]
    f = (_dot(ch, yc_h) + _dot(ch, yc_l) + _dot(cl, yc_h)
         + _dot(sh, ys_h) + _dot(sh, ys_l) + _dot(sl, ys_h))
    o_ref[...] = f.astype(BF16)


def _pos_dft_kernel(yhp_ref, ylp_ref, chp_ref, clp_ref, shp_ref, slp_ref,
                    yhs_ref, yls_ref, c0_ref, s0_ref, cb_ref, sb_ref, o_ref):
    @pl.when(pl.program_id(0) < NP_TILES)
    def _():
        _pos_dft_body(yhp_ref, ylp_ref, chp_ref[...], clp_ref[...], shp_ref[...], slp_ref[...], o_ref)

    @pl.when(pl.program_id(0) >= NP_TILES)
    def _():
        j = lax.rem(pl.program_id(0) - NP_TILES, TPS)
        cb, sb = cb_ref[pl.ds(j, 1), :], sb_ref[pl.ds(j, 1), :]
        c0, s0 = c0_ref[...], s0_ref[...]
        ch, cl = _split_bf16(c0 * cb - s0 * sb)
        sh, sl = _split_bf16(s0 * cb + c0 * sb)
        _pos_dft_body(yhs_ref, yls_ref, ch, cl, sh, sl, o_ref)


def _pos_dft(yh, yl, tabs_p, base_s, step_s):
    ctx_blk = lambda i: (jnp.minimum(i, NP_TILES - 1), 0)
    lat_seq = lambda i: (T_P // DEC_SEQ + jnp.maximum(i - NP_TILES, 0) // TPS, 0)
    return pl.pallas_call(
        _pos_dft_kernel,
        grid=(N_TILES,),
        in_specs=[pl.BlockSpec((SEQ, 2 * FNET_DIM), ctx_blk)] * 2
        + [_const_spec((SEQ, SEQ))] * 4
        + [pl.BlockSpec((DEC_SEQ, 2 * FNET_DIM), lat_seq)] * 2
        + [_const_spec((TM, DEC_SEQ))] * 2
        + [_const_spec((TPS, DEC_SEQ))] * 2,
        out_specs=pl.BlockSpec((TM, FNET_DIM), lambda i: (i, 0)),
        out_shape=jax.ShapeDtypeStruct((T, FNET_DIM), BF16),
        compiler_params=_params(1),
        name="pos_dft",
    )(yh, yl, *tabs_p, yh, yl, *base_s, *step_s)


def _odd_merge_kernel(x_ref, attn_ref, f_ref, mod_ref, fw_ref, wo_ref, g_ref, b_ref, rh_ref, rl_ref, tri_ref,
                      xo_ref, info_ref, cnt_ref, carry_ref):
    @pl.when(pl.program_id(0) == 0)
    def _():
        carry_ref[...] = jnp.zeros_like(carry_ref)

    gate = mod_ref[2:3, :]
    shift2, scale2 = mod_ref[3:4, :], mod_ref[4:5, :]
    fm = _dot(f_ref[...], fw_ref[...]).astype(BF16)
    y = _dot(attn_ref[...], wo_ref[0:N_HEADS * V_DIM, :]) + _dot(fm, wo_ref[N_HEADS * V_DIM:, :])
    x = _layer_norm(ALPHA * x_ref[...] + gate * y, g_ref[...], b_ref[...])
    xo_ref[...] = x
    h = x * (1.0 + scale2) + shift2

    h_hi, h_lo = _split_bf16(h)
    logits = _dot(h_hi, rh_ref[...]) + _dot(h_lo, rh_ref[...]) + _dot(h_hi, rl_ref[...])
    lane = lax.broadcasted_iota(jnp.int32, (TM, 128), 1)
    neg = jnp.float32(-jnp.inf)
    logits = jnp.where(lane < N_EXPERTS, logits, neg)
    m1 = jnp.max(logits, axis=-1, keepdims=True)
    i1 = jnp.min(jnp.where(logits == m1, lane, 128), axis=-1, keepdims=True)
    rest = jnp.where(lane == i1, neg, logits)
    m2 = jnp.max(rest, axis=-1, keepdims=True)
    i2 = jnp.min(jnp.where(rest == m2, lane, 128), axis=-1, keepdims=True)
    e2 = jnp.exp(m2 - m1)
    w1 = 1.0 / (1.0 + e2)
    w2 = e2 / (1.0 + e2)
    info = jnp.where(lane == 0, w1, 0.0)
    info = jnp.where(lane == 1, w2, info)
    info = jnp.where(lane == 2, i1.astype(F32), info)
    info = jnp.where(lane == 3, i2.astype(F32), info)

    uses = jnp.logical_or(lane == i1, lane == i2)
    seen = _dot(tri_ref[...], jnp.where(uses, 1.0, 0.0).astype(BF16)) + carry_ref[...]
    r1 = jnp.sum(jnp.where(lane == i1, seen, 0.0), axis=-1, keepdims=True)
    r2 = jnp.sum(jnp.where(lane == i2, seen, 0.0), axis=-1, keepdims=True)
    info = jnp.where(lane == 4, r1, info)
    info = jnp.where(lane == 5, r2, info)
    info_ref[...] = info
    total = carry_ref[...] + jnp.sum(jnp.where(uses, 1.0, 0.0), axis=0, keepdims=True)
    carry_ref[...] = total
    cnt_ref[...] = jnp.broadcast_to(total, cnt_ref.shape)


def _odd_merge(x, attn, f, mods, fnet_w, w_out, g, b, r_hi, r_lo):
    row = lambda i: (i, 0)
    tri = jnp.tril(jnp.ones((TM, TM), F32), -1).astype(BF16)
    return pl.pallas_call(
        _odd_merge_kernel,
        grid=(N_TILES,),
        in_specs=[
            pl.BlockSpec((TM, D), row),
            pl.BlockSpec((TM, N_HEADS * V_DIM), row),
            pl.BlockSpec((TM, FNET_DIM), row),
            pl.BlockSpec((None, 6, D), lambda i: (_cond_index(i), 0, 0)),
            _const_spec((FNET_DIM, FNET_DIM)),
            _const_spec((N_HEADS * V_DIM + FNET_DIM, D)),
            _const_spec((1, D)),
            _const_spec((1, D)),
            _const_spec((D, 128)),
            _const_spec((D, 128)),
            _const_spec((TM, TM)),
        ],
        out_specs=[pl.BlockSpec((TM, D), row), pl.BlockSpec((TM, 128), row),
                   pl.BlockSpec((8, 128), lambda i: (0, 0))],
        out_shape=[jax.ShapeDtypeStruct((T, D), F32), jax.ShapeDtypeStruct((T, 128), F32),
                   jax.ShapeDtypeStruct((8, 128), F32)],
        scratch_shapes=[pltpu.VMEM((1, 128), F32)],
        compiler_params=_params(1),
        name="odd_merge_router",
    )(x, attn, f, mods, fnet_w, w_out, g, b, r_hi, r_lo, tri)


TM_D = 512
ROW = (8, 128)
DMA_UNROLL = 8


def _row_copy(src, s, dst, d, sem):
    return pltpu.make_async_copy(src.at[s], dst.at[d], sem)


def _dispatch_kernel(dest_ref, pad_lo_ref, pad_hi_ref, x_ref, mod_ref, xs_ref, h_ref, zero_ref, sem):
    i = pl.program_id(0)
    base = i * TM_D
    shift2, scale2 = mod_ref[3:4, :], mod_ref[4:5, :]
    h_ref[...] = (x_ref[...] * (1.0 + scale2) + shift2).reshape((TM_D,) + ROW)

    def issue(r, carry):
        t = base + r
        _row_copy(h_ref, r, xs_ref, dest_ref[t], sem).start(priority=0)
        _row_copy(h_ref, r, xs_ref, dest_ref[T + t], sem).start(priority=1)
        return carry

    lax.fori_loop(0, TM_D, issue, 0, unroll=DMA_UNROLL)

    def drain(r, carry):
        _row_copy(h_ref, 0, xs_ref, 0, sem).wait()
        _row_copy(h_ref, 0, xs_ref, 0, sem).wait()
        return carry

    lax.fori_loop(0, TM_D, drain, 0, unroll=DMA_UNROLL)

    @pl.when(i == 0)
    def _():
        zero_ref[...] = jnp.zeros_like(zero_ref)
        for e in range(N_EXPERTS + 1):
            def zissue(r, carry):
                _row_copy(zero_ref, 0, xs_ref, r, sem).start()
                return carry

            def zdrain(r, carry):
                _row_copy(zero_ref, 0, xs_ref, 0, sem).wait()
                return carry

            lax.fori_loop(pad_lo_ref[e], pad_hi_ref[e], zissue, 0)
            lax.fori_loop(pad_lo_ref[e], pad_hi_ref[e], zdrain, 0)


def _dispatch(dest, pad_lo, pad_hi, x, mods):
    return pl.pallas_call(
        _dispatch_kernel,
        grid_spec=pltpu.PrefetchScalarGridSpec(
            num_scalar_prefetch=3,
            grid=(T // TM_D,),
            in_specs=[pl.BlockSpec((TM_D, D), lambda i, *_: (i, 0)),
                      pl.BlockSpec((None, 6, D), lambda i, *_: (_cond_index(i, TM_D), 0, 0))],
            out_specs=pl.BlockSpec(memory_space=pl.ANY),
            scratch_shapes=[pltpu.VMEM((TM_D,) + ROW, F32), pltpu.VMEM((1,) + ROW, F32),
                            pltpu.SemaphoreType.DMA(())],
        ),
        out_shape=jax.ShapeDtypeStruct((R_MAX,) + ROW, F32),
        compiler_params=pltpu.CompilerParams(dimension_semantics=("arbitrary",), has_side_effects=True),
        name="expert_dispatch",
    )(dest, pad_lo, pad_hi, x, mods)


def _expert_kernel(te_ref, nt_ref, xs_ref, wg_ref, wu_ref, wd_ref, o_ref):
    @pl.when(pl.program_id(0) < nt_ref[0])
    def _():
        h = xs_ref[...].reshape(TM_E, D).astype(BF16)
        a = _silu(_dot(h, wg_ref[...])) * _dot(h, wu_ref[...])
        o_ref[...] = _dot(a.astype(BF16), wd_ref[...]).reshape((TM_E,) + ROW)

    @pl.when(pl.program_id(0) >= nt_ref[0])
    def _():
        o_ref[...] = jnp.zeros_like(o_ref)


def _experts(tile_expert, n_used, xs, wg, wu, wd):
    return pl.pallas_call(
        _expert_kernel,
        grid_spec=pltpu.PrefetchScalarGridSpec(
            num_scalar_prefetch=2,
            grid=(N_ETILES,),
            in_specs=[
                pl.BlockSpec((TM_E,) + ROW, lambda i, te, nt: (jnp.minimum(i, nt[0] - 1), 0, 0)),
                pl.BlockSpec((None, D, D_FF_EXPERT), lambda i, te, nt: (te[i], 0, 0)),
                pl.BlockSpec((None, D, D_FF_EXPERT), lambda i, te, nt: (te[i], 0, 0)),
                pl.BlockSpec((None, D_FF_EXPERT, D), lambda i, te, nt: (te[i], 0, 0)),
            ],
            out_specs=pl.BlockSpec((TM_E,) + ROW, lambda i, te, nt: (i, 0, 0)),
        ),
        out_shape=jax.ShapeDtypeStruct((R_MAX,) + ROW, F32),
        compiler_params=_params(1),
        name="expert_swiglu",
    )(tile_expert, n_used, xs, wg, wu, wd)


def _combine_kernel(dest_ref, x_ref, info_ref, mod_ref, g_ref, b_ref, ys_ref, op_ref, os_ref, buf, sem):
    i = pl.program_id(0)
    slot = lax.rem(i, 2)

    def gather(tile, s):
        def issue(r, carry):
            t = tile * TM + r
            _row_copy(ys_ref, dest_ref[t], buf.at[s, 0], r, sem.at[s]).start(priority=0)
            _row_copy(ys_ref, dest_ref[T + t], buf.at[s, 1], r, sem.at[s]).start(priority=1)
            return carry

        lax.fori_loop(0, TM, issue, 0, unroll=DMA_UNROLL)

    @pl.when(i == 0)
    def _():
        gather(0, 0)

    @pl.when(i + 1 < N_TILES)
    def _():
        gather(i + 1, 1 - slot)

    def drain(r, carry):
        _row_copy(ys_ref, 0, buf.at[slot, 0], 0, sem.at[slot]).wait()
        _row_copy(ys_ref, 0, buf.at[slot, 1], 0, sem.at[slot]).wait()
        return carry

    lax.fori_loop(0, TM, drain, 0, unroll=DMA_UNROLL)

    gate = mod_ref[5:6, :]
    w1, w2 = info_ref[:, 0:1], info_ref[:, 1:2]
    y = w1 * buf[slot, 0].reshape(TM, D) + w2 * buf[slot, 1].reshape(TM, D)
    out = _layer_norm(ALPHA * x_ref[...] + gate * y, g_ref[...], b_ref[...])

    @pl.when(pl.program_id(0) < NP_TILES)
    def _():
        op_ref[...] = out

    @pl.when(pl.program_id(0) >= NP_TILES)
    def _():
        os_ref[...] = out


def _combine(dest, x, info, mods, g, b, ys):
    return pl.pallas_call(
        _combine_kernel,
        grid_spec=pltpu.PrefetchScalarGridSpec(
            num_scalar_prefetch=1,
            grid=(N_TILES,),
            in_specs=[
                pl.BlockSpec((TM, D), lambda i, d: (i, 0)),
                pl.BlockSpec((TM, 128), lambda i, d: (i, 0)),
                pl.BlockSpec((None, 6, D), lambda i, d: (_cond_index(i), 0, 0)),
                pl.BlockSpec((1, D), lambda i, d: (0, 0)),
                pl.BlockSpec((1, D), lambda i, d: (0, 0)),
                pl.BlockSpec(memory_space=pl.ANY),
            ],
            out_specs=[pl.BlockSpec((TM, D), lambda i, d: (jnp.minimum(i, NP_TILES - 1), 0)),
                       pl.BlockSpec((TM, D), lambda i, d: (jnp.maximum(i - NP_TILES, 0), 0))],
            scratch_shapes=[pltpu.VMEM((2, 2, TM) + ROW, F32), pltpu.SemaphoreType.DMA((2,))],
        ),
        out_shape=[jax.ShapeDtypeStruct((T_P, D), F32), jax.ShapeDtypeStruct((T_S, D), F32)],
        compiler_params=_params(1),
        name="expert_combine",
    )(dest, x, info, mods, g, b, ys)


def _rot_cols(w):
    w4 = w.reshape(w.shape[:-1] + (2, 2, QK_ROPE // 4))
    return jnp.stack([-w4[..., 1, :], w4[..., 0, :]], axis=-2).reshape(w.shape)


def _rope_table():
    rows = DEC_SEQ // GRID_W
    row = jnp.repeat(jnp.arange(rows), GRID_W).astype(F32)
    col = jnp.tile(jnp.arange(GRID_W), rows).astype(F32)
    half = QK_ROPE // 2
    inv = ROPE_THETA ** (-jnp.arange(0, half, 2, dtype=F32) / half)
    ar, ac = row[:, None] * inv, col[:, None] * inv
    ang = jnp.concatenate([ar, ar, ac, ac], axis=-1)
    cos = jnp.concatenate([jnp.ones((TM, QK_ROPE), F32), jnp.cos(ang)], axis=0)
    sin = jnp.concatenate([jnp.zeros((TM, QK_ROPE), F32), jnp.sin(ang)], axis=0)
    n = cos.shape[0]
    return jnp.concatenate([cos, jnp.ones((n, 64), F32), sin, jnp.zeros((n, 64), F32)], axis=1)


def _dft_tables(n):
    k = jnp.arange(n, dtype=jnp.int32)
    ang = ((k[:, None] * k[None, :]) % n).astype(F32) * (2.0 * np.pi / n)
    sc = n ** -0.5
    return jnp.cos(ang) * sc, jnp.sin(ang) * sc


def _hi_lo(m):
    hi = m.astype(BF16)
    return hi, (m - hi.astype(F32)).astype(BF16)


def _block_diag4(m):
    return jnp.kron(jnp.eye(4, dtype=m.dtype), m)


def kernel(x_prompt, x_sample, cache_ckv, cache_kpe, c, c_ctx, ada_w, ada_b, ln_g, ln_b, ev_w_in, ev_conv_w, ev_pool_w, ev_pool_scale, ev_w_out, ffn_w_gate, ffn_w_up, ffn_w_down, od_w_in, od_q_norm, od_kv_norm, od_w_q_b, od_w_kv_b, od_fnet_w, od_w_out, moe_router, moe_w_gate, moe_w_up, moe_w_down):
    cond8 = jnp.concatenate([c_ctx[None, :], c, jnp.zeros((8 - N_COND, D), F32)], axis=0)
    mods = _modulation(cond8, ada_w, ada_b)[:, :N_COND].reshape(DEPTH, N_COND, 6, D)

    cast_srcs = [moe_w_gate[0].reshape(N_EXPERTS * D, D_FF_EXPERT), moe_w_up[0].reshape(N_EXPERTS * D, D_FF_EXPERT),
                 moe_w_down[0].reshape(N_EXPERTS * D_FF_EXPERT, D), ffn_w_gate[0], ffn_w_up[0], ffn_w_down[0]]
    x, (moe_wg, moe_wu, moe_wd, ffn_wg, ffn_wu, ffn_wd) = _even_mixer(
        x_prompt.reshape(T_P, D), x_sample.reshape(T_S, D), mods[0], ev_w_in[0].astype(BF16), ev_conv_w[0],
        ev_pool_w[0].astype(BF16), ev_pool_scale[0][None, :], ev_w_out[0].astype(BF16),
        ln_g[0, 0][None, :], ln_b[0, 0][None, :], cast_srcs)
    x = _ffn(x, mods[0], ffn_wg, ffn_wu, ffn_wd, ln_g[0, 1][None, :], ln_b[0, 1][None, :])

    w_in = od_w_in[0]
    w_pe = w_in[:, Q_RANK + KV_RANK:Q_RANK + KV_RANK + QK_ROPE]
    zpad = jnp.zeros((D, 64), F32)
    w_in_ext = jnp.concatenate([w_in[:, :Q_RANK + KV_RANK], w_in[:, Q_RANK + KV_RANK + QK_ROPE:],
                                w_pe, zpad, _rot_cols(w_pe), zpad], axis=1).astype(BF16)
    wq = od_w_q_b[0].reshape(Q_RANK, N_HEADS, QK_NOPE + QK_ROPE)
    zq = jnp.zeros((Q_RANK, N_HEADS, 64), F32)
    wqa = jnp.concatenate([wq, zq], axis=-1).reshape(Q_RANK, N_HEADS * QH).astype(BF16)
    wqb = jnp.concatenate([_rot_cols(wq[..., QK_NOPE:]), zq], axis=-1).reshape(Q_RANK, N_HEADS * 128).astype(BF16)
    wkv = od_w_kv_b[0].reshape(KV_RANK, N_HEADS, QK_NOPE + V_DIM)
    w_kv = jnp.concatenate([wkv[..., :QK_NOPE].reshape(KV_RANK, -1), wkv[..., QK_NOPE:].reshape(KV_RANK, -1)],
                           axis=1).astype(BF16)

    avg = _block_diag4(jnp.full((FNET_GROUP_DIM, FNET_GROUP_DIM), 1.0 / FNET_GROUP_DIM, F32)).astype(BF16)
    cc, sc = _dft_tables(FNET_GROUP_DIM)
    dft_c = jnp.concatenate([_block_diag4(cc), -_block_diag4(sc)], axis=1)
    dft_hi, dft_lo = _hi_lo(dft_c)

    q, ckv, kpe, yh, yl = _odd_proj(x, mods[1], _rope_table(), w_in_ext, od_q_norm[0][None, :],
                                    od_kv_norm[0][None, :], wqa, wqb, avg, dft_hi, dft_lo)

    cache_kpe128 = jnp.pad(cache_kpe[:, 0], ((0, 0), (0, 0), (0, 128 - QK_ROPE)))
    ckv_parts, kpe_parts = [], []
    for bi in range(DEC_BATCH):
        lat = slice(T_P + bi * DEC_SEQ, T_P + (bi + 1) * DEC_SEQ)
        ckv_parts += [cache_ckv[bi, 0], ckv[lat]]
        kpe_parts += [cache_kpe128[bi], kpe[lat]]
    ckv_all = jnp.concatenate(ckv_parts + [ckv[:T_P]], axis=0)
    kpe_all = jnp.concatenate(kpe_parts + [kpe[:T_P]], axis=0)
    kv = _kv_expand(ckv_all, w_kv)
    attn = _attention(q, kv, kpe_all)

    tabs_p = sum((_hi_lo(m) for m in _dft_tables(SEQ)), ())
    k_s = jnp.arange(DEC_SEQ, dtype=jnp.int32)
    ang = lambda rows: ((rows[:, None] * k_s[None, :]) % DEC_SEQ).astype(F32) * (2.0 * np.pi / DEC_SEQ)
    a_base = ang(jnp.arange(TM, dtype=jnp.int32))
    a_step = ang(jnp.arange(TPS, dtype=jnp.int32) * TM)
    base_s = (jnp.cos(a_base) * DEC_SEQ ** -0.5, jnp.sin(a_base) * DEC_SEQ ** -0.5)
    step_s = (jnp.cos(a_step), jnp.sin(a_step))
    f = _pos_dft(yh, yl, tabs_p, base_s, step_s)

    router = jnp.pad(moe_router[0], ((0, 0), (0, 128 - N_EXPERTS)))
    r_hi, r_lo = _hi_lo(router)
    x, info, cnt = _odd_merge(x, attn, f, mods[1], od_fnet_w[0].astype(BF16), od_w_out[0].astype(BF16),
                              ln_g[1, 0][None, :], ln_b[1, 0][None, :], r_hi, r_lo)

    counts = cnt[0, :N_EXPERTS].astype(jnp.int32)
    padded = ((counts + TM_E - 1) // TM_E) * TM_E
    g_end = jnp.cumsum(padded)
    g_start = g_end - padded
    choice = info[:, 2:4].astype(jnp.int32)
    rank = info[:, 4:6].astype(jnp.int32)
    is_e = choice[..., None] == jnp.arange(N_EXPERTS, dtype=jnp.int32)
    dest = (jnp.sum(jnp.where(is_e, g_start, 0), axis=-1) + rank).T.reshape(-1).astype(jnp.int32)
    tile_row = jnp.arange(N_ETILES, dtype=jnp.int32) * TM_E
    tile_expert = jnp.minimum(jnp.sum((tile_row[:, None] >= g_end[None, :]).astype(jnp.int32), axis=1),
                              N_EXPERTS - 1).astype(jnp.int32)
    n_used = (g_end[-1:] // TM_E).astype(jnp.int32)

    pad_lo = jnp.concatenate([g_start + counts, g_end[-1:]]).astype(jnp.int32)
    pad_hi = jnp.concatenate([g_end, jnp.full((1,), R_MAX, jnp.int32)]).astype(jnp.int32)
    xs = _dispatch(dest, pad_lo, pad_hi, x, mods[1])
    ys = _experts(tile_expert, n_used, xs, moe_wg.reshape(N_EXPERTS, D, D_FF_EXPERT),
                  moe_wu.reshape(N_EXPERTS, D, D_FF_EXPERT), moe_wd.reshape(N_EXPERTS, D_FF_EXPERT, D))
    yp, ysm = _combine(dest, x, info, mods[1], ln_g[1, 1][None, :], ln_b[1, 1][None, :], ys)

    y_prompt = yp.reshape(BATCH, SEQ, D)
    y_sample = ysm.reshape(DEC_BATCH, DEC_SEQ, D)
    new_ckv = ckv[:T_P].reshape(BATCH, 1, SEQ, KV_RANK)
    new_kpe = kpe[:T_P, :QK_ROPE].reshape(BATCH, 1, SEQ, QK_ROPE)
    return (y_prompt, y_sample, new_ckv, new_kpe)
```

```python
import functools

import numpy as np
import jax
import jax.numpy as jnp
from jax import lax
from jax.experimental import pallas as pl
from jax.experimental.pallas import tpu as pltpu

F32 = jnp.float32
BF16 = jnp.bfloat16

D = 1024
BATCH, SEQ = 32, 256
DEC_BATCH, DEC_SEQ = 2, 2048
PAST = 512
GRID_W = 64
T_P = BATCH * SEQ
T_S = DEC_BATCH * DEC_SEQ
T = T_P + T_S
N_COND = 1 + DEC_BATCH

CONV_DIM = 512
POOL_WINDOWS = (2, 4, 8, 16)
POOL_GROUP = 128
N_HEADS = 8
QK_NOPE, QK_ROPE, V_DIM = 128, 64, 128
Q_RANK, KV_RANK = 384, 256
FNET_DIM, FNET_GROUP_DIM = 256, 64
D_FF = 2816
N_EXPERTS = 8
D_FF_EXPERT = 1792
DEPTH = 2
ALPHA = (2 * DEPTH) ** 0.25
LN_EPS = 1e-5
RMS_EPS = 1e-6
ROPE_THETA = 10000.0

TM = 256
NP_TILES = T_P // TM
TPS = DEC_SEQ // TM
N_TILES = T // TM
HALO = 8
TM2 = 512
TM_FFN = 512
TM_KV = 1024
LK_S = PAST + DEC_SEQ
KV_ROWS = DEC_BATCH * LK_S + T_P
TM_E = 256
R_MAX = 2 * T + N_EXPERTS * TM_E
N_ETILES = R_MAX // TM_E
VMEM_LIMIT = 56 * 1024 * 1024


def _cond_index(i, tm=TM):
    return jnp.where(i < T_P // tm, 0, 1 + (i - T_P // tm) // (DEC_SEQ // tm))


def _const_spec(shape):
    nd = len(shape)
    return pl.BlockSpec(shape, lambda *_: (0,) * nd, pipeline_mode=pl.Buffered(1))


def _params(n_axes=1, vmem=VMEM_LIMIT):
    return pltpu.CompilerParams(dimension_semantics=("arbitrary",) * n_axes, vmem_limit_bytes=vmem)


def _layer_norm(v, g, b):
    mu = jnp.mean(v, axis=-1, keepdims=True)
    d = v - mu
    var = jnp.mean(d * d, axis=-1, keepdims=True)
    return d * lax.rsqrt(var + LN_EPS) * g + b


def _split_bf16(v):
    hi = v.astype(BF16)
    lo = (v - hi.astype(F32)).astype(BF16)
    return hi, lo


def _dot(a, b):
    return jnp.dot(a, b, preferred_element_type=F32)


def _silu(v):
    return v / (1.0 + jnp.exp(-v))


def _mod_kernel(cond_ref, w_ref, b_ref, o_ref):
    s = _silu(cond_ref[...]).astype(BF16)
    o_ref[...] = _dot(s, w_ref[...].astype(BF16)) + b_ref[...]


def _modulation(cond8, ada_w, ada_b):
    nb = 6 * D // 1024
    return pl.pallas_call(
        _mod_kernel,
        grid=(DEPTH, nb),
        in_specs=[
            pl.BlockSpec((8, D), lambda l, j: (0, 0)),
            pl.BlockSpec((None, D, 1024), lambda l, j: (l, 0, j)),
            pl.BlockSpec((None, 1, 1024), lambda l, j: (l, 0, j)),
        ],
        out_specs=pl.BlockSpec((None, 8, 1024), lambda l, j: (l, 0, j)),
        out_shape=jax.ShapeDtypeStruct((DEPTH, 8, 6 * D), F32),
        compiler_params=_params(2),
        name="adaln_modulation",
    )(cond8, ada_w, ada_b.reshape(DEPTH, 1, 6 * D))


CAST_SPECS = (
    (N_EXPERTS * D, D_FF_EXPERT, 32),
    (N_EXPERTS * D, D_FF_EXPERT, 32),
    (N_EXPERTS * D_FF_EXPERT, D, 32),
    (D, D_FF, 32),
    (D, D_FF, 32),
    (D_FF, D, 16),
)
N_CAST = len(CAST_SPECS)


def _even_mixer_kernel(xctx_ref, xprev_ref, xlat_ref, xnext_ref, mod_ref, win_ref, convw_ref, poolw_ref,
                       pscale_ref, wout_ref, g_ref, b_ref, *rest):
    cast_in, o_ref, cast_out, xall_ref = rest[:N_CAST], rest[N_CAST], rest[N_CAST + 1:2 * N_CAST + 1], rest[-1]
    i = pl.program_id(0)
    j = i - NP_TILES
    is_latent = i >= NP_TILES
    left_ok = jnp.logical_and(is_latent, lax.rem(j, TPS) != 0)
    right_ok = jnp.logical_and(is_latent, lax.rem(j, TPS) != TPS - 1)
    row_lo = jnp.where(left_ok, 0, HALO)
    row_hi = jnp.where(right_ok, TM + 2 * HALO, TM + HALO)

    for src, dst, (_, _, n_blocks) in zip(cast_in, cast_out, CAST_SPECS):
        @pl.when(i < n_blocks)
        def _(src=src, dst=dst):
            dst[...] = src[...].astype(BF16)

    @pl.when(jnp.logical_not(is_latent))
    def _():
        xall_ref[0:HALO, :] = jnp.zeros((HALO, D), F32)
        xall_ref[HALO:HALO + TM, :] = xctx_ref[...]
        xall_ref[HALO + TM:, :] = jnp.zeros((HALO, D), F32)

    @pl.when(is_latent)
    def _():
        xall_ref[0:HALO, :] = xprev_ref[...]
        xall_ref[HALO:HALO + TM, :] = xlat_ref[...]
        xall_ref[HALO + TM:, :] = xnext_ref[...]

    shift, scale, gate = mod_ref[0:1, :], mod_ref[1:2, :], mod_ref[2:3, :]
    xc = xall_ref[HALO:HALO + TM, :]
    x_all = xall_ref[...]
    h = (x_all * (1.0 + scale) + shift).astype(BF16)
    u = _dot(h, win_ref[...])
    rows = lax.broadcasted_iota(jnp.int32, (TM + 2 * HALO, 1), 0)
    in_seq = jnp.logical_and(rows >= row_lo, rows < row_hi)
    u = jnp.where(in_seq, u, 0.0)

    ux, ub = u[:, 0:CONV_DIM], u[:, CONV_DIM:2 * CONV_DIM]
    uc, up = u[:, 2 * CONV_DIM:3 * CONV_DIM], u[:, 3 * CONV_DIM:]
    z = uc * ux
    conv = (z[HALO - 1:HALO - 1 + TM] * convw_ref[0:1, :]
            + z[HALO:HALO + TM] * convw_ref[1:2, :]
            + z[HALO + 1:HALO + 1 + TM] * convw_ref[2:3, :])
    ya = ub[HALO:HALO + TM] * conv

    t_idx = lax.broadcasted_iota(jnp.int32, (TM, TM + 2 * HALO), 0)
    r_idx = lax.broadcasted_iota(jnp.int32, (TM, TM + 2 * HALO), 1)
    col_ok = jnp.logical_and(r_idx >= row_lo, r_idx < row_hi)
    pos = r_idx - HALO
    t_col = lax.broadcasted_iota(jnp.int32, (TM, 1), 0)
    first = jnp.where(left_ok, -HALO, 0)
    last = jnp.where(right_ok, TM + HALO, TM)
    up_hi, up_lo = _split_bf16(up)
    yb_groups = []
    for gi, w in enumerate(POOL_WINDOWS):
        band = jnp.logical_and(jnp.logical_and(pos >= t_idx - w // 2, pos < t_idx + w // 2), col_ok)
        band = jnp.where(band, 1.0, 0.0).astype(BF16)
        sl = slice(gi * POOL_GROUP, (gi + 1) * POOL_GROUP)
        tot = _dot(band, up_hi[:, sl]) + _dot(band, up_lo[:, sl])
        cnt = (jnp.minimum(t_col + w // 2, last) - jnp.maximum(t_col - w // 2, first)).astype(F32)
        p = tot / cnt - up[HALO:HALO + TM, sl]
        yb_groups.append(_dot(p.astype(BF16), poolw_ref[gi]))
    yb = jnp.concatenate(yb_groups, axis=1) * pscale_ref[...]

    mix = jnp.concatenate([ya, yb], axis=1).astype(BF16)
    y = _dot(mix, wout_ref[...])
    o_ref[...] = _layer_norm(ALPHA * xc + gate * y, g_ref[...], b_ref[...])


def _even_mixer(x_ctx, x_lat, mods, w_in, conv_w, pool_w, pool_scale, w_out, g, b, cast_srcs):
    hb = TM // HALO
    n8 = T_S // HALO
    lat = lambda i: jnp.maximum(i - NP_TILES, 0)
    cast_specs = [pl.BlockSpec((r // nb, c), lambda i, nb=nb: (jnp.minimum(i, nb - 1), 0))
                  for r, c, nb in CAST_SPECS]
    outs = pl.pallas_call(
        _even_mixer_kernel,
        grid=(N_TILES,),
        in_specs=[
            pl.BlockSpec((TM, D), lambda i: (jnp.minimum(i, NP_TILES - 1), 0)),
            pl.BlockSpec((HALO, D), lambda i: (jnp.maximum(lat(i) * hb - 1, 0), 0)),
            pl.BlockSpec((TM, D), lambda i: (lat(i), 0)),
            pl.BlockSpec((HALO, D), lambda i: (jnp.minimum((lat(i) + 1) * hb, n8 - 1), 0)),
            pl.BlockSpec((None, 6, D), lambda i: (_cond_index(i), 0, 0)),
            _const_spec((D, 4 * CONV_DIM)),
            _const_spec((3, CONV_DIM)),
            _const_spec((4, POOL_GROUP, POOL_GROUP)),
            _const_spec((1, 4 * POOL_GROUP)),
            _const_spec((D, D)),
            _const_spec((1, D)),
            _const_spec((1, D)),
        ] + cast_specs,
        out_specs=[pl.BlockSpec((TM, D), lambda i: (i, 0))] + cast_specs,
        out_shape=[jax.ShapeDtypeStruct((T, D), F32)]
        + [jax.ShapeDtypeStruct((r, c), BF16) for r, c, _ in CAST_SPECS],
        scratch_shapes=[pltpu.VMEM((TM + 2 * HALO, D), F32)],
        compiler_params=_params(1),
        name="even_mixer",
    )(x_ctx, x_lat, x_lat, x_lat, mods, w_in, conv_w, pool_w, pool_scale, w_out, g, b, *cast_srcs)
    return outs[0], outs[1:]


FF_CHUNK = D_FF // 2


def _ffn_kernel(x_ref, mod_ref, wg_ref, wu_ref, wd_ref, g_ref, b_ref, o_ref):
    shift, scale, gate = mod_ref[3:4, :], mod_ref[4:5, :], mod_ref[5:6, :]
    x = x_ref[...]
    h = (x * (1.0 + scale) + shift).astype(BF16)
    f = jnp.zeros((TM_FFN, D), F32)
    for c in range(D_FF // FF_CHUNK):
        sl = slice(c * FF_CHUNK, (c + 1) * FF_CHUNK)
        a = _silu(_dot(h, wg_ref[:, sl])) * _dot(h, wu_ref[:, sl])
        f = f + _dot(a.astype(BF16), wd_ref[sl, :])
    o_ref[...] = _layer_norm(ALPHA * x + gate * f, g_ref[...], b_ref[...])


def _ffn(x, mods, wg, wu, wd, g, b):
    return pl.pallas_call(
        _ffn_kernel,
        grid=(T // TM_FFN,),
        in_specs=[
            pl.BlockSpec((TM_FFN, D), lambda i: (i, 0)),
            pl.BlockSpec((None, 6, D), lambda i: (_cond_index(i, TM_FFN), 0, 0)),
            _const_spec((D, D_FF)),
            _const_spec((D, D_FF)),
            _const_spec((D_FF, D)),
            _const_spec((1, D)),
            _const_spec((1, D)),
        ],
        out_specs=pl.BlockSpec((TM_FFN, D), lambda i: (i, 0)),
        out_shape=jax.ShapeDtypeStruct((T, D), F32),
        compiler_params=_params(1),
        name="dense_swiglu",
    )(x, mods, wg, wu, wd, g, b)


W_IN_EXT = Q_RANK + KV_RANK + FNET_DIM + 128 + 128
QH = 256
ATT_SCALE = (QK_NOPE + QK_ROPE) ** -0.5 * float(np.log2(np.e))


def _odd_proj_kernel(x_ref, mod_ref, rope_ref, win_ref, qn_ref, kvn_ref, wqa_ref, wqb_ref,
                     avg_ref, dfth_ref, dftl_ref,
                     q_ref, ckv_ref, kpe_ref, yh_ref, yl_ref):
    shift, scale = mod_ref[0:1, :], mod_ref[1:2, :]
    h = (x_ref[...] * (1.0 + scale) + shift).astype(BF16)
    u = _dot(h, win_ref[...])
    uq = u[:, 0:Q_RANK]
    ukv = u[:, Q_RANK:Q_RANK + KV_RANK]
    uf = u[:, Q_RANK + KV_RANK:Q_RANK + KV_RANK + FNET_DIM]
    o = Q_RANK + KV_RANK + FNET_DIM
    upe, upe_rot = u[:, o:o + 128], u[:, o + 128:o + 256]
    cos, sin = rope_ref[:, 0:128], rope_ref[:, 128:256]

    ckv_ref[...] = ukv * lax.rsqrt(jnp.mean(ukv * ukv, axis=-1, keepdims=True) + RMS_EPS) * kvn_ref[...]
    kpe_ref[...] = upe * cos + upe_rot * sin

    qlat = (uq * lax.rsqrt(jnp.mean(uq * uq, axis=-1, keepdims=True) + RMS_EPS) * qn_ref[...]).astype(BF16)
    qa = _dot(qlat, wqa_ref[...])
    qb = _dot(qlat, wqb_ref[...])
    for hd in range(N_HEADS):
        nope = qa[:, hd * QH:hd * QH + 128]
        pe = qa[:, hd * QH + 128:(hd + 1) * QH] * cos + qb[:, hd * 128:(hd + 1) * 128] * sin
        q_ref[:, hd * QH:hd * QH + 128] = (nope * ATT_SCALE).astype(BF16)
        q_ref[:, hd * QH + 128:(hd + 1) * QH] = (pe * ATT_SCALE).astype(BF16)

    avg = avg_ref[...]
    uf_hi, uf_lo = _split_bf16(uf)
    mu = _dot(uf_hi, avg) + _dot(uf_lo, avg)
    dlt = uf - mu
    sq_hi, sq_lo = _split_bf16(dlt * dlt)
    var = _dot(sq_hi, avg) + _dot(sq_lo, avg)
    xn = dlt * lax.rsqrt(var + LN_EPS)
    xn_hi, xn_lo = _split_bf16(xn)
    y = _dot(xn_hi, dfth_ref[...]) + _dot(xn_lo, dfth_ref[...]) + _dot(xn_hi, dftl_ref[...])
    y_hi, y_lo = _split_bf16(y)
    yh_ref[...] = y_hi
    yl_ref[...] = y_lo


def _odd_proj(x, mods, rope_tab, w_in_ext, q_norm, kv_norm, wqa, wqb, avg, dft_hi, dft_lo):
    def rope_index(i):
        return (jnp.where(i < T_P // TM2, 0, 1 + lax.rem(i - T_P // TM2, DEC_SEQ // TM2)), 0)

    return pl.pallas_call(
        _odd_proj_kernel,
        grid=(T // TM2,),
        in_specs=[
            pl.BlockSpec((TM2, D), lambda i: (i, 0)),
            pl.BlockSpec((None, 6, D), lambda i: (_cond_index(i, TM2), 0, 0)),
            pl.BlockSpec((TM2, 256), rope_index),
            _const_spec((D, W_IN_EXT)),
            _const_spec((1, Q_RANK)),
            _const_spec((1, KV_RANK)),
            _const_spec((Q_RANK, N_HEADS * QH)),
            _const_spec((Q_RANK, N_HEADS * 128)),
            _const_spec((FNET_DIM, FNET_DIM)),
            _const_spec((FNET_DIM, 2 * FNET_DIM)),
            _const_spec((FNET_DIM, 2 * FNET_DIM)),
        ],
        out_specs=[
            pl.BlockSpec((TM2, N_HEADS * QH), lambda i: (i, 0)),
            pl.BlockSpec((TM2, KV_RANK), lambda i: (i, 0)),
            pl.BlockSpec((TM2, 128), lambda i: (i, 0)),
            pl.BlockSpec((TM2, 2 * FNET_DIM), lambda i: (i, 0)),
            pl.BlockSpec((TM2, 2 * FNET_DIM), lambda i: (i, 0)),
        ],
        out_shape=[
            jax.ShapeDtypeStruct((T, N_HEADS * QH), BF16),
            jax.ShapeDtypeStruct((T, KV_RANK), F32),
            jax.ShapeDtypeStruct((T, 128), F32),
            jax.ShapeDtypeStruct((T, 2 * FNET_DIM), BF16),
            jax.ShapeDtypeStruct((T, 2 * FNET_DIM), BF16),
        ],
        compiler_params=_params(1),
        name="odd_projections",
    )(x, mods, rope_tab, w_in_ext, q_norm, kv_norm, wqa, wqb, avg, dft_hi, dft_lo)


def _kv_kernel(c_ref, w_ref, o_ref):
    o_ref[...] = _dot(c_ref[...].astype(BF16), w_ref[...]).astype(BF16)


def _kv_expand(ckv_all, w_kv):
    return pl.pallas_call(
        _kv_kernel,
        grid=(KV_ROWS // TM_KV,),
        in_specs=[pl.BlockSpec((TM_KV, KV_RANK), lambda i: (i, 0)),
                  _const_spec((KV_RANK, 2 * N_HEADS * 128))],
        out_specs=pl.BlockSpec((TM_KV, 2 * N_HEADS * 128), lambda i: (i, 0)),
        out_shape=jax.ShapeDtypeStruct((KV_ROWS, 2 * N_HEADS * 128), BF16),
        compiler_params=_params(1),
        name="kv_expand",
    )(ckv_all, w_kv)


def _attn_body(q_ref, kv_ref, kpe_ref, o_ref):
    kpe = kpe_ref[...].astype(BF16)
    for hd in range(N_HEADS):
        qh = q_ref[:, hd * QH:(hd + 1) * QH]
        kh = jnp.concatenate([kv_ref[:, hd * 128:(hd + 1) * 128], kpe], axis=1)
        s = lax.dot_general(qh, kh, (((1,), (1,)), ((), ())), preferred_element_type=F32)
        p = jnp.exp2(s - jnp.max(s, axis=-1, keepdims=True))
        den = jnp.sum(p, axis=-1, keepdims=True)
        vh = kv_ref[:, (N_HEADS + hd) * 128:(N_HEADS + hd + 1) * 128]
        o = _dot(p.astype(BF16), vh)
        o_ref[:, hd * 128:(hd + 1) * 128] = (o / den).astype(BF16)


def _attn_kernel(q_ref, kvp_ref, kpep_ref, kvs_ref, kpes_ref, o_ref):
    @pl.when(pl.program_id(0) < NP_TILES)
    def _():
        _attn_body(q_ref, kvp_ref, kpep_ref, o_ref)

    @pl.when(pl.program_id(0) >= NP_TILES)
    def _():
        _attn_body(q_ref, kvs_ref, kpes_ref, o_ref)


def _attention(q, kv, kpe_all):
    kv_p0 = DEC_BATCH * LK_S // SEQ
    ctx_blk = lambda i: (kv_p0 + jnp.minimum(i, NP_TILES - 1), 0)
    lat_blk = lambda i: (jnp.maximum(i - NP_TILES, 0) // TPS, 0)
    return pl.pallas_call(
        _attn_kernel,
        grid=(N_TILES,),
        in_specs=[
            pl.BlockSpec((TM, N_HEADS * QH), lambda i: (i, 0)),
            pl.BlockSpec((SEQ, 2 * N_HEADS * 128), ctx_blk),
            pl.BlockSpec((SEQ, 128), ctx_blk),
            pl.BlockSpec((LK_S, 2 * N_HEADS * 128), lat_blk),
            pl.BlockSpec((LK_S, 128), lat_blk),
        ],
        out_specs=pl.BlockSpec((TM, N_HEADS * V_DIM), lambda i: (i, 0)),
        out_shape=jax.ShapeDtypeStruct((T, N_HEADS * V_DIM), BF16),
        compiler_params=_params(1),
        name="attention",
    )(q, kv, kpe_all, kv, kpe_all)


def _pos_dft_body(yh_ref, yl_ref, ch, cl, sh, sl, o_ref):
    yc_h, ys_h = yh_ref[:, 0:FNET_DIM], yh_ref[:, FNET_DIM:]
    yc_l, ys_l = yl_ref[:, 0:FNET_DIM], yl_ref[:, FNET_DIM:]
    f = (_dot(ch, yc_h) + _dot(ch, yc_l) + _dot(cl, yc_h)
         + _dot(sh, ys_h) + _dot(sh, ys_l) + _dot(sl, ys_h))
    o_ref[...] = f.astype(BF16)


def _pos_dft_kernel(yhp_ref, ylp_ref, chp_ref, clp_ref, shp_ref, slp_ref,
                    yhs_ref, yls_ref, c0_ref, s0_ref, cb_ref, sb_ref, o_ref):
    @pl.when(pl.program_id(0) < NP_TILES)
    def _():
        _pos_dft_body(yhp_ref, ylp_ref, chp_ref[...], clp_ref[...], shp_ref[...], slp_ref[...], o_ref)

    @pl.when(pl.program_id(0) >= NP_TILES)
    def _():
        j = lax.rem(pl.program_id(0) - NP_TILES, TPS)
        cb, sb = cb_ref[pl.ds(j, 1), :], sb_ref[pl.ds(j, 1), :]
        c0, s0 = c0_ref[...], s0_ref[...]
        ch, cl = _split_bf16(c0 * cb - s0 * sb)
        sh, sl = _split_bf16(s0 * cb + c0 * sb)
        _pos_dft_body(yhs_ref, yls_ref, ch, cl, sh, sl, o_ref)


def _pos_dft(yh, yl, tabs_p, base_s, step_s):
    ctx_blk = lambda i: (jnp.minimum(i, NP_TILES - 1), 0)
    lat_seq = lambda i: (T_P // DEC_SEQ + jnp.maximum(i - NP_TILES, 0) // TPS, 0)
    return pl.pallas_call(
        _pos_dft_kernel,
        grid=(N_TILES,),
        in_specs=[pl.BlockSpec((SEQ, 2 * FNET_DIM), ctx_blk)] * 2
        + [_const_spec((SEQ, SEQ))] * 4
        + [pl.BlockSpec((DEC_SEQ, 2 * FNET_DIM), lat_seq)] * 2
        + [_const_spec((TM, DEC_SEQ))] * 2
        + [_const_spec((TPS, DEC_SEQ))] * 2,
        out_specs=pl.BlockSpec((TM, FNET_DIM), lambda i: (i, 0)),
        out_shape=jax.ShapeDtypeStruct((T, FNET_DIM), BF16),
        compiler_params=_params(1),
        name="pos_dft",
    )(yh, yl, *tabs_p, yh, yl, *base_s, *step_s)


def _odd_merge_kernel(x_ref, attn_ref, f_ref, mod_ref, fw_ref, wo_ref, g_ref, b_ref, rh_ref, rl_ref, tri_ref,
                      xo_ref, info_ref, infot_ref, cnt_ref, carry_ref):
    @pl.when(pl.program_id(0) == 0)
    def _():
        carry_ref[...] = jnp.zeros_like(carry_ref)

    gate = mod_ref[2:3, :]
    shift2, scale2 = mod_ref[3:4, :], mod_ref[4:5, :]
    fm = _dot(f_ref[...], fw_ref[...]).astype(BF16)
    y = _dot(attn_ref[...], wo_ref[0:N_HEADS * V_DIM, :]) + _dot(fm, wo_ref[N_HEADS * V_DIM:, :])
    x = _layer_norm(ALPHA * x_ref[...] + gate * y, g_ref[...], b_ref[...])
    xo_ref[...] = x
    h = x * (1.0 + scale2) + shift2

    h_hi, h_lo = _split_bf16(h)
    logits = _dot(h_hi, rh_ref[...]) + _dot(h_lo, rh_ref[...]) + _dot(h_hi, rl_ref[...])
    lane = lax.broadcasted_iota(jnp.int32, (TM2, 128), 1)
    neg = jnp.float32(-jnp.inf)
    logits = jnp.where(lane < N_EXPERTS, logits, neg)
    m1 = jnp.max(logits, axis=-1, keepdims=True)
    i1 = jnp.min(jnp.where(logits == m1, lane, 128), axis=-1, keepdims=True)
    rest = jnp.where(lane == i1, neg, logits)
    m2 = jnp.max(rest, axis=-1, keepdims=True)
    i2 = jnp.min(jnp.where(rest == m2, lane, 128), axis=-1, keepdims=True)
    e2 = jnp.exp(m2 - m1)
    w1 = 1.0 / (1.0 + e2)
    w2 = e2 / (1.0 + e2)
    info = jnp.where(lane == 0, w1, 0.0)
    info = jnp.where(lane == 1, w2, info)
    info = jnp.where(lane == 2, i1.astype(F32), info)
    info = jnp.where(lane == 3, i2.astype(F32), info)

    uses = jnp.logical_or(lane == i1, lane == i2)
    seen = _dot(tri_ref[...], jnp.where(uses, 1.0, 0.0).astype(BF16)) + carry_ref[...]
    r1 = jnp.sum(jnp.where(lane == i1, seen, 0.0), axis=-1, keepdims=True)
    r2 = jnp.sum(jnp.where(lane == i2, seen, 0.0), axis=-1, keepdims=True)
    info = jnp.where(lane == 4, r1, info)
    info = jnp.where(lane == 5, r2, info)
    info_ref[...] = info
    infot_ref[...] = info.T
    total = carry_ref[...] + jnp.sum(jnp.where(uses, 1.0, 0.0), axis=0, keepdims=True)
    carry_ref[...] = total
    cnt_ref[...] = jnp.broadcast_to(total, cnt_ref.shape)


def _odd_merge(x, attn, f, mods, fnet_w, w_out, g, b, r_hi, r_lo):
    row = lambda i: (i, 0)
    tri = jnp.asarray(np.tril(np.ones((TM2, TM2), np.float32), -1), BF16)
    return pl.pallas_call(
        _odd_merge_kernel,
        grid=(T // TM2,),
        in_specs=[
            pl.BlockSpec((TM2, D), row),
            pl.BlockSpec((TM2, N_HEADS * V_DIM), row),
            pl.BlockSpec((TM2, FNET_DIM), row),
            pl.BlockSpec((None, 6, D), lambda i: (_cond_index(i, TM2), 0, 0)),
            _const_spec((FNET_DIM, FNET_DIM)),
            _const_spec((N_HEADS * V_DIM + FNET_DIM, D)),
            _const_spec((1, D)),
            _const_spec((1, D)),
            _const_spec((D, 128)),
            _const_spec((D, 128)),
            _const_spec((TM2, TM2)),
        ],
        out_specs=[pl.BlockSpec((TM2, D), row), pl.BlockSpec((TM2, 128), row),
                   pl.BlockSpec((128, TM2), lambda i: (0, i)), pl.BlockSpec((8, 128), lambda i: (0, 0))],
        out_shape=[jax.ShapeDtypeStruct((T, D), F32), jax.ShapeDtypeStruct((T, 128), F32),
                   jax.ShapeDtypeStruct((128, T), F32), jax.ShapeDtypeStruct((8, 128), F32)],
        scratch_shapes=[pltpu.VMEM((1, 128), F32)],
        compiler_params=_params(1),
        name="odd_merge_router",
    )(x, attn, f, mods, fnet_w, w_out, g, b, r_hi, r_lo, tri)


TM_D = 512
ROW = (8, 128)
DMA_UNROLL = 8


def _row_copy(src, s, dst, d, sem):
    return pltpu.make_async_copy(src.at[s], dst.at[d], sem)


def _dispatch_kernel(dest_ref, pad_lo_ref, pad_hi_ref, x_ref, mod_ref, xs_ref, h_ref, zero_ref, sem):
    i = pl.program_id(0)
    base = i * TM_D
    shift2, scale2 = mod_ref[3:4, :], mod_ref[4:5, :]
    h_ref[...] = (x_ref[...] * (1.0 + scale2) + shift2).reshape((TM_D,) + ROW)

    def issue(r, carry):
        t = base + r
        _row_copy(h_ref, r, xs_ref, dest_ref[t], sem).start(priority=0)
        _row_copy(h_ref, r, xs_ref, dest_ref[T + t], sem).start(priority=1)
        return carry

    lax.fori_loop(0, TM_D, issue, 0, unroll=DMA_UNROLL)

    def drain(r, carry):
        _row_copy(h_ref, 0, xs_ref, 0, sem).wait()
        _row_copy(h_ref, 0, xs_ref, 0, sem).wait()
        return carry

    lax.fori_loop(0, TM_D, drain, 0, unroll=DMA_UNROLL)

    @pl.when(i == 0)
    def _():
        zero_ref[...] = jnp.zeros_like(zero_ref)
        for e in range(N_EXPERTS + 1):
            def zissue(r, carry):
                _row_copy(zero_ref, 0, xs_ref, r, sem).start()
                return carry

            def zdrain(r, carry):
                _row_copy(zero_ref, 0, xs_ref, 0, sem).wait()
                return carry

            lax.fori_loop(pad_lo_ref[e], pad_hi_ref[e], zissue, 0)
            lax.fori_loop(pad_lo_ref[e], pad_hi_ref[e], zdrain, 0)


def _dispatch(dest, pad_lo, pad_hi, x, mods):
    return pl.pallas_call(
        _dispatch_kernel,
        grid_spec=pltpu.PrefetchScalarGridSpec(
            num_scalar_prefetch=3,
            grid=(T // TM_D,),
            in_specs=[pl.BlockSpec((TM_D, D), lambda i, *_: (i, 0)),
                      pl.BlockSpec((None, 6, D), lambda i, *_: (_cond_index(i, TM_D), 0, 0))],
            out_specs=pl.BlockSpec(memory_space=pl.ANY),
            scratch_shapes=[pltpu.VMEM((TM_D,) + ROW, F32), pltpu.VMEM((1,) + ROW, F32),
                            pltpu.SemaphoreType.DMA(())],
        ),
        out_shape=jax.ShapeDtypeStruct((R_MAX,) + ROW, F32),
        compiler_params=pltpu.CompilerParams(dimension_semantics=("arbitrary",), has_side_effects=True),
        name="expert_dispatch",
    )(dest, pad_lo, pad_hi, x, mods)


def _expert_kernel(te_ref, nt_ref, xs_ref, wg_ref, wu_ref, wd_ref, o_ref):
    @pl.when(pl.program_id(0) < nt_ref[0])
    def _():
        h = xs_ref[...].reshape(TM_E, D).astype(BF16)
        a = _silu(_dot(h, wg_ref[...])) * _dot(h, wu_ref[...])
        o_ref[...] = _dot(a.astype(BF16), wd_ref[...]).reshape((TM_E,) + ROW)

    @pl.when(pl.program_id(0) >= nt_ref[0])
    def _():
        o_ref[...] = jnp.zeros_like(o_ref)


def _experts(tile_expert, n_used, xs, wg, wu, wd):
    return pl.pallas_call(
        _expert_kernel,
        grid_spec=pltpu.PrefetchScalarGridSpec(
            num_scalar_prefetch=2,
            grid=(N_ETILES,),
            in_specs=[
                pl.BlockSpec((TM_E,) + ROW, lambda i, te, nt: (jnp.minimum(i, nt[0] - 1), 0, 0)),
                pl.BlockSpec((None, D, D_FF_EXPERT), lambda i, te, nt: (te[i], 0, 0)),
                pl.BlockSpec((None, D, D_FF_EXPERT), lambda i, te, nt: (te[i], 0, 0)),
                pl.BlockSpec((None, D_FF_EXPERT, D), lambda i, te, nt: (te[i], 0, 0)),
            ],
            out_specs=pl.BlockSpec((TM_E,) + ROW, lambda i, te, nt: (i, 0, 0)),
        ),
        out_shape=jax.ShapeDtypeStruct((R_MAX,) + ROW, F32),
        compiler_params=_params(1),
        name="expert_swiglu",
    )(tile_expert, n_used, xs, wg, wu, wd)


def _combine_kernel(dest_ref, x_ref, info_ref, mod_ref, g_ref, b_ref, ys_ref, op_ref, os_ref, buf, sem):
    i = pl.program_id(0)
    slot = lax.rem(i, 2)

    def gather(tile, s):
        def issue(r, carry):
            t = tile * TM2 + r
            _row_copy(ys_ref, dest_ref[t], buf.at[s, 0], r, sem.at[s]).start(priority=0)
            _row_copy(ys_ref, dest_ref[T + t], buf.at[s, 1], r, sem.at[s]).start(priority=1)
            return carry

        lax.fori_loop(0, TM2, issue, 0, unroll=DMA_UNROLL)

    @pl.when(i == 0)
    def _():
        gather(0, 0)

    @pl.when(i + 1 < T // TM2)
    def _():
        gather(i + 1, 1 - slot)

    def drain(r, carry):
        _row_copy(ys_ref, 0, buf.at[slot, 0], 0, sem.at[slot]).wait()
        _row_copy(ys_ref, 0, buf.at[slot, 1], 0, sem.at[slot]).wait()
        return carry

    lax.fori_loop(0, TM2, drain, 0, unroll=DMA_UNROLL)

    gate = mod_ref[5:6, :]
    w1, w2 = info_ref[:, 0:1], info_ref[:, 1:2]
    y = w1 * buf[slot, 0].reshape(TM2, D) + w2 * buf[slot, 1].reshape(TM2, D)
    out = _layer_norm(ALPHA * x_ref[...] + gate * y, g_ref[...], b_ref[...])

    @pl.when(pl.program_id(0) < T_P // TM2)
    def _():
        op_ref[...] = out

    @pl.when(pl.program_id(0) >= T_P // TM2)
    def _():
        os_ref[...] = out


def _combine(dest, x, info, mods, g, b, ys):
    return pl.pallas_call(
        _combine_kernel,
        grid_spec=pltpu.PrefetchScalarGridSpec(
            num_scalar_prefetch=1,
            grid=(T // TM2,),
            in_specs=[
                pl.BlockSpec((TM2, D), lambda i, d: (i, 0)),
                pl.BlockSpec((TM2, 128), lambda i, d: (i, 0)),
                pl.BlockSpec((None, 6, D), lambda i, d: (_cond_index(i, TM2), 0, 0)),
                pl.BlockSpec((1, D), lambda i, d: (0, 0)),
                pl.BlockSpec((1, D), lambda i, d: (0, 0)),
                pl.BlockSpec(memory_space=pl.ANY),
            ],
            out_specs=[pl.BlockSpec((TM2, D), lambda i, d: (jnp.minimum(i, T_P // TM2 - 1), 0)),
                       pl.BlockSpec((TM2, D), lambda i, d: (jnp.maximum(i - T_P // TM2, 0), 0))],
            scratch_shapes=[pltpu.VMEM((2, 2, TM2) + ROW, F32), pltpu.SemaphoreType.DMA((2,))],
        ),
        out_shape=[jax.ShapeDtypeStruct((T_P, D), F32), jax.ShapeDtypeStruct((T_S, D), F32)],
        compiler_params=_params(1),
        name="expert_combine",
    )(dest, x, info, mods, g, b, ys)


def _rot_cols(w):
    w4 = w.reshape(w.shape[:-1] + (2, 2, QK_ROPE // 4))
    return jnp.stack([-w4[..., 1, :], w4[..., 0, :]], axis=-2).reshape(w.shape)


def _rope_table():
    rows = DEC_SEQ // GRID_W
    row = np.repeat(np.arange(rows), GRID_W).astype(np.float32)
    col = np.tile(np.arange(GRID_W), rows).astype(np.float32)
    half = QK_ROPE // 2
    inv = (ROPE_THETA ** (-np.arange(0, half, 2, dtype=np.float32) / half)).astype(np.float32)
    ar, ac = row[:, None] * inv, col[:, None] * inv
    ang = np.concatenate([ar, ar, ac, ac], axis=-1)
    cos = np.concatenate([np.ones((TM2, QK_ROPE)), np.cos(ang)], axis=0)
    sin = np.concatenate([np.zeros((TM2, QK_ROPE)), np.sin(ang)], axis=0)
    n = cos.shape[0]
    return jnp.asarray(np.concatenate([cos, np.ones((n, 64)), sin, np.zeros((n, 64))], axis=1), F32)


def _dft_angles(rows, n):
    k = np.arange(n, dtype=np.int64)
    return ((np.asarray(rows, np.int64)[:, None] * k[None, :]) % n) * (2.0 * np.pi / n)


def _dft_tables(n):
    ang = _dft_angles(np.arange(n), n)
    return np.cos(ang) * n ** -0.5, np.sin(ang) * n ** -0.5


def _hi_lo(m):
    m = jnp.asarray(m, F32)
    hi = m.astype(BF16)
    return hi, (m - hi.astype(F32)).astype(BF16)


def _block_diag4(m):
    return np.kron(np.eye(4), m)


def kernel(x_prompt, x_sample, cache_ckv, cache_kpe, c, c_ctx, ada_w, ada_b, ln_g, ln_b, ev_w_in, ev_conv_w, ev_pool_w, ev_pool_scale, ev_w_out, ffn_w_gate, ffn_w_up, ffn_w_down, od_w_in, od_q_norm, od_kv_norm, od_w_q_b, od_w_kv_b, od_fnet_w, od_w_out, moe_router, moe_w_gate, moe_w_up, moe_w_down):
    cond8 = jnp.concatenate([c_ctx[None, :], c, jnp.zeros((8 - N_COND, D), F32)], axis=0)
    mods = _modulation(cond8, ada_w, ada_b)[:, :N_COND].reshape(DEPTH, N_COND, 6, D)

    cast_srcs = [moe_w_gate[0].reshape(N_EXPERTS * D, D_FF_EXPERT), moe_w_up[0].reshape(N_EXPERTS * D, D_FF_EXPERT),
                 moe_w_down[0].reshape(N_EXPERTS * D_FF_EXPERT, D), ffn_w_gate[0], ffn_w_up[0], ffn_w_down[0]]
    x, (moe_wg, moe_wu, moe_wd, ffn_wg, ffn_wu, ffn_wd) = _even_mixer(
        x_prompt.reshape(T_P, D), x_sample.reshape(T_S, D), mods[0], ev_w_in[0].astype(BF16), ev_conv_w[0],
        ev_pool_w[0].astype(BF16), ev_pool_scale[0][None, :], ev_w_out[0].astype(BF16),
        ln_g[0, 0][None, :], ln_b[0, 0][None, :], cast_srcs)
    x = _ffn(x, mods[0], ffn_wg, ffn_wu, ffn_wd, ln_g[0, 1][None, :], ln_b[0, 1][None, :])

    w_in = od_w_in[0]
    w_pe = w_in[:, Q_RANK + KV_RANK:Q_RANK + KV_RANK + QK_ROPE]
    zpad = jnp.zeros((D, 64), F32)
    w_in_ext = jnp.concatenate([w_in[:, :Q_RANK + KV_RANK], w_in[:, Q_RANK + KV_RANK + QK_ROPE:],
                                w_pe, zpad, _rot_cols(w_pe), zpad], axis=1).astype(BF16)
    wq = od_w_q_b[0].reshape(Q_RANK, N_HEADS, QK_NOPE + QK_ROPE)
    zq = jnp.zeros((Q_RANK, N_HEADS, 64), F32)
    wqa = jnp.concatenate([wq, zq], axis=-1).reshape(Q_RANK, N_HEADS * QH).astype(BF16)
    wqb = jnp.concatenate([_rot_cols(wq[..., QK_NOPE:]), zq], axis=-1).reshape(Q_RANK, N_HEADS * 128).astype(BF16)
    wkv = od_w_kv_b[0].reshape(KV_RANK, N_HEADS, QK_NOPE + V_DIM)
    w_kv = jnp.concatenate([wkv[..., :QK_NOPE].reshape(KV_RANK, -1), wkv[..., QK_NOPE:].reshape(KV_RANK, -1)],
                           axis=1).astype(BF16)

    avg = jnp.asarray(_block_diag4(np.full((FNET_GROUP_DIM, FNET_GROUP_DIM), 1.0 / FNET_GROUP_DIM)), BF16)
    cc, sc = _dft_tables(FNET_GROUP_DIM)
    dft_hi, dft_lo = _hi_lo(np.concatenate([_block_diag4(cc), -_block_diag4(sc)], axis=1))

    q, ckv, kpe, yh, yl = _odd_proj(x, mods[1], _rope_table(), w_in_ext, od_q_norm[0][None, :],
                                    od_kv_norm[0][None, :], wqa, wqb, avg, dft_hi, dft_lo)

    cache_kpe128 = jnp.pad(cache_kpe[:, 0], ((0, 0), (0, 0), (0, 128 - QK_ROPE)))
    ckv_parts, kpe_parts = [], []
    for bi in range(DEC_BATCH):
        lat = slice(T_P + bi * DEC_SEQ, T_P + (bi + 1) * DEC_SEQ)
        ckv_parts += [cache_ckv[bi, 0], ckv[lat]]
        kpe_parts += [cache_kpe128[bi], kpe[lat]]
    ckv_all = jnp.concatenate(ckv_parts + [ckv[:T_P]], axis=0)
    kpe_all = jnp.concatenate(kpe_parts + [kpe[:T_P]], axis=0)
    kv = _kv_expand(ckv_all, w_kv)
    attn = _attention(q, kv, kpe_all)

    tabs_p = sum((_hi_lo(m) for m in _dft_tables(SEQ)), ())
    a_base = _dft_angles(np.arange(TM), DEC_SEQ)
    a_step = _dft_angles(np.arange(TPS) * TM, DEC_SEQ)
    base_s = (jnp.asarray(np.cos(a_base) * DEC_SEQ ** -0.5, F32), jnp.asarray(np.sin(a_base) * DEC_SEQ ** -0.5, F32))
    step_s = (jnp.asarray(np.cos(a_step), F32), jnp.asarray(np.sin(a_step), F32))
    f = _pos_dft(yh, yl, tabs_p, base_s, step_s)

    router = jnp.pad(moe_router[0], ((0, 0), (0, 128 - N_EXPERTS)))
    r_hi, r_lo = _hi_lo(router)
    x, info, info_t, cnt = _odd_merge(x, attn, f, mods[1], od_fnet_w[0].astype(BF16), od_w_out[0].astype(BF16),
                                      ln_g[1, 0][None, :], ln_b[1, 0][None, :], r_hi, r_lo)

    counts = cnt[0, :N_EXPERTS].astype(jnp.int32)
    padded = ((counts + TM_E - 1) // TM_E) * TM_E
    g_end = jnp.cumsum(padded)
    g_start = g_end - padded
    choice = info_t[2:4].astype(jnp.int32)
    rank = info_t[4:6].astype(jnp.int32)
    dest = rank
    for e in range(N_EXPERTS):
        dest = dest + jnp.where(choice == e, g_start[e], 0)
    dest = dest.reshape(-1).astype(jnp.int32)
    tile_row = jnp.arange(N_ETILES, dtype=jnp.int32) * TM_E
    tile_expert = jnp.minimum(jnp.sum((tile_row[:, None] >= g_end[None, :]).astype(jnp.int32), axis=1),
                              N_EXPERTS - 1).astype(jnp.int32)
    n_used = (g_end[-1:] // TM_E).astype(jnp.int32)

    pad_lo = jnp.concatenate([g_start + counts, g_end[-1:]]).astype(jnp.int32)
    pad_hi = jnp.concatenate([g_end, jnp.full((1,), R_MAX, jnp.int32)]).astype(jnp.int32)
    xs = _dispatch(dest, pad_lo, pad_hi, x, mods[1])
    ys = _experts(tile_expert, n_used, xs, moe_wg.reshape(N_EXPERTS, D, D_FF_EXPERT),
                  moe_wu.reshape(N_EXPERTS, D, D_FF_EXPERT), moe_wd.reshape(N_EXPERTS, D_FF_EXPERT, D))
    yp, ysm = _combine(dest, x, info, mods[1], ln_g[1, 1][None, :], ln_b[1, 1][None, :], ys)

    y_prompt = yp.reshape(BATCH, SEQ, D)
    y_sample = ysm.reshape(DEC_BATCH, DEC_SEQ, D)
    new_ckv = ckv[:T_P].reshape(BATCH, 1, SEQ, KV_RANK)
    new_kpe = kpe[:T_P, :QK_ROPE].reshape(BATCH, 1, SEQ, QK_ROPE)
    return (y_prompt, y_sample, new_ckv, new_kpe)
```

```python
import functools

import numpy as np
import jax
import jax.numpy as jnp
from jax import lax
from jax.experimental import pallas as pl
from jax.experimental.pallas import tpu as pltpu

F32 = jnp.float32
BF16 = jnp.bfloat16

D = 1024
BATCH, SEQ = 32, 256
DEC_BATCH, DEC_SEQ = 2, 2048
PAST = 512
GRID_W = 64
T_P = BATCH * SEQ
T_S = DEC_BATCH * DEC_SEQ
T = T_P + T_S
N_COND = 1 + DEC_BATCH

CONV_DIM = 512
POOL_WINDOWS = (2, 4, 8, 16)
POOL_GROUP = 128
N_HEADS = 8
QK_NOPE, QK_ROPE, V_DIM = 128, 64, 128
Q_RANK, KV_RANK = 384, 256
FNET_DIM, FNET_GROUP_DIM = 256, 64
D_FF = 2816
N_EXPERTS = 8
D_FF_EXPERT = 1792
DEPTH = 2
ALPHA = (2 * DEPTH) ** 0.25
LN_EPS = 1e-5
RMS_EPS = 1e-6
ROPE_THETA = 10000.0

TM = 256
NP_TILES = T_P // TM
TPS = DEC_SEQ // TM
N_TILES = T // TM
HALO = 8
TM2 = 512
TM_FFN = 512
LK_S = PAST + DEC_SEQ
TM_E = 256
R_MAX = 2 * T + N_EXPERTS * TM_E
N_ETILES = R_MAX // TM_E
VMEM_LIMIT = 56 * 1024 * 1024


def _cond_index(i, tm=TM):
    return jnp.where(i < T_P // tm, 0, 1 + (i - T_P // tm) // (DEC_SEQ // tm))


def _const_spec(shape):
    nd = len(shape)
    return pl.BlockSpec(shape, lambda *_: (0,) * nd, pipeline_mode=pl.Buffered(1))


def _params(n_axes=1, vmem=VMEM_LIMIT):
    return pltpu.CompilerParams(dimension_semantics=("arbitrary",) * n_axes, vmem_limit_bytes=vmem)


def _layer_norm(v, g, b):
    mu = jnp.mean(v, axis=-1, keepdims=True)
    d = v - mu
    var = jnp.mean(d * d, axis=-1, keepdims=True)
    return d * lax.rsqrt(var + LN_EPS) * g + b


def _split_bf16(v):
    hi = v.astype(BF16)
    lo = (v - hi.astype(F32)).astype(BF16)
    return hi, lo


def _dot(a, b):
    return jnp.dot(a, b, preferred_element_type=F32)


def _silu(v):
    return v / (1.0 + jnp.exp(-v))


def _mod_kernel(cond_ref, w_ref, b_ref, o_ref):
    s = _silu(cond_ref[...]).astype(BF16)
    o_ref[...] = _dot(s, w_ref[...].astype(BF16)) + b_ref[...]


def _modulation(cond8, ada_w, ada_b):
    nb = 6 * D // 1024
    return pl.pallas_call(
        _mod_kernel,
        grid=(DEPTH, nb),
        in_specs=[
            pl.BlockSpec((8, D), lambda l, j: (0, 0)),
            pl.BlockSpec((None, D, 1024), lambda l, j: (l, 0, j)),
            pl.BlockSpec((None, 1, 1024), lambda l, j: (l, 0, j)),
        ],
        out_specs=pl.BlockSpec((None, 8, 1024), lambda l, j: (l, 0, j)),
        out_shape=jax.ShapeDtypeStruct((DEPTH, 8, 6 * D), F32),
        compiler_params=_params(2),
        name="adaln_modulation",
    )(cond8, ada_w, ada_b.reshape(DEPTH, 1, 6 * D))


CAST_SPECS = (
    (N_EXPERTS * D, D_FF_EXPERT, 32),
    (N_EXPERTS * D, D_FF_EXPERT, 32),
    (N_EXPERTS * D_FF_EXPERT, D, 32),
    (D, D_FF, 32),
    (D, D_FF, 32),
    (D_FF, D, 16),
)
N_CAST = len(CAST_SPECS)


def _pool_tables():
    t = np.arange(TM)[:, None]
    r = np.arange(TM + 2 * HALO)[None, :]
    pos = r - HALO
    bands = np.zeros((4, len(POOL_WINDOWS), TM, TM + 2 * HALO), np.float32)
    inv = np.zeros((4, TM, 128), np.float32)
    for variant in range(4):
        left_ok, right_ok = variant & 1, variant >> 1
        col_ok = (r >= (0 if left_ok else HALO)) & (r < (TM + 2 * HALO if right_ok else TM + HALO))
        first, last = (-HALO if left_ok else 0), (TM + HALO if right_ok else TM)
        for gi, w in enumerate(POOL_WINDOWS):
            bands[variant, gi] = (pos >= t - w // 2) & (pos < t + w // 2) & col_ok
            cnt = np.minimum(t[:, 0] + w // 2, last) - np.maximum(t[:, 0] - w // 2, first)
            inv[variant, :, gi] = 1.0 / cnt
    return jnp.asarray(bands, BF16), jnp.asarray(inv, F32)


def _even_tiles(x_alls, left_oks, right_oks, mod_ref, win_ref, convw_ref, poolw_ref, pscale_ref, wout_ref,
                g_ref, b_ref, band_ref, inv_ref):
    shift, scale, gate = mod_ref[0:1, :], mod_ref[1:2, :], mod_ref[2:3, :]
    n = len(x_alls)
    hs = []
    for x_all, left_ok, right_ok in zip(x_alls, left_oks, right_oks):
        h = x_all * (1.0 + scale) + shift
        hs.append(jnp.concatenate([jnp.where(left_ok, h[:HALO], 0.0), h[HALO:HALO + TM],
                                   jnp.where(right_ok, h[HALO + TM:], 0.0)], axis=0).astype(BF16))
    us = [_dot(h, win_ref[...]) for h in hs]

    mixes = []
    for u, left_ok, right_ok in zip(us, left_oks, right_oks):
        ux, ub = u[:, 0:CONV_DIM], u[:, CONV_DIM:2 * CONV_DIM]
        uc, up = u[:, 2 * CONV_DIM:3 * CONV_DIM], u[:, 3 * CONV_DIM:]
        z = uc * ux
        conv = (z[HALO - 1:HALO - 1 + TM] * convw_ref[0:1, :]
                + z[HALO:HALO + TM] * convw_ref[1:2, :]
                + z[HALO + 1:HALO + 1 + TM] * convw_ref[2:3, :])
        ya = ub[HALO:HALO + TM] * conv

        variant = left_ok.astype(jnp.int32) + 2 * right_ok.astype(jnp.int32)
        inv_cnt = inv_ref[variant]
        up_hi, up_lo = _split_bf16(up)
        yb_groups = []
        for gi in range(len(POOL_WINDOWS)):
            band = band_ref[variant, gi]
            sl = slice(gi * POOL_GROUP, (gi + 1) * POOL_GROUP)
            tot = _dot(band, up_hi[:, sl]) + _dot(band, up_lo[:, sl])
            p = tot * inv_cnt[:, gi:gi + 1] - up[HALO:HALO + TM, sl]
            yb_groups.append(_dot(p.astype(BF16), poolw_ref[gi]))
        yb = jnp.concatenate(yb_groups, axis=1) * pscale_ref[...]
        mixes.append(jnp.concatenate([ya, yb], axis=1).astype(BF16))

    ys = [_dot(mix, wout_ref[...]) for mix in mixes]
    return [_layer_norm(ALPHA * x_alls[k][HALO:HALO + TM] + gate * ys[k], g_ref[...], b_ref[...])
            for k in range(n)]


EV_TILES = 2
EV_ROWS = EV_TILES * TM


def _even_mixer_kernel(xctx_ref, xprev_ref, xlat_ref, xnext_ref, mod_ref, win_ref, convw_ref, poolw_ref,
                       pscale_ref, wout_ref, g_ref, b_ref, band_ref, inv_ref, *rest):
    cast_in, o_ref, cast_out, xall_ref = rest[:N_CAST], rest[N_CAST], rest[N_CAST + 1:2 * N_CAST + 1], rest[-1]
    i = pl.program_id(0)

    for src, dst, (_, _, n_blocks) in zip(cast_in, cast_out, CAST_SPECS):
        @pl.when(i < n_blocks)
        def _(src=src, dst=dst):
            dst[...] = src[...].astype(BF16)

    @pl.when(lax.rem(i, EV_TILES) == 0)
    def _():
        s = i // EV_TILES
        n_ctx = T_P // EV_ROWS
        is_latent = s >= n_ctx
        first_tile = lax.rem(s - n_ctx, DEC_SEQ // EV_ROWS) * EV_TILES

        @pl.when(jnp.logical_not(is_latent))
        def _():
            for k in range(EV_TILES):
                xall_ref[k, 0:HALO, :] = jnp.zeros((HALO, D), F32)
                xall_ref[k, HALO:HALO + TM, :] = xctx_ref[k * TM:(k + 1) * TM, :]
                xall_ref[k, HALO + TM:, :] = jnp.zeros((HALO, D), F32)

        @pl.when(is_latent)
        def _():
            for k in range(EV_TILES):
                lo, hi = k * TM, (k + 1) * TM
                xall_ref[k, 0:HALO, :] = xprev_ref[...] if k == 0 else xlat_ref[lo - HALO:lo, :]
                xall_ref[k, HALO:HALO + TM, :] = xlat_ref[lo:hi, :]
                xall_ref[k, HALO + TM:, :] = xnext_ref[...] if k == EV_TILES - 1 else xlat_ref[hi:hi + HALO, :]

        left_oks = [jnp.logical_and(is_latent, first_tile + k != 0) for k in range(EV_TILES)]
        right_oks = [jnp.logical_and(is_latent, first_tile + k != TPS - 1) for k in range(EV_TILES)]
        outs = _even_tiles([xall_ref[k] for k in range(EV_TILES)], left_oks, right_oks, mod_ref, win_ref,
                           convw_ref, poolw_ref, pscale_ref, wout_ref, g_ref, b_ref, band_ref, inv_ref)
        for k in range(EV_TILES):
            o_ref[k * TM:(k + 1) * TM, :] = outs[k]


def _even_mixer(x_ctx, x_lat, mods, w_in, conv_w, pool_w, pool_scale, w_out, g, b, cast_srcs):
    hb = EV_ROWS // HALO
    n8 = T_S // HALO
    n_ctx = T_P // EV_ROWS
    sup = lambda i: i // EV_TILES
    lat = lambda i: jnp.maximum(sup(i) - n_ctx, 0)
    cast_specs = [pl.BlockSpec((r // nb, c), lambda i, nb=nb: (jnp.minimum(i, nb - 1), 0))
                  for r, c, nb in CAST_SPECS]
    bands, inv_cnt = _pool_tables()
    outs = pl.pallas_call(
        _even_mixer_kernel,
        grid=(N_TILES,),
        in_specs=[
            pl.BlockSpec((EV_ROWS, D), lambda i: (jnp.minimum(sup(i), n_ctx - 1), 0)),
            pl.BlockSpec((HALO, D), lambda i: (jnp.maximum(lat(i) * hb - 1, 0), 0)),
            pl.BlockSpec((EV_ROWS, D), lambda i: (lat(i), 0)),
            pl.BlockSpec((HALO, D), lambda i: (jnp.minimum((lat(i) + 1) * hb, n8 - 1), 0)),
            pl.BlockSpec((None, 6, D), lambda i: (_cond_index(sup(i), EV_ROWS), 0, 0)),
            _const_spec((D, 4 * CONV_DIM)),
            _const_spec((3, CONV_DIM)),
            _const_spec((4, POOL_GROUP, POOL_GROUP)),
            _const_spec((1, 4 * POOL_GROUP)),
            _const_spec((D, D)),
            _const_spec((1, D)),
            _const_spec((1, D)),
            _const_spec(bands.shape),
            _const_spec(inv_cnt.shape),
        ] + cast_specs,
        out_specs=[pl.BlockSpec((EV_ROWS, D), lambda i: (sup(i), 0))] + cast_specs,
        out_shape=[jax.ShapeDtypeStruct((T, D), F32)]
        + [jax.ShapeDtypeStruct((r, c), BF16) for r, c, _ in CAST_SPECS],
        scratch_shapes=[pltpu.VMEM((EV_TILES, TM + 2 * HALO, D), F32)],
        compiler_params=_params(1),
        name="even_mixer",
    )(x_ctx, x_lat, x_lat, x_lat, mods, w_in, conv_w, pool_w, pool_scale, w_out, g, b, bands, inv_cnt, *cast_srcs)
    return outs[0], outs[1:]


FF_CHUNK = D_FF // 2


def _ffn_kernel(x_ref, mod_ref, wg_ref, wu_ref, wd_ref, g_ref, b_ref, o_ref):
    shift, scale, gate = mod_ref[3:4, :], mod_ref[4:5, :], mod_ref[5:6, :]
    x = x_ref[...]
    h = (x * (1.0 + scale) + shift).astype(BF16)
    f = jnp.zeros((TM_FFN, D), F32)
    for c in range(D_FF // FF_CHUNK):
        sl = slice(c * FF_CHUNK, (c + 1) * FF_CHUNK)
        a = _silu(_dot(h, wg_ref[:, sl])) * _dot(h, wu_ref[:, sl])
        f = f + _dot(a.astype(BF16), wd_ref[sl, :])
    o_ref[...] = _layer_norm(ALPHA * x + gate * f, g_ref[...], b_ref[...])


def _ffn(x, mods, wg, wu, wd, g, b):
    return pl.pallas_call(
        _ffn_kernel,
        grid=(T // TM_FFN,),
        in_specs=[
            pl.BlockSpec((TM_FFN, D), lambda i: (i, 0)),
            pl.BlockSpec((None, 6, D), lambda i: (_cond_index(i, TM_FFN), 0, 0)),
            _const_spec((D, D_FF)),
            _const_spec((D, D_FF)),
            _const_spec((D_FF, D)),
            _const_spec((1, D)),
            _const_spec((1, D)),
        ],
        out_specs=pl.BlockSpec((TM_FFN, D), lambda i: (i, 0)),
        out_shape=jax.ShapeDtypeStruct((T, D), F32),
        compiler_params=_params(1),
        name="dense_swiglu",
    )(x, mods, wg, wu, wd, g, b)


W_IN_EXT = Q_RANK + KV_RANK + FNET_DIM + 128 + 128
QH = 256
ATT_SCALE = (QK_NOPE + QK_ROPE) ** -0.5 * float(np.log2(np.e))


def _odd_proj_kernel(x_ref, mod_ref, rope_ref, win_ref, qn_ref, kvn_ref, wqa_ref, wqb_ref,
                     avg_ref, dfth_ref, dftl_ref,
                     q_ref, ckv_ref, kpe_ref, yh_ref, yl_ref):
    shift, scale = mod_ref[0:1, :], mod_ref[1:2, :]
    h = (x_ref[...] * (1.0 + scale) + shift).astype(BF16)
    u = _dot(h, win_ref[...])
    uq = u[:, 0:Q_RANK]
    ukv = u[:, Q_RANK:Q_RANK + KV_RANK]
    uf = u[:, Q_RANK + KV_RANK:Q_RANK + KV_RANK + FNET_DIM]
    o = Q_RANK + KV_RANK + FNET_DIM
    upe, upe_rot = u[:, o:o + 128], u[:, o + 128:o + 256]
    cos, sin = rope_ref[:, 0:128], rope_ref[:, 128:256]

    ckv_ref[...] = ukv * lax.rsqrt(jnp.mean(ukv * ukv, axis=-1, keepdims=True) + RMS_EPS) * kvn_ref[...]
    kpe_ref[...] = upe * cos + upe_rot * sin

    qlat = (uq * lax.rsqrt(jnp.mean(uq * uq, axis=-1, keepdims=True) + RMS_EPS) * qn_ref[...]).astype(BF16)
    qa = _dot(qlat, wqa_ref[...])
    qb = _dot(qlat, wqb_ref[...])
    for hd in range(N_HEADS):
        nope = qa[:, hd * QH:hd * QH + 128]
        pe = qa[:, hd * QH + 128:(hd + 1) * QH] * cos + qb[:, hd * 128:(hd + 1) * 128] * sin
        q_ref[:, hd * QH:hd * QH + 128] = (nope * ATT_SCALE).astype(BF16)
        q_ref[:, hd * QH + 128:(hd + 1) * QH] = (pe * ATT_SCALE).astype(BF16)

    avg = avg_ref[...]
    uf_hi, uf_lo = _split_bf16(uf)
    mu = _dot(uf_hi, avg) + _dot(uf_lo, avg)
    dlt = uf - mu
    sq_hi, sq_lo = _split_bf16(dlt * dlt)
    var = _dot(sq_hi, avg) + _dot(sq_lo, avg)
    xn = dlt * lax.rsqrt(var + LN_EPS)
    xn_hi, xn_lo = _split_bf16(xn)
    y = _dot(xn_hi, dfth_ref[...]) + _dot(xn_lo, dfth_ref[...]) + _dot(xn_hi, dftl_ref[...])
    y_hi, y_lo = _split_bf16(y)
    yh_ref[...] = y_hi
    yl_ref[...] = y_lo


def _odd_proj(x, mods, rope_tab, w_in_ext, q_norm, kv_norm, wqa, wqb, avg, dft_hi, dft_lo):
    def rope_index(i):
        return (jnp.where(i < T_P // TM2, 0, 1 + lax.rem(i - T_P // TM2, DEC_SEQ // TM2)), 0)

    return pl.pallas_call(
        _odd_proj_kernel,
        grid=(T // TM2,),
        in_specs=[
            pl.BlockSpec((TM2, D), lambda i: (i, 0)),
            pl.BlockSpec((None, 6, D), lambda i: (_cond_index(i, TM2), 0, 0)),
            pl.BlockSpec((TM2, 256), rope_index),
            _const_spec((D, W_IN_EXT)),
            _const_spec((1, Q_RANK)),
            _const_spec((1, KV_RANK)),
            _const_spec((Q_RANK, N_HEADS * QH)),
            _const_spec((Q_RANK, N_HEADS * 128)),
            _const_spec((FNET_DIM, FNET_DIM)),
            _const_spec((FNET_DIM, 2 * FNET_DIM)),
            _const_spec((FNET_DIM, 2 * FNET_DIM)),
        ],
        out_specs=[
            pl.BlockSpec((TM2, N_HEADS * QH), lambda i: (i, 0)),
            pl.BlockSpec((TM2, KV_RANK), lambda i: (i, 0)),
            pl.BlockSpec((TM2, 128), lambda i: (i, 0)),
            pl.BlockSpec((TM2, 2 * FNET_DIM), lambda i: (i, 0)),
            pl.BlockSpec((TM2, 2 * FNET_DIM), lambda i: (i, 0)),
        ],
        out_shape=[
            jax.ShapeDtypeStruct((T, N_HEADS * QH), BF16),
            jax.ShapeDtypeStruct((T, KV_RANK), F32),
            jax.ShapeDtypeStruct((T, 128), F32),
            jax.ShapeDtypeStruct((T, 2 * FNET_DIM), BF16),
            jax.ShapeDtypeStruct((T, 2 * FNET_DIM), BF16),
        ],
        compiler_params=_params(1),
        name="odd_projections",
    )(x, mods, rope_tab, w_in_ext, q_norm, kv_norm, wqa, wqb, avg, dft_hi, dft_lo)


def _attn_body(q_ref, kv_ref, kpe_ref, o_ref):
    kpe = kpe_ref[...].astype(BF16)
    for hd in range(N_HEADS):
        qh = q_ref[:, hd * QH:(hd + 1) * QH]
        kh = jnp.concatenate([kv_ref[:, hd * 128:(hd + 1) * 128], kpe], axis=1)
        s = lax.dot_general(qh, kh, (((1,), (1,)), ((), ())), preferred_element_type=F32)
        p = jnp.exp2(s - jnp.max(s, axis=-1, keepdims=True))
        den = jnp.sum(p, axis=-1, keepdims=True)
        vh = kv_ref[:, (N_HEADS + hd) * 128:(N_HEADS + hd + 1) * 128]
        o = _dot(p.astype(BF16), vh)
        o_ref[:, hd * 128:(hd + 1) * 128] = (o / den).astype(BF16)


KV_CHUNK = 512


def _attn_kernel(q_ref, ckvp_ref, kpep_ref, ckvs_ref, kpes_ref, cckv_ref, ckpe_ref, wkv_ref, o_ref,
                 kvp_s, kpep_s, kvs_s, kpes_s):
    i = pl.program_id(0)

    def expand(ckv):
        return _dot(ckv.astype(BF16), wkv_ref[...]).astype(BF16)

    @pl.when(i < NP_TILES)
    def _():
        kvp_s[...] = expand(ckvp_ref[...])
        kpep_s[...] = kpep_ref[...].astype(BF16)
        _attn_body(q_ref, kvp_s, kpep_s, o_ref)

    @pl.when(i >= NP_TILES)
    def _():
        @pl.when(lax.rem(i - NP_TILES, TPS) == 0)
        def _():
            kvs_s[0:PAST, :] = expand(cckv_ref[...])
            kpes_s[0:PAST, :] = ckpe_ref[...].astype(BF16)
            kpes_s[PAST:, :] = kpes_ref[...].astype(BF16)
            for c in range(DEC_SEQ // KV_CHUNK):
                rows = slice(c * KV_CHUNK, (c + 1) * KV_CHUNK)
                kvs_s[PAST + c * KV_CHUNK:PAST + (c + 1) * KV_CHUNK, :] = expand(ckvs_ref[rows, :])

        _attn_body(q_ref, kvs_s, kpes_s, o_ref)


def _attention(q, ckv, kpe, cache_ckv, cache_kpe128, w_kv):
    ctx_blk = lambda i: (jnp.minimum(i, NP_TILES - 1), 0)
    lat_b = lambda i: jnp.maximum(i - NP_TILES, 0) // TPS
    lat_blk = lambda i: (T_P // DEC_SEQ + lat_b(i), 0)
    return pl.pallas_call(
        _attn_kernel,
        grid=(N_TILES,),
        in_specs=[
            pl.BlockSpec((TM, N_HEADS * QH), lambda i: (i, 0)),
            pl.BlockSpec((SEQ, KV_RANK), ctx_blk),
            pl.BlockSpec((SEQ, 128), ctx_blk),
            pl.BlockSpec((DEC_SEQ, KV_RANK), lat_blk),
            pl.BlockSpec((DEC_SEQ, 128), lat_blk),
            pl.BlockSpec((None, PAST, KV_RANK), lambda i: (lat_b(i), 0, 0)),
            pl.BlockSpec((None, PAST, 128), lambda i: (lat_b(i), 0, 0)),
            _const_spec((KV_RANK, 2 * N_HEADS * 128)),
        ],
        out_specs=pl.BlockSpec((TM, N_HEADS * V_DIM), lambda i: (i, 0)),
        out_shape=jax.ShapeDtypeStruct((T, N_HEADS * V_DIM), BF16),
        scratch_shapes=[pltpu.VMEM((SEQ, 2 * N_HEADS * 128), BF16), pltpu.VMEM((SEQ, 128), BF16),
                        pltpu.VMEM((LK_S, 2 * N_HEADS * 128), BF16), pltpu.VMEM((LK_S, 128), BF16)],
        compiler_params=_params(1),
        name="attention",
    )(q, ckv, kpe, ckv, kpe, cache_ckv, cache_kpe128, w_kv)


def _pos_dft_body(yh_ref, yl_ref, ch, cl, sh, sl, o_ref):
    yc_h, ys_h = yh_ref[:, 0:FNET_DIM], yh_ref[:, FNET_DIM:]
    yc_l, ys_l = yl_ref[:, 0:FNET_DIM], yl_ref[:, FNET_DIM:]
    f = (_dot(ch, yc_h) + _dot(ch, yc_l) + _dot(cl, yc_h)
         + _dot(sh, ys_h) + _dot(sh, ys_l) + _dot(sl, ys_h))
    o_ref[...] = f.astype(BF16)


def _pos_dft_kernel(yhp_ref, ylp_ref, chp_ref, clp_ref, shp_ref, slp_ref,
                    yhs_ref, yls_ref, c0_ref, s0_ref, cb_ref, sb_ref, o_ref):
    @pl.when(pl.program_id(0) < NP_TILES)
    def _():
        _pos_dft_body(yhp_ref, ylp_ref, chp_ref[...], clp_ref[...], shp_ref[...], slp_ref[...], o_ref)

    @pl.when(pl.program_id(0) >= NP_TILES)
    def _():
        j = lax.rem(pl.program_id(0) - NP_TILES, TPS)
        cb, sb = cb_ref[pl.ds(j, 1), :], sb_ref[pl.ds(j, 1), :]
        c0, s0 = c0_ref[...], s0_ref[...]
        ch, cl = _split_bf16(c0 * cb - s0 * sb)
        sh, sl = _split_bf16(s0 * cb + c0 * sb)
        _pos_dft_body(yhs_ref, yls_ref, ch, cl, sh, sl, o_ref)


def _pos_dft(yh, yl, tabs_p, base_s, step_s):
    ctx_blk = lambda i: (jnp.minimum(i, NP_TILES - 1), 0)
    lat_seq = lambda i: (T_P // DEC_SEQ + jnp.maximum(i - NP_TILES, 0) // TPS, 0)
    return pl.pallas_call(
        _pos_dft_kernel,
        grid=(N_TILES,),
        in_specs=[pl.BlockSpec((SEQ, 2 * FNET_DIM), ctx_blk)] * 2
        + [_const_spec((SEQ, SEQ))] * 4
        + [pl.BlockSpec((DEC_SEQ, 2 * FNET_DIM), lat_seq)] * 2
        + [_const_spec((TM, DEC_SEQ))] * 2
        + [_const_spec((TPS, DEC_SEQ))] * 2,
        out_specs=pl.BlockSpec((TM, FNET_DIM), lambda i: (i, 0)),
        out_shape=jax.ShapeDtypeStruct((T, FNET_DIM), BF16),
        compiler_params=_params(1),
        name="pos_dft",
    )(yh, yl, *tabs_p, yh, yl, *base_s, *step_s)


def _odd_merge_kernel(x_ref, attn_ref, f_ref, mod_ref, fw_ref, wo_ref, g_ref, b_ref, rh_ref, rl_ref, tri_ref,
                      xo_ref, info_ref, infot_ref, cnt_ref, carry_ref):
    @pl.when(pl.program_id(0) == 0)
    def _():
        carry_ref[...] = jnp.zeros_like(carry_ref)

    gate = mod_ref[2:3, :]
    shift2, scale2 = mod_ref[3:4, :], mod_ref[4:5, :]
    fm = _dot(f_ref[...], fw_ref[...]).astype(BF16)
    y = _dot(attn_ref[...], wo_ref[0:N_HEADS * V_DIM, :]) + _dot(fm, wo_ref[N_HEADS * V_DIM:, :])
    x = _layer_norm(ALPHA * x_ref[...] + gate * y, g_ref[...], b_ref[...])
    xo_ref[...] = x
    h = x * (1.0 + scale2) + shift2

    h_hi, h_lo = _split_bf16(h)
    logits = _dot(h_hi, rh_ref[...]) + _dot(h_lo, rh_ref[...]) + _dot(h_hi, rl_ref[...])
    lane = lax.broadcasted_iota(jnp.int32, (TM2, 128), 1)
    neg = jnp.float32(-jnp.inf)
    logits = jnp.where(lane < N_EXPERTS, logits, neg)
    m1 = jnp.max(logits, axis=-1, keepdims=True)
    i1 = jnp.min(jnp.where(logits == m1, lane, 128), axis=-1, keepdims=True)
    rest = jnp.where(lane == i1, neg, logits)
    m2 = jnp.max(rest, axis=-1, keepdims=True)
    i2 = jnp.min(jnp.where(rest == m2, lane, 128), axis=-1, keepdims=True)
    e2 = jnp.exp(m2 - m1)
    w1 = 1.0 / (1.0 + e2)
    w2 = e2 / (1.0 + e2)
    info = jnp.where(lane == 0, w1, 0.0)
    info = jnp.where(lane == 1, w2, info)
    info = jnp.where(lane == 2, i1.astype(F32), info)
    info = jnp.where(lane == 3, i2.astype(F32), info)

    uses = jnp.logical_or(lane == i1, lane == i2)
    seen = _dot(tri_ref[...], jnp.where(uses, 1.0, 0.0).astype(BF16)) + carry_ref[...]
    r1 = jnp.sum(jnp.where(lane == i1, seen, 0.0), axis=-1, keepdims=True)
    r2 = jnp.sum(jnp.where(lane == i2, seen, 0.0), axis=-1, keepdims=True)
    info = jnp.where(lane == 4, r1, info)
    info = jnp.where(lane == 5, r2, info)
    info_ref[...] = info
    infot_ref[...] = info.T
    total = carry_ref[...] + jnp.sum(jnp.where(uses, 1.0, 0.0), axis=0, keepdims=True)
    carry_ref[...] = total
    cnt_ref[...] = jnp.broadcast_to(total, cnt_ref.shape)


def _odd_merge(x, attn, f, mods, fnet_w, w_out, g, b, r_hi, r_lo):
    row = lambda i: (i, 0)
    tri = jnp.asarray(np.tril(np.ones((TM2, TM2), np.float32), -1), BF16)
    return pl.pallas_call(
        _odd_merge_kernel,
        grid=(T // TM2,),
        in_specs=[
            pl.BlockSpec((TM2, D), row),
            pl.BlockSpec((TM2, N_HEADS * V_DIM), row),
            pl.BlockSpec((TM2, FNET_DIM), row),
            pl.BlockSpec((None, 6, D), lambda i: (_cond_index(i, TM2), 0, 0)),
            _const_spec((FNET_DIM, FNET_DIM)),
            _const_spec((N_HEADS * V_DIM + FNET_DIM, D)),
            _const_spec((1, D)),
            _const_spec((1, D)),
            _const_spec((D, 128)),
            _const_spec((D, 128)),
            _const_spec((TM2, TM2)),
        ],
        out_specs=[pl.BlockSpec((TM2, D), row), pl.BlockSpec((TM2, 128), row),
                   pl.BlockSpec((128, TM2), lambda i: (0, i)), pl.BlockSpec((8, 128), lambda i: (0, 0))],
        out_shape=[jax.ShapeDtypeStruct((T, D), F32), jax.ShapeDtypeStruct((T, 128), F32),
                   jax.ShapeDtypeStruct((128, T), F32), jax.ShapeDtypeStruct((8, 128), F32)],
        scratch_shapes=[pltpu.VMEM((1, 128), F32)],
        compiler_params=_params(1),
        name="odd_merge_router",
    )(x, attn, f, mods, fnet_w, w_out, g, b, r_hi, r_lo, tri)


TM_D = 512
ROW = (8, 128)
DMA_UNROLL = 8


def _row_copy(src, s, dst, d, sem):
    return pltpu.make_async_copy(src.at[s], dst.at[d], sem)


def _dispatch_kernel(dest_ref, pad_lo_ref, pad_hi_ref, x_ref, mod_ref, xs_ref, h_ref, zero_ref, sem):
    i = pl.program_id(0)
    base = i * TM_D
    shift2, scale2 = mod_ref[3:4, :], mod_ref[4:5, :]
    h_ref[...] = (x_ref[...] * (1.0 + scale2) + shift2).reshape((TM_D,) + ROW)

    def issue(r, carry):
        t = base + r
        _row_copy(h_ref, r, xs_ref, dest_ref[t], sem).start(priority=0)
        _row_copy(h_ref, r, xs_ref, dest_ref[T + t], sem).start(priority=1)
        return carry

    lax.fori_loop(0, TM_D, issue, 0, unroll=DMA_UNROLL)

    def drain(r, carry):
        _row_copy(h_ref, 0, xs_ref, 0, sem).wait()
        _row_copy(h_ref, 0, xs_ref, 0, sem).wait()
        return carry

    lax.fori_loop(0, TM_D, drain, 0, unroll=DMA_UNROLL)

    @pl.when(i == 0)
    def _():
        zero_ref[...] = jnp.zeros_like(zero_ref)
        for e in range(N_EXPERTS + 1):
            def zissue(r, carry):
                _row_copy(zero_ref, 0, xs_ref, r, sem).start()
                return carry

            def zdrain(r, carry):
                _row_copy(zero_ref, 0, xs_ref, 0, sem).wait()
                return carry

            lax.fori_loop(pad_lo_ref[e], pad_hi_ref[e], zissue, 0)
            lax.fori_loop(pad_lo_ref[e], pad_hi_ref[e], zdrain, 0)


def _dispatch(dest, pad_lo, pad_hi, x, mods):
    return pl.pallas_call(
        _dispatch_kernel,
        grid_spec=pltpu.PrefetchScalarGridSpec(
            num_scalar_prefetch=3,
            grid=(T // TM_D,),
            in_specs=[pl.BlockSpec((TM_D, D), lambda i, *_: (i, 0)),
                      pl.BlockSpec((None, 6, D), lambda i, *_: (_cond_index(i, TM_D), 0, 0))],
            out_specs=pl.BlockSpec(memory_space=pl.ANY),
            scratch_shapes=[pltpu.VMEM((TM_D,) + ROW, F32), pltpu.VMEM((1,) + ROW, F32),
                            pltpu.SemaphoreType.DMA(())],
        ),
        out_shape=jax.ShapeDtypeStruct((R_MAX,) + ROW, F32),
        compiler_params=pltpu.CompilerParams(dimension_semantics=("arbitrary",), has_side_effects=True),
        name="expert_dispatch",
    )(dest, pad_lo, pad_hi, x, mods)


def _expert_kernel(te_ref, nt_ref, xs_ref, wg_ref, wu_ref, wd_ref, o_ref):
    @pl.when(pl.program_id(0) < nt_ref[0])
    def _():
        h = xs_ref[...].reshape(TM_E, D).astype(BF16)
        a = _silu(_dot(h, wg_ref[...])) * _dot(h, wu_ref[...])
        o_ref[...] = _dot(a.astype(BF16), wd_ref[...]).reshape((TM_E,) + ROW)

    @pl.when(pl.program_id(0) >= nt_ref[0])
    def _():
        o_ref[...] = jnp.zeros_like(o_ref)


def _experts(tile_expert, n_used, xs, wg, wu, wd):
    return pl.pallas_call(
        _expert_kernel,
        grid_spec=pltpu.PrefetchScalarGridSpec(
            num_scalar_prefetch=2,
            grid=(N_ETILES,),
            in_specs=[
                pl.BlockSpec((TM_E,) + ROW, lambda i, te, nt: (jnp.minimum(i, nt[0] - 1), 0, 0)),
                pl.BlockSpec((None, D, D_FF_EXPERT), lambda i, te, nt: (te[i], 0, 0)),
                pl.BlockSpec((None, D, D_FF_EXPERT), lambda i, te, nt: (te[i], 0, 0)),
                pl.BlockSpec((None, D_FF_EXPERT, D), lambda i, te, nt: (te[i], 0, 0)),
            ],
            out_specs=pl.BlockSpec((TM_E,) + ROW, lambda i, te, nt: (i, 0, 0)),
        ),
        out_shape=jax.ShapeDtypeStruct((R_MAX,) + ROW, F32),
        compiler_params=_params(1),
        name="expert_swiglu",
    )(tile_expert, n_used, xs, wg, wu, wd)


def _combine_kernel(dest_ref, x_ref, info_ref, mod_ref, g_ref, b_ref, ys_ref, op_ref, os_ref, buf, sem):
    i = pl.program_id(0)
    slot = lax.rem(i, 2)

    def gather(tile, s):
        def issue(r, carry):
            t = tile * TM2 + r
            _row_copy(ys_ref, dest_ref[t], buf.at[s, 0], r, sem.at[s]).start(priority=0)
            _row_copy(ys_ref, dest_ref[T + t], buf.at[s, 1], r, sem.at[s]).start(priority=1)
            return carry

        lax.fori_loop(0, TM2, issue, 0, unroll=DMA_UNROLL)

    @pl.when(i == 0)
    def _():
        gather(0, 0)

    @pl.when(i + 1 < T // TM2)
    def _():
        gather(i + 1, 1 - slot)

    def drain(r, carry):
        _row_copy(ys_ref, 0, buf.at[slot, 0], 0, sem.at[slot]).wait()
        _row_copy(ys_ref, 0, buf.at[slot, 1], 0, sem.at[slot]).wait()
        return carry

    lax.fori_loop(0, TM2, drain, 0, unroll=DMA_UNROLL)

    gate = mod_ref[5:6, :]
    w1, w2 = info_ref[:, 0:1], info_ref[:, 1:2]
    y = w1 * buf[slot, 0].reshape(TM2, D) + w2 * buf[slot, 1].reshape(TM2, D)
    out = _layer_norm(ALPHA * x_ref[...] + gate * y, g_ref[...], b_ref[...])

    @pl.when(pl.program_id(0) < T_P // TM2)
    def _():
        op_ref[...] = out

    @pl.when(pl.program_id(0) >= T_P // TM2)
    def _():
        os_ref[...] = out


def _combine(dest, x, info, mods, g, b, ys):
    return pl.pallas_call(
        _combine_kernel,
        grid_spec=pltpu.PrefetchScalarGridSpec(
            num_scalar_prefetch=1,
            grid=(T // TM2,),
            in_specs=[
                pl.BlockSpec((TM2, D), lambda i, d: (i, 0)),
                pl.BlockSpec((TM2, 128), lambda i, d: (i, 0)),
                pl.BlockSpec((None, 6, D), lambda i, d: (_cond_index(i, TM2), 0, 0)),
                pl.BlockSpec((1, D), lambda i, d: (0, 0)),
                pl.BlockSpec((1, D), lambda i, d: (0, 0)),
                pl.BlockSpec(memory_space=pl.ANY),
            ],
            out_specs=[pl.BlockSpec((TM2, D), lambda i, d: (jnp.minimum(i, T_P // TM2 - 1), 0)),
                       pl.BlockSpec((TM2, D), lambda i, d: (jnp.maximum(i - T_P // TM2, 0), 0))],
            scratch_shapes=[pltpu.VMEM((2, 2, TM2) + ROW, F32), pltpu.SemaphoreType.DMA((2,))],
        ),
        out_shape=[jax.ShapeDtypeStruct((T_P, D), F32), jax.ShapeDtypeStruct((T_S, D), F32)],
        compiler_params=_params(1),
        name="expert_combine",
    )(dest, x, info, mods, g, b, ys)


def _rot_cols(w):
    w4 = w.reshape(w.shape[:-1] + (2, 2, QK_ROPE // 4))
    return jnp.stack([-w4[..., 1, :], w4[..., 0, :]], axis=-2).reshape(w.shape)


def _rope_table():
    rows = DEC_SEQ // GRID_W
    row = np.repeat(np.arange(rows), GRID_W).astype(np.float32)
    col = np.tile(np.arange(GRID_W), rows).astype(np.float32)
    half = QK_ROPE // 2
    inv = (ROPE_THETA ** (-np.arange(0, half, 2, dtype=np.float32) / half)).astype(np.float32)
    ar, ac = row[:, None] * inv, col[:, None] * inv
    ang = np.concatenate([ar, ar, ac, ac], axis=-1)
    cos = np.concatenate([np.ones((TM2, QK_ROPE)), np.cos(ang)], axis=0)
    sin = np.concatenate([np.zeros((TM2, QK_ROPE)), np.sin(ang)], axis=0)
    n = cos.shape[0]
    return jnp.asarray(np.concatenate([cos, np.ones((n, 64)), sin, np.zeros((n, 64))], axis=1), F32)


def _dft_angles(rows, n):
    k = np.arange(n, dtype=np.int64)
    return ((np.asarray(rows, np.int64)[:, None] * k[None, :]) % n) * (2.0 * np.pi / n)


def _dft_tables(n):
    ang = _dft_angles(np.arange(n), n)
    return np.cos(ang) * n ** -0.5, np.sin(ang) * n ** -0.5


def _hi_lo(m):
    m = jnp.asarray(m, F32)
    hi = m.astype(BF16)
    return hi, (m - hi.astype(F32)).astype(BF16)


def _block_diag4(m):
    return np.kron(np.eye(4), m)


def kernel(x_prompt, x_sample, cache_ckv, cache_kpe, c, c_ctx, ada_w, ada_b, ln_g, ln_b, ev_w_in, ev_conv_w, ev_pool_w, ev_pool_scale, ev_w_out, ffn_w_gate, ffn_w_up, ffn_w_down, od_w_in, od_q_norm, od_kv_norm, od_w_q_b, od_w_kv_b, od_fnet_w, od_w_out, moe_router, moe_w_gate, moe_w_up, moe_w_down):
    cond8 = jnp.concatenate([c_ctx[None, :], c, jnp.zeros((8 - N_COND, D), F32)], axis=0)
    mods = _modulation(cond8, ada_w, ada_b)[:, :N_COND].reshape(DEPTH, N_COND, 6, D)

    cast_srcs = [moe_w_gate[0].reshape(N_EXPERTS * D, D_FF_EXPERT), moe_w_up[0].reshape(N_EXPERTS * D, D_FF_EXPERT),
                 moe_w_down[0].reshape(N_EXPERTS * D_FF_EXPERT, D), ffn_w_gate[0], ffn_w_up[0], ffn_w_down[0]]
    x, (moe_wg, moe_wu, moe_wd, ffn_wg, ffn_wu, ffn_wd) = _even_mixer(
        x_prompt.reshape(T_P, D), x_sample.reshape(T_S, D), mods[0], ev_w_in[0].astype(BF16), ev_conv_w[0],
        ev_pool_w[0].astype(BF16), ev_pool_scale[0][None, :], ev_w_out[0].astype(BF16),
        ln_g[0, 0][None, :], ln_b[0, 0][None, :], cast_srcs)
    x = _ffn(x, mods[0], ffn_wg, ffn_wu, ffn_wd, ln_g[0, 1][None, :], ln_b[0, 1][None, :])

    w_in = od_w_in[0]
    w_pe = w_in[:, Q_RANK + KV_RANK:Q_RANK + KV_RANK + QK_ROPE]
    zpad = jnp.zeros((D, 64), F32)
    w_in_ext = jnp.concatenate([w_in[:, :Q_RANK + KV_RANK], w_in[:, Q_RANK + KV_RANK + QK_ROPE:],
                                w_pe, zpad, _rot_cols(w_pe), zpad], axis=1).astype(BF16)
    wq = od_w_q_b[0].reshape(Q_RANK, N_HEADS, QK_NOPE + QK_ROPE)
    zq = jnp.zeros((Q_RANK, N_HEADS, 64), F32)
    wqa = jnp.concatenate([wq, zq], axis=-1).reshape(Q_RANK, N_HEADS * QH).astype(BF16)
    wqb = jnp.concatenate([_rot_cols(wq[..., QK_NOPE:]), zq], axis=-1).reshape(Q_RANK, N_HEADS * 128).astype(BF16)
    wkv = od_w_kv_b[0].reshape(KV_RANK, N_HEADS, QK_NOPE + V_DIM)
    w_kv = jnp.concatenate([wkv[..., :QK_NOPE].reshape(KV_RANK, -1), wkv[..., QK_NOPE:].reshape(KV_RANK, -1)],
                           axis=1).astype(BF16)

    avg = jnp.asarray(_block_diag4(np.full((FNET_GROUP_DIM, FNET_GROUP_DIM), 1.0 / FNET_GROUP_DIM)), BF16)
    cc, sc = _dft_tables(FNET_GROUP_DIM)
    dft_hi, dft_lo = _hi_lo(np.concatenate([_block_diag4(cc), -_block_diag4(sc)], axis=1))

    q, ckv, kpe, yh, yl = _odd_proj(x, mods[1], _rope_table(), w_in_ext, od_q_norm[0][None, :],
                                    od_kv_norm[0][None, :], wqa, wqb, avg, dft_hi, dft_lo)

    cache_kpe128 = jnp.pad(cache_kpe[:, 0], ((0, 0), (0, 0), (0, 128 - QK_ROPE)))
    attn = _attention(q, ckv, kpe, cache_ckv[:, 0], cache_kpe128, w_kv)

    tabs_p = sum((_hi_lo(m) for m in _dft_tables(SEQ)), ())
    a_base = _dft_angles(np.arange(TM), DEC_SEQ)
    a_step = _dft_angles(np.arange(TPS) * TM, DEC_SEQ)
    base_s = (jnp.asarray(np.cos(a_base) * DEC_SEQ ** -0.5, F32), jnp.asarray(np.sin(a_base) * DEC_SEQ ** -0.5, F32))
    step_s = (jnp.asarray(np.cos(a_step), F32), jnp.asarray(np.sin(a_step), F32))
    f = _pos_dft(yh, yl, tabs_p, base_s, step_s)

    router = jnp.pad(moe_router[0], ((0, 0), (0, 128 - N_EXPERTS)))
    r_hi, r_lo = _hi_lo(router)
    x, info, info_t, cnt = _odd_merge(x, attn, f, mods[1], od_fnet_w[0].astype(BF16), od_w_out[0].astype(BF16),
                                      ln_g[1, 0][None, :], ln_b[1, 0][None, :], r_hi, r_lo)

    counts = cnt[0, :N_EXPERTS].astype(jnp.int32)
    padded = ((counts + TM_E - 1) // TM_E) * TM_E
    g_end = jnp.cumsum(padded)
    g_start = g_end - padded
    choice = info_t[2:4].astype(jnp.int32)
    rank = info_t[4:6].astype(jnp.int32)
    dest = rank
    for e in range(N_EXPERTS):
        dest = dest + jnp.where(choice == e, g_start[e], 0)
    dest = dest.reshape(-1).astype(jnp.int32)
    tile_row = jnp.arange(N_ETILES, dtype=jnp.int32) * TM_E
    tile_expert = jnp.minimum(jnp.sum((tile_row[:, None] >= g_end[None, :]).astype(jnp.int32), axis=1),
                              N_EXPERTS - 1).astype(jnp.int32)
    n_used = (g_end[-1:] // TM_E).astype(jnp.int32)

    pad_lo = jnp.concatenate([g_start + counts, g_end[-1:]]).astype(jnp.int32)
    pad_hi = jnp.concatenate([g_end, jnp.full((1,), R_MAX, jnp.int32)]).astype(jnp.int32)
    xs = _dispatch(dest, pad_lo, pad_hi, x, mods[1])
    ys = _experts(tile_expert, n_used, xs, moe_wg.reshape(N_EXPERTS, D, D_FF_EXPERT),
                  moe_wu.reshape(N_EXPERTS, D, D_FF_EXPERT), moe_wd.reshape(N_EXPERTS, D_FF_EXPERT, D))
    yp, ysm = _combine(dest, x, info, mods[1], ln_g[1, 1][None, :], ln_b[1, 1][None, :], ys)

    y_prompt = yp.reshape(BATCH, SEQ, D)
    y_sample = ysm.reshape(DEC_BATCH, DEC_SEQ, D)
    new_ckv = ckv[:T_P].reshape(BATCH, 1, SEQ, KV_RANK)
    new_kpe = kpe[:T_P, :QK_ROPE].reshape(BATCH, 1, SEQ, QK_ROPE)
    return (y_prompt, y_sample, new_ckv, new_kpe)
```

```python
import functools

import numpy as np
import jax
import jax.numpy as jnp
from jax import lax
from jax.experimental import pallas as pl
from jax.experimental.pallas import tpu as pltpu

F32 = jnp.float32
BF16 = jnp.bfloat16

D = 1024
BATCH, SEQ = 32, 256
DEC_BATCH, DEC_SEQ = 2, 2048
PAST = 512
GRID_W = 64
T_P = BATCH * SEQ
T_S = DEC_BATCH * DEC_SEQ
T = T_P + T_S
N_COND = 1 + DEC_BATCH

CONV_DIM = 512
POOL_WINDOWS = (2, 4, 8, 16)
POOL_GROUP = 128
N_HEADS = 8
QK_NOPE, QK_ROPE, V_DIM = 128, 64, 128
Q_RANK, KV_RANK = 384, 256
FNET_DIM, FNET_GROUP_DIM = 256, 64
D_FF = 2816
N_EXPERTS = 8
D_FF_EXPERT = 1792
DEPTH = 2
ALPHA = (2 * DEPTH) ** 0.25
LN_EPS = 1e-5
RMS_EPS = 1e-6
ROPE_THETA = 10000.0

TM = 256
NP_TILES = T_P // TM
TPS = DEC_SEQ // TM
N_TILES = T // TM
HALO = 8
TM2 = 512
TM_FFN = 512
LK_S = PAST + DEC_SEQ
TM_E = 256
R_MAX = 2 * T + N_EXPERTS * TM_E
N_ETILES = R_MAX // TM_E
VMEM_LIMIT = 56 * 1024 * 1024


def _cond_index(i, tm=TM):
    return jnp.where(i < T_P // tm, 0, 1 + (i - T_P // tm) // (DEC_SEQ // tm))


def _const_spec(shape):
    nd = len(shape)
    return pl.BlockSpec(shape, lambda *_: (0,) * nd, pipeline_mode=pl.Buffered(1))


def _params(n_axes=1, vmem=VMEM_LIMIT):
    return pltpu.CompilerParams(dimension_semantics=("arbitrary",) * n_axes, vmem_limit_bytes=vmem)


def _layer_norm(v, g, b):
    mu = jnp.mean(v, axis=-1, keepdims=True)
    d = v - mu
    var = jnp.mean(d * d, axis=-1, keepdims=True)
    return d * lax.rsqrt(var + LN_EPS) * g + b


def _split_bf16(v):
    hi = v.astype(BF16)
    lo = (v - hi.astype(F32)).astype(BF16)
    return hi, lo


def _dot(a, b):
    return jnp.dot(a, b, preferred_element_type=F32)


def _silu(v):
    return v / (1.0 + jnp.exp(-v))


CAST_BLOCKS = 16


def _cast_stream(srcs):
    specs = [pl.BlockSpec((w.shape[0] // CAST_BLOCKS, w.shape[1]),
                          lambda i, *_: (jnp.minimum(i, CAST_BLOCKS - 1), 0)) for w in srcs]
    return specs, [jax.ShapeDtypeStruct(w.shape, BF16) for w in srcs]


def _cast_step(srcs, dsts):
    @pl.when(pl.program_id(0) < CAST_BLOCKS)
    def _():
        for src, dst in zip(srcs, dsts):
            dst[...] = src[...].astype(BF16)


def _mod_kernel(cond_ref, w_ref, b_ref, o_ref):
    s = _silu(cond_ref[...]).astype(BF16)
    o_ref[...] = _dot(s, w_ref[...].astype(BF16)) + b_ref[...]


def _modulation(cond8, ada_w, ada_b):
    nb = 6 * D // 1024
    return pl.pallas_call(
        _mod_kernel,
        grid=(DEPTH, nb),
        in_specs=[
            pl.BlockSpec((8, D), lambda l, j: (0, 0)),
            pl.BlockSpec((None, D, 1024), lambda l, j: (l, 0, j)),
            pl.BlockSpec((None, 1, 1024), lambda l, j: (l, 0, j)),
        ],
        out_specs=pl.BlockSpec((None, 8, 1024), lambda l, j: (l, 0, j)),
        out_shape=jax.ShapeDtypeStruct((DEPTH, 8, 6 * D), F32),
        compiler_params=_params(2),
        name="adaln_modulation",
    )(cond8, ada_w, ada_b.reshape(DEPTH, 1, 6 * D))


def _pool_tables():
    t = np.arange(TM)[:, None]
    r = np.arange(TM + 2 * HALO)[None, :]
    pos = r - HALO
    bands = np.zeros((4, len(POOL_WINDOWS), TM, TM + 2 * HALO), np.float32)
    inv = np.zeros((4, TM, 128), np.float32)
    for variant in range(4):
        left_ok, right_ok = variant & 1, variant >> 1
        col_ok = (r >= (0 if left_ok else HALO)) & (r < (TM + 2 * HALO if right_ok else TM + HALO))
        first, last = (-HALO if left_ok else 0), (TM + HALO if right_ok else TM)
        for gi, w in enumerate(POOL_WINDOWS):
            bands[variant, gi] = (pos >= t - w // 2) & (pos < t + w // 2) & col_ok
            cnt = np.minimum(t[:, 0] + w // 2, last) - np.maximum(t[:, 0] - w // 2, first)
            inv[variant, :, gi] = 1.0 / cnt
    return jnp.asarray(bands, BF16), jnp.asarray(inv, F32)


def _even_tiles(x_alls, left_oks, right_oks, mod_ref, win_ref, convw_ref, poolw_ref, pscale_ref, wout_ref,
                g_ref, b_ref, band_ref, inv_ref):
    shift, scale, gate = mod_ref[0:1, :], mod_ref[1:2, :], mod_ref[2:3, :]
    n = len(x_alls)
    hs = []
    for x_all, left_ok, right_ok in zip(x_alls, left_oks, right_oks):
        h = x_all * (1.0 + scale) + shift
        hs.append(jnp.concatenate([jnp.where(left_ok, h[:HALO], 0.0), h[HALO:HALO + TM],
                                   jnp.where(right_ok, h[HALO + TM:], 0.0)], axis=0).astype(BF16))
    us = [_dot(h, win_ref[...]) for h in hs]

    mixes = []
    for u, left_ok, right_ok in zip(us, left_oks, right_oks):
        ux, ub = u[:, 0:CONV_DIM], u[:, CONV_DIM:2 * CONV_DIM]
        uc, up = u[:, 2 * CONV_DIM:3 * CONV_DIM], u[:, 3 * CONV_DIM:]
        z = uc * ux
        conv = (z[HALO - 1:HALO - 1 + TM] * convw_ref[0:1, :]
                + z[HALO:HALO + TM] * convw_ref[1:2, :]
                + z[HALO + 1:HALO + 1 + TM] * convw_ref[2:3, :])
        ya = ub[HALO:HALO + TM] * conv

        variant = left_ok.astype(jnp.int32) + 2 * right_ok.astype(jnp.int32)
        inv_cnt = inv_ref[variant]
        up_hi, up_lo = _split_bf16(up)
        yb_groups = []
        for gi in range(len(POOL_WINDOWS)):
            band = band_ref[variant, gi]
            sl = slice(gi * POOL_GROUP, (gi + 1) * POOL_GROUP)
            tot = _dot(band, up_hi[:, sl]) + _dot(band, up_lo[:, sl])
            p = tot * inv_cnt[:, gi:gi + 1] - up[HALO:HALO + TM, sl]
            yb_groups.append(_dot(p.astype(BF16), poolw_ref[gi]))
        yb = jnp.concatenate(yb_groups, axis=1) * pscale_ref[...]
        mixes.append(jnp.concatenate([ya, yb], axis=1).astype(BF16))

    ys = [_dot(mix, wout_ref[...]) for mix in mixes]
    return [_layer_norm(ALPHA * x_alls[k][HALO:HALO + TM] + gate * ys[k], g_ref[...], b_ref[...])
            for k in range(n)]


EV_TILES = 2
EV_ROWS = EV_TILES * TM


N_CAST_EVEN = 3


def _even_mixer_kernel(xctx_ref, xprev_ref, xlat_ref, xnext_ref, mod_ref, win_ref, convw_ref, poolw_ref,
                       pscale_ref, wout_ref, g_ref, b_ref, band_ref, inv_ref, *rest):
    n = N_CAST_EVEN
    cast_in, o_ref, cast_out, xall_ref = rest[:n], rest[n], rest[n + 1:2 * n + 1], rest[-1]
    _cast_step(cast_in, cast_out)
    s = pl.program_id(0)
    n_ctx = T_P // EV_ROWS
    is_latent = s >= n_ctx
    first_tile = lax.rem(s - n_ctx, DEC_SEQ // EV_ROWS) * EV_TILES

    @pl.when(jnp.logical_not(is_latent))
    def _():
        for k in range(EV_TILES):
            xall_ref[k, 0:HALO, :] = jnp.zeros((HALO, D), F32)
            xall_ref[k, HALO:HALO + TM, :] = xctx_ref[k * TM:(k + 1) * TM, :]
            xall_ref[k, HALO + TM:, :] = jnp.zeros((HALO, D), F32)

    @pl.when(is_latent)
    def _():
        for k in range(EV_TILES):
            lo, hi = k * TM, (k + 1) * TM
            xall_ref[k, 0:HALO, :] = xprev_ref[...] if k == 0 else xlat_ref[lo - HALO:lo, :]
            xall_ref[k, HALO:HALO + TM, :] = xlat_ref[lo:hi, :]
            xall_ref[k, HALO + TM:, :] = xnext_ref[...] if k == EV_TILES - 1 else xlat_ref[hi:hi + HALO, :]

    left_oks = [jnp.logical_and(is_latent, first_tile + k != 0) for k in range(EV_TILES)]
    right_oks = [jnp.logical_and(is_latent, first_tile + k != TPS - 1) for k in range(EV_TILES)]
    outs = _even_tiles([xall_ref[k] for k in range(EV_TILES)], left_oks, right_oks, mod_ref, win_ref,
                       convw_ref, poolw_ref, pscale_ref, wout_ref, g_ref, b_ref, band_ref, inv_ref)
    for k in range(EV_TILES):
        o_ref[k * TM:(k + 1) * TM, :] = outs[k]


def _even_mixer(x_ctx, x_lat, mods, w_in, conv_w, pool_w, pool_scale, w_out, g, b, cast_srcs):
    assert len(cast_srcs) == N_CAST_EVEN
    hb = EV_ROWS // HALO
    n8 = T_S // HALO
    n_ctx = T_P // EV_ROWS
    lat = lambda i: jnp.maximum(i - n_ctx, 0)
    cast_specs, cast_shapes = _cast_stream(cast_srcs)
    bands, inv_cnt = _pool_tables()
    outs = pl.pallas_call(
        _even_mixer_kernel,
        grid=(T // EV_ROWS,),
        in_specs=[
            pl.BlockSpec((EV_ROWS, D), lambda i: (jnp.minimum(i, n_ctx - 1), 0)),
            pl.BlockSpec((HALO, D), lambda i: (jnp.maximum(lat(i) * hb - 1, 0), 0)),
            pl.BlockSpec((EV_ROWS, D), lambda i: (lat(i), 0)),
            pl.BlockSpec((HALO, D), lambda i: (jnp.minimum((lat(i) + 1) * hb, n8 - 1), 0)),
            pl.BlockSpec((None, 6, D), lambda i: (_cond_index(i, EV_ROWS), 0, 0)),
            _const_spec((D, 4 * CONV_DIM)),
            _const_spec((3, CONV_DIM)),
            _const_spec((4, POOL_GROUP, POOL_GROUP)),
            _const_spec((1, 4 * POOL_GROUP)),
            _const_spec((D, D)),
            _const_spec((1, D)),
            _const_spec((1, D)),
            _const_spec(bands.shape),
            _const_spec(inv_cnt.shape),
        ] + cast_specs,
        out_specs=[pl.BlockSpec((EV_ROWS, D), lambda i: (i, 0))] + cast_specs,
        out_shape=[jax.ShapeDtypeStruct((T, D), F32)] + cast_shapes,
        scratch_shapes=[pltpu.VMEM((EV_TILES, TM + 2 * HALO, D), F32)],
        compiler_params=_params(1),
        name="even_mixer",
    )(x_ctx, x_lat, x_lat, x_lat, mods, w_in, conv_w, pool_w, pool_scale, w_out, g, b, bands, inv_cnt, *cast_srcs)
    return outs[0], outs[1:]


FF_CHUNK = D_FF // 2


def _ffn_kernel(x_ref, mod_ref, wg_ref, wu_ref, wd_ref, g_ref, b_ref, cast_ref, o_ref, cast_out_ref):
    _cast_step([cast_ref], [cast_out_ref])
    shift, scale, gate = mod_ref[3:4, :], mod_ref[4:5, :], mod_ref[5:6, :]
    x = x_ref[...]
    h = (x * (1.0 + scale) + shift).astype(BF16)
    f = jnp.zeros((TM_FFN, D), F32)
    for c in range(D_FF // FF_CHUNK):
        sl = slice(c * FF_CHUNK, (c + 1) * FF_CHUNK)
        a = _silu(_dot(h, wg_ref[:, sl])) * _dot(h, wu_ref[:, sl])
        f = f + _dot(a.astype(BF16), wd_ref[sl, :])
    o_ref[...] = _layer_norm(ALPHA * x + gate * f, g_ref[...], b_ref[...])


def _ffn(x, mods, wg, wu, wd, g, b, cast_src):
    cast_specs, cast_shapes = _cast_stream([cast_src])
    return pl.pallas_call(
        _ffn_kernel,
        grid=(T // TM_FFN,),
        in_specs=[
            pl.BlockSpec((TM_FFN, D), lambda i: (i, 0)),
            pl.BlockSpec((None, 6, D), lambda i: (_cond_index(i, TM_FFN), 0, 0)),
            _const_spec((D, D_FF)),
            _const_spec((D, D_FF)),
            _const_spec((D_FF, D)),
            _const_spec((1, D)),
            _const_spec((1, D)),
        ] + cast_specs,
        out_specs=[pl.BlockSpec((TM_FFN, D), lambda i: (i, 0))] + cast_specs,
        out_shape=[jax.ShapeDtypeStruct((T, D), F32)] + cast_shapes,
        compiler_params=_params(1),
        name="dense_swiglu",
    )(x, mods, wg, wu, wd, g, b, cast_src)


W_IN_EXT = Q_RANK + KV_RANK + FNET_DIM + 128 + 128
QH = 256
ATT_SCALE = (QK_NOPE + QK_ROPE) ** -0.5 * float(np.log2(np.e))


def _odd_proj_kernel(x_ref, mod_ref, rope_ref, win_ref, qn_ref, kvn_ref, wqa_ref, wqb_ref,
                     avg_ref, dfth_ref, dftl_ref, cast_ref,
                     q_ref, ckv_ref, kpe_ref, yh_ref, yl_ref, cast_out_ref):
    _cast_step([cast_ref], [cast_out_ref])
    shift, scale = mod_ref[0:1, :], mod_ref[1:2, :]
    h = (x_ref[...] * (1.0 + scale) + shift).astype(BF16)
    u = _dot(h, win_ref[...])
    uq = u[:, 0:Q_RANK]
    ukv = u[:, Q_RANK:Q_RANK + KV_RANK]
    uf = u[:, Q_RANK + KV_RANK:Q_RANK + KV_RANK + FNET_DIM]
    o = Q_RANK + KV_RANK + FNET_DIM
    upe, upe_rot = u[:, o:o + 128], u[:, o + 128:o + 256]
    cos, sin = rope_ref[:, 0:128], rope_ref[:, 128:256]

    ckv_ref[...] = ukv * lax.rsqrt(jnp.mean(ukv * ukv, axis=-1, keepdims=True) + RMS_EPS) * kvn_ref[...]
    kpe_ref[...] = upe * cos + upe_rot * sin

    qlat = (uq * lax.rsqrt(jnp.mean(uq * uq, axis=-1, keepdims=True) + RMS_EPS) * qn_ref[...]).astype(BF16)
    qa = _dot(qlat, wqa_ref[...])
    qb = _dot(qlat, wqb_ref[...])
    for hd in range(N_HEADS):
        nope = qa[:, hd * QH:hd * QH + 128]
        pe = qa[:, hd * QH + 128:(hd + 1) * QH] * cos + qb[:, hd * 128:(hd + 1) * 128] * sin
        q_ref[:, hd * QH:hd * QH + 128] = (nope * ATT_SCALE).astype(BF16)
        q_ref[:, hd * QH + 128:(hd + 1) * QH] = (pe * ATT_SCALE).astype(BF16)

    avg = avg_ref[...]
    uf_hi, uf_lo = _split_bf16(uf)
    mu = _dot(uf_hi, avg) + _dot(uf_lo, avg)
    dlt = uf - mu
    sq_hi, sq_lo = _split_bf16(dlt * dlt)
    var = _dot(sq_hi, avg) + _dot(sq_lo, avg)
    xn = dlt * lax.rsqrt(var + LN_EPS)
    xn_hi, xn_lo = _split_bf16(xn)
    y = _dot(xn_hi, dfth_ref[...]) + _dot(xn_lo, dfth_ref[...]) + _dot(xn_hi, dftl_ref[...])
    y_hi, y_lo = _split_bf16(y)
    yh_ref[...] = y_hi
    yl_ref[...] = y_lo


def _odd_proj(x, mods, rope_tab, w_in_ext, q_norm, kv_norm, wqa, wqb, avg, dft_hi, dft_lo, cast_src):
    cast_specs, cast_shapes = _cast_stream([cast_src])

    def rope_index(i):
        return (jnp.where(i < T_P // TM2, 0, 1 + lax.rem(i - T_P // TM2, DEC_SEQ // TM2)), 0)

    return pl.pallas_call(
        _odd_proj_kernel,
        grid=(T // TM2,),
        in_specs=[
            pl.BlockSpec((TM2, D), lambda i: (i, 0)),
            pl.BlockSpec((None, 6, D), lambda i: (_cond_index(i, TM2), 0, 0)),
            pl.BlockSpec((TM2, 256), rope_index),
            _const_spec((D, W_IN_EXT)),
            _const_spec((1, Q_RANK)),
            _const_spec((1, KV_RANK)),
            _const_spec((Q_RANK, N_HEADS * QH)),
            _const_spec((Q_RANK, N_HEADS * 128)),
            _const_spec((FNET_DIM, FNET_DIM)),
            _const_spec((FNET_DIM, 2 * FNET_DIM)),
            _const_spec((FNET_DIM, 2 * FNET_DIM)),
        ] + cast_specs,
        out_specs=[
            pl.BlockSpec((TM2, N_HEADS * QH), lambda i: (i, 0)),
            pl.BlockSpec((TM2, KV_RANK), lambda i: (i, 0)),
            pl.BlockSpec((TM2, 128), lambda i: (i, 0)),
            pl.BlockSpec((TM2, 2 * FNET_DIM), lambda i: (i, 0)),
            pl.BlockSpec((TM2, 2 * FNET_DIM), lambda i: (i, 0)),
        ] + cast_specs,
        out_shape=[
            jax.ShapeDtypeStruct((T, N_HEADS * QH), BF16),
            jax.ShapeDtypeStruct((T, KV_RANK), F32),
            jax.ShapeDtypeStruct((T, 128), F32),
            jax.ShapeDtypeStruct((T, 2 * FNET_DIM), BF16),
            jax.ShapeDtypeStruct((T, 2 * FNET_DIM), BF16),
        ] + cast_shapes,
        compiler_params=_params(1),
        name="odd_projections",
    )(x, mods, rope_tab, w_in_ext, q_norm, kv_norm, wqa, wqb, avg, dft_hi, dft_lo, cast_src)


def _attn_body(q_ref, kv_ref, kpe_ref, o_ref):
    kpe = kpe_ref[...].astype(BF16)
    for hd in range(N_HEADS):
        qh = q_ref[:, hd * QH:(hd + 1) * QH]
        kh = jnp.concatenate([kv_ref[:, hd * 128:(hd + 1) * 128], kpe], axis=1)
        s = lax.dot_general(qh, kh, (((1,), (1,)), ((), ())), preferred_element_type=F32)
        p = jnp.exp2(s - jnp.max(s, axis=-1, keepdims=True))
        den = jnp.sum(p, axis=-1, keepdims=True)
        vh = kv_ref[:, (N_HEADS + hd) * 128:(N_HEADS + hd + 1) * 128]
        o = _dot(p.astype(BF16), vh)
        o_ref[:, hd * 128:(hd + 1) * 128] = (o / den).astype(BF16)


KV_CHUNK = 512


def _attn_kernel(q_ref, ckvp_ref, kpep_ref, ckvs_ref, kpes_ref, cckv_ref, ckpe_ref, wkv_ref, o_ref,
                 kvp_s, kpep_s, kvs_s, kpes_s):
    i = pl.program_id(0)

    def expand(ckv):
        return _dot(ckv.astype(BF16), wkv_ref[...]).astype(BF16)

    @pl.when(i < NP_TILES)
    def _():
        kvp_s[...] = expand(ckvp_ref[...])
        kpep_s[...] = kpep_ref[...].astype(BF16)
        _attn_body(q_ref, kvp_s, kpep_s, o_ref)

    @pl.when(i >= NP_TILES)
    def _():
        @pl.when(lax.rem(i - NP_TILES, TPS) == 0)
        def _():
            kvs_s[0:PAST, :] = expand(cckv_ref[...])
            kpes_s[0:PAST, :] = ckpe_ref[...].astype(BF16)
            kpes_s[PAST:, :] = kpes_ref[...].astype(BF16)
            for c in range(DEC_SEQ // KV_CHUNK):
                rows = slice(c * KV_CHUNK, (c + 1) * KV_CHUNK)
                kvs_s[PAST + c * KV_CHUNK:PAST + (c + 1) * KV_CHUNK, :] = expand(ckvs_ref[rows, :])

        _attn_body(q_ref, kvs_s, kpes_s, o_ref)


def _attention(q, ckv, kpe, cache_ckv, cache_kpe128, w_kv):
    ctx_blk = lambda i: (jnp.minimum(i, NP_TILES - 1), 0)
    lat_b = lambda i: jnp.maximum(i - NP_TILES, 0) // TPS
    lat_blk = lambda i: (T_P // DEC_SEQ + lat_b(i), 0)
    return pl.pallas_call(
        _attn_kernel,
        grid=(N_TILES,),
        in_specs=[
            pl.BlockSpec((TM, N_HEADS * QH), lambda i: (i, 0)),
            pl.BlockSpec((SEQ, KV_RANK), ctx_blk),
            pl.BlockSpec((SEQ, 128), ctx_blk),
            pl.BlockSpec((DEC_SEQ, KV_RANK), lat_blk),
            pl.BlockSpec((DEC_SEQ, 128), lat_blk),
            pl.BlockSpec((None, PAST, KV_RANK), lambda i: (lat_b(i), 0, 0)),
            pl.BlockSpec((None, PAST, 128), lambda i: (lat_b(i), 0, 0)),
            _const_spec((KV_RANK, 2 * N_HEADS * 128)),
        ],
        out_specs=pl.BlockSpec((TM, N_HEADS * V_DIM), lambda i: (i, 0)),
        out_shape=jax.ShapeDtypeStruct((T, N_HEADS * V_DIM), BF16),
        scratch_shapes=[pltpu.VMEM((SEQ, 2 * N_HEADS * 128), BF16), pltpu.VMEM((SEQ, 128), BF16),
                        pltpu.VMEM((LK_S, 2 * N_HEADS * 128), BF16), pltpu.VMEM((LK_S, 128), BF16)],
        compiler_params=_params(1),
        name="attention",
    )(q, ckv, kpe, ckv, kpe, cache_ckv, cache_kpe128, w_kv)


def _pos_dft_body(yh_ref, yl_ref, ch, cl, sh, sl, o_ref):
    yc_h, ys_h = yh_ref[:, 0:FNET_DIM], yh_ref[:, FNET_DIM:]
    yc_l, ys_l = yl_ref[:, 0:FNET_DIM], yl_ref[:, FNET_DIM:]
    f = (_dot(ch, yc_h) + _dot(ch, yc_l) + _dot(cl, yc_h)
         + _dot(sh, ys_h) + _dot(sh, ys_l) + _dot(sl, ys_h))
    o_ref[...] = f.astype(BF16)


def _pos_dft_kernel(yhp_ref, ylp_ref, chp_ref, clp_ref, shp_ref, slp_ref,
                    yhs_ref, yls_ref, c0_ref, s0_ref, cb_ref, sb_ref, o_ref):
    @pl.when(pl.program_id(0) < NP_TILES)
    def _():
        _pos_dft_body(yhp_ref, ylp_ref, chp_ref[...], clp_ref[...], shp_ref[...], slp_ref[...], o_ref)

    @pl.when(pl.program_id(0) >= NP_TILES)
    def _():
        j = lax.rem(pl.program_id(0) - NP_TILES, TPS)
        cb, sb = cb_ref[pl.ds(j, 1), :], sb_ref[pl.ds(j, 1), :]
        c0, s0 = c0_ref[...], s0_ref[...]
        ch, cl = _split_bf16(c0 * cb - s0 * sb)
        sh, sl = _split_bf16(s0 * cb + c0 * sb)
        _pos_dft_body(yhs_ref, yls_ref, ch, cl, sh, sl, o_ref)


def _pos_dft(yh, yl, tabs_p, base_s, step_s):
    ctx_blk = lambda i: (jnp.minimum(i, NP_TILES - 1), 0)
    lat_seq = lambda i: (T_P // DEC_SEQ + jnp.maximum(i - NP_TILES, 0) // TPS, 0)
    return pl.pallas_call(
        _pos_dft_kernel,
        grid=(N_TILES,),
        in_specs=[pl.BlockSpec((SEQ, 2 * FNET_DIM), ctx_blk)] * 2
        + [_const_spec((SEQ, SEQ))] * 4
        + [pl.BlockSpec((DEC_SEQ, 2 * FNET_DIM), lat_seq)] * 2
        + [_const_spec((TM, DEC_SEQ))] * 2
        + [_const_spec((TPS, DEC_SEQ))] * 2,
        out_specs=pl.BlockSpec((TM, FNET_DIM), lambda i: (i, 0)),
        out_shape=jax.ShapeDtypeStruct((T, FNET_DIM), BF16),
        compiler_params=_params(1),
        name="pos_dft",
    )(yh, yl, *tabs_p, yh, yl, *base_s, *step_s)


def _odd_merge_kernel(x_ref, attn_ref, f_ref, mod_ref, fw_ref, wo_ref, g_ref, b_ref, rh_ref, rl_ref, tri_ref,
                      cast_ref, xo_ref, info_ref, infot_ref, cnt_ref, cast_out_ref, carry_ref):
    _cast_step([cast_ref], [cast_out_ref])

    @pl.when(pl.program_id(0) == 0)
    def _():
        carry_ref[...] = jnp.zeros_like(carry_ref)

    gate = mod_ref[2:3, :]
    shift2, scale2 = mod_ref[3:4, :], mod_ref[4:5, :]
    fm = _dot(f_ref[...], fw_ref[...]).astype(BF16)
    y = _dot(attn_ref[...], wo_ref[0:N_HEADS * V_DIM, :]) + _dot(fm, wo_ref[N_HEADS * V_DIM:, :])
    x = _layer_norm(ALPHA * x_ref[...] + gate * y, g_ref[...], b_ref[...])
    xo_ref[...] = x
    h = x * (1.0 + scale2) + shift2

    h_hi, h_lo = _split_bf16(h)
    logits = _dot(h_hi, rh_ref[...]) + _dot(h_lo, rh_ref[...]) + _dot(h_hi, rl_ref[...])
    lane = lax.broadcasted_iota(jnp.int32, (TM2, 128), 1)
    neg = jnp.float32(-jnp.inf)
    logits = jnp.where(lane < N_EXPERTS, logits, neg)
    m1 = jnp.max(logits, axis=-1, keepdims=True)
    i1 = jnp.min(jnp.where(logits == m1, lane, 128), axis=-1, keepdims=True)
    rest = jnp.where(lane == i1, neg, logits)
    m2 = jnp.max(rest, axis=-1, keepdims=True)
    i2 = jnp.min(jnp.where(rest == m2, lane, 128), axis=-1, keepdims=True)
    e2 = jnp.exp(m2 - m1)
    w1 = 1.0 / (1.0 + e2)
    w2 = e2 / (1.0 + e2)
    info = jnp.where(lane == 0, w1, 0.0)
    info = jnp.where(lane == 1, w2, info)
    info = jnp.where(lane == 2, i1.astype(F32), info)
    info = jnp.where(lane == 3, i2.astype(F32), info)

    uses = jnp.logical_or(lane == i1, lane == i2)
    seen = _dot(tri_ref[...], jnp.where(uses, 1.0, 0.0).astype(BF16)) + carry_ref[...]
    r1 = jnp.sum(jnp.where(lane == i1, seen, 0.0), axis=-1, keepdims=True)
    r2 = jnp.sum(jnp.where(lane == i2, seen, 0.0), axis=-1, keepdims=True)
    info = jnp.where(lane == 4, r1, info)
    info = jnp.where(lane == 5, r2, info)
    info_ref[...] = info
    infot_ref[...] = info.T
    total = carry_ref[...] + jnp.sum(jnp.where(uses, 1.0, 0.0), axis=0, keepdims=True)
    carry_ref[...] = total
    cnt_ref[...] = jnp.broadcast_to(total, cnt_ref.shape)


def _odd_merge(x, attn, f, mods, fnet_w, w_out, g, b, r_hi, r_lo, cast_src):
    row = lambda i: (i, 0)
    tri = jnp.asarray(np.tril(np.ones((TM2, TM2), np.float32), -1), BF16)
    cast_specs, cast_shapes = _cast_stream([cast_src])
    return pl.pallas_call(
        _odd_merge_kernel,
        grid=(T // TM2,),
        in_specs=[
            pl.BlockSpec((TM2, D), row),
            pl.BlockSpec((TM2, N_HEADS * V_DIM), row),
            pl.BlockSpec((TM2, FNET_DIM), row),
            pl.BlockSpec((None, 6, D), lambda i: (_cond_index(i, TM2), 0, 0)),
            _const_spec((FNET_DIM, FNET_DIM)),
            _const_spec((N_HEADS * V_DIM + FNET_DIM, D)),
            _const_spec((1, D)),
            _const_spec((1, D)),
            _const_spec((D, 128)),
            _const_spec((D, 128)),
            _const_spec((TM2, TM2)),
        ] + cast_specs,
        out_specs=[pl.BlockSpec((TM2, D), row), pl.BlockSpec((TM2, 128), row),
                   pl.BlockSpec((128, TM2), lambda i: (0, i)), pl.BlockSpec((8, 128), lambda i: (0, 0))] + cast_specs,
        out_shape=[jax.ShapeDtypeStruct((T, D), F32), jax.ShapeDtypeStruct((T, 128), F32),
                   jax.ShapeDtypeStruct((128, T), F32), jax.ShapeDtypeStruct((8, 128), F32)] + cast_shapes,
        scratch_shapes=[pltpu.VMEM((1, 128), F32)],
        compiler_params=_params(1),
        name="odd_merge_router",
    )(x, attn, f, mods, fnet_w, w_out, g, b, r_hi, r_lo, tri, cast_src)


TM_D = 512
ROW = (8, 128)
DMA_UNROLL = 8


def _row_copy(src, s, dst, d, sem):
    return pltpu.make_async_copy(src.at[s], dst.at[d], sem)


def _dispatch_kernel(dest_ref, pad_lo_ref, pad_hi_ref, x_ref, mod_ref, xs_ref, h_ref, zero_ref, sem):
    i = pl.program_id(0)
    base = i * TM_D
    shift2, scale2 = mod_ref[3:4, :], mod_ref[4:5, :]
    h_ref[...] = (x_ref[...] * (1.0 + scale2) + shift2).reshape((TM_D,) + ROW)

    def issue(r, carry):
        t = base + r
        _row_copy(h_ref, r, xs_ref, dest_ref[t], sem).start(priority=0)
        _row_copy(h_ref, r, xs_ref, dest_ref[T + t], sem).start(priority=1)
        return carry

    lax.fori_loop(0, TM_D, issue, 0, unroll=DMA_UNROLL)

    def drain(r, carry):
        _row_copy(h_ref, 0, xs_ref, 0, sem).wait()
        _row_copy(h_ref, 0, xs_ref, 0, sem).wait()
        return carry

    lax.fori_loop(0, TM_D, drain, 0, unroll=DMA_UNROLL)

    @pl.when(i == 0)
    def _():
        zero_ref[...] = jnp.zeros_like(zero_ref)
        for e in range(N_EXPERTS + 1):
            def zissue(r, carry):
                _row_copy(zero_ref, 0, xs_ref, r, sem).start()
                return carry

            def zdrain(r, carry):
                _row_copy(zero_ref, 0, xs_ref, 0, sem).wait()
                return carry

            lax.fori_loop(pad_lo_ref[e], pad_hi_ref[e], zissue, 0)
            lax.fori_loop(pad_lo_ref[e], pad_hi_ref[e], zdrain, 0)


def _dispatch(dest, pad_lo, pad_hi, x, mods):
    return pl.pallas_call(
        _dispatch_kernel,
        grid_spec=pltpu.PrefetchScalarGridSpec(
            num_scalar_prefetch=3,
            grid=(T // TM_D,),
            in_specs=[pl.BlockSpec((TM_D, D), lambda i, *_: (i, 0)),
                      pl.BlockSpec((None, 6, D), lambda i, *_: (_cond_index(i, TM_D), 0, 0))],
            out_specs=pl.BlockSpec(memory_space=pl.ANY),
            scratch_shapes=[pltpu.VMEM((TM_D,) + ROW, F32), pltpu.VMEM((1,) + ROW, F32),
                            pltpu.SemaphoreType.DMA(())],
        ),
        out_shape=jax.ShapeDtypeStruct((R_MAX,) + ROW, F32),
        compiler_params=pltpu.CompilerParams(dimension_semantics=("arbitrary",), has_side_effects=True),
        name="expert_dispatch",
    )(dest, pad_lo, pad_hi, x, mods)


def _expert_kernel(te_ref, nt_ref, xs_ref, wg_ref, wu_ref, wd_ref, o_ref):
    @pl.when(pl.program_id(0) < nt_ref[0])
    def _():
        h = xs_ref[...].reshape(TM_E, D).astype(BF16)
        a = _silu(_dot(h, wg_ref[...])) * _dot(h, wu_ref[...])
        o_ref[...] = _dot(a.astype(BF16), wd_ref[...]).reshape((TM_E,) + ROW)

    @pl.when(pl.program_id(0) >= nt_ref[0])
    def _():
        o_ref[...] = jnp.zeros_like(o_ref)


def _experts(tile_expert, n_used, xs, wg, wu, wd):
    return pl.pallas_call(
        _expert_kernel,
        grid_spec=pltpu.PrefetchScalarGridSpec(
            num_scalar_prefetch=2,
            grid=(N_ETILES,),
            in_specs=[
                pl.BlockSpec((TM_E,) + ROW, lambda i, te, nt: (jnp.minimum(i, nt[0] - 1), 0, 0)),
                pl.BlockSpec((None, D, D_FF_EXPERT), lambda i, te, nt: (te[i], 0, 0)),
                pl.BlockSpec((None, D, D_FF_EXPERT), lambda i, te, nt: (te[i], 0, 0)),
                pl.BlockSpec((None, D_FF_EXPERT, D), lambda i, te, nt: (te[i], 0, 0)),
            ],
            out_specs=pl.BlockSpec((TM_E,) + ROW, lambda i, te, nt: (i, 0, 0)),
        ),
        out_shape=jax.ShapeDtypeStruct((R_MAX,) + ROW, F32),
        compiler_params=_params(1),
        name="expert_swiglu",
    )(tile_expert, n_used, xs, wg, wu, wd)


def _combine_kernel(dest_ref, x_ref, info_ref, mod_ref, g_ref, b_ref, ys_ref, op_ref, os_ref, buf, sem):
    i = pl.program_id(0)
    slot = lax.rem(i, 2)

    def gather(tile, s):
        def issue(r, carry):
            t = tile * TM2 + r
            _row_copy(ys_ref, dest_ref[t], buf.at[s, 0], r, sem.at[s]).start(priority=0)
            _row_copy(ys_ref, dest_ref[T + t], buf.at[s, 1], r, sem.at[s]).start(priority=1)
            return carry

        lax.fori_loop(0, TM2, issue, 0, unroll=DMA_UNROLL)

    @pl.when(i == 0)
    def _():
        gather(0, 0)

    @pl.when(i + 1 < T // TM2)
    def _():
        gather(i + 1, 1 - slot)

    def drain(r, carry):
        _row_copy(ys_ref, 0, buf.at[slot, 0], 0, sem.at[slot]).wait()
        _row_copy(ys_ref, 0, buf.at[slot, 1], 0, sem.at[slot]).wait()
        return carry

    lax.fori_loop(0, TM2, drain, 0, unroll=DMA_UNROLL)

    gate = mod_ref[5:6, :]
    w1, w2 = info_ref[:, 0:1], info_ref[:, 1:2]
    y = w1 * buf[slot, 0].reshape(TM2, D) + w2 * buf[slot, 1].reshape(TM2, D)
    out = _layer_norm(ALPHA * x_ref[...] + gate * y, g_ref[...], b_ref[...])

    @pl.when(pl.program_id(0) < T_P // TM2)
    def _():
        op_ref[...] = out

    @pl.when(pl.program_id(0) >= T_P // TM2)
    def _():
        os_ref[...] = out


def _combine(dest, x, info, mods, g, b, ys):
    return pl.pallas_call(
        _combine_kernel,
        grid_spec=pltpu.PrefetchScalarGridSpec(
            num_scalar_prefetch=1,
            grid=(T // TM2,),
            in_specs=[
                pl.BlockSpec((TM2, D), lambda i, d: (i, 0)),
                pl.BlockSpec((TM2, 128), lambda i, d: (i, 0)),
                pl.BlockSpec((None, 6, D), lambda i, d: (_cond_index(i, TM2), 0, 0)),
                pl.BlockSpec((1, D), lambda i, d: (0, 0)),
                pl.BlockSpec((1, D), lambda i, d: (0, 0)),
                pl.BlockSpec(memory_space=pl.ANY),
            ],
            out_specs=[pl.BlockSpec((TM2, D), lambda i, d: (jnp.minimum(i, T_P // TM2 - 1), 0)),
                       pl.BlockSpec((TM2, D), lambda i, d: (jnp.maximum(i - T_P // TM2, 0), 0))],
            scratch_shapes=[pltpu.VMEM((2, 2, TM2) + ROW, F32), pltpu.SemaphoreType.DMA((2,))],
        ),
        out_shape=[jax.ShapeDtypeStruct((T_P, D), F32), jax.ShapeDtypeStruct((T_S, D), F32)],
        compiler_params=_params(1),
        name="expert_combine",
    )(dest, x, info, mods, g, b, ys)


def _rot_cols(w):
    w4 = w.reshape(w.shape[:-1] + (2, 2, QK_ROPE // 4))
    return jnp.stack([-w4[..., 1, :], w4[..., 0, :]], axis=-2).reshape(w.shape)


def _rope_table():
    rows = DEC_SEQ // GRID_W
    row = np.repeat(np.arange(rows), GRID_W).astype(np.float32)
    col = np.tile(np.arange(GRID_W), rows).astype(np.float32)
    half = QK_ROPE // 2
    inv = (ROPE_THETA ** (-np.arange(0, half, 2, dtype=np.float32) / half)).astype(np.float32)
    ar, ac = row[:, None] * inv, col[:, None] * inv
    ang = np.concatenate([ar, ar, ac, ac], axis=-1)
    cos = np.concatenate([np.ones((TM2, QK_ROPE)), np.cos(ang)], axis=0)
    sin = np.concatenate([np.zeros((TM2, QK_ROPE)), np.sin(ang)], axis=0)
    n = cos.shape[0]
    return jnp.asarray(np.concatenate([cos, np.ones((n, 64)), sin, np.zeros((n, 64))], axis=1), F32)


def _dft_angles(rows, n):
    k = np.arange(n, dtype=np.int64)
    return ((np.asarray(rows, np.int64)[:, None] * k[None, :]) % n) * (2.0 * np.pi / n)


def _dft_tables(n):
    ang = _dft_angles(np.arange(n), n)
    return np.cos(ang) * n ** -0.5, np.sin(ang) * n ** -0.5


def _hi_lo(m):
    m = jnp.asarray(m, F32)
    hi = m.astype(BF16)
    return hi, (m - hi.astype(F32)).astype(BF16)


def _block_diag4(m):
    return np.kron(np.eye(4), m)


def kernel(x_prompt, x_sample, cache_ckv, cache_kpe, c, c_ctx, ada_w, ada_b, ln_g, ln_b, ev_w_in, ev_conv_w, ev_pool_w, ev_pool_scale, ev_w_out, ffn_w_gate, ffn_w_up, ffn_w_down, od_w_in, od_q_norm, od_kv_norm, od_w_q_b, od_w_kv_b, od_fnet_w, od_w_out, moe_router, moe_w_gate, moe_w_up, moe_w_down):
    cond8 = jnp.concatenate([c_ctx[None, :], c, jnp.zeros((8 - N_COND, D), F32)], axis=0)
    mods = _modulation(cond8, ada_w, ada_b)[:, :N_COND].reshape(DEPTH, N_COND, 6, D)

    x, (ffn_wg, ffn_wu, ffn_wd) = _even_mixer(
        x_prompt.reshape(T_P, D), x_sample.reshape(T_S, D), mods[0], ev_w_in[0].astype(BF16), ev_conv_w[0],
        ev_pool_w[0].astype(BF16), ev_pool_scale[0][None, :], ev_w_out[0].astype(BF16),
        ln_g[0, 0][None, :], ln_b[0, 0][None, :], [ffn_w_gate[0], ffn_w_up[0], ffn_w_down[0]])
    x, moe_wd = _ffn(x, mods[0], ffn_wg, ffn_wu, ffn_wd, ln_g[0, 1][None, :], ln_b[0, 1][None, :],
                     moe_w_down[0].reshape(N_EXPERTS * D_FF_EXPERT, D))

    w_in = od_w_in[0]
    w_pe = w_in[:, Q_RANK + KV_RANK:Q_RANK + KV_RANK + QK_ROPE]
    zpad = jnp.zeros((D, 64), F32)
    w_in_ext = jnp.concatenate([w_in[:, :Q_RANK + KV_RANK], w_in[:, Q_RANK + KV_RANK + QK_ROPE:],
                                w_pe, zpad, _rot_cols(w_pe), zpad], axis=1).astype(BF16)
    wq = od_w_q_b[0].reshape(Q_RANK, N_HEADS, QK_NOPE + QK_ROPE)
    zq = jnp.zeros((Q_RANK, N_HEADS, 64), F32)
    wqa = jnp.concatenate([wq, zq], axis=-1).reshape(Q_RANK, N_HEADS * QH).astype(BF16)
    wqb = jnp.concatenate([_rot_cols(wq[..., QK_NOPE:]), zq], axis=-1).reshape(Q_RANK, N_HEADS * 128).astype(BF16)
    wkv = od_w_kv_b[0].reshape(KV_RANK, N_HEADS, QK_NOPE + V_DIM)
    w_kv = jnp.concatenate([wkv[..., :QK_NOPE].reshape(KV_RANK, -1), wkv[..., QK_NOPE:].reshape(KV_RANK, -1)],
                           axis=1).astype(BF16)

    avg = jnp.asarray(_block_diag4(np.full((FNET_GROUP_DIM, FNET_GROUP_DIM), 1.0 / FNET_GROUP_DIM)), BF16)
    cc, sc = _dft_tables(FNET_GROUP_DIM)
    dft_hi, dft_lo = _hi_lo(np.concatenate([_block_diag4(cc), -_block_diag4(sc)], axis=1))

    q, ckv, kpe, yh, yl, moe_wg = _odd_proj(x, mods[1], _rope_table(), w_in_ext, od_q_norm[0][None, :],
                                            od_kv_norm[0][None, :], wqa, wqb, avg, dft_hi, dft_lo,
                                            moe_w_gate[0].reshape(N_EXPERTS * D, D_FF_EXPERT))

    cache_kpe128 = jnp.pad(cache_kpe[:, 0], ((0, 0), (0, 0), (0, 128 - QK_ROPE)))
    attn = _attention(q, ckv, kpe, cache_ckv[:, 0], cache_kpe128, w_kv)

    tabs_p = sum((_hi_lo(m) for m in _dft_tables(SEQ)), ())
    a_base = _dft_angles(np.arange(TM), DEC_SEQ)
    a_step = _dft_angles(np.arange(TPS) * TM, DEC_SEQ)
    base_s = (jnp.asarray(np.cos(a_base) * DEC_SEQ ** -0.5, F32), jnp.asarray(np.sin(a_base) * DEC_SEQ ** -0.5, F32))
    step_s = (jnp.asarray(np.cos(a_step), F32), jnp.asarray(np.sin(a_step), F32))
    f = _pos_dft(yh, yl, tabs_p, base_s, step_s)

    router = jnp.pad(moe_router[0], ((0, 0), (0, 128 - N_EXPERTS)))
    r_hi, r_lo = _hi_lo(router)
    x, info, info_t, cnt, moe_wu = _odd_merge(x, attn, f, mods[1], od_fnet_w[0].astype(BF16),
                                              od_w_out[0].astype(BF16), ln_g[1, 0][None, :], ln_b[1, 0][None, :],
                                              r_hi, r_lo, moe_w_up[0].reshape(N_EXPERTS * D, D_FF_EXPERT))

    counts = cnt[0, :N_EXPERTS].astype(jnp.int32)
    padded = ((counts + TM_E - 1) // TM_E) * TM_E
    g_end = jnp.cumsum(padded)
    g_start = g_end - padded
    choice = info_t[2:4].astype(jnp.int32)
    rank = info_t[4:6].astype(jnp.int32)
    dest = rank
    for e in range(N_EXPERTS):
        dest = dest + jnp.where(choice == e, g_start[e], 0)
    dest = dest.reshape(-1).astype(jnp.int32)
    tile_row = jnp.arange(N_ETILES, dtype=jnp.int32) * TM_E
    tile_expert = jnp.minimum(jnp.sum((tile_row[:, None] >= g_end[None, :]).astype(jnp.int32), axis=1),
                              N_EXPERTS - 1).astype(jnp.int32)
    n_used = (g_end[-1:] // TM_E).astype(jnp.int32)

    pad_lo = jnp.concatenate([g_start + counts, g_end[-1:]]).astype(jnp.int32)
    pad_hi = jnp.concatenate([g_end, jnp.full((1,), R_MAX, jnp.int32)]).astype(jnp.int32)
    xs = _dispatch(dest, pad_lo, pad_hi, x, mods[1])
    ys = _experts(tile_expert, n_used, xs, moe_wg.reshape(N_EXPERTS, D, D_FF_EXPERT),
                  moe_wu.reshape(N_EXPERTS, D, D_FF_EXPERT), moe_wd.reshape(N_EXPERTS, D_FF_EXPERT, D))
    yp, ysm = _combine(dest, x, info, mods[1], ln_g[1, 1][None, :], ln_b[1, 1][None, :], ys)

    y_prompt = yp.reshape(BATCH, SEQ, D)
    y_sample = ysm.reshape(DEC_BATCH, DEC_SEQ, D)
    new_ckv = ckv[:T_P].reshape(BATCH, 1, SEQ, KV_RANK)
    new_kpe = kpe[:T_P, :QK_ROPE].reshape(BATCH, 1, SEQ, QK_ROPE)
    return (y_prompt, y_sample, new_ckv, new_kpe)
```

```python
import functools

import numpy as np
import jax
import jax.numpy as jnp
from jax import lax
from jax.experimental import pallas as pl
from jax.experimental.pallas import tpu as pltpu

F32 = jnp.float32
BF16 = jnp.bfloat16

D = 1024
BATCH, SEQ = 32, 256
DEC_BATCH, DEC_SEQ = 2, 2048
PAST = 512
GRID_W = 64
T_P = BATCH * SEQ
T_S = DEC_BATCH * DEC_SEQ
T = T_P + T_S
N_COND = 1 + DEC_BATCH

CONV_DIM = 512
POOL_WINDOWS = (2, 4, 8, 16)
POOL_GROUP = 128
N_HEADS = 8
QK_NOPE, QK_ROPE, V_DIM = 128, 64, 128
Q_RANK, KV_RANK = 384, 256
FNET_DIM, FNET_GROUP_DIM = 256, 64
D_FF = 2816
N_EXPERTS = 8
D_FF_EXPERT = 1792
DEPTH = 2
ALPHA = (2 * DEPTH) ** 0.25
LN_EPS = 1e-5
RMS_EPS = 1e-6
ROPE_THETA = 10000.0

TM = 256
NP_TILES = T_P // TM
TPS = DEC_SEQ // TM
N_TILES = T // TM
HALO = 8
TM2 = 512
TM_FFN = 512
LK_S = PAST + DEC_SEQ
TM_E = 256
N_ETILES = (2 * T) // TM_E + N_EXPERTS + 1
R_MAX = N_ETILES * TM_E
VMEM_LIMIT = 56 * 1024 * 1024


def _cond_index(i, tm=TM):
    return jnp.where(i < T_P // tm, 0, 1 + (i - T_P // tm) // (DEC_SEQ // tm))


def _const_spec(shape):
    nd = len(shape)
    return pl.BlockSpec(shape, lambda *_: (0,) * nd, pipeline_mode=pl.Buffered(1))


def _params(n_axes=1, vmem=VMEM_LIMIT):
    return pltpu.CompilerParams(dimension_semantics=("arbitrary",) * n_axes, vmem_limit_bytes=vmem)


def _layer_norm(v, g, b):
    mu = jnp.mean(v, axis=-1, keepdims=True)
    d = v - mu
    var = jnp.mean(d * d, axis=-1, keepdims=True)
    return d * lax.rsqrt(var + LN_EPS) * g + b


def _split_bf16(v):
    hi = v.astype(BF16)
    lo = (v - hi.astype(F32)).astype(BF16)
    return hi, lo


def _dot(a, b):
    return jnp.dot(a, b, preferred_element_type=F32)


def _silu(v):
    return v / (1.0 + jnp.exp(-v))


CAST_BLOCKS = 16


def _cast_stream(srcs):
    specs = [pl.BlockSpec((w.shape[0] // CAST_BLOCKS, w.shape[1]),
                          lambda i, *_: (jnp.minimum(i, CAST_BLOCKS - 1), 0)) for w in srcs]
    return specs, [jax.ShapeDtypeStruct(w.shape, BF16) for w in srcs]


def _cast_step(srcs, dsts):
    @pl.when(pl.program_id(0) < CAST_BLOCKS)
    def _():
        for src, dst in zip(srcs, dsts):
            dst[...] = src[...].astype(BF16)


def _mod_kernel(cond_ref, w_ref, b_ref, o_ref):
    s = _silu(cond_ref[...]).astype(BF16)
    o_ref[...] = _dot(s, w_ref[...].astype(BF16)) + b_ref[...]


def _modulation(cond8, ada_w, ada_b):
    nb = 6 * D // 1024
    return pl.pallas_call(
        _mod_kernel,
        grid=(DEPTH, nb),
        in_specs=[
            pl.BlockSpec((8, D), lambda l, j: (0, 0)),
            pl.BlockSpec((None, D, 1024), lambda l, j: (l, 0, j)),
            pl.BlockSpec((None, 1, 1024), lambda l, j: (l, 0, j)),
        ],
        out_specs=pl.BlockSpec((None, 8, 1024), lambda l, j: (l, 0, j)),
        out_shape=jax.ShapeDtypeStruct((DEPTH, 8, 6 * D), F32),
        compiler_params=_params(2),
        name="adaln_modulation",
    )(cond8, ada_w, ada_b.reshape(DEPTH, 1, 6 * D))


def _pool_tables():
    t = np.arange(TM)[:, None]
    r = np.arange(TM + 2 * HALO)[None, :]
    pos = r - HALO
    bands = np.zeros((4, len(POOL_WINDOWS), TM, TM + 2 * HALO), np.float32)
    inv = np.zeros((4, TM, 128), np.float32)
    for variant in range(4):
        left_ok, right_ok = variant & 1, variant >> 1
        col_ok = (r >= (0 if left_ok else HALO)) & (r < (TM + 2 * HALO if right_ok else TM + HALO))
        first, last = (-HALO if left_ok else 0), (TM + HALO if right_ok else TM)
        for gi, w in enumerate(POOL_WINDOWS):
            bands[variant, gi] = (pos >= t - w // 2) & (pos < t + w // 2) & col_ok
            cnt = np.minimum(t[:, 0] + w // 2, last) - np.maximum(t[:, 0] - w // 2, first)
            inv[variant, :, gi] = 1.0 / cnt
    return jnp.asarray(bands, BF16), jnp.asarray(inv, F32)


def _even_tiles(x_alls, left_oks, right_oks, mod_ref, win_ref, convw_ref, poolw_ref, pscale_ref, wout_ref,
                g_ref, b_ref, band_ref, inv_ref):
    shift, scale, gate = mod_ref[0:1, :], mod_ref[1:2, :], mod_ref[2:3, :]
    n = len(x_alls)
    hs = []
    for x_all, left_ok, right_ok in zip(x_alls, left_oks, right_oks):
        h = x_all * (1.0 + scale) + shift
        hs.append(jnp.concatenate([jnp.where(left_ok, h[:HALO], 0.0), h[HALO:HALO + TM],
                                   jnp.where(right_ok, h[HALO + TM:], 0.0)], axis=0).astype(BF16))
    us = [_dot(h, win_ref[...]) for h in hs]

    mixes = []
    for u, left_ok, right_ok in zip(us, left_oks, right_oks):
        ux, ub = u[:, 0:CONV_DIM], u[:, CONV_DIM:2 * CONV_DIM]
        uc, up = u[:, 2 * CONV_DIM:3 * CONV_DIM], u[:, 3 * CONV_DIM:]
        z = uc * ux
        conv = (z[HALO - 1:HALO - 1 + TM] * convw_ref[0:1, :]
                + z[HALO:HALO + TM] * convw_ref[1:2, :]
                + z[HALO + 1:HALO + 1 + TM] * convw_ref[2:3, :])
        ya = ub[HALO:HALO + TM] * conv

        variant = left_ok.astype(jnp.int32) + 2 * right_ok.astype(jnp.int32)
        inv_cnt = inv_ref[variant]
        up_hi, up_lo = _split_bf16(up)
        yb_groups = []
        for gi in range(len(POOL_WINDOWS)):
            band = band_ref[variant, gi]
            sl = slice(gi * POOL_GROUP, (gi + 1) * POOL_GROUP)
            tot = _dot(band, up_hi[:, sl]) + _dot(band, up_lo[:, sl])
            p = tot * inv_cnt[:, gi:gi + 1] - up[HALO:HALO + TM, sl]
            yb_groups.append(_dot(p.astype(BF16), poolw_ref[gi]))
        yb = jnp.concatenate(yb_groups, axis=1) * pscale_ref[...]
        mixes.append(jnp.concatenate([ya, yb], axis=1).astype(BF16))

    ys = [_dot(mix, wout_ref[...]) for mix in mixes]
    return [_layer_norm(ALPHA * x_alls[k][HALO:HALO + TM] + gate * ys[k], g_ref[...], b_ref[...])
            for k in range(n)]


EV_TILES = 2
EV_ROWS = EV_TILES * TM


N_CAST_EVEN = 3


def _even_mixer_kernel(xctx_ref, xprev_ref, xlat_ref, xnext_ref, mod_ref, win_ref, convw_ref, poolw_ref,
                       pscale_ref, wout_ref, g_ref, b_ref, band_ref, inv_ref, *rest):
    n = N_CAST_EVEN
    cast_in, o_ref, cast_out, xall_ref = rest[:n], rest[n], rest[n + 1:2 * n + 1], rest[-1]
    _cast_step(cast_in, cast_out)
    s = pl.program_id(0)
    n_ctx = T_P // EV_ROWS
    is_latent = s >= n_ctx
    first_tile = lax.rem(s - n_ctx, DEC_SEQ // EV_ROWS) * EV_TILES

    @pl.when(jnp.logical_not(is_latent))
    def _():
        for k in range(EV_TILES):
            xall_ref[k, 0:HALO, :] = jnp.zeros((HALO, D), F32)
            xall_ref[k, HALO:HALO + TM, :] = xctx_ref[k * TM:(k + 1) * TM, :]
            xall_ref[k, HALO + TM:, :] = jnp.zeros((HALO, D), F32)

    @pl.when(is_latent)
    def _():
        for k in range(EV_TILES):
            lo, hi = k * TM, (k + 1) * TM
            xall_ref[k, 0:HALO, :] = xprev_ref[...] if k == 0 else xlat_ref[lo - HALO:lo, :]
            xall_ref[k, HALO:HALO + TM, :] = xlat_ref[lo:hi, :]
            xall_ref[k, HALO + TM:, :] = xnext_ref[...] if k == EV_TILES - 1 else xlat_ref[hi:hi + HALO, :]

    left_oks = [jnp.logical_and(is_latent, first_tile + k != 0) for k in range(EV_TILES)]
    right_oks = [jnp.logical_and(is_latent, first_tile + k != TPS - 1) for k in range(EV_TILES)]
    outs = _even_tiles([xall_ref[k] for k in range(EV_TILES)], left_oks, right_oks, mod_ref, win_ref,
                       convw_ref, poolw_ref, pscale_ref, wout_ref, g_ref, b_ref, band_ref, inv_ref)
    for k in range(EV_TILES):
        o_ref[k * TM:(k + 1) * TM, :] = outs[k]


def _even_mixer(x_ctx, x_lat, mods, w_in, conv_w, pool_w, pool_scale, w_out, g, b, cast_srcs):
    assert len(cast_srcs) == N_CAST_EVEN
    hb = EV_ROWS // HALO
    n8 = T_S // HALO
    n_ctx = T_P // EV_ROWS
    lat = lambda i: jnp.maximum(i - n_ctx, 0)
    cast_specs, cast_shapes = _cast_stream(cast_srcs)
    bands, inv_cnt = _pool_tables()
    outs = pl.pallas_call(
        _even_mixer_kernel,
        grid=(T // EV_ROWS,),
        in_specs=[
            pl.BlockSpec((EV_ROWS, D), lambda i: (jnp.minimum(i, n_ctx - 1), 0)),
            pl.BlockSpec((HALO, D), lambda i: (jnp.maximum(lat(i) * hb - 1, 0), 0)),
            pl.BlockSpec((EV_ROWS, D), lambda i: (lat(i), 0)),
            pl.BlockSpec((HALO, D), lambda i: (jnp.minimum((lat(i) + 1) * hb, n8 - 1), 0)),
            pl.BlockSpec((None, 6, D), lambda i: (_cond_index(i, EV_ROWS), 0, 0)),
            _const_spec((D, 4 * CONV_DIM)),
            _const_spec((3, CONV_DIM)),
            _const_spec((4, POOL_GROUP, POOL_GROUP)),
            _const_spec((1, 4 * POOL_GROUP)),
            _const_spec((D, D)),
            _const_spec((1, D)),
            _const_spec((1, D)),
            _const_spec(bands.shape),
            _const_spec(inv_cnt.shape),
        ] + cast_specs,
        out_specs=[pl.BlockSpec((EV_ROWS, D), lambda i: (i, 0))] + cast_specs,
        out_shape=[jax.ShapeDtypeStruct((T, D), F32)] + cast_shapes,
        scratch_shapes=[pltpu.VMEM((EV_TILES, TM + 2 * HALO, D), F32)],
        compiler_params=_params(1),
        name="even_mixer",
    )(x_ctx, x_lat, x_lat, x_lat, mods, w_in, conv_w, pool_w, pool_scale, w_out, g, b, bands, inv_cnt, *cast_srcs)
    return outs[0], outs[1:]


FF_CHUNK = D_FF // 2


def _ffn_kernel(x_ref, mod_ref, wg_ref, wu_ref, wd_ref, g_ref, b_ref, cast_ref, o_ref, cast_out_ref):
    _cast_step([cast_ref], [cast_out_ref])
    shift, scale, gate = mod_ref[3:4, :], mod_ref[4:5, :], mod_ref[5:6, :]
    x = x_ref[...]
    h = (x * (1.0 + scale) + shift).astype(BF16)
    f = jnp.zeros((TM_FFN, D), F32)
    for c in range(D_FF // FF_CHUNK):
        sl = slice(c * FF_CHUNK, (c + 1) * FF_CHUNK)
        a = _silu(_dot(h, wg_ref[:, sl])) * _dot(h, wu_ref[:, sl])
        f = f + _dot(a.astype(BF16), wd_ref[sl, :])
    o_ref[...] = _layer_norm(ALPHA * x + gate * f, g_ref[...], b_ref[...])


def _ffn(x, mods, wg, wu, wd, g, b, cast_src):
    cast_specs, cast_shapes = _cast_stream([cast_src])
    return pl.pallas_call(
        _ffn_kernel,
        grid=(T // TM_FFN,),
        in_specs=[
            pl.BlockSpec((TM_FFN, D), lambda i: (i, 0)),
            pl.BlockSpec((None, 6, D), lambda i: (_cond_index(i, TM_FFN), 0, 0)),
            _const_spec((D, D_FF)),
            _const_spec((D, D_FF)),
            _const_spec((D_FF, D)),
            _const_spec((1, D)),
            _const_spec((1, D)),
        ] + cast_specs,
        out_specs=[pl.BlockSpec((TM_FFN, D), lambda i: (i, 0))] + cast_specs,
        out_shape=[jax.ShapeDtypeStruct((T, D), F32)] + cast_shapes,
        compiler_params=_params(1),
        name="dense_swiglu",
    )(x, mods, wg, wu, wd, g, b, cast_src)


W_IN_EXT = Q_RANK + KV_RANK + FNET_DIM + 128 + 128
QH = 256
ATT_SCALE = (QK_NOPE + QK_ROPE) ** -0.5 * float(np.log2(np.e))


def _odd_proj_kernel(x_ref, mod_ref, rope_ref, win_ref, qn_ref, kvn_ref, wqa_ref, wqb_ref,
                     avg_ref, dfth_ref, dftl_ref, cast_ref,
                     q_ref, ckv_ref, kpe_ref, y_ref, cast_out_ref):
    _cast_step([cast_ref], [cast_out_ref])
    shift, scale = mod_ref[0:1, :], mod_ref[1:2, :]
    h = (x_ref[...] * (1.0 + scale) + shift).astype(BF16)
    u = _dot(h, win_ref[...])
    uq = u[:, 0:Q_RANK]
    ukv = u[:, Q_RANK:Q_RANK + KV_RANK]
    uf = u[:, Q_RANK + KV_RANK:Q_RANK + KV_RANK + FNET_DIM]
    o = Q_RANK + KV_RANK + FNET_DIM
    upe, upe_rot = u[:, o:o + 128], u[:, o + 128:o + 256]
    cos, sin = rope_ref[:, 0:128], rope_ref[:, 128:256]

    ckv_ref[...] = ukv * lax.rsqrt(jnp.mean(ukv * ukv, axis=-1, keepdims=True) + RMS_EPS) * kvn_ref[...]
    kpe_ref[...] = upe * cos + upe_rot * sin

    qlat = (uq * lax.rsqrt(jnp.mean(uq * uq, axis=-1, keepdims=True) + RMS_EPS) * qn_ref[...]).astype(BF16)
    qa = _dot(qlat, wqa_ref[...])
    qb = _dot(qlat, wqb_ref[...])
    for hd in range(N_HEADS):
        nope = qa[:, hd * QH:hd * QH + 128]
        pe = qa[:, hd * QH + 128:(hd + 1) * QH] * cos + qb[:, hd * 128:(hd + 1) * 128] * sin
        q_ref[:, hd * QH:hd * QH + 128] = (nope * ATT_SCALE).astype(BF16)
        q_ref[:, hd * QH + 128:(hd + 1) * QH] = (pe * ATT_SCALE).astype(BF16)

    avg = avg_ref[...]
    uf_hi, uf_lo = _split_bf16(uf)
    mu = _dot(uf_hi, avg) + _dot(uf_lo, avg)
    dlt = uf - mu
    sq_hi, sq_lo = _split_bf16(dlt * dlt)
    var = _dot(sq_hi, avg) + _dot(sq_lo, avg)
    xn = dlt * lax.rsqrt(var + LN_EPS)
    xn_hi, xn_lo = _split_bf16(xn)
    y = _dot(xn_hi, dfth_ref[...]) + _dot(xn_lo, dfth_ref[...]) + _dot(xn_hi, dftl_ref[...])
    y_ref[...] = y.astype(BF16)


def _odd_proj(x, mods, rope_tab, w_in_ext, q_norm, kv_norm, wqa, wqb, avg, dft_hi, dft_lo, cast_src):
    cast_specs, cast_shapes = _cast_stream([cast_src])

    def rope_index(i):
        return (jnp.where(i < T_P // TM2, 0, 1 + lax.rem(i - T_P // TM2, DEC_SEQ // TM2)), 0)

    return pl.pallas_call(
        _odd_proj_kernel,
        grid=(T // TM2,),
        in_specs=[
            pl.BlockSpec((TM2, D), lambda i: (i, 0)),
            pl.BlockSpec((None, 6, D), lambda i: (_cond_index(i, TM2), 0, 0)),
            pl.BlockSpec((TM2, 256), rope_index),
            _const_spec((D, W_IN_EXT)),
            _const_spec((1, Q_RANK)),
            _const_spec((1, KV_RANK)),
            _const_spec((Q_RANK, N_HEADS * QH)),
            _const_spec((Q_RANK, N_HEADS * 128)),
            _const_spec((FNET_DIM, FNET_DIM)),
            _const_spec((FNET_DIM, 2 * FNET_DIM)),
            _const_spec((FNET_DIM, 2 * FNET_DIM)),
        ] + cast_specs,
        out_specs=[
            pl.BlockSpec((TM2, N_HEADS * QH), lambda i: (i, 0)),
            pl.BlockSpec((TM2, KV_RANK), lambda i: (i, 0)),
            pl.BlockSpec((TM2, 128), lambda i: (i, 0)),
            pl.BlockSpec((TM2, 2 * FNET_DIM), lambda i: (i, 0)),
        ] + cast_specs,
        out_shape=[
            jax.ShapeDtypeStruct((T, N_HEADS * QH), BF16),
            jax.ShapeDtypeStruct((T, KV_RANK), F32),
            jax.ShapeDtypeStruct((T, 128), F32),
            jax.ShapeDtypeStruct((T, 2 * FNET_DIM), BF16),
        ] + cast_shapes,
        compiler_params=_params(1),
        name="odd_projections",
    )(x, mods, rope_tab, w_in_ext, q_norm, kv_norm, wqa, wqb, avg, dft_hi, dft_lo, cast_src)


V_OFF = N_HEADS * QH
KV_COLS = V_OFF + N_HEADS * V_DIM


def _attn_body(q_ref, kv_ref, o_ref):
    for hd in range(N_HEADS):
        qh = q_ref[:, hd * QH:(hd + 1) * QH]
        kh = kv_ref[:, hd * QH:(hd + 1) * QH]
        s = lax.dot_general(qh, kh, (((1,), (1,)), ((), ())), preferred_element_type=F32)
        p = jnp.exp2(s - jnp.max(s, axis=-1, keepdims=True))
        den = jnp.sum(p, axis=-1, keepdims=True)
        vh = kv_ref[:, V_OFF + hd * V_DIM:V_OFF + (hd + 1) * V_DIM]
        o = _dot(p.astype(BF16), vh)
        o_ref[:, hd * 128:(hd + 1) * 128] = (o / den).astype(BF16)


KV_CHUNK = 512


def _attn_kernel(q_ref, ckvp_ref, kpep_ref, ckvs_ref, kpes_ref, cckv_ref, ckpe_ref, wkv_ref, o_ref,
                 kvp_s, kvs_s):
    i = pl.program_id(0)

    def expand(ckv, kpe, dst, row0):
        n = ckv.shape[0]
        kv = _dot(ckv.astype(BF16), wkv_ref[...]).astype(BF16)
        kpe = kpe.astype(BF16)
        for hd in range(N_HEADS):
            dst[row0:row0 + n, hd * QH:hd * QH + QK_NOPE] = kv[:, hd * QK_NOPE:(hd + 1) * QK_NOPE]
            dst[row0:row0 + n, hd * QH + QK_NOPE:(hd + 1) * QH] = kpe
        dst[row0:row0 + n, V_OFF:] = kv[:, N_HEADS * QK_NOPE:]

    @pl.when(i < NP_TILES)
    def _():
        expand(ckvp_ref[...], kpep_ref[...], kvp_s, 0)
        _attn_body(q_ref, kvp_s, o_ref)

    @pl.when(i >= NP_TILES)
    def _():
        @pl.when(lax.rem(i - NP_TILES, TPS) == 0)
        def _():
            expand(cckv_ref[...], ckpe_ref[...], kvs_s, 0)
            for c in range(DEC_SEQ // KV_CHUNK):
                rows = slice(c * KV_CHUNK, (c + 1) * KV_CHUNK)
                expand(ckvs_ref[rows, :], kpes_ref[rows, :], kvs_s, PAST + c * KV_CHUNK)

        _attn_body(q_ref, kvs_s, o_ref)


def _attention(q, ckv, kpe, cache_ckv, cache_kpe128, w_kv):
    ctx_blk = lambda i: (jnp.minimum(i, NP_TILES - 1), 0)
    lat_b = lambda i: jnp.maximum(i - NP_TILES, 0) // TPS
    lat_blk = lambda i: (T_P // DEC_SEQ + lat_b(i), 0)
    return pl.pallas_call(
        _attn_kernel,
        grid=(N_TILES,),
        in_specs=[
            pl.BlockSpec((TM, N_HEADS * QH), lambda i: (i, 0)),
            pl.BlockSpec((SEQ, KV_RANK), ctx_blk),
            pl.BlockSpec((SEQ, 128), ctx_blk),
            pl.BlockSpec((DEC_SEQ, KV_RANK), lat_blk),
            pl.BlockSpec((DEC_SEQ, 128), lat_blk),
            pl.BlockSpec((None, PAST, KV_RANK), lambda i: (lat_b(i), 0, 0)),
            pl.BlockSpec((None, PAST, 128), lambda i: (lat_b(i), 0, 0)),
            _const_spec((KV_RANK, 2 * N_HEADS * 128)),
        ],
        out_specs=pl.BlockSpec((TM, N_HEADS * V_DIM), lambda i: (i, 0)),
        out_shape=jax.ShapeDtypeStruct((T, N_HEADS * V_DIM), BF16),
        scratch_shapes=[pltpu.VMEM((SEQ, KV_COLS), BF16), pltpu.VMEM((LK_S, KV_COLS), BF16)],
        compiler_params=_params(1),
        name="attention",
    )(q, ckv, kpe, ckv, kpe, cache_ckv, cache_kpe128, w_kv)


def _pos_dft_body(y_ref, c, s, o_ref):
    f = _dot(c, y_ref[:, 0:FNET_DIM]) + _dot(s, y_ref[:, FNET_DIM:])
    o_ref[...] = f.astype(BF16)


def _pos_dft_kernel(yp_ref, cp_ref, sp_ref, ys_ref, c0_ref, s0_ref, cb_ref, sb_ref, o_ref):
    @pl.when(pl.program_id(0) < NP_TILES)
    def _():
        _pos_dft_body(yp_ref, cp_ref[...].astype(BF16), sp_ref[...].astype(BF16), o_ref)

    @pl.when(pl.program_id(0) >= NP_TILES)
    def _():
        j = lax.rem(pl.program_id(0) - NP_TILES, TPS)
        cb, sb = cb_ref[pl.ds(j, 1), :], sb_ref[pl.ds(j, 1), :]
        c0, s0 = c0_ref[...], s0_ref[...]
        _pos_dft_body(ys_ref, (c0 * cb - s0 * sb).astype(BF16), (s0 * cb + c0 * sb).astype(BF16), o_ref)


def _pos_dft(y, tabs_p, base_s, step_s):
    ctx_blk = lambda i: (jnp.minimum(i, NP_TILES - 1), 0)
    lat_seq = lambda i: (T_P // DEC_SEQ + jnp.maximum(i - NP_TILES, 0) // TPS, 0)
    return pl.pallas_call(
        _pos_dft_kernel,
        grid=(N_TILES,),
        in_specs=[pl.BlockSpec((SEQ, 2 * FNET_DIM), ctx_blk)]
        + [_const_spec((SEQ, SEQ))] * 2
        + [pl.BlockSpec((DEC_SEQ, 2 * FNET_DIM), lat_seq)]
        + [_const_spec((TM, DEC_SEQ))] * 2
        + [_const_spec((TPS, DEC_SEQ))] * 2,
        out_specs=pl.BlockSpec((TM, FNET_DIM), lambda i: (i, 0)),
        out_shape=jax.ShapeDtypeStruct((T, FNET_DIM), BF16),
        compiler_params=_params(1),
        name="pos_dft",
    )(y, *tabs_p, y, *base_s, *step_s)


def _odd_merge_kernel(x_ref, attn_ref, f_ref, mod_ref, fw_ref, wo_ref, g_ref, b_ref, rh_ref, rl_ref, tri_ref,
                      cast_ref, xo_ref, info_ref, infot_ref, cnt_ref, cast_out_ref, carry_ref):
    _cast_step([cast_ref], [cast_out_ref])

    @pl.when(pl.program_id(0) == 0)
    def _():
        carry_ref[...] = jnp.zeros_like(carry_ref)

    gate = mod_ref[2:3, :]
    shift2, scale2 = mod_ref[3:4, :], mod_ref[4:5, :]
    fm = _dot(f_ref[...], fw_ref[...]).astype(BF16)
    y = _dot(attn_ref[...], wo_ref[0:N_HEADS * V_DIM, :]) + _dot(fm, wo_ref[N_HEADS * V_DIM:, :])
    x = _layer_norm(ALPHA * x_ref[...] + gate * y, g_ref[...], b_ref[...])
    xo_ref[...] = x
    h = x * (1.0 + scale2) + shift2

    h_hi, h_lo = _split_bf16(h)
    logits = _dot(h_hi, rh_ref[...]) + _dot(h_lo, rh_ref[...]) + _dot(h_hi, rl_ref[...])
    lane = lax.broadcasted_iota(jnp.int32, (TM2, 128), 1)
    neg = jnp.float32(-jnp.inf)
    logits = jnp.where(lane < N_EXPERTS, logits, neg)
    m1 = jnp.max(logits, axis=-1, keepdims=True)
    i1 = jnp.min(jnp.where(logits == m1, lane, 128), axis=-1, keepdims=True)
    rest = jnp.where(lane == i1, neg, logits)
    m2 = jnp.max(rest, axis=-1, keepdims=True)
    i2 = jnp.min(jnp.where(rest == m2, lane, 128), axis=-1, keepdims=True)
    e2 = jnp.exp(m2 - m1)
    w1 = 1.0 / (1.0 + e2)
    w2 = e2 / (1.0 + e2)
    info = jnp.where(lane == 0, w1, 0.0)
    info = jnp.where(lane == 1, w2, info)
    info = jnp.where(lane == 2, i1.astype(F32), info)
    info = jnp.where(lane == 3, i2.astype(F32), info)

    uses = jnp.logical_or(lane == i1, lane == i2)
    seen = _dot(tri_ref[...], jnp.where(uses, 1.0, 0.0).astype(BF16)) + carry_ref[...]
    r1 = jnp.sum(jnp.where(lane == i1, seen, 0.0), axis=-1, keepdims=True)
    r2 = jnp.sum(jnp.where(lane == i2, seen, 0.0), axis=-1, keepdims=True)
    info = jnp.where(lane == 4, r1, info)
    info = jnp.where(lane == 5, r2, info)
    info_ref[...] = info
    infot_ref[...] = info.T
    total = carry_ref[...] + jnp.sum(jnp.where(uses, 1.0, 0.0), axis=0, keepdims=True)
    carry_ref[...] = total
    cnt_ref[...] = jnp.broadcast_to(total, cnt_ref.shape)


def _odd_merge(x, attn, f, mods, fnet_w, w_out, g, b, r_hi, r_lo, cast_src):
    row = lambda i: (i, 0)
    tri = jnp.asarray(np.tril(np.ones((TM2, TM2), np.float32), -1), BF16)
    cast_specs, cast_shapes = _cast_stream([cast_src])
    return pl.pallas_call(
        _odd_merge_kernel,
        grid=(T // TM2,),
        in_specs=[
            pl.BlockSpec((TM2, D), row),
            pl.BlockSpec((TM2, N_HEADS * V_DIM), row),
            pl.BlockSpec((TM2, FNET_DIM), row),
            pl.BlockSpec((None, 6, D), lambda i: (_cond_index(i, TM2), 0, 0)),
            _const_spec((FNET_DIM, FNET_DIM)),
            _const_spec((N_HEADS * V_DIM + FNET_DIM, D)),
            _const_spec((1, D)),
            _const_spec((1, D)),
            _const_spec((D, 128)),
            _const_spec((D, 128)),
            _const_spec((TM2, TM2)),
        ] + cast_specs,
        out_specs=[pl.BlockSpec((TM2, D), row), pl.BlockSpec((TM2, 128), row),
                   pl.BlockSpec((128, TM2), lambda i: (0, i)), pl.BlockSpec((8, 128), lambda i: (i, 0))] + cast_specs,
        out_shape=[jax.ShapeDtypeStruct((T, D), F32), jax.ShapeDtypeStruct((T, 128), F32),
                   jax.ShapeDtypeStruct((128, T), F32), jax.ShapeDtypeStruct((T // TM2 * 8, 128), F32)] + cast_shapes,
        scratch_shapes=[pltpu.VMEM((1, 128), F32)],
        compiler_params=_params(1),
        name="odd_merge_router",
    )(x, attn, f, mods, fnet_w, w_out, g, b, r_hi, r_lo, tri, cast_src)


TM_D = 512
ROW = (8, 128)
DMA_UNROLL = 8


def _row_copy(src, s, dst, d, sem):
    return pltpu.make_async_copy(src.at[s], dst.at[d], sem)


CHUNK = 16
LOCAL_ROWS = 2 * TM2 + N_EXPERTS * CHUNK


def _chunk_copies(tile, nch_ref, fn):
    for e in range(N_EXPERTS):
        def body(c, carry, e=e):
            fn(tile * N_EXPERTS + e, c)
            return carry

        lax.fori_loop(0, nch_ref[tile * N_EXPERTS + e], body, 0)


def _dispatch_kernel(lp_ref, d0_ref, off_ref, nch_ref, pad_lo_ref, pad_hi_ref, x_ref, mod_ref, xs_ref,
                     h_ref, local_ref, zero_ref, sem):
    i = pl.program_id(0)
    base = i * TM_D

    @pl.when(i == 0)
    def _():
        local_ref[...] = jnp.zeros_like(local_ref)
        zero_ref[...] = jnp.zeros_like(zero_ref)
        for e in range(N_EXPERTS + 1):
            def zissue(r, carry):
                _row_copy(zero_ref, 0, xs_ref, r, sem).start()
                return carry

            def zdrain(r, carry):
                _row_copy(zero_ref, 0, xs_ref, 0, sem).wait()
                return carry

            lax.fori_loop(pad_lo_ref[e], pad_hi_ref[e], zissue, 0)
            lax.fori_loop(pad_lo_ref[e], pad_hi_ref[e], zdrain, 0)

    shift2, scale2 = mod_ref[3:4, :], mod_ref[4:5, :]
    h_ref[...] = (x_ref[...] * (1.0 + scale2) + shift2).reshape((TM_D,) + ROW)

    def place(r, carry):
        row = h_ref[r]
        local_ref[lp_ref[base + r]] = row
        local_ref[lp_ref[T + base + r]] = row
        return carry

    lax.fori_loop(0, TM_D, place, 0, unroll=DMA_UNROLL)

    def chunk(slot, c):
        return pltpu.make_async_copy(local_ref.at[pl.ds(off_ref[slot] + c * CHUNK, CHUNK)],
                                     xs_ref.at[pl.ds(d0_ref[slot] + c * CHUNK, CHUNK)], sem)

    _chunk_copies(i, nch_ref, lambda slot, c: chunk(slot, c).start())
    _chunk_copies(i, nch_ref, lambda slot, c: chunk(slot, c).wait())


def _dispatch(lp, d0, off, nch, pad_lo, pad_hi, x, mods):
    assert TM_D == TM2
    return pl.pallas_call(
        _dispatch_kernel,
        grid_spec=pltpu.PrefetchScalarGridSpec(
            num_scalar_prefetch=6,
            grid=(T // TM_D,),
            in_specs=[pl.BlockSpec((TM_D, D), lambda i, *_: (i, 0)),
                      pl.BlockSpec((None, 6, D), lambda i, *_: (_cond_index(i, TM_D), 0, 0))],
            out_specs=pl.BlockSpec(memory_space=pl.ANY),
            scratch_shapes=[pltpu.VMEM((TM_D,) + ROW, F32), pltpu.VMEM((LOCAL_ROWS,) + ROW, F32),
                            pltpu.VMEM((1,) + ROW, F32), pltpu.SemaphoreType.DMA(())],
        ),
        out_shape=jax.ShapeDtypeStruct((R_MAX,) + ROW, F32),
        compiler_params=pltpu.CompilerParams(dimension_semantics=("arbitrary",), has_side_effects=True),
        name="expert_dispatch",
    )(lp, d0, off, nch, pad_lo, pad_hi, x, mods)


def _expert_kernel(te_ref, nt_ref, xs_ref, wg_ref, wu_ref, wd_ref, o_ref):
    @pl.when(pl.program_id(0) < nt_ref[0])
    def _():
        h = xs_ref[...].reshape(TM_E, D).astype(BF16)
        a = _silu(_dot(h, wg_ref[...])) * _dot(h, wu_ref[...])
        o_ref[...] = _dot(a.astype(BF16), wd_ref[...]).reshape((TM_E,) + ROW)

    @pl.when(pl.program_id(0) >= nt_ref[0])
    def _():
        o_ref[...] = jnp.zeros_like(o_ref)


def _experts(tile_expert, n_used, xs, wg, wu, wd):
    return pl.pallas_call(
        _expert_kernel,
        grid_spec=pltpu.PrefetchScalarGridSpec(
            num_scalar_prefetch=2,
            grid=(N_ETILES,),
            in_specs=[
                pl.BlockSpec((TM_E,) + ROW, lambda i, te, nt: (jnp.minimum(i, nt[0] - 1), 0, 0)),
                pl.BlockSpec((None, D, D_FF_EXPERT), lambda i, te, nt: (te[i], 0, 0)),
                pl.BlockSpec((None, D, D_FF_EXPERT), lambda i, te, nt: (te[i], 0, 0)),
                pl.BlockSpec((None, D_FF_EXPERT, D), lambda i, te, nt: (te[i], 0, 0)),
            ],
            out_specs=pl.BlockSpec((TM_E,) + ROW, lambda i, te, nt: (i, 0, 0)),
        ),
        out_shape=jax.ShapeDtypeStruct((R_MAX,) + ROW, F32),
        compiler_params=_params(1),
        name="expert_swiglu",
    )(tile_expert, n_used, xs, wg, wu, wd)


def _combine_kernel(lp_ref, d0_ref, off_ref, nch_ref, x_ref, info_ref, mod_ref, g_ref, b_ref, ys_ref,
                    op_ref, os_ref, local_ref, rows_ref, sem):
    i = pl.program_id(0)
    slot = lax.rem(i, 2)

    def chunk(s, seg, c):
        return pltpu.make_async_copy(ys_ref.at[pl.ds(d0_ref[seg] + c * CHUNK, CHUNK)],
                                     local_ref.at[s, pl.ds(off_ref[seg] + c * CHUNK, CHUNK)], sem.at[s])

    @pl.when(i == 0)
    def _():
        _chunk_copies(0, nch_ref, lambda seg, c: chunk(0, seg, c).start())

    @pl.when(i + 1 < T // TM2)
    def _():
        _chunk_copies(i + 1, nch_ref, lambda seg, c: chunk(1 - slot, seg, c).start())

    _chunk_copies(i, nch_ref, lambda seg, c: chunk(slot, seg, c).wait())

    def pick(r, carry):
        rows_ref[0, r] = local_ref[slot, lp_ref[i * TM2 + r]]
        rows_ref[1, r] = local_ref[slot, lp_ref[T + i * TM2 + r]]
        return carry

    lax.fori_loop(0, TM2, pick, 0, unroll=DMA_UNROLL)

    gate = mod_ref[5:6, :]
    w1, w2 = info_ref[:, 0:1], info_ref[:, 1:2]
    y = w1 * rows_ref[0].reshape(TM2, D) + w2 * rows_ref[1].reshape(TM2, D)
    out = _layer_norm(ALPHA * x_ref[...] + gate * y, g_ref[...], b_ref[...])

    @pl.when(pl.program_id(0) < T_P // TM2)
    def _():
        op_ref[...] = out

    @pl.when(pl.program_id(0) >= T_P // TM2)
    def _():
        os_ref[...] = out


def _combine(lp, d0, off, nch, x, info, mods, g, b, ys):
    return pl.pallas_call(
        _combine_kernel,
        grid_spec=pltpu.PrefetchScalarGridSpec(
            num_scalar_prefetch=4,
            grid=(T // TM2,),
            in_specs=[
                pl.BlockSpec((TM2, D), lambda i, *_: (i, 0)),
                pl.BlockSpec((TM2, 128), lambda i, *_: (i, 0)),
                pl.BlockSpec((None, 6, D), lambda i, *_: (_cond_index(i, TM2), 0, 0)),
                pl.BlockSpec((1, D), lambda i, *_: (0, 0)),
                pl.BlockSpec((1, D), lambda i, *_: (0, 0)),
                pl.BlockSpec(memory_space=pl.ANY),
            ],
            out_specs=[pl.BlockSpec((TM2, D), lambda i, *_: (jnp.minimum(i, T_P // TM2 - 1), 0)),
                       pl.BlockSpec((TM2, D), lambda i, *_: (jnp.maximum(i - T_P // TM2, 0), 0))],
            scratch_shapes=[pltpu.VMEM((2, LOCAL_ROWS) + ROW, F32), pltpu.VMEM((2, TM2) + ROW, F32),
                            pltpu.SemaphoreType.DMA((2,))],
        ),
        out_shape=[jax.ShapeDtypeStruct((T_P, D), F32), jax.ShapeDtypeStruct((T_S, D), F32)],
        compiler_params=_params(1),
        name="expert_combine",
    )(lp, d0, off, nch, x, info, mods, g, b, ys)


def _rot_cols(w):
    w4 = w.reshape(w.shape[:-1] + (2, 2, QK_ROPE // 4))
    return jnp.stack([-w4[..., 1, :], w4[..., 0, :]], axis=-2).reshape(w.shape)


def _rope_table():
    rows = DEC_SEQ // GRID_W
    row = np.repeat(np.arange(rows), GRID_W).astype(np.float32)
    col = np.tile(np.arange(GRID_W), rows).astype(np.float32)
    half = QK_ROPE // 2
    inv = (ROPE_THETA ** (-np.arange(0, half, 2, dtype=np.float32) / half)).astype(np.float32)
    ar, ac = row[:, None] * inv, col[:, None] * inv
    ang = np.concatenate([ar, ar, ac, ac], axis=-1)
    cos = np.concatenate([np.ones((TM2, QK_ROPE)), np.cos(ang)], axis=0)
    sin = np.concatenate([np.zeros((TM2, QK_ROPE)), np.sin(ang)], axis=0)
    n = cos.shape[0]
    return jnp.asarray(np.concatenate([cos, np.ones((n, 64)), sin, np.zeros((n, 64))], axis=1), F32)


def _dft_angles(rows, n):
    k = np.arange(n, dtype=np.int64)
    return ((np.asarray(rows, np.int64)[:, None] * k[None, :]) % n) * (2.0 * np.pi / n)


def _dft_tables(n):
    ang = _dft_angles(np.arange(n), n)
    return np.cos(ang) * n ** -0.5, np.sin(ang) * n ** -0.5


def _hi_lo(m):
    m = jnp.asarray(m, F32)
    hi = m.astype(BF16)
    return hi, (m - hi.astype(F32)).astype(BF16)


def _block_diag4(m):
    return np.kron(np.eye(4), m)


def kernel(x_prompt, x_sample, cache_ckv, cache_kpe, c, c_ctx, ada_w, ada_b, ln_g, ln_b, ev_w_in, ev_conv_w, ev_pool_w, ev_pool_scale, ev_w_out, ffn_w_gate, ffn_w_up, ffn_w_down, od_w_in, od_q_norm, od_kv_norm, od_w_q_b, od_w_kv_b, od_fnet_w, od_w_out, moe_router, moe_w_gate, moe_w_up, moe_w_down):
    cond8 = jnp.concatenate([c_ctx[None, :], c, jnp.zeros((8 - N_COND, D), F32)], axis=0)
    mods = _modulation(cond8, ada_w, ada_b)[:, :N_COND].reshape(DEPTH, N_COND, 6, D)

    x, (ffn_wg, ffn_wu, ffn_wd) = _even_mixer(
        x_prompt.reshape(T_P, D), x_sample.reshape(T_S, D), mods[0], ev_w_in[0].astype(BF16), ev_conv_w[0],
        ev_pool_w[0].astype(BF16), ev_pool_scale[0][None, :], ev_w_out[0].astype(BF16),
        ln_g[0, 0][None, :], ln_b[0, 0][None, :], [ffn_w_gate[0], ffn_w_up[0], ffn_w_down[0]])
    x, moe_wd = _ffn(x, mods[0], ffn_wg, ffn_wu, ffn_wd, ln_g[0, 1][None, :], ln_b[0, 1][None, :],
                     moe_w_down[0].reshape(N_EXPERTS * D_FF_EXPERT, D))

    w_in = od_w_in[0]
    w_pe = w_in[:, Q_RANK + KV_RANK:Q_RANK + KV_RANK + QK_ROPE]
    zpad = jnp.zeros((D, 64), F32)
    w_in_ext = jnp.concatenate([w_in[:, :Q_RANK + KV_RANK], w_in[:, Q_RANK + KV_RANK + QK_ROPE:],
                                w_pe, zpad, _rot_cols(w_pe), zpad], axis=1).astype(BF16)
    wq = od_w_q_b[0].reshape(Q_RANK, N_HEADS, QK_NOPE + QK_ROPE)
    zq = jnp.zeros((Q_RANK, N_HEADS, 64), F32)
    wqa = jnp.concatenate([wq, zq], axis=-1).reshape(Q_RANK, N_HEADS * QH).astype(BF16)
    wqb = jnp.concatenate([_rot_cols(wq[..., QK_NOPE:]), zq], axis=-1).reshape(Q_RANK, N_HEADS * 128).astype(BF16)
    wkv = od_w_kv_b[0].reshape(KV_RANK, N_HEADS, QK_NOPE + V_DIM)
    w_kv = jnp.concatenate([wkv[..., :QK_NOPE].reshape(KV_RANK, -1), wkv[..., QK_NOPE:].reshape(KV_RANK, -1)],
                           axis=1).astype(BF16)

    avg = jnp.asarray(_block_diag4(np.full((FNET_GROUP_DIM, FNET_GROUP_DIM), 1.0 / FNET_GROUP_DIM)), BF16)
    cc, sc = _dft_tables(FNET_GROUP_DIM)
    dft_hi, dft_lo = _hi_lo(np.concatenate([_block_diag4(cc), -_block_diag4(sc)], axis=1))

    q, ckv, kpe, y_dft, moe_wg = _odd_proj(x, mods[1], _rope_table(), w_in_ext, od_q_norm[0][None, :],
                                            od_kv_norm[0][None, :], wqa, wqb, avg, dft_hi, dft_lo,
                                            moe_w_gate[0].reshape(N_EXPERTS * D, D_FF_EXPERT))

    cache_kpe128 = jnp.pad(cache_kpe[:, 0], ((0, 0), (0, 0), (0, 128 - QK_ROPE)))
    attn = _attention(q, ckv, kpe, cache_ckv[:, 0], cache_kpe128, w_kv)

    tabs_p = tuple(jnp.asarray(m, F32) for m in _dft_tables(SEQ))
    a_base = _dft_angles(np.arange(TM), DEC_SEQ)
    a_step = _dft_angles(np.arange(TPS) * TM, DEC_SEQ)
    base_s = (jnp.asarray(np.cos(a_base) * DEC_SEQ ** -0.5, F32), jnp.asarray(np.sin(a_base) * DEC_SEQ ** -0.5, F32))
    step_s = (jnp.asarray(np.cos(a_step), F32), jnp.asarray(np.sin(a_step), F32))
    f = _pos_dft(y_dft, tabs_p, base_s, step_s)

    router = jnp.pad(moe_router[0], ((0, 0), (0, 128 - N_EXPERTS)))
    r_hi, r_lo = _hi_lo(router)
    x, info, info_t, cnt, moe_wu = _odd_merge(x, attn, f, mods[1], od_fnet_w[0].astype(BF16),
                                              od_w_out[0].astype(BF16), ln_g[1, 0][None, :], ln_b[1, 0][None, :],
                                              r_hi, r_lo, moe_w_up[0].reshape(N_EXPERTS * D, D_FF_EXPERT))

    n_tiles = T // TM2
    after = cnt.reshape(n_tiles, 8, 128)[:, 0, :N_EXPERTS].astype(jnp.int32)
    before = jnp.concatenate([jnp.zeros((1, N_EXPERTS), jnp.int32), after[:-1]], axis=0)
    counts = after[-1]
    padded = ((counts + CHUNK - 1 + TM_E - 1) // TM_E) * TM_E
    g_end = jnp.cumsum(padded)
    g_start = g_end - padded
    n_chunks = (after - before + CHUNK - 1) // CHUNK
    seg_off = CHUNK * (jnp.cumsum(n_chunks, axis=1) - n_chunks)
    seg_dst = g_start[None, :] + before
    choice = info_t[2:4].astype(jnp.int32)
    rank = info_t[4:6].astype(jnp.int32)
    shift_t = jnp.repeat((seg_off - before).T, TM2, axis=1)
    lp = rank
    for e in range(N_EXPERTS):
        lp = lp + jnp.where(choice == e, shift_t[e][None, :], 0)
    lp = lp.reshape(-1).astype(jnp.int32)
    d0, off, nch = (a.reshape(-1).astype(jnp.int32) for a in (seg_dst, seg_off, n_chunks))
    tile_row = jnp.arange(N_ETILES, dtype=jnp.int32) * TM_E
    tile_expert = jnp.minimum(jnp.sum((tile_row[:, None] >= g_end[None, :]).astype(jnp.int32), axis=1),
                              N_EXPERTS - 1).astype(jnp.int32)
    n_used = (g_end[-1:] // TM_E).astype(jnp.int32)

    pad_lo = jnp.concatenate([g_start + counts, g_end[-1:]]).astype(jnp.int32)
    pad_hi = jnp.concatenate([g_end, jnp.full((1,), R_MAX, jnp.int32)]).astype(jnp.int32)
    xs = _dispatch(lp, d0, off, nch, pad_lo, pad_hi, x, mods[1])
    ys = _experts(tile_expert, n_used, xs, moe_wg.reshape(N_EXPERTS, D, D_FF_EXPERT),
                  moe_wu.reshape(N_EXPERTS, D, D_FF_EXPERT), moe_wd.reshape(N_EXPERTS, D_FF_EXPERT, D))
    yp, ysm = _combine(lp, d0, off, nch, x, info, mods[1], ln_g[1, 1][None, :], ln_b[1, 1][None, :], ys)

    y_prompt = yp.reshape(BATCH, SEQ, D)
    y_sample = ysm.reshape(DEC_BATCH, DEC_SEQ, D)
    new_ckv = ckv[:T_P].reshape(BATCH, 1, SEQ, KV_RANK)
    new_kpe = kpe[:T_P, :QK_ROPE].reshape(BATCH, 1, SEQ, QK_ROPE)
    return (y_prompt, y_sample, new_ckv, new_kpe)
```

```python
import functools

import numpy as np
import jax
import jax.numpy as jnp
from jax import lax
from jax.experimental import pallas as pl
from jax.experimental.pallas import tpu as pltpu

F32 = jnp.float32
BF16 = jnp.bfloat16

D = 1024
BATCH, SEQ = 32, 256
DEC_BATCH, DEC_SEQ = 2, 2048
PAST = 512
GRID_W = 64
T_P = BATCH * SEQ
T_S = DEC_BATCH * DEC_SEQ
T = T_P + T_S
N_COND = 1 + DEC_BATCH

CONV_DIM = 512
POOL_WINDOWS = (2, 4, 8, 16)
POOL_GROUP = 128
N_HEADS = 8
QK_NOPE, QK_ROPE, V_DIM = 128, 64, 128
Q_RANK, KV_RANK = 384, 256
FNET_DIM, FNET_GROUP_DIM = 256, 64
D_FF = 2816
N_EXPERTS = 8
D_FF_EXPERT = 1792
DEPTH = 2
ALPHA = (2 * DEPTH) ** 0.25
LN_EPS = 1e-5
RMS_EPS = 1e-6
ROPE_THETA = 10000.0

TM = 256
NP_TILES = T_P // TM
TPS = DEC_SEQ // TM
N_TILES = T // TM
HALO = 8
TM2 = 512
TM_FFN = 512
LK_S = PAST + DEC_SEQ
TM_E = 256
N_ETILES = (2 * T) // TM_E + N_EXPERTS + 1
R_MAX = N_ETILES * TM_E
VMEM_LIMIT = 56 * 1024 * 1024


def _cond_index(i, tm=TM):
    return jnp.where(i < T_P // tm, 0, 1 + (i - T_P // tm) // (DEC_SEQ // tm))


def _const_spec(shape):
    nd = len(shape)
    return pl.BlockSpec(shape, lambda *_: (0,) * nd, pipeline_mode=pl.Buffered(1))


def _params(n_axes=1, vmem=VMEM_LIMIT):
    return pltpu.CompilerParams(dimension_semantics=("arbitrary",) * n_axes, vmem_limit_bytes=vmem)


def _layer_norm(v, g, b):
    mu = jnp.mean(v, axis=-1, keepdims=True)
    d = v - mu
    var = jnp.mean(d * d, axis=-1, keepdims=True)
    return d * lax.rsqrt(var + LN_EPS) * g + b


def _split_bf16(v):
    hi = v.astype(BF16)
    lo = (v - hi.astype(F32)).astype(BF16)
    return hi, lo


def _dot(a, b):
    return jnp.dot(a, b, preferred_element_type=F32)


def _silu(v):
    return v / (1.0 + jnp.exp(-v))


CAST_BLOCKS = 16


def _cast_stream(srcs):
    specs = [pl.BlockSpec((w.shape[0] // CAST_BLOCKS, w.shape[1]),
                          lambda i, *_: (jnp.minimum(i, CAST_BLOCKS - 1), 0)) for w in srcs]
    return specs, [jax.ShapeDtypeStruct(w.shape, BF16) for w in srcs]


def _cast_step(srcs, dsts):
    @pl.when(pl.program_id(0) < CAST_BLOCKS)
    def _():
        for src, dst in zip(srcs, dsts):
            dst[...] = src[...].astype(BF16)


def _mod_kernel(cond_ref, w_ref, b_ref, o_ref):
    s = _silu(cond_ref[...]).astype(BF16)
    o_ref[...] = _dot(s, w_ref[...].astype(BF16)) + b_ref[...]


def _modulation(cond8, ada_w, ada_b):
    nb = 6 * D // 1024
    return pl.pallas_call(
        _mod_kernel,
        grid=(DEPTH, nb),
        in_specs=[
            pl.BlockSpec((8, D), lambda l, j: (0, 0)),
            pl.BlockSpec((None, D, 1024), lambda l, j: (l, 0, j)),
            pl.BlockSpec((None, 1, 1024), lambda l, j: (l, 0, j)),
        ],
        out_specs=pl.BlockSpec((None, 8, 1024), lambda l, j: (l, 0, j)),
        out_shape=jax.ShapeDtypeStruct((DEPTH, 8, 6 * D), F32),
        compiler_params=_params(2),
        name="adaln_modulation",
    )(cond8, ada_w, ada_b.reshape(DEPTH, 1, 6 * D))


def _pool_tables():
    t = np.arange(TM)[:, None]
    r = np.arange(TM + 2 * HALO)[None, :]
    pos = r - HALO
    bands = np.zeros((4, len(POOL_WINDOWS), TM, TM + 2 * HALO), np.float32)
    inv = np.zeros((4, TM, 128), np.float32)
    for variant in range(4):
        left_ok, right_ok = variant & 1, variant >> 1
        col_ok = (r >= (0 if left_ok else HALO)) & (r < (TM + 2 * HALO if right_ok else TM + HALO))
        first, last = (-HALO if left_ok else 0), (TM + HALO if right_ok else TM)
        for gi, w in enumerate(POOL_WINDOWS):
            bands[variant, gi] = (pos >= t - w // 2) & (pos < t + w // 2) & col_ok
            cnt = np.minimum(t[:, 0] + w // 2, last) - np.maximum(t[:, 0] - w // 2, first)
            inv[variant, :, gi] = 1.0 / cnt
    return jnp.asarray(bands, BF16), jnp.asarray(inv, F32)


def _even_tiles(x_alls, left_oks, right_oks, mod_ref, win_ref, convw_ref, poolw_ref, pscale_ref, wout_ref,
                g_ref, b_ref, band_ref, inv_ref):
    shift, scale, gate = mod_ref[0:1, :], mod_ref[1:2, :], mod_ref[2:3, :]
    n = len(x_alls)
    hs = []
    for x_all, left_ok, right_ok in zip(x_alls, left_oks, right_oks):
        h = x_all * (1.0 + scale) + shift
        hs.append(jnp.concatenate([jnp.where(left_ok, h[:HALO], 0.0), h[HALO:HALO + TM],
                                   jnp.where(right_ok, h[HALO + TM:], 0.0)], axis=0).astype(BF16))
    us = [_dot(h, win_ref[...]) for h in hs]

    mixes = []
    for u, left_ok, right_ok in zip(us, left_oks, right_oks):
        ux, ub = u[:, 0:CONV_DIM], u[:, CONV_DIM:2 * CONV_DIM]
        uc, up = u[:, 2 * CONV_DIM:3 * CONV_DIM], u[:, 3 * CONV_DIM:]
        z = uc * ux
        conv = (z[HALO - 1:HALO - 1 + TM] * convw_ref[0:1, :]
                + z[HALO:HALO + TM] * convw_ref[1:2, :]
                + z[HALO + 1:HALO + 1 + TM] * convw_ref[2:3, :])
        ya = ub[HALO:HALO + TM] * conv

        variant = left_ok.astype(jnp.int32) + 2 * right_ok.astype(jnp.int32)
        inv_cnt = inv_ref[variant]
        up_hi, up_lo = _split_bf16(up)
        yb_groups = []
        for gi in range(len(POOL_WINDOWS)):
            band = band_ref[variant, gi]
            sl = slice(gi * POOL_GROUP, (gi + 1) * POOL_GROUP)
            tot = _dot(band, up_hi[:, sl]) + _dot(band, up_lo[:, sl])
            p = tot * inv_cnt[:, gi:gi + 1] - up[HALO:HALO + TM, sl]
            yb_groups.append(_dot(p.astype(BF16), poolw_ref[gi]))
        yb = jnp.concatenate(yb_groups, axis=1) * pscale_ref[...]
        mixes.append(jnp.concatenate([ya, yb], axis=1).astype(BF16))

    ys = [_dot(mix, wout_ref[...]) for mix in mixes]
    return [_layer_norm(ALPHA * x_alls[k][HALO:HALO + TM] + gate * ys[k], g_ref[...], b_ref[...])
            for k in range(n)]


EV_TILES = 2
EV_ROWS = EV_TILES * TM


N_CAST_EVEN = 3


def _even_mixer_kernel(xctx_ref, xprev_ref, xlat_ref, xnext_ref, mod_ref, win_ref, convw_ref, poolw_ref,
                       pscale_ref, wout_ref, g_ref, b_ref, band_ref, inv_ref, *rest):
    n = N_CAST_EVEN
    cast_in, o_ref, cast_out, xall_ref = rest[:n], rest[n], rest[n + 1:2 * n + 1], rest[-1]
    _cast_step(cast_in, cast_out)
    s = pl.program_id(0)
    n_ctx = T_P // EV_ROWS
    is_latent = s >= n_ctx
    first_tile = lax.rem(s - n_ctx, DEC_SEQ // EV_ROWS) * EV_TILES

    @pl.when(jnp.logical_not(is_latent))
    def _():
        for k in range(EV_TILES):
            xall_ref[k, 0:HALO, :] = jnp.zeros((HALO, D), F32)
            xall_ref[k, HALO:HALO + TM, :] = xctx_ref[k * TM:(k + 1) * TM, :]
            xall_ref[k, HALO + TM:, :] = jnp.zeros((HALO, D), F32)

    @pl.when(is_latent)
    def _():
        for k in range(EV_TILES):
            lo, hi = k * TM, (k + 1) * TM
            xall_ref[k, 0:HALO, :] = xprev_ref[...] if k == 0 else xlat_ref[lo - HALO:lo, :]
            xall_ref[k, HALO:HALO + TM, :] = xlat_ref[lo:hi, :]
            xall_ref[k, HALO + TM:, :] = xnext_ref[...] if k == EV_TILES - 1 else xlat_ref[hi:hi + HALO, :]

    left_oks = [jnp.logical_and(is_latent, first_tile + k != 0) for k in range(EV_TILES)]
    right_oks = [jnp.logical_and(is_latent, first_tile + k != TPS - 1) for k in range(EV_TILES)]
    outs = _even_tiles([xall_ref[k] for k in range(EV_TILES)], left_oks, right_oks, mod_ref, win_ref,
                       convw_ref, poolw_ref, pscale_ref, wout_ref, g_ref, b_ref, band_ref, inv_ref)
    for k in range(EV_TILES):
        o_ref[k * TM:(k + 1) * TM, :] = outs[k]


def _even_mixer(x_ctx, x_lat, mods, w_in, conv_w, pool_w, pool_scale, w_out, g, b, cast_srcs):
    assert len(cast_srcs) == N_CAST_EVEN
    hb = EV_ROWS // HALO
    n8 = T_S // HALO
    n_ctx = T_P // EV_ROWS
    lat = lambda i: jnp.maximum(i - n_ctx, 0)
    cast_specs, cast_shapes = _cast_stream(cast_srcs)
    bands, inv_cnt = _pool_tables()
    outs = pl.pallas_call(
        _even_mixer_kernel,
        grid=(T // EV_ROWS,),
        in_specs=[
            pl.BlockSpec((EV_ROWS, D), lambda i: (jnp.minimum(i, n_ctx - 1), 0)),
            pl.BlockSpec((HALO, D), lambda i: (jnp.maximum(lat(i) * hb - 1, 0), 0)),
            pl.BlockSpec((EV_ROWS, D), lambda i: (lat(i), 0)),
            pl.BlockSpec((HALO, D), lambda i: (jnp.minimum((lat(i) + 1) * hb, n8 - 1), 0)),
            pl.BlockSpec((None, 6, D), lambda i: (_cond_index(i, EV_ROWS), 0, 0)),
            _const_spec((D, 4 * CONV_DIM)),
            _const_spec((3, CONV_DIM)),
            _const_spec((4, POOL_GROUP, POOL_GROUP)),
            _const_spec((1, 4 * POOL_GROUP)),
            _const_spec((D, D)),
            _const_spec((1, D)),
            _const_spec((1, D)),
            _const_spec(bands.shape),
            _const_spec(inv_cnt.shape),
        ] + cast_specs,
        out_specs=[pl.BlockSpec((EV_ROWS, D), lambda i: (i, 0))] + cast_specs,
        out_shape=[jax.ShapeDtypeStruct((T, D), F32)] + cast_shapes,
        scratch_shapes=[pltpu.VMEM((EV_TILES, TM + 2 * HALO, D), F32)],
        compiler_params=_params(1),
        name="even_mixer",
    )(x_ctx, x_lat, x_lat, x_lat, mods, w_in, conv_w, pool_w, pool_scale, w_out, g, b, bands, inv_cnt, *cast_srcs)
    return outs[0], outs[1:]


FF_CHUNK = D_FF // 2


def _ffn_kernel(x_ref, mod_ref, wg_ref, wu_ref, wd_ref, g_ref, b_ref, cast_ref, o_ref, cast_out_ref):
    _cast_step([cast_ref], [cast_out_ref])
    shift, scale, gate = mod_ref[3:4, :], mod_ref[4:5, :], mod_ref[5:6, :]
    x = x_ref[...]
    h = (x * (1.0 + scale) + shift).astype(BF16)
    f = jnp.zeros((TM_FFN, D), F32)
    for c in range(D_FF // FF_CHUNK):
        sl = slice(c * FF_CHUNK, (c + 1) * FF_CHUNK)
        a = _silu(_dot(h, wg_ref[:, sl])) * _dot(h, wu_ref[:, sl])
        f = f + _dot(a.astype(BF16), wd_ref[sl, :])
    o_ref[...] = _layer_norm(ALPHA * x + gate * f, g_ref[...], b_ref[...])


def _ffn(x, mods, wg, wu, wd, g, b, cast_src):
    cast_specs, cast_shapes = _cast_stream([cast_src])
    return pl.pallas_call(
        _ffn_kernel,
        grid=(T // TM_FFN,),
        in_specs=[
            pl.BlockSpec((TM_FFN, D), lambda i: (i, 0)),
            pl.BlockSpec((None, 6, D), lambda i: (_cond_index(i, TM_FFN), 0, 0)),
            _const_spec((D, D_FF)),
            _const_spec((D, D_FF)),
            _const_spec((D_FF, D)),
            _const_spec((1, D)),
            _const_spec((1, D)),
        ] + cast_specs,
        out_specs=[pl.BlockSpec((TM_FFN, D), lambda i: (i, 0))] + cast_specs,
        out_shape=[jax.ShapeDtypeStruct((T, D), F32)] + cast_shapes,
        compiler_params=_params(1),
        name="dense_swiglu",
    )(x, mods, wg, wu, wd, g, b, cast_src)


W_IN_EXT = Q_RANK + KV_RANK + FNET_DIM + 128 + 128
QH = 256
ATT_SCALE = (QK_NOPE + QK_ROPE) ** -0.5 * float(np.log2(np.e))


def _odd_proj_kernel(x_ref, mod_ref, rope_ref, win_ref, qn_ref, kvn_ref, wqa_ref, wqb_ref,
                     avg_ref, dfth_ref, dftl_ref, cast_ref,
                     q_ref, ckv_ref, kpe_ref, y_ref, cast_out_ref):
    _cast_step([cast_ref], [cast_out_ref])
    shift, scale = mod_ref[0:1, :], mod_ref[1:2, :]
    h = (x_ref[...] * (1.0 + scale) + shift).astype(BF16)
    u = _dot(h, win_ref[...])
    uq = u[:, 0:Q_RANK]
    ukv = u[:, Q_RANK:Q_RANK + KV_RANK]
    uf = u[:, Q_RANK + KV_RANK:Q_RANK + KV_RANK + FNET_DIM]
    o = Q_RANK + KV_RANK + FNET_DIM
    upe, upe_rot = u[:, o:o + 128], u[:, o + 128:o + 256]
    cos, sin = rope_ref[:, 0:128], rope_ref[:, 128:256]

    ckv_ref[...] = ukv * lax.rsqrt(jnp.mean(ukv * ukv, axis=-1, keepdims=True) + RMS_EPS) * kvn_ref[...]
    kpe_ref[...] = upe * cos + upe_rot * sin

    qlat = (uq * lax.rsqrt(jnp.mean(uq * uq, axis=-1, keepdims=True) + RMS_EPS) * qn_ref[...]).astype(BF16)
    qa = _dot(qlat, wqa_ref[...])
    qb = _dot(qlat, wqb_ref[...])
    for hd in range(N_HEADS):
        nope = qa[:, hd * QH:hd * QH + 128]
        pe = qa[:, hd * QH + 128:(hd + 1) * QH] * cos + qb[:, hd * 128:(hd + 1) * 128] * sin
        q_ref[:, hd * QH:hd * QH + 128] = (nope * ATT_SCALE).astype(BF16)
        q_ref[:, hd * QH + 128:(hd + 1) * QH] = (pe * ATT_SCALE).astype(BF16)

    avg = avg_ref[...]
    uf_hi, uf_lo = _split_bf16(uf)
    mu = _dot(uf_hi, avg) + _dot(uf_lo, avg)
    dlt = uf - mu
    sq_hi, sq_lo = _split_bf16(dlt * dlt)
    var = _dot(sq_hi, avg) + _dot(sq_lo, avg)
    xn = dlt * lax.rsqrt(var + LN_EPS)
    xn_hi, xn_lo = _split_bf16(xn)
    y = _dot(xn_hi, dfth_ref[...]) + _dot(xn_lo, dfth_ref[...]) + _dot(xn_hi, dftl_ref[...])
    y_ref[...] = y.astype(BF16)


def _odd_proj(x, mods, rope_tab, w_in_ext, q_norm, kv_norm, wqa, wqb, avg, dft_hi, dft_lo, cast_src):
    cast_specs, cast_shapes = _cast_stream([cast_src])

    def rope_index(i):
        return (jnp.where(i < T_P // TM2, 0, 1 + lax.rem(i - T_P // TM2, DEC_SEQ // TM2)), 0)

    return pl.pallas_call(
        _odd_proj_kernel,
        grid=(T // TM2,),
        in_specs=[
            pl.BlockSpec((TM2, D), lambda i: (i, 0)),
            pl.BlockSpec((None, 6, D), lambda i: (_cond_index(i, TM2), 0, 0)),
            pl.BlockSpec((TM2, 256), rope_index),
            _const_spec((D, W_IN_EXT)),
            _const_spec((1, Q_RANK)),
            _const_spec((1, KV_RANK)),
            _const_spec((Q_RANK, N_HEADS * QH)),
            _const_spec((Q_RANK, N_HEADS * 128)),
            _const_spec((FNET_DIM, FNET_DIM)),
            _const_spec((FNET_DIM, 2 * FNET_DIM)),
            _const_spec((FNET_DIM, 2 * FNET_DIM)),
        ] + cast_specs,
        out_specs=[
            pl.BlockSpec((TM2, N_HEADS * QH), lambda i: (i, 0)),
            pl.BlockSpec((TM2, KV_RANK), lambda i: (i, 0)),
            pl.BlockSpec((TM2, 128), lambda i: (i, 0)),
            pl.BlockSpec((TM2, 2 * FNET_DIM), lambda i: (i, 0)),
        ] + cast_specs,
        out_shape=[
            jax.ShapeDtypeStruct((T, N_HEADS * QH), BF16),
            jax.ShapeDtypeStruct((T, KV_RANK), F32),
            jax.ShapeDtypeStruct((T, 128), F32),
            jax.ShapeDtypeStruct((T, 2 * FNET_DIM), BF16),
        ] + cast_shapes,
        compiler_params=_params(1),
        name="odd_projections",
    )(x, mods, rope_tab, w_in_ext, q_norm, kv_norm, wqa, wqb, avg, dft_hi, dft_lo, cast_src)


V_OFF = N_HEADS * QH
KV_COLS = V_OFF + N_HEADS * V_DIM


def _attn_body(q_ref, kv_ref, o_ref):
    for hd in range(N_HEADS):
        qh = q_ref[:, hd * QH:(hd + 1) * QH]
        kh = kv_ref[:, hd * QH:(hd + 1) * QH]
        s = lax.dot_general(qh, kh, (((1,), (1,)), ((), ())), preferred_element_type=F32)
        p = jnp.exp2(s - jnp.max(s, axis=-1, keepdims=True))
        den = jnp.sum(p, axis=-1, keepdims=True)
        vh = kv_ref[:, V_OFF + hd * V_DIM:V_OFF + (hd + 1) * V_DIM]
        o = _dot(p.astype(BF16), vh)
        o_ref[:, hd * 128:(hd + 1) * 128] = (o / den).astype(BF16)


KV_CHUNK = 512


def _attn_kernel(q_ref, ckvp_ref, kpep_ref, ckvs_ref, kpes_ref, cckv_ref, ckpe_ref, wkv_ref, o_ref,
                 kvp_s, kvs_s):
    i = pl.program_id(0)

    def expand(ckv, kpe, dst, row0):
        n = ckv.shape[0]
        kv = _dot(ckv.astype(BF16), wkv_ref[...]).astype(BF16)
        kpe = kpe.astype(BF16)
        for hd in range(N_HEADS):
            dst[row0:row0 + n, hd * QH:hd * QH + QK_NOPE] = kv[:, hd * QK_NOPE:(hd + 1) * QK_NOPE]
            dst[row0:row0 + n, hd * QH + QK_NOPE:(hd + 1) * QH] = kpe
        dst[row0:row0 + n, V_OFF:] = kv[:, N_HEADS * QK_NOPE:]

    @pl.when(i < NP_TILES)
    def _():
        expand(ckvp_ref[...], kpep_ref[...], kvp_s, 0)
        _attn_body(q_ref, kvp_s, o_ref)

    @pl.when(i >= NP_TILES)
    def _():
        @pl.when(lax.rem(i - NP_TILES, TPS) == 0)
        def _():
            expand(cckv_ref[...], ckpe_ref[...], kvs_s, 0)
            for c in range(DEC_SEQ // KV_CHUNK):
                rows = slice(c * KV_CHUNK, (c + 1) * KV_CHUNK)
                expand(ckvs_ref[rows, :], kpes_ref[rows, :], kvs_s, PAST + c * KV_CHUNK)

        _attn_body(q_ref, kvs_s, o_ref)


def _attention(q, ckv, kpe, cache_ckv, cache_kpe128, w_kv):
    ctx_blk = lambda i: (jnp.minimum(i, NP_TILES - 1), 0)
    lat_b = lambda i: jnp.maximum(i - NP_TILES, 0) // TPS
    lat_blk = lambda i: (T_P // DEC_SEQ + lat_b(i), 0)
    return pl.pallas_call(
        _attn_kernel,
        grid=(N_TILES,),
        in_specs=[
            pl.BlockSpec((TM, N_HEADS * QH), lambda i: (i, 0)),
            pl.BlockSpec((SEQ, KV_RANK), ctx_blk),
            pl.BlockSpec((SEQ, 128), ctx_blk),
            pl.BlockSpec((DEC_SEQ, KV_RANK), lat_blk),
            pl.BlockSpec((DEC_SEQ, 128), lat_blk),
            pl.BlockSpec((None, PAST, KV_RANK), lambda i: (lat_b(i), 0, 0)),
            pl.BlockSpec((None, PAST, 128), lambda i: (lat_b(i), 0, 0)),
            _const_spec((KV_RANK, 2 * N_HEADS * 128)),
        ],
        out_specs=pl.BlockSpec((TM, N_HEADS * V_DIM), lambda i: (i, 0)),
        out_shape=jax.ShapeDtypeStruct((T, N_HEADS * V_DIM), BF16),
        scratch_shapes=[pltpu.VMEM((SEQ, KV_COLS), BF16), pltpu.VMEM((LK_S, KV_COLS), BF16)],
        compiler_params=_params(1),
        name="attention",
    )(q, ckv, kpe, ckv, kpe, cache_ckv, cache_kpe128, w_kv)


def _pos_dft_body(y_ref, c, s, o_ref):
    f = _dot(c, y_ref[:, 0:FNET_DIM]) + _dot(s, y_ref[:, FNET_DIM:])
    o_ref[...] = f.astype(BF16)


def _pos_dft_kernel(yp_ref, cp_ref, sp_ref, ys_ref, c0_ref, s0_ref, cb_ref, sb_ref, o_ref):
    @pl.when(pl.program_id(0) < NP_TILES)
    def _():
        _pos_dft_body(yp_ref, cp_ref[...].astype(BF16), sp_ref[...].astype(BF16), o_ref)

    @pl.when(pl.program_id(0) >= NP_TILES)
    def _():
        j = lax.rem(pl.program_id(0) - NP_TILES, TPS)
        cb, sb = cb_ref[pl.ds(j, 1), :], sb_ref[pl.ds(j, 1), :]
        c0, s0 = c0_ref[...], s0_ref[...]
        _pos_dft_body(ys_ref, (c0 * cb - s0 * sb).astype(BF16), (s0 * cb + c0 * sb).astype(BF16), o_ref)


def _pos_dft(y, tabs_p, base_s, step_s):
    ctx_blk = lambda i: (jnp.minimum(i, NP_TILES - 1), 0)
    lat_seq = lambda i: (T_P // DEC_SEQ + jnp.maximum(i - NP_TILES, 0) // TPS, 0)
    return pl.pallas_call(
        _pos_dft_kernel,
        grid=(N_TILES,),
        in_specs=[pl.BlockSpec((SEQ, 2 * FNET_DIM), ctx_blk)]
        + [_const_spec((SEQ, SEQ))] * 2
        + [pl.BlockSpec((DEC_SEQ, 2 * FNET_DIM), lat_seq)]
        + [_const_spec((TM, DEC_SEQ))] * 2
        + [_const_spec((TPS, DEC_SEQ))] * 2,
        out_specs=pl.BlockSpec((TM, FNET_DIM), lambda i: (i, 0)),
        out_shape=jax.ShapeDtypeStruct((T, FNET_DIM), BF16),
        compiler_params=_params(1),
        name="pos_dft",
    )(y, *tabs_p, y, *base_s, *step_s)


def _odd_merge_kernel(x_ref, attn_ref, f_ref, mod_ref, fw_ref, wo_ref, g_ref, b_ref, rh_ref, rl_ref, tri_ref,
                      cast_ref, xo_ref, info_ref, infot_ref, cnt_ref, cast_out_ref, carry_ref):
    _cast_step([cast_ref], [cast_out_ref])

    @pl.when(pl.program_id(0) == 0)
    def _():
        carry_ref[...] = jnp.zeros_like(carry_ref)

    gate = mod_ref[2:3, :]
    shift2, scale2 = mod_ref[3:4, :], mod_ref[4:5, :]
    fm = _dot(f_ref[...], fw_ref[...]).astype(BF16)
    y = _dot(attn_ref[...], wo_ref[0:N_HEADS * V_DIM, :]) + _dot(fm, wo_ref[N_HEADS * V_DIM:, :])
    x = _layer_norm(ALPHA * x_ref[...] + gate * y, g_ref[...], b_ref[...])
    xo_ref[...] = x
    h = x * (1.0 + scale2) + shift2

    h_hi, h_lo = _split_bf16(h)
    logits = _dot(h_hi, rh_ref[...]) + _dot(h_lo, rh_ref[...]) + _dot(h_hi, rl_ref[...])
    lane = lax.broadcasted_iota(jnp.int32, (TM2, 128), 1)
    neg = jnp.float32(-jnp.inf)
    logits = jnp.where(lane < N_EXPERTS, logits, neg)
    m1 = jnp.max(logits, axis=-1, keepdims=True)
    i1 = jnp.min(jnp.where(logits == m1, lane, 128), axis=-1, keepdims=True)
    rest = jnp.where(lane == i1, neg, logits)
    m2 = jnp.max(rest, axis=-1, keepdims=True)
    i2 = jnp.min(jnp.where(rest == m2, lane, 128), axis=-1, keepdims=True)
    e2 = jnp.exp(m2 - m1)
    w1 = 1.0 / (1.0 + e2)
    w2 = e2 / (1.0 + e2)
    info = jnp.where(lane == 0, w1, 0.0)
    info = jnp.where(lane == 1, w2, info)
    info = jnp.where(lane == 2, i1.astype(F32), info)
    info = jnp.where(lane == 3, i2.astype(F32), info)

    uses = jnp.logical_or(lane == i1, lane == i2)
    seen = _dot(tri_ref[...], jnp.where(uses, 1.0, 0.0).astype(BF16)) + carry_ref[...]
    r1 = jnp.sum(jnp.where(lane == i1, seen, 0.0), axis=-1, keepdims=True)
    r2 = jnp.sum(jnp.where(lane == i2, seen, 0.0), axis=-1, keepdims=True)
    info = jnp.where(lane == 4, r1, info)
    info = jnp.where(lane == 5, r2, info)
    info_ref[...] = info
    infot_ref[...] = info.T
    total = carry_ref[...] + jnp.sum(jnp.where(uses, 1.0, 0.0), axis=0, keepdims=True)
    carry_ref[...] = total
    cnt_ref[...] = jnp.broadcast_to(total, cnt_ref.shape)


def _odd_merge(x, attn, f, mods, fnet_w, w_out, g, b, r_hi, r_lo, cast_src):
    row = lambda i: (i, 0)
    tri = jnp.asarray(np.tril(np.ones((TM2, TM2), np.float32), -1), BF16)
    cast_specs, cast_shapes = _cast_stream([cast_src])
    return pl.pallas_call(
        _odd_merge_kernel,
        grid=(T // TM2,),
        in_specs=[
            pl.BlockSpec((TM2, D), row),
            pl.BlockSpec((TM2, N_HEADS * V_DIM), row),
            pl.BlockSpec((TM2, FNET_DIM), row),
            pl.BlockSpec((None, 6, D), lambda i: (_cond_index(i, TM2), 0, 0)),
            _const_spec((FNET_DIM, FNET_DIM)),
            _const_spec((N_HEADS * V_DIM + FNET_DIM, D)),
            _const_spec((1, D)),
            _const_spec((1, D)),
            _const_spec((D, 128)),
            _const_spec((D, 128)),
            _const_spec((TM2, TM2)),
        ] + cast_specs,
        out_specs=[pl.BlockSpec((TM2, D), row), pl.BlockSpec((TM2, 128), row),
                   pl.BlockSpec((128, TM2), lambda i: (0, i)), pl.BlockSpec((8, 128), lambda i: (i, 0))] + cast_specs,
        out_shape=[jax.ShapeDtypeStruct((T, D), F32), jax.ShapeDtypeStruct((T, 128), F32),
                   jax.ShapeDtypeStruct((128, T), F32), jax.ShapeDtypeStruct((T // TM2 * 8, 128), F32)] + cast_shapes,
        scratch_shapes=[pltpu.VMEM((1, 128), F32)],
        compiler_params=_params(1),
        name="odd_merge_router",
    )(x, attn, f, mods, fnet_w, w_out, g, b, r_hi, r_lo, tri, cast_src)


TM_D = 512
ROW = (8, 128)
DMA_UNROLL = 8


def _row_copy(src, s, dst, d, sem):
    return pltpu.make_async_copy(src.at[s], dst.at[d], sem)


CHUNK = 16
LOCAL_ROWS = 2 * TM2 + N_EXPERTS * CHUNK


def _chunk_copies(tile, nch_ref, fn):
    for e in range(N_EXPERTS):
        def body(c, carry, e=e):
            fn(tile * N_EXPERTS + e, c, e % 2)
            return carry

        lax.fori_loop(0, nch_ref[tile * N_EXPERTS + e], body, 0)


def _dispatch_kernel(lp_ref, d0_ref, off_ref, nch_ref, pad_lo_ref, pad_hi_ref, x_ref, mod_ref, xs_ref,
                     h_ref, local_ref, zero_ref, sem, zsem):
    i = pl.program_id(0)
    base = i * TM_D
    slot = lax.rem(i, 2)

    @pl.when(i == 0)
    def _():
        local_ref[...] = jnp.zeros_like(local_ref)
        zero_ref[...] = jnp.zeros_like(zero_ref)
        for e in range(N_EXPERTS + 1):
            def zissue(r, carry):
                _row_copy(zero_ref, 0, xs_ref, r, zsem).start()
                return carry

            def zdrain(r, carry):
                _row_copy(zero_ref, 0, xs_ref, 0, zsem).wait()
                return carry

            lax.fori_loop(pad_lo_ref[e], pad_hi_ref[e], zissue, 0)
            lax.fori_loop(pad_lo_ref[e], pad_hi_ref[e], zdrain, 0)

    shift2, scale2 = mod_ref[3:4, :], mod_ref[4:5, :]
    h_ref[...] = (x_ref[...] * (1.0 + scale2) + shift2).reshape((TM_D,) + ROW)

    def place(r, carry):
        row = h_ref[r]
        local_ref[slot, lp_ref[base + r]] = row
        local_ref[slot, lp_ref[T + base + r]] = row
        return carry

    lax.fori_loop(0, TM_D, place, 0, unroll=DMA_UNROLL)

    def chunk(s, seg, c):
        return pltpu.make_async_copy(local_ref.at[s, pl.ds(off_ref[seg] + c * CHUNK, CHUNK)],
                                     xs_ref.at[pl.ds(d0_ref[seg] + c * CHUNK, CHUNK)], sem.at[s])

    @pl.when(i > 0)
    def _():
        _chunk_copies(i - 1, nch_ref, lambda seg, c, prio: chunk(1 - slot, seg, c).wait())

    _chunk_copies(i, nch_ref, lambda seg, c, prio: chunk(slot, seg, c).start(priority=prio))

    @pl.when(i == T // TM_D - 1)
    def _():
        _chunk_copies(i, nch_ref, lambda seg, c, prio: chunk(slot, seg, c).wait())


def _dispatch(lp, d0, off, nch, pad_lo, pad_hi, x, mods):
    assert TM_D == TM2
    return pl.pallas_call(
        _dispatch_kernel,
        grid_spec=pltpu.PrefetchScalarGridSpec(
            num_scalar_prefetch=6,
            grid=(T // TM_D,),
            in_specs=[pl.BlockSpec((TM_D, D), lambda i, *_: (i, 0)),
                      pl.BlockSpec((None, 6, D), lambda i, *_: (_cond_index(i, TM_D), 0, 0))],
            out_specs=pl.BlockSpec(memory_space=pl.ANY),
            scratch_shapes=[pltpu.VMEM((TM_D,) + ROW, F32), pltpu.VMEM((2, LOCAL_ROWS) + ROW, F32),
                            pltpu.VMEM((1,) + ROW, F32), pltpu.SemaphoreType.DMA((2,)),
                            pltpu.SemaphoreType.DMA(())],
        ),
        out_shape=jax.ShapeDtypeStruct((R_MAX,) + ROW, F32),
        compiler_params=pltpu.CompilerParams(dimension_semantics=("arbitrary",), has_side_effects=True),
        name="expert_dispatch",
    )(lp, d0, off, nch, pad_lo, pad_hi, x, mods)


def _expert_kernel(te_ref, nt_ref, xs_ref, wg_ref, wu_ref, wd_ref, o_ref):
    @pl.when(pl.program_id(0) < nt_ref[0])
    def _():
        h = xs_ref[...].reshape(TM_E, D).astype(BF16)
        a = _silu(_dot(h, wg_ref[...])) * _dot(h, wu_ref[...])
        o_ref[...] = _dot(a.astype(BF16), wd_ref[...]).reshape((TM_E,) + ROW)

    @pl.when(pl.program_id(0) >= nt_ref[0])
    def _():
        o_ref[...] = jnp.zeros_like(o_ref)


def _experts(tile_expert, n_used, xs, wg, wu, wd):
    return pl.pallas_call(
        _expert_kernel,
        grid_spec=pltpu.PrefetchScalarGridSpec(
            num_scalar_prefetch=2,
            grid=(N_ETILES,),
            in_specs=[
                pl.BlockSpec((TM_E,) + ROW, lambda i, te, nt: (jnp.minimum(i, nt[0] - 1), 0, 0)),
                pl.BlockSpec((None, D, D_FF_EXPERT), lambda i, te, nt: (te[i], 0, 0)),
                pl.BlockSpec((None, D, D_FF_EXPERT), lambda i, te, nt: (te[i], 0, 0)),
                pl.BlockSpec((None, D_FF_EXPERT, D), lambda i, te, nt: (te[i], 0, 0)),
            ],
            out_specs=pl.BlockSpec((TM_E,) + ROW, lambda i, te, nt: (i, 0, 0)),
        ),
        out_shape=jax.ShapeDtypeStruct((R_MAX,) + ROW, F32),
        compiler_params=_params(1),
        name="expert_swiglu",
    )(tile_expert, n_used, xs, wg, wu, wd)


def _combine_kernel(lp_ref, d0_ref, off_ref, nch_ref, x_ref, info_ref, mod_ref, g_ref, b_ref, ys_ref,
                    op_ref, os_ref, local_ref, rows_ref, sem):
    i = pl.program_id(0)
    slot = lax.rem(i, 2)

    def chunk(s, seg, c):
        return pltpu.make_async_copy(ys_ref.at[pl.ds(d0_ref[seg] + c * CHUNK, CHUNK)],
                                     local_ref.at[s, pl.ds(off_ref[seg] + c * CHUNK, CHUNK)], sem.at[s])

    @pl.when(i == 0)
    def _():
        _chunk_copies(0, nch_ref, lambda seg, c, prio: chunk(0, seg, c).start(priority=prio))

    @pl.when(i + 1 < T // TM2)
    def _():
        _chunk_copies(i + 1, nch_ref, lambda seg, c, prio: chunk(1 - slot, seg, c).start(priority=prio))

    _chunk_copies(i, nch_ref, lambda seg, c, prio: chunk(slot, seg, c).wait())

    def pick(r, carry):
        rows_ref[0, r] = local_ref[slot, lp_ref[i * TM2 + r]]
        rows_ref[1, r] = local_ref[slot, lp_ref[T + i * TM2 + r]]
        return carry

    lax.fori_loop(0, TM2, pick, 0, unroll=DMA_UNROLL)

    gate = mod_ref[5:6, :]
    w1, w2 = info_ref[:, 0:1], info_ref[:, 1:2]
    y = w1 * rows_ref[0].reshape(TM2, D) + w2 * rows_ref[1].reshape(TM2, D)
    out = _layer_norm(ALPHA * x_ref[...] + gate * y, g_ref[...], b_ref[...])

    @pl.when(pl.program_id(0) < T_P // TM2)
    def _():
        op_ref[...] = out

    @pl.when(pl.program_id(0) >= T_P // TM2)
    def _():
        os_ref[...] = out


def _combine(lp, d0, off, nch, x, info, mods, g, b, ys):
    return pl.pallas_call(
        _combine_kernel,
        grid_spec=pltpu.PrefetchScalarGridSpec(
            num_scalar_prefetch=4,
            grid=(T // TM2,),
            in_specs=[
                pl.BlockSpec((TM2, D), lambda i, *_: (i, 0)),
                pl.BlockSpec((TM2, 128), lambda i, *_: (i, 0)),
                pl.BlockSpec((None, 6, D), lambda i, *_: (_cond_index(i, TM2), 0, 0)),
                pl.BlockSpec((1, D), lambda i, *_: (0, 0)),
                pl.BlockSpec((1, D), lambda i, *_: (0, 0)),
                pl.BlockSpec(memory_space=pl.ANY),
            ],
            out_specs=[pl.BlockSpec((TM2, D), lambda i, *_: (jnp.minimum(i, T_P // TM2 - 1), 0)),
                       pl.BlockSpec((TM2, D), lambda i, *_: (jnp.maximum(i - T_P // TM2, 0), 0))],
            scratch_shapes=[pltpu.VMEM((2, LOCAL_ROWS) + ROW, F32), pltpu.VMEM((2, TM2) + ROW, F32),
                            pltpu.SemaphoreType.DMA((2,))],
        ),
        out_shape=[jax.ShapeDtypeStruct((T_P, D), F32), jax.ShapeDtypeStruct((T_S, D), F32)],
        compiler_params=_params(1),
        name="expert_combine",
    )(lp, d0, off, nch, x, info, mods, g, b, ys)


def _rot_cols(w):
    w4 = w.reshape(w.shape[:-1] + (2, 2, QK_ROPE // 4))
    return jnp.stack([-w4[..., 1, :], w4[..., 0, :]], axis=-2).reshape(w.shape)


def _rope_table():
    rows = DEC_SEQ // GRID_W
    row = np.repeat(np.arange(rows), GRID_W).astype(np.float32)
    col = np.tile(np.arange(GRID_W), rows).astype(np.float32)
    half = QK_ROPE // 2
    inv = (ROPE_THETA ** (-np.arange(0, half, 2, dtype=np.float32) / half)).astype(np.float32)
    ar, ac = row[:, None] * inv, col[:, None] * inv
    ang = np.concatenate([ar, ar, ac, ac], axis=-1)
    cos = np.concatenate([np.ones((TM2, QK_ROPE)), np.cos(ang)], axis=0)
    sin = np.concatenate([np.zeros((TM2, QK_ROPE)), np.sin(ang)], axis=0)
    n = cos.shape[0]
    return jnp.asarray(np.concatenate([cos, np.ones((n, 64)), sin, np.zeros((n, 64))], axis=1), F32)


def _dft_angles(rows, n):
    k = np.arange(n, dtype=np.int64)
    return ((np.asarray(rows, np.int64)[:, None] * k[None, :]) % n) * (2.0 * np.pi / n)


def _dft_tables(n):
    ang = _dft_angles(np.arange(n), n)
    return np.cos(ang) * n ** -0.5, np.sin(ang) * n ** -0.5


def _hi_lo(m):
    m = jnp.asarray(m, F32)
    hi = m.astype(BF16)
    return hi, (m - hi.astype(F32)).astype(BF16)


def _block_diag4(m):
    return np.kron(np.eye(4), m)


def kernel(x_prompt, x_sample, cache_ckv, cache_kpe, c, c_ctx, ada_w, ada_b, ln_g, ln_b, ev_w_in, ev_conv_w, ev_pool_w, ev_pool_scale, ev_w_out, ffn_w_gate, ffn_w_up, ffn_w_down, od_w_in, od_q_norm, od_kv_norm, od_w_q_b, od_w_kv_b, od_fnet_w, od_w_out, moe_router, moe_w_gate, moe_w_up, moe_w_down):
    cond8 = jnp.concatenate([c_ctx[None, :], c, jnp.zeros((8 - N_COND, D), F32)], axis=0)
    mods = _modulation(cond8, ada_w, ada_b)[:, :N_COND].reshape(DEPTH, N_COND, 6, D)

    x, (ffn_wg, ffn_wu, ffn_wd) = _even_mixer(
        x_prompt.reshape(T_P, D), x_sample.reshape(T_S, D), mods[0], ev_w_in[0].astype(BF16), ev_conv_w[0],
        ev_pool_w[0].astype(BF16), ev_pool_scale[0][None, :], ev_w_out[0].astype(BF16),
        ln_g[0, 0][None, :], ln_b[0, 0][None, :], [ffn_w_gate[0], ffn_w_up[0], ffn_w_down[0]])
    x, moe_wd = _ffn(x, mods[0], ffn_wg, ffn_wu, ffn_wd, ln_g[0, 1][None, :], ln_b[0, 1][None, :],
                     moe_w_down[0].reshape(N_EXPERTS * D_FF_EXPERT, D))

    w_in = od_w_in[0]
    w_pe = w_in[:, Q_RANK + KV_RANK:Q_RANK + KV_RANK + QK_ROPE]
    zpad = jnp.zeros((D, 64), F32)
    w_in_ext = jnp.concatenate([w_in[:, :Q_RANK + KV_RANK], w_in[:, Q_RANK + KV_RANK + QK_ROPE:],
                                w_pe, zpad, _rot_cols(w_pe), zpad], axis=1).astype(BF16)
    wq = od_w_q_b[0].reshape(Q_RANK, N_HEADS, QK_NOPE + QK_ROPE)
    zq = jnp.zeros((Q_RANK, N_HEADS, 64), F32)
    wqa = jnp.concatenate([wq, zq], axis=-1).reshape(Q_RANK, N_HEADS * QH).astype(BF16)
    wqb = jnp.concatenate([_rot_cols(wq[..., QK_NOPE:]), zq], axis=-1).reshape(Q_RANK, N_HEADS * 128).astype(BF16)
    wkv = od_w_kv_b[0].reshape(KV_RANK, N_HEADS, QK_NOPE + V_DIM)
    w_kv = jnp.concatenate([wkv[..., :QK_NOPE].reshape(KV_RANK, -1), wkv[..., QK_NOPE:].reshape(KV_RANK, -1)],
                           axis=1).astype(BF16)

    avg = jnp.asarray(_block_diag4(np.full((FNET_GROUP_DIM, FNET_GROUP_DIM), 1.0 / FNET_GROUP_DIM)), BF16)
    cc, sc = _dft_tables(FNET_GROUP_DIM)
    dft_hi, dft_lo = _hi_lo(np.concatenate([_block_diag4(cc), -_block_diag4(sc)], axis=1))

    q, ckv, kpe, y_dft, moe_wg = _odd_proj(x, mods[1], _rope_table(), w_in_ext, od_q_norm[0][None, :],
                                            od_kv_norm[0][None, :], wqa, wqb, avg, dft_hi, dft_lo,
                                            moe_w_gate[0].reshape(N_EXPERTS * D, D_FF_EXPERT))

    cache_kpe128 = jnp.pad(cache_kpe[:, 0], ((0, 0), (0, 0), (0, 128 - QK_ROPE)))
    attn = _attention(q, ckv, kpe, cache_ckv[:, 0], cache_kpe128, w_kv)

    tabs_p = tuple(jnp.asarray(m, F32) for m in _dft_tables(SEQ))
    a_base = _dft_angles(np.arange(TM), DEC_SEQ)
    a_step = _dft_angles(np.arange(TPS) * TM, DEC_SEQ)
    base_s = (jnp.asarray(np.cos(a_base) * DEC_SEQ ** -0.5, F32), jnp.asarray(np.sin(a_base) * DEC_SEQ ** -0.5, F32))
    step_s = (jnp.asarray(np.cos(a_step), F32), jnp.asarray(np.sin(a_step), F32))
    f = _pos_dft(y_dft, tabs_p, base_s, step_s)

    router = jnp.pad(moe_router[0], ((0, 0), (0, 128 - N_EXPERTS)))
    r_hi, r_lo = _hi_lo(router)
    x, info, info_t, cnt, moe_wu = _odd_merge(x, attn, f, mods[1], od_fnet_w[0].astype(BF16),
                                              od_w_out[0].astype(BF16), ln_g[1, 0][None, :], ln_b[1, 0][None, :],
                                              r_hi, r_lo, moe_w_up[0].reshape(N_EXPERTS * D, D_FF_EXPERT))

    n_tiles = T // TM2
    after = cnt.reshape(n_tiles, 8, 128)[:, 0, :N_EXPERTS].astype(jnp.int32)
    before = jnp.concatenate([jnp.zeros((1, N_EXPERTS), jnp.int32), after[:-1]], axis=0)
    counts = after[-1]
    padded = ((counts + CHUNK - 1 + TM_E - 1) // TM_E) * TM_E
    g_end = jnp.cumsum(padded)
    g_start = g_end - padded
    n_chunks = (after - before + CHUNK - 1) // CHUNK
    seg_off = CHUNK * (jnp.cumsum(n_chunks, axis=1) - n_chunks)
    seg_dst = g_start[None, :] + before
    choice = info_t[2:4].astype(jnp.int32)
    rank = info_t[4:6].astype(jnp.int32)
    shift_t = jnp.repeat((seg_off - before).T, TM2, axis=1)
    lp = rank
    for e in range(N_EXPERTS):
        lp = lp + jnp.where(choice == e, shift_t[e][None, :], 0)
    lp = lp.reshape(-1).astype(jnp.int32)
    d0, off, nch = (a.reshape(-1).astype(jnp.int32) for a in (seg_dst, seg_off, n_chunks))
    tile_row = jnp.arange(N_ETILES, dtype=jnp.int32) * TM_E
    tile_expert = jnp.minimum(jnp.sum((tile_row[:, None] >= g_end[None, :]).astype(jnp.int32), axis=1),
                              N_EXPERTS - 1).astype(jnp.int32)
    n_used = (g_end[-1:] // TM_E).astype(jnp.int32)

    pad_lo = jnp.concatenate([g_start + counts, g_end[-1:]]).astype(jnp.int32)
    pad_hi = jnp.concatenate([g_end, jnp.full((1,), R_MAX, jnp.int32)]).astype(jnp.int32)
    xs = _dispatch(lp, d0, off, nch, pad_lo, pad_hi, x, mods[1])
    ys = _experts(tile_expert, n_used, xs, moe_wg.reshape(N_EXPERTS, D, D_FF_EXPERT),
                  moe_wu.reshape(N_EXPERTS, D, D_FF_EXPERT), moe_wd.reshape(N_EXPERTS, D_FF_EXPERT, D))
    yp, ysm = _combine(lp, d0, off, nch, x, info, mods[1], ln_g[1, 1][None, :], ln_b[1, 1][None, :], ys)

    y_prompt = yp.reshape(BATCH, SEQ, D)
    y_sample = ysm.reshape(DEC_BATCH, DEC_SEQ, D)
    new_ckv = ckv[:T_P].reshape(BATCH, 1, SEQ, KV_RANK)
    new_kpe = kpe[:T_P, :QK_ROPE].reshape(BATCH, 1, SEQ, QK_ROPE)
    return (y_prompt, y_sample, new_ckv, new_kpe)
```

```python
import functools

import numpy as np
import jax
import jax.numpy as jnp
from jax import lax
from jax.experimental import pallas as pl
from jax.experimental.pallas import tpu as pltpu

F32 = jnp.float32
BF16 = jnp.bfloat16

D = 1024
BATCH, SEQ = 32, 256
DEC_BATCH, DEC_SEQ = 2, 2048
PAST = 512
GRID_W = 64
T_P = BATCH * SEQ
T_S = DEC_BATCH * DEC_SEQ
T = T_P + T_S
N_COND = 1 + DEC_BATCH

CONV_DIM = 512
POOL_WINDOWS = (2, 4, 8, 16)
POOL_GROUP = 128
N_HEADS = 8
QK_NOPE, QK_ROPE, V_DIM = 128, 64, 128
Q_RANK, KV_RANK = 384, 256
FNET_DIM, FNET_GROUP_DIM = 256, 64
D_FF = 2816
N_EXPERTS = 8
D_FF_EXPERT = 1792
DEPTH = 2
ALPHA = (2 * DEPTH) ** 0.25
LN_EPS = 1e-5
RMS_EPS = 1e-6
ROPE_THETA = 10000.0

TM = 256
NP_TILES = T_P // TM
TPS = DEC_SEQ // TM
N_TILES = T // TM
HALO = 8
TM2 = 512
TM_FFN = 512
LK_S = PAST + DEC_SEQ
TM_E = 256
N_ETILES = (2 * T) // TM_E + N_EXPERTS + 1
R_MAX = N_ETILES * TM_E
VMEM_LIMIT = 56 * 1024 * 1024


def _cond_index(i, tm=TM):
    return jnp.where(i < T_P // tm, 0, 1 + (i - T_P // tm) // (DEC_SEQ // tm))


def _const_spec(shape):
    nd = len(shape)
    return pl.BlockSpec(shape, lambda *_: (0,) * nd, pipeline_mode=pl.Buffered(1))


def _params(n_axes=1, vmem=VMEM_LIMIT):
    return pltpu.CompilerParams(dimension_semantics=("arbitrary",) * n_axes, vmem_limit_bytes=vmem)


def _layer_norm(v, g, b):
    mu = jnp.mean(v, axis=-1, keepdims=True)
    d = v - mu
    var = jnp.mean(d * d, axis=-1, keepdims=True)
    return d * lax.rsqrt(var + LN_EPS) * g + b


def _split_bf16(v):
    hi = v.astype(BF16)
    lo = (v - hi.astype(F32)).astype(BF16)
    return hi, lo


def _dot(a, b):
    return jnp.dot(a, b, preferred_element_type=F32)


def _silu(v):
    return v / (1.0 + jnp.exp(-v))


CAST_BLOCKS = 16


def _cast_stream(srcs):
    specs = [pl.BlockSpec((w.shape[0] // CAST_BLOCKS, w.shape[1]),
                          lambda i, *_: (jnp.minimum(i, CAST_BLOCKS - 1), 0)) for w in srcs]
    return specs, [jax.ShapeDtypeStruct(w.shape, BF16) for w in srcs]


def _cast_step(srcs, dsts):
    @pl.when(pl.program_id(0) < CAST_BLOCKS)
    def _():
        for src, dst in zip(srcs, dsts):
            dst[...] = src[...].astype(BF16)


def _mod_kernel(cond_ref, w_ref, b_ref, o_ref):
    s = _silu(cond_ref[...]).astype(BF16)
    o_ref[...] = _dot(s, w_ref[...].astype(BF16)) + b_ref[...]


def _modulation(cond8, ada_w, ada_b):
    nb = 6 * D // 1024
    return pl.pallas_call(
        _mod_kernel,
        grid=(DEPTH, nb),
        in_specs=[
            pl.BlockSpec((8, D), lambda l, j: (0, 0)),
            pl.BlockSpec((None, D, 1024), lambda l, j: (l, 0, j)),
            pl.BlockSpec((None, 1, 1024), lambda l, j: (l, 0, j)),
        ],
        out_specs=pl.BlockSpec((None, 8, 1024), lambda l, j: (l, 0, j)),
        out_shape=jax.ShapeDtypeStruct((DEPTH, 8, 6 * D), F32),
        compiler_params=_params(2),
        name="adaln_modulation",
    )(cond8, ada_w, ada_b.reshape(DEPTH, 1, 6 * D))


def _pool_tables():
    t = np.arange(TM)[:, None]
    r = np.arange(TM + 2 * HALO)[None, :]
    pos = r - HALO
    bands = np.zeros((4, len(POOL_WINDOWS), TM, TM + 2 * HALO), np.float32)
    inv = np.zeros((4, TM, 128), np.float32)
    for variant in range(4):
        left_ok, right_ok = variant & 1, variant >> 1
        col_ok = (r >= (0 if left_ok else HALO)) & (r < (TM + 2 * HALO if right_ok else TM + HALO))
        first, last = (-HALO if left_ok else 0), (TM + HALO if right_ok else TM)
        for gi, w in enumerate(POOL_WINDOWS):
            bands[variant, gi] = (pos >= t - w // 2) & (pos < t + w // 2) & col_ok
            cnt = np.minimum(t[:, 0] + w // 2, last) - np.maximum(t[:, 0] - w // 2, first)
            inv[variant, :, gi] = 1.0 / cnt
    return jnp.asarray(bands, BF16), jnp.asarray(inv, F32)


def _even_tiles(x_alls, left_oks, right_oks, mod_ref, win_ref, convw_ref, poolw_ref, pscale_ref, wout_ref,
                g_ref, b_ref, band_ref, inv_ref):
    shift, scale, gate = mod_ref[0:1, :], mod_ref[1:2, :], mod_ref[2:3, :]
    n = len(x_alls)
    hs = []
    for x_all, left_ok, right_ok in zip(x_alls, left_oks, right_oks):
        h = x_all * (1.0 + scale) + shift
        hs.append(jnp.concatenate([jnp.where(left_ok, h[:HALO], 0.0), h[HALO:HALO + TM],
                                   jnp.where(right_ok, h[HALO + TM:], 0.0)], axis=0).astype(BF16))
    us = [_dot(h, win_ref[...]) for h in hs]

    mixes = []
    for u, left_ok, right_ok in zip(us, left_oks, right_oks):
        ux, ub = u[:, 0:CONV_DIM], u[:, CONV_DIM:2 * CONV_DIM]
        uc, up = u[:, 2 * CONV_DIM:3 * CONV_DIM], u[:, 3 * CONV_DIM:]
        z = uc * ux
        conv = (z[HALO - 1:HALO - 1 + TM] * convw_ref[0:1, :]
                + z[HALO:HALO + TM] * convw_ref[1:2, :]
                + z[HALO + 1:HALO + 1 + TM] * convw_ref[2:3, :])
        ya = ub[HALO:HALO + TM] * conv

        variant = left_ok.astype(jnp.int32) + 2 * right_ok.astype(jnp.int32)
        inv_cnt = inv_ref[variant]
        up_hi, up_lo = _split_bf16(up)
        yb_groups = []
        for gi in range(len(POOL_WINDOWS)):
            band = band_ref[variant, gi]
            sl = slice(gi * POOL_GROUP, (gi + 1) * POOL_GROUP)
            tot = _dot(band, up_hi[:, sl]) + _dot(band, up_lo[:, sl])
            p = tot * inv_cnt[:, gi:gi + 1] - up[HALO:HALO + TM, sl]
            yb_groups.append(_dot(p.astype(BF16), poolw_ref[gi]))
        yb = jnp.concatenate(yb_groups, axis=1) * pscale_ref[...]
        mixes.append(jnp.concatenate([ya, yb], axis=1).astype(BF16))

    ys = [_dot(mix, wout_ref[...]) for mix in mixes]
    return [_layer_norm(ALPHA * x_alls[k][HALO:HALO + TM] + gate * ys[k], g_ref[...], b_ref[...])
            for k in range(n)]


EV_TILES = 2
EV_ROWS = EV_TILES * TM


N_CAST_EVEN = 3


def _even_mixer_kernel(xctx_ref, xprev_ref, xlat_ref, xnext_ref, mod_ref, win_ref, convw_ref, poolw_ref,
                       pscale_ref, wout_ref, g_ref, b_ref, band_ref, inv_ref, *rest):
    n = N_CAST_EVEN
    cast_in, o_ref, cast_out, xall_ref = rest[:n], rest[n], rest[n + 1:2 * n + 1], rest[-1]
    _cast_step(cast_in, cast_out)
    s = pl.program_id(0)
    n_ctx = T_P // EV_ROWS
    is_latent = s >= n_ctx
    first_tile = lax.rem(s - n_ctx, DEC_SEQ // EV_ROWS) * EV_TILES

    @pl.when(jnp.logical_not(is_latent))
    def _():
        for k in range(EV_TILES):
            xall_ref[k, 0:HALO, :] = jnp.zeros((HALO, D), F32)
            xall_ref[k, HALO:HALO + TM, :] = xctx_ref[k * TM:(k + 1) * TM, :]
            xall_ref[k, HALO + TM:, :] = jnp.zeros((HALO, D), F32)

    @pl.when(is_latent)
    def _():
        for k in range(EV_TILES):
            lo, hi = k * TM, (k + 1) * TM
            xall_ref[k, 0:HALO, :] = xprev_ref[...] if k == 0 else xlat_ref[lo - HALO:lo, :]
            xall_ref[k, HALO:HALO + TM, :] = xlat_ref[lo:hi, :]
            xall_ref[k, HALO + TM:, :] = xnext_ref[...] if k == EV_TILES - 1 else xlat_ref[hi:hi + HALO, :]

    left_oks = [jnp.logical_and(is_latent, first_tile + k != 0) for k in range(EV_TILES)]
    right_oks = [jnp.logical_and(is_latent, first_tile + k != TPS - 1) for k in range(EV_TILES)]
    outs = _even_tiles([xall_ref[k] for k in range(EV_TILES)], left_oks, right_oks, mod_ref, win_ref,
                       convw_ref, poolw_ref, pscale_ref, wout_ref, g_ref, b_ref, band_ref, inv_ref)
    for k in range(EV_TILES):
        o_ref[k * TM:(k + 1) * TM, :] = outs[k]


def _even_mixer(x_ctx, x_lat, mods, w_in, conv_w, pool_w, pool_scale, w_out, g, b, cast_srcs):
    assert len(cast_srcs) == N_CAST_EVEN
    hb = EV_ROWS // HALO
    n8 = T_S // HALO
    n_ctx = T_P // EV_ROWS
    lat = lambda i: jnp.maximum(i - n_ctx, 0)
    cast_specs, cast_shapes = _cast_stream(cast_srcs)
    bands, inv_cnt = _pool_tables()
    outs = pl.pallas_call(
        _even_mixer_kernel,
        grid=(T // EV_ROWS,),
        in_specs=[
            pl.BlockSpec((EV_ROWS, D), lambda i: (jnp.minimum(i, n_ctx - 1), 0)),
            pl.BlockSpec((HALO, D), lambda i: (jnp.maximum(lat(i) * hb - 1, 0), 0)),
            pl.BlockSpec((EV_ROWS, D), lambda i: (lat(i), 0)),
            pl.BlockSpec((HALO, D), lambda i: (jnp.minimum((lat(i) + 1) * hb, n8 - 1), 0)),
            pl.BlockSpec((None, 6, D), lambda i: (_cond_index(i, EV_ROWS), 0, 0)),
            _const_spec((D, 4 * CONV_DIM)),
            _const_spec((3, CONV_DIM)),
            _const_spec((4, POOL_GROUP, POOL_GROUP)),
            _const_spec((1, 4 * POOL_GROUP)),
            _const_spec((D, D)),
            _const_spec((1, D)),
            _const_spec((1, D)),
            _const_spec(bands.shape),
            _const_spec(inv_cnt.shape),
        ] + cast_specs,
        out_specs=[pl.BlockSpec((EV_ROWS, D), lambda i: (i, 0))] + cast_specs,
        out_shape=[jax.ShapeDtypeStruct((T, D), F32)] + cast_shapes,
        scratch_shapes=[pltpu.VMEM((EV_TILES, TM + 2 * HALO, D), F32)],
        compiler_params=_params(1),
        name="even_mixer",
    )(x_ctx, x_lat, x_lat, x_lat, mods, w_in, conv_w, pool_w, pool_scale, w_out, g, b, bands, inv_cnt, *cast_srcs)
    return outs[0], outs[1:]


FF_CHUNK = D_FF // 2


def _ffn_kernel(x_ref, mod_ref, wg_ref, wu_ref, wd_ref, g_ref, b_ref, cast_ref, o_ref, cast_out_ref):
    _cast_step([cast_ref], [cast_out_ref])
    shift, scale, gate = mod_ref[3:4, :], mod_ref[4:5, :], mod_ref[5:6, :]
    x = x_ref[...]
    h = (x * (1.0 + scale) + shift).astype(BF16)
    f = jnp.zeros((TM_FFN, D), F32)
    for c in range(D_FF // FF_CHUNK):
        sl = slice(c * FF_CHUNK, (c + 1) * FF_CHUNK)
        a = _silu(_dot(h, wg_ref[:, sl])) * _dot(h, wu_ref[:, sl])
        f = f + _dot(a.astype(BF16), wd_ref[sl, :])
    o_ref[...] = _layer_norm(ALPHA * x + gate * f, g_ref[...], b_ref[...])


def _ffn(x, mods, wg, wu, wd, g, b, cast_src):
    cast_specs, cast_shapes = _cast_stream([cast_src])
    return pl.pallas_call(
        _ffn_kernel,
        grid=(T // TM_FFN,),
        in_specs=[
            pl.BlockSpec((TM_FFN, D), lambda i: (i, 0)),
            pl.BlockSpec((None, 6, D), lambda i: (_cond_index(i, TM_FFN), 0, 0)),
            _const_spec((D, D_FF)),
            _const_spec((D, D_FF)),
            _const_spec((D_FF, D)),
            _const_spec((1, D)),
            _const_spec((1, D)),
        ] + cast_specs,
        out_specs=[pl.BlockSpec((TM_FFN, D), lambda i: (i, 0))] + cast_specs,
        out_shape=[jax.ShapeDtypeStruct((T, D), F32)] + cast_shapes,
        compiler_params=_params(1),
        name="dense_swiglu",
    )(x, mods, wg, wu, wd, g, b, cast_src)


W_IN_EXT = Q_RANK + KV_RANK + FNET_DIM + 128 + 128
QH = 256
ATT_SCALE = (QK_NOPE + QK_ROPE) ** -0.5 * float(np.log2(np.e))


def _odd_proj_kernel(x_ref, mod_ref, rope_ref, win_ref, qn_ref, kvn_ref, wqa_ref, wqb_ref,
                     avg_ref, dfth_ref, dftl_ref, cast_ref,
                     q_ref, ckv_ref, kpe_ref, y_ref, cast_out_ref):
    _cast_step([cast_ref], [cast_out_ref])
    shift, scale = mod_ref[0:1, :], mod_ref[1:2, :]
    h = (x_ref[...] * (1.0 + scale) + shift).astype(BF16)
    u = _dot(h, win_ref[...])
    uq = u[:, 0:Q_RANK]
    ukv = u[:, Q_RANK:Q_RANK + KV_RANK]
    uf = u[:, Q_RANK + KV_RANK:Q_RANK + KV_RANK + FNET_DIM]
    o = Q_RANK + KV_RANK + FNET_DIM
    upe, upe_rot = u[:, o:o + 128], u[:, o + 128:o + 256]
    cos, sin = rope_ref[:, 0:128], rope_ref[:, 128:256]

    ckv_ref[...] = ukv * lax.rsqrt(jnp.mean(ukv * ukv, axis=-1, keepdims=True) + RMS_EPS) * kvn_ref[...]
    kpe_ref[...] = upe * cos + upe_rot * sin

    qlat = (uq * lax.rsqrt(jnp.mean(uq * uq, axis=-1, keepdims=True) + RMS_EPS) * qn_ref[...]).astype(BF16)
    qa = _dot(qlat, wqa_ref[...])
    qb = _dot(qlat, wqb_ref[...])
    for hd in range(N_HEADS):
        nope = qa[:, hd * QH:hd * QH + 128]
        pe = qa[:, hd * QH + 128:(hd + 1) * QH] * cos + qb[:, hd * 128:(hd + 1) * 128] * sin
        q_ref[:, hd * QH:hd * QH + 128] = (nope * ATT_SCALE).astype(BF16)
        q_ref[:, hd * QH + 128:(hd + 1) * QH] = (pe * ATT_SCALE).astype(BF16)

    avg = avg_ref[...]
    uf_hi, uf_lo = _split_bf16(uf)
    mu = _dot(uf_hi, avg) + _dot(uf_lo, avg)
    dlt = uf - mu
    sq_hi, sq_lo = _split_bf16(dlt * dlt)
    var = _dot(sq_hi, avg) + _dot(sq_lo, avg)
    xn = dlt * lax.rsqrt(var + LN_EPS)
    xn_hi, xn_lo = _split_bf16(xn)
    y = _dot(xn_hi, dfth_ref[...]) + _dot(xn_lo, dfth_ref[...]) + _dot(xn_hi, dftl_ref[...])
    y_ref[...] = y.astype(BF16)


def _odd_proj(x, mods, rope_tab, w_in_ext, q_norm, kv_norm, wqa, wqb, avg, dft_hi, dft_lo, cast_src):
    cast_specs, cast_shapes = _cast_stream([cast_src])

    def rope_index(i):
        return (jnp.where(i < T_P // TM2, 0, 1 + lax.rem(i - T_P // TM2, DEC_SEQ // TM2)), 0)

    return pl.pallas_call(
        _odd_proj_kernel,
        grid=(T // TM2,),
        in_specs=[
            pl.BlockSpec((TM2, D), lambda i: (i, 0)),
            pl.BlockSpec((None, 6, D), lambda i: (_cond_index(i, TM2), 0, 0)),
            pl.BlockSpec((TM2, 256), rope_index),
            _const_spec((D, W_IN_EXT)),
            _const_spec((1, Q_RANK)),
            _const_spec((1, KV_RANK)),
            _const_spec((Q_RANK, N_HEADS * QH)),
            _const_spec((Q_RANK, N_HEADS * 128)),
            _const_spec((FNET_DIM, FNET_DIM)),
            _const_spec((FNET_DIM, 2 * FNET_DIM)),
            _const_spec((FNET_DIM, 2 * FNET_DIM)),
        ] + cast_specs,
        out_specs=[
            pl.BlockSpec((TM2, N_HEADS * QH), lambda i: (i, 0)),
            pl.BlockSpec((TM2, KV_RANK), lambda i: (i, 0)),
            pl.BlockSpec((TM2, 128), lambda i: (i, 0)),
            pl.BlockSpec((TM2, 2 * FNET_DIM), lambda i: (i, 0)),
        ] + cast_specs,
        out_shape=[
            jax.ShapeDtypeStruct((T, N_HEADS * QH), BF16),
            jax.ShapeDtypeStruct((T, KV_RANK), F32),
            jax.ShapeDtypeStruct((T, 128), F32),
            jax.ShapeDtypeStruct((T, 2 * FNET_DIM), BF16),
        ] + cast_shapes,
        compiler_params=_params(1),
        name="odd_projections",
    )(x, mods, rope_tab, w_in_ext, q_norm, kv_norm, wqa, wqb, avg, dft_hi, dft_lo, cast_src)


V_OFF = N_HEADS * QH
KV_COLS = V_OFF + N_HEADS * V_DIM


def _attn_body(q_ref, kv_ref, o_ref):
    for hd in range(N_HEADS):
        qh = q_ref[:, hd * QH:(hd + 1) * QH]
        kh = kv_ref[:, hd * QH:(hd + 1) * QH]
        s = lax.dot_general(qh, kh, (((1,), (1,)), ((), ())), preferred_element_type=F32)
        p = jnp.exp2(s - jnp.max(s, axis=-1, keepdims=True))
        den = jnp.sum(p, axis=-1, keepdims=True)
        vh = kv_ref[:, V_OFF + hd * V_DIM:V_OFF + (hd + 1) * V_DIM]
        o = _dot(p.astype(BF16), vh)
        o_ref[:, hd * 128:(hd + 1) * 128] = (o / den).astype(BF16)


KV_CHUNK = 512


def _attn_kernel(q_ref, ckvp_ref, kpep_ref, ckvs_ref, kpes_ref, cckv_ref, ckpe_ref, wkv_ref, o_ref,
                 kvp_s, kvs_s):
    i = pl.program_id(0)

    def expand(ckv, kpe, dst, row0):
        n = ckv.shape[0]
        kv = _dot(ckv.astype(BF16), wkv_ref[...]).astype(BF16)
        kpe = kpe.astype(BF16)
        for hd in range(N_HEADS):
            dst[row0:row0 + n, hd * QH:hd * QH + QK_NOPE] = kv[:, hd * QK_NOPE:(hd + 1) * QK_NOPE]
            dst[row0:row0 + n, hd * QH + QK_NOPE:(hd + 1) * QH] = kpe
        dst[row0:row0 + n, V_OFF:] = kv[:, N_HEADS * QK_NOPE:]

    @pl.when(i < NP_TILES)
    def _():
        expand(ckvp_ref[...], kpep_ref[...], kvp_s, 0)
        _attn_body(q_ref, kvp_s, o_ref)

    @pl.when(i >= NP_TILES)
    def _():
        @pl.when(lax.rem(i - NP_TILES, TPS) == 0)
        def _():
            expand(cckv_ref[...], ckpe_ref[...], kvs_s, 0)
            for c in range(DEC_SEQ // KV_CHUNK):
                rows = slice(c * KV_CHUNK, (c + 1) * KV_CHUNK)
                expand(ckvs_ref[rows, :], kpes_ref[rows, :], kvs_s, PAST + c * KV_CHUNK)

        _attn_body(q_ref, kvs_s, o_ref)


def _attention(q, ckv, kpe, cache_ckv, cache_kpe128, w_kv):
    ctx_blk = lambda i: (jnp.minimum(i, NP_TILES - 1), 0)
    lat_b = lambda i: jnp.maximum(i - NP_TILES, 0) // TPS
    lat_blk = lambda i: (T_P // DEC_SEQ + lat_b(i), 0)
    return pl.pallas_call(
        _attn_kernel,
        grid=(N_TILES,),
        in_specs=[
            pl.BlockSpec((TM, N_HEADS * QH), lambda i: (i, 0)),
            pl.BlockSpec((SEQ, KV_RANK), ctx_blk),
            pl.BlockSpec((SEQ, 128), ctx_blk),
            pl.BlockSpec((DEC_SEQ, KV_RANK), lat_blk),
            pl.BlockSpec((DEC_SEQ, 128), lat_blk),
            pl.BlockSpec((None, PAST, KV_RANK), lambda i: (lat_b(i), 0, 0)),
            pl.BlockSpec((None, PAST, 128), lambda i: (lat_b(i), 0, 0)),
            _const_spec((KV_RANK, 2 * N_HEADS * 128)),
        ],
        out_specs=pl.BlockSpec((TM, N_HEADS * V_DIM), lambda i: (i, 0)),
        out_shape=jax.ShapeDtypeStruct((T, N_HEADS * V_DIM), BF16),
        scratch_shapes=[pltpu.VMEM((SEQ, KV_COLS), BF16), pltpu.VMEM((LK_S, KV_COLS), BF16)],
        compiler_params=_params(1),
        name="attention",
    )(q, ckv, kpe, ckv, kpe, cache_ckv, cache_kpe128, w_kv)


def _pos_dft_body(y_ref, c, s, o_ref):
    f = _dot(c, y_ref[:, 0:FNET_DIM]) + _dot(s, y_ref[:, FNET_DIM:])
    o_ref[...] = f.astype(BF16)


def _pos_dft_kernel(yp_ref, cp_ref, sp_ref, ys_ref, c0_ref, s0_ref, cb_ref, sb_ref, o_ref):
    @pl.when(pl.program_id(0) < NP_TILES)
    def _():
        _pos_dft_body(yp_ref, cp_ref[...].astype(BF16), sp_ref[...].astype(BF16), o_ref)

    @pl.when(pl.program_id(0) >= NP_TILES)
    def _():
        j = lax.rem(pl.program_id(0) - NP_TILES, TPS)
        cb, sb = cb_ref[pl.ds(j, 1), :], sb_ref[pl.ds(j, 1), :]
        c0, s0 = c0_ref[...], s0_ref[...]
        _pos_dft_body(ys_ref, (c0 * cb - s0 * sb).astype(BF16), (s0 * cb + c0 * sb).astype(BF16), o_ref)


def _pos_dft(y, tabs_p, base_s, step_s):
    ctx_blk = lambda i: (jnp.minimum(i, NP_TILES - 1), 0)
    lat_seq = lambda i: (T_P // DEC_SEQ + jnp.maximum(i - NP_TILES, 0) // TPS, 0)
    return pl.pallas_call(
        _pos_dft_kernel,
        grid=(N_TILES,),
        in_specs=[pl.BlockSpec((SEQ, 2 * FNET_DIM), ctx_blk)]
        + [_const_spec((SEQ, SEQ))] * 2
        + [pl.BlockSpec((DEC_SEQ, 2 * FNET_DIM), lat_seq)]
        + [_const_spec((TM, DEC_SEQ))] * 2
        + [_const_spec((TPS, DEC_SEQ))] * 2,
        out_specs=pl.BlockSpec((TM, FNET_DIM), lambda i: (i, 0)),
        out_shape=jax.ShapeDtypeStruct((T, FNET_DIM), BF16),
        compiler_params=_params(1),
        name="pos_dft",
    )(y, *tabs_p, y, *base_s, *step_s)


def _odd_merge_kernel(x_ref, attn_ref, f_ref, mod_ref, fw_ref, wo_ref, g_ref, b_ref, rh_ref, rl_ref, tri_ref,
                      cast_ref, xo_ref, info_ref, infot_ref, cnt_ref, cast_out_ref, carry_ref):
    _cast_step([cast_ref], [cast_out_ref])

    @pl.when(pl.program_id(0) == 0)
    def _():
        carry_ref[...] = jnp.zeros_like(carry_ref)

    gate = mod_ref[2:3, :]
    shift2, scale2 = mod_ref[3:4, :], mod_ref[4:5, :]
    fm = _dot(f_ref[...], fw_ref[...]).astype(BF16)
    y = _dot(attn_ref[...], wo_ref[0:N_HEADS * V_DIM, :]) + _dot(fm, wo_ref[N_HEADS * V_DIM:, :])
    x = _layer_norm(ALPHA * x_ref[...] + gate * y, g_ref[...], b_ref[...])
    xo_ref[...] = x
    h = x * (1.0 + scale2) + shift2

    h_hi, h_lo = _split_bf16(h)
    logits = _dot(h_hi, rh_ref[...]) + _dot(h_lo, rh_ref[...]) + _dot(h_hi, rl_ref[...])
    lane = lax.broadcasted_iota(jnp.int32, (TM2, 128), 1)
    neg = jnp.float32(-jnp.inf)
    logits = jnp.where(lane < N_EXPERTS, logits, neg)
    m1 = jnp.max(logits, axis=-1, keepdims=True)
    i1 = jnp.min(jnp.where(logits == m1, lane, 128), axis=-1, keepdims=True)
    rest = jnp.where(lane == i1, neg, logits)
    m2 = jnp.max(rest, axis=-1, keepdims=True)
    i2 = jnp.min(jnp.where(rest == m2, lane, 128), axis=-1, keepdims=True)
    e2 = jnp.exp(m2 - m1)
    w1 = 1.0 / (1.0 + e2)
    w2 = e2 / (1.0 + e2)
    info = jnp.where(lane == 0, w1, 0.0)
    info = jnp.where(lane == 1, w2, info)
    info = jnp.where(lane == 2, i1.astype(F32), info)
    info = jnp.where(lane == 3, i2.astype(F32), info)

    uses = jnp.logical_or(lane == i1, lane == i2)
    seen = _dot(tri_ref[...], jnp.where(uses, 1.0, 0.0).astype(BF16)) + carry_ref[...]
    r1 = jnp.sum(jnp.where(lane == i1, seen, 0.0), axis=-1, keepdims=True)
    r2 = jnp.sum(jnp.where(lane == i2, seen, 0.0), axis=-1, keepdims=True)
    info = jnp.where(lane == 4, r1, info)
    info = jnp.where(lane == 5, r2, info)
    info_ref[...] = info
    infot_ref[...] = info.T
    total = carry_ref[...] + jnp.sum(jnp.where(uses, 1.0, 0.0), axis=0, keepdims=True)
    carry_ref[...] = total
    cnt_ref[...] = jnp.broadcast_to(total, cnt_ref.shape)


def _odd_merge(x, attn, f, mods, fnet_w, w_out, g, b, r_hi, r_lo, cast_src):
    row = lambda i: (i, 0)
    tri = jnp.asarray(np.tril(np.ones((TM2, TM2), np.float32), -1), BF16)
    cast_specs, cast_shapes = _cast_stream([cast_src])
    return pl.pallas_call(
        _odd_merge_kernel,
        grid=(T // TM2,),
        in_specs=[
            pl.BlockSpec((TM2, D), row),
            pl.BlockSpec((TM2, N_HEADS * V_DIM), row),
            pl.BlockSpec((TM2, FNET_DIM), row),
            pl.BlockSpec((None, 6, D), lambda i: (_cond_index(i, TM2), 0, 0)),
            _const_spec((FNET_DIM, FNET_DIM)),
            _const_spec((N_HEADS * V_DIM + FNET_DIM, D)),
            _const_spec((1, D)),
            _const_spec((1, D)),
            _const_spec((D, 128)),
            _const_spec((D, 128)),
            _const_spec((TM2, TM2)),
        ] + cast_specs,
        out_specs=[pl.BlockSpec((TM2, D), row), pl.BlockSpec((TM2, 128), row),
                   pl.BlockSpec((128, TM2), lambda i: (0, i)), pl.BlockSpec((8, 128), lambda i: (i, 0))] + cast_specs,
        out_shape=[jax.ShapeDtypeStruct((T, D), F32), jax.ShapeDtypeStruct((T, 128), F32),
                   jax.ShapeDtypeStruct((128, T), F32), jax.ShapeDtypeStruct((T // TM2 * 8, 128), F32)] + cast_shapes,
        scratch_shapes=[pltpu.VMEM((1, 128), F32)],
        compiler_params=_params(1),
        name="odd_merge_router",
    )(x, attn, f, mods, fnet_w, w_out, g, b, r_hi, r_lo, tri, cast_src)


TM_D = 512
ROW = (8, 128)
ROW_DT = BF16
DMA_UNROLL = 8


def _row_copy(src, s, dst, d, sem):
    return pltpu.make_async_copy(src.at[s], dst.at[d], sem)


CHUNK = 16
LOCAL_ROWS = 2 * TM2 + N_EXPERTS * CHUNK


def _chunk_copies(tile, nch_ref, fn):
    for e in range(N_EXPERTS):
        def body(c, carry, e=e):
            fn(tile * N_EXPERTS + e, c, e % 2)
            return carry

        lax.fori_loop(0, nch_ref[tile * N_EXPERTS + e], body, 0)


def _dispatch_kernel(lp_ref, d0_ref, off_ref, nch_ref, pad_lo_ref, pad_hi_ref, x_ref, mod_ref, xs_ref,
                     h_ref, local_ref, zero_ref, sem, zsem):
    i = pl.program_id(0)
    base = i * TM_D
    slot = lax.rem(i, 2)

    @pl.when(i == 0)
    def _():
        local_ref[...] = jnp.zeros_like(local_ref)
        zero_ref[...] = jnp.zeros_like(zero_ref)
        for e in range(N_EXPERTS + 1):
            def zissue(r, carry):
                _row_copy(zero_ref, 0, xs_ref, r, zsem).start()
                return carry

            def zdrain(r, carry):
                _row_copy(zero_ref, 0, xs_ref, 0, zsem).wait()
                return carry

            lax.fori_loop(pad_lo_ref[e], pad_hi_ref[e], zissue, 0)
            lax.fori_loop(pad_lo_ref[e], pad_hi_ref[e], zdrain, 0)

    shift2, scale2 = mod_ref[3:4, :], mod_ref[4:5, :]
    h_ref[...] = (x_ref[...] * (1.0 + scale2) + shift2).astype(ROW_DT).reshape((TM_D,) + ROW)

    def place(r, carry):
        row = h_ref[r]
        local_ref[slot, lp_ref[base + r]] = row
        local_ref[slot, lp_ref[T + base + r]] = row
        return carry

    lax.fori_loop(0, TM_D, place, 0, unroll=DMA_UNROLL)

    def chunk(s, seg, c):
        return pltpu.make_async_copy(local_ref.at[s, pl.ds(off_ref[seg] + c * CHUNK, CHUNK)],
                                     xs_ref.at[pl.ds(d0_ref[seg] + c * CHUNK, CHUNK)], sem.at[s])

    @pl.when(i > 0)
    def _():
        _chunk_copies(i - 1, nch_ref, lambda seg, c, prio: chunk(1 - slot, seg, c).wait())

    _chunk_copies(i, nch_ref, lambda seg, c, prio: chunk(slot, seg, c).start(priority=prio))

    @pl.when(i == T // TM_D - 1)
    def _():
        _chunk_copies(i, nch_ref, lambda seg, c, prio: chunk(slot, seg, c).wait())


def _dispatch(lp, d0, off, nch, pad_lo, pad_hi, x, mods):
    assert TM_D == TM2
    return pl.pallas_call(
        _dispatch_kernel,
        grid_spec=pltpu.PrefetchScalarGridSpec(
            num_scalar_prefetch=6,
            grid=(T // TM_D,),
            in_specs=[pl.BlockSpec((TM_D, D), lambda i, *_: (i, 0)),
                      pl.BlockSpec((None, 6, D), lambda i, *_: (_cond_index(i, TM_D), 0, 0))],
            out_specs=pl.BlockSpec(memory_space=pl.ANY),
            scratch_shapes=[pltpu.VMEM((TM_D,) + ROW, ROW_DT), pltpu.VMEM((2, LOCAL_ROWS) + ROW, ROW_DT),
                            pltpu.VMEM((1,) + ROW, ROW_DT), pltpu.SemaphoreType.DMA((2,)),
                            pltpu.SemaphoreType.DMA(())],
        ),
        out_shape=jax.ShapeDtypeStruct((R_MAX,) + ROW, ROW_DT),
        compiler_params=pltpu.CompilerParams(dimension_semantics=("arbitrary",), has_side_effects=True),
        name="expert_dispatch",
    )(lp, d0, off, nch, pad_lo, pad_hi, x, mods)


def _expert_kernel(te_ref, nt_ref, xs_ref, wg_ref, wu_ref, wd_ref, o_ref):
    @pl.when(pl.program_id(0) < nt_ref[0])
    def _():
        h = xs_ref[...].reshape(TM_E, D)
        a = _silu(_dot(h, wg_ref[...])) * _dot(h, wu_ref[...])
        o_ref[...] = _dot(a.astype(BF16), wd_ref[...]).astype(ROW_DT).reshape((TM_E,) + ROW)

    @pl.when(pl.program_id(0) >= nt_ref[0])
    def _():
        o_ref[...] = jnp.zeros_like(o_ref)


def _experts(tile_expert, n_used, xs, wg, wu, wd):
    return pl.pallas_call(
        _expert_kernel,
        grid_spec=pltpu.PrefetchScalarGridSpec(
            num_scalar_prefetch=2,
            grid=(N_ETILES,),
            in_specs=[
                pl.BlockSpec((TM_E,) + ROW, lambda i, te, nt: (jnp.minimum(i, nt[0] - 1), 0, 0)),
                pl.BlockSpec((None, D, D_FF_EXPERT), lambda i, te, nt: (te[i], 0, 0)),
                pl.BlockSpec((None, D, D_FF_EXPERT), lambda i, te, nt: (te[i], 0, 0)),
                pl.BlockSpec((None, D_FF_EXPERT, D), lambda i, te, nt: (te[i], 0, 0)),
            ],
            out_specs=pl.BlockSpec((TM_E,) + ROW, lambda i, te, nt: (i, 0, 0)),
        ),
        out_shape=jax.ShapeDtypeStruct((R_MAX,) + ROW, ROW_DT),
        compiler_params=_params(1),
        name="expert_swiglu",
    )(tile_expert, n_used, xs, wg, wu, wd)


def _combine_kernel(lp_ref, d0_ref, off_ref, nch_ref, x_ref, info_ref, mod_ref, g_ref, b_ref, ys_ref,
                    op_ref, os_ref, local_ref, rows_ref, sem):
    i = pl.program_id(0)
    slot = lax.rem(i, 2)

    def chunk(s, seg, c):
        return pltpu.make_async_copy(ys_ref.at[pl.ds(d0_ref[seg] + c * CHUNK, CHUNK)],
                                     local_ref.at[s, pl.ds(off_ref[seg] + c * CHUNK, CHUNK)], sem.at[s])

    @pl.when(i == 0)
    def _():
        _chunk_copies(0, nch_ref, lambda seg, c, prio: chunk(0, seg, c).start(priority=prio))

    @pl.when(i + 1 < T // TM2)
    def _():
        _chunk_copies(i + 1, nch_ref, lambda seg, c, prio: chunk(1 - slot, seg, c).start(priority=prio))

    _chunk_copies(i, nch_ref, lambda seg, c, prio: chunk(slot, seg, c).wait())

    def pick(r, carry):
        rows_ref[0, r] = local_ref[slot, lp_ref[i * TM2 + r]]
        rows_ref[1, r] = local_ref[slot, lp_ref[T + i * TM2 + r]]
        return carry

    lax.fori_loop(0, TM2, pick, 0, unroll=DMA_UNROLL)

    gate = mod_ref[5:6, :]
    w1, w2 = info_ref[:, 0:1], info_ref[:, 1:2]
    y = w1 * rows_ref[0].reshape(TM2, D).astype(F32) + w2 * rows_ref[1].reshape(TM2, D).astype(F32)
    out = _layer_norm(ALPHA * x_ref[...] + gate * y, g_ref[...], b_ref[...])

    @pl.when(pl.program_id(0) < T_P // TM2)
    def _():
        op_ref[...] = out

    @pl.when(pl.program_id(0) >= T_P // TM2)
    def _():
        os_ref[...] = out


def _combine(lp, d0, off, nch, x, info, mods, g, b, ys):
    return pl.pallas_call(
        _combine_kernel,
        grid_spec=pltpu.PrefetchScalarGridSpec(
            num_scalar_prefetch=4,
            grid=(T // TM2,),
            in_specs=[
                pl.BlockSpec((TM2, D), lambda i, *_: (i, 0)),
                pl.BlockSpec((TM2, 128), lambda i, *_: (i, 0)),
                pl.BlockSpec((None, 6, D), lambda i, *_: (_cond_index(i, TM2), 0, 0)),
                pl.BlockSpec((1, D), lambda i, *_: (0, 0)),
                pl.BlockSpec((1, D), lambda i, *_: (0, 0)),
                pl.BlockSpec(memory_space=pl.ANY),
            ],
            out_specs=[pl.BlockSpec((TM2, D), lambda i, *_: (jnp.minimum(i, T_P // TM2 - 1), 0)),
                       pl.BlockSpec((TM2, D), lambda i, *_: (jnp.maximum(i - T_P // TM2, 0), 0))],
            scratch_shapes=[pltpu.VMEM((2, LOCAL_ROWS) + ROW, ROW_DT), pltpu.VMEM((2, TM2) + ROW, ROW_DT),
                            pltpu.SemaphoreType.DMA((2,))],
        ),
        out_shape=[jax.ShapeDtypeStruct((T_P, D), F32), jax.ShapeDtypeStruct((T_S, D), F32)],
        compiler_params=_params(1),
        name="expert_combine",
    )(lp, d0, off, nch, x, info, mods, g, b, ys)


def _rot_cols(w):
    w4 = w.reshape(w.shape[:-1] + (2, 2, QK_ROPE // 4))
    return jnp.stack([-w4[..., 1, :], w4[..., 0, :]], axis=-2).reshape(w.shape)


def _rope_table():
    rows = DEC_SEQ // GRID_W
    row = np.repeat(np.arange(rows), GRID_W).astype(np.float32)
    col = np.tile(np.arange(GRID_W), rows).astype(np.float32)
    half = QK_ROPE // 2
    inv = (ROPE_THETA ** (-np.arange(0, half, 2, dtype=np.float32) / half)).astype(np.float32)
    ar, ac = row[:, None] * inv, col[:, None] * inv
    ang = np.concatenate([ar, ar, ac, ac], axis=-1)
    cos = np.concatenate([np.ones((TM2, QK_ROPE)), np.cos(ang)], axis=0)
    sin = np.concatenate([np.zeros((TM2, QK_ROPE)), np.sin(ang)], axis=0)
    n = cos.shape[0]
    return jnp.asarray(np.concatenate([cos, np.ones((n, 64)), sin, np.zeros((n, 64))], axis=1), F32)


def _dft_angles(rows, n):
    k = np.arange(n, dtype=np.int64)
    return ((np.asarray(rows, np.int64)[:, None] * k[None, :]) % n) * (2.0 * np.pi / n)


def _dft_tables(n):
    ang = _dft_angles(np.arange(n), n)
    return np.cos(ang) * n ** -0.5, np.sin(ang) * n ** -0.5


def _hi_lo(m):
    m = jnp.asarray(m, F32)
    hi = m.astype(BF16)
    return hi, (m - hi.astype(F32)).astype(BF16)


def _block_diag4(m):
    return np.kron(np.eye(4), m)


def kernel(x_prompt, x_sample, cache_ckv, cache_kpe, c, c_ctx, ada_w, ada_b, ln_g, ln_b, ev_w_in, ev_conv_w, ev_pool_w, ev_pool_scale, ev_w_out, ffn_w_gate, ffn_w_up, ffn_w_down, od_w_in, od_q_norm, od_kv_norm, od_w_q_b, od_w_kv_b, od_fnet_w, od_w_out, moe_router, moe_w_gate, moe_w_up, moe_w_down):
    cond8 = jnp.concatenate([c_ctx[None, :], c, jnp.zeros((8 - N_COND, D), F32)], axis=0)
    mods = _modulation(cond8, ada_w, ada_b)[:, :N_COND].reshape(DEPTH, N_COND, 6, D)

    x, (ffn_wg, ffn_wu, ffn_wd) = _even_mixer(
        x_prompt.reshape(T_P, D), x_sample.reshape(T_S, D), mods[0], ev_w_in[0].astype(BF16), ev_conv_w[0],
        ev_pool_w[0].astype(BF16), ev_pool_scale[0][None, :], ev_w_out[0].astype(BF16),
        ln_g[0, 0][None, :], ln_b[0, 0][None, :], [ffn_w_gate[0], ffn_w_up[0], ffn_w_down[0]])
    x, moe_wd = _ffn(x, mods[0], ffn_wg, ffn_wu, ffn_wd, ln_g[0, 1][None, :], ln_b[0, 1][None, :],
                     moe_w_down[0].reshape(N_EXPERTS * D_FF_EXPERT, D))

    w_in = od_w_in[0]
    w_pe = w_in[:, Q_RANK + KV_RANK:Q_RANK + KV_RANK + QK_ROPE]
    zpad = jnp.zeros((D, 64), F32)
    w_in_ext = jnp.concatenate([w_in[:, :Q_RANK + KV_RANK], w_in[:, Q_RANK + KV_RANK + QK_ROPE:],
                                w_pe, zpad, _rot_cols(w_pe), zpad], axis=1).astype(BF16)
    wq = od_w_q_b[0].reshape(Q_RANK, N_HEADS, QK_NOPE + QK_ROPE)
    zq = jnp.zeros((Q_RANK, N_HEADS, 64), F32)
    wqa = jnp.concatenate([wq, zq], axis=-1).reshape(Q_RANK, N_HEADS * QH).astype(BF16)
    wqb = jnp.concatenate([_rot_cols(wq[..., QK_NOPE:]), zq], axis=-1).reshape(Q_RANK, N_HEADS * 128).astype(BF16)
    wkv = od_w_kv_b[0].reshape(KV_RANK, N_HEADS, QK_NOPE + V_DIM)
    w_kv = jnp.concatenate([wkv[..., :QK_NOPE].reshape(KV_RANK, -1), wkv[..., QK_NOPE:].reshape(KV_RANK, -1)],
                           axis=1).astype(BF16)

    avg = jnp.asarray(_block_diag4(np.full((FNET_GROUP_DIM, FNET_GROUP_DIM), 1.0 / FNET_GROUP_DIM)), BF16)
    cc, sc = _dft_tables(FNET_GROUP_DIM)
    dft_hi, dft_lo = _hi_lo(np.concatenate([_block_diag4(cc), -_block_diag4(sc)], axis=1))

    q, ckv, kpe, y_dft, moe_wg = _odd_proj(x, mods[1], _rope_table(), w_in_ext, od_q_norm[0][None, :],
                                            od_kv_norm[0][None, :], wqa, wqb, avg, dft_hi, dft_lo,
                                            moe_w_gate[0].reshape(N_EXPERTS * D, D_FF_EXPERT))

    cache_kpe128 = jnp.pad(cache_kpe[:, 0], ((0, 0), (0, 0), (0, 128 - QK_ROPE)))
    attn = _attention(q, ckv, kpe, cache_ckv[:, 0], cache_kpe128, w_kv)

    tabs_p = tuple(jnp.asarray(m, F32) for m in _dft_tables(SEQ))
    a_base = _dft_angles(np.arange(TM), DEC_SEQ)
    a_step = _dft_angles(np.arange(TPS) * TM, DEC_SEQ)
    base_s = (jnp.asarray(np.cos(a_base) * DEC_SEQ ** -0.5, F32), jnp.asarray(np.sin(a_base) * DEC_SEQ ** -0.5, F32))
    step_s = (jnp.asarray(np.cos(a_step), F32), jnp.asarray(np.sin(a_step), F32))
    f = _pos_dft(y_dft, tabs_p, base_s, step_s)

    router = jnp.pad(moe_router[0], ((0, 0), (0, 128 - N_EXPERTS)))
    r_hi, r_lo = _hi_lo(router)
    x, info, info_t, cnt, moe_wu = _odd_merge(x, attn, f, mods[1], od_fnet_w[0].astype(BF16),
                                              od_w_out[0].astype(BF16), ln_g[1, 0][None, :], ln_b[1, 0][None, :],
                                              r_hi, r_lo, moe_w_up[0].reshape(N_EXPERTS * D, D_FF_EXPERT))

    n_tiles = T // TM2
    after = cnt.reshape(n_tiles, 8, 128)[:, 0, :N_EXPERTS].astype(jnp.int32)
    before = jnp.concatenate([jnp.zeros((1, N_EXPERTS), jnp.int32), after[:-1]], axis=0)
    counts = after[-1]
    padded = ((counts + CHUNK - 1 + TM_E - 1) // TM_E) * TM_E
    g_end = jnp.cumsum(padded)
    g_start = g_end - padded
    n_chunks = (after - before + CHUNK - 1) // CHUNK
    seg_off = CHUNK * (jnp.cumsum(n_chunks, axis=1) - n_chunks)
    seg_dst = g_start[None, :] + before
    choice = info_t[2:4].astype(jnp.int32)
    rank = info_t[4:6].astype(jnp.int32)
    shift_t = jnp.repeat((seg_off - before).T, TM2, axis=1)
    lp = rank
    for e in range(N_EXPERTS):
        lp = lp + jnp.where(choice == e, shift_t[e][None, :], 0)
    lp = lp.reshape(-1).astype(jnp.int32)
    d0, off, nch = (a.reshape(-1).astype(jnp.int32) for a in (seg_dst, seg_off, n_chunks))
    tile_row = jnp.arange(N_ETILES, dtype=jnp.int32) * TM_E
    tile_expert = jnp.minimum(jnp.sum((tile_row[:, None] >= g_end[None, :]).astype(jnp.int32), axis=1),
                              N_EXPERTS - 1).astype(jnp.int32)
    n_used = (g_end[-1:] // TM_E).astype(jnp.int32)

    pad_lo = jnp.concatenate([g_start + counts, g_end[-1:]]).astype(jnp.int32)
    pad_hi = jnp.concatenate([g_end, jnp.full((1,), R_MAX, jnp.int32)]).astype(jnp.int32)
    xs = _dispatch(lp, d0, off, nch, pad_lo, pad_hi, x, mods[1])
    ys = _experts(tile_expert, n_used, xs, moe_wg.reshape(N_EXPERTS, D, D_FF_EXPERT),
                  moe_wu.reshape(N_EXPERTS, D, D_FF_EXPERT), moe_wd.reshape(N_EXPERTS, D_FF_EXPERT, D))
    yp, ysm = _combine(lp, d0, off, nch, x, info, mods[1], ln_g[1, 1][None, :], ln_b[1, 1][None, :], ys)

    y_prompt = yp.reshape(BATCH, SEQ, D)
    y_sample = ysm.reshape(DEC_BATCH, DEC_SEQ, D)
    new_ckv = ckv[:T_P].reshape(BATCH, 1, SEQ, KV_RANK)
    new_kpe = kpe[:T_P, :QK_ROPE].reshape(BATCH, 1, SEQ, QK_ROPE)
    return (y_prompt, y_sample, new_ckv, new_kpe)
```

```python
import functools

import numpy as np
import jax
import jax.numpy as jnp
from jax import lax
from jax.experimental import pallas as pl
from jax.experimental.pallas import tpu as pltpu

F32 = jnp.float32
BF16 = jnp.bfloat16

D = 1024
BATCH, SEQ = 32, 256
DEC_BATCH, DEC_SEQ = 2, 2048
PAST = 512
GRID_W = 64
T_P = BATCH * SEQ
T_S = DEC_BATCH * DEC_SEQ
T = T_P + T_S
N_COND = 1 + DEC_BATCH

CONV_DIM = 512
POOL_WINDOWS = (2, 4, 8, 16)
POOL_GROUP = 128
N_HEADS = 8
QK_NOPE, QK_ROPE, V_DIM = 128, 64, 128
Q_RANK, KV_RANK = 384, 256
FNET_DIM, FNET_GROUP_DIM = 256, 64
D_FF = 2816
N_EXPERTS = 8
D_FF_EXPERT = 1792
DEPTH = 2
ALPHA = (2 * DEPTH) ** 0.25
LN_EPS = 1e-5
RMS_EPS = 1e-6
ROPE_THETA = 10000.0

TM = 256
NP_TILES = T_P // TM
TPS = DEC_SEQ // TM
N_TILES = T // TM
HALO = 8
TM2 = 512
TM_FFN = 512
LK_S = PAST + DEC_SEQ
TM_E = 256
N_ETILES = (2 * T) // TM_E + N_EXPERTS + 1
R_MAX = N_ETILES * TM_E
VMEM_LIMIT = 56 * 1024 * 1024


def _cond_index(i, tm=TM):
    return jnp.where(i < T_P // tm, 0, 1 + (i - T_P // tm) // (DEC_SEQ // tm))


def _const_spec(shape):
    nd = len(shape)
    return pl.BlockSpec(shape, lambda *_: (0,) * nd, pipeline_mode=pl.Buffered(1))


def _params(n_axes=1, vmem=VMEM_LIMIT):
    return pltpu.CompilerParams(dimension_semantics=("arbitrary",) * n_axes, vmem_limit_bytes=vmem)


def _layer_norm(v, g, b):
    mu = jnp.mean(v, axis=-1, keepdims=True)
    d = v - mu
    var = jnp.mean(d * d, axis=-1, keepdims=True)
    return d * lax.rsqrt(var + LN_EPS) * g + b


def _split_bf16(v):
    hi = v.astype(BF16)
    lo = (v - hi.astype(F32)).astype(BF16)
    return hi, lo


def _dot(a, b):
    return jnp.dot(a, b, preferred_element_type=F32)


def _silu(v):
    return v / (1.0 + jnp.exp(-v))


CAST_BLOCKS = 16


def _cast_stream(srcs):
    specs = [pl.BlockSpec((w.shape[0] // CAST_BLOCKS, w.shape[1]),
                          lambda i, *_: (jnp.minimum(i, CAST_BLOCKS - 1), 0)) for w in srcs]
    return specs, [jax.ShapeDtypeStruct(w.shape, BF16) for w in srcs]


def _cast_step(srcs, dsts):
    @pl.when(pl.program_id(0) < CAST_BLOCKS)
    def _():
        for src, dst in zip(srcs, dsts):
            dst[...] = src[...].astype(BF16)


def _mod_kernel(cond_ref, w_ref, b_ref, o_ref):
    s = _silu(cond_ref[...]).astype(BF16)
    o_ref[...] = _dot(s, w_ref[...].astype(BF16)) + b_ref[...]


def _modulation(cond8, ada_w, ada_b):
    nb = 6 * D // 1024
    return pl.pallas_call(
        _mod_kernel,
        grid=(DEPTH, nb),
        in_specs=[
            pl.BlockSpec((8, D), lambda l, j: (0, 0)),
            pl.BlockSpec((None, D, 1024), lambda l, j: (l, 0, j)),
            pl.BlockSpec((None, 1, 1024), lambda l, j: (l, 0, j)),
        ],
        out_specs=pl.BlockSpec((None, 8, 1024), lambda l, j: (l, 0, j)),
        out_shape=jax.ShapeDtypeStruct((DEPTH, 8, 6 * D), F32),
        compiler_params=_params(2),
        name="adaln_modulation",
    )(cond8, ada_w, ada_b.reshape(DEPTH, 1, 6 * D))


def _pool_tables():
    t = np.arange(TM)[:, None]
    r = np.arange(TM + 2 * HALO)[None, :]
    pos = r - HALO
    bands = np.zeros((4, len(POOL_WINDOWS), TM, TM + 2 * HALO), np.float32)
    inv = np.zeros((4, TM, 128), np.float32)
    for variant in range(4):
        left_ok, right_ok = variant & 1, variant >> 1
        col_ok = (r >= (0 if left_ok else HALO)) & (r < (TM + 2 * HALO if right_ok else TM + HALO))
        first, last = (-HALO if left_ok else 0), (TM + HALO if right_ok else TM)
        for gi, w in enumerate(POOL_WINDOWS):
            bands[variant, gi] = (pos >= t - w // 2) & (pos < t + w // 2) & col_ok
            cnt = np.minimum(t[:, 0] + w // 2, last) - np.maximum(t[:, 0] - w // 2, first)
            inv[variant, :, gi] = 1.0 / cnt
    return jnp.asarray(bands, BF16), jnp.asarray(inv, F32)


def _even_tiles(x_alls, left_oks, right_oks, mod_ref, win_ref, convw_ref, poolw_ref, pscale_ref, wout_ref,
                g_ref, b_ref, band_ref, inv_ref):
    shift, scale, gate = mod_ref[0:1, :], mod_ref[1:2, :], mod_ref[2:3, :]
    n = len(x_alls)
    hs = []
    for x_all, left_ok, right_ok in zip(x_alls, left_oks, right_oks):
        h = x_all * (1.0 + scale) + shift
        hs.append(jnp.concatenate([jnp.where(left_ok, h[:HALO], 0.0), h[HALO:HALO + TM],
                                   jnp.where(right_ok, h[HALO + TM:], 0.0)], axis=0).astype(BF16))
    us = [_dot(h, win_ref[...]) for h in hs]

    mixes = []
    for u, left_ok, right_ok in zip(us, left_oks, right_oks):
        ux, ub = u[:, 0:CONV_DIM], u[:, CONV_DIM:2 * CONV_DIM]
        uc, up = u[:, 2 * CONV_DIM:3 * CONV_DIM], u[:, 3 * CONV_DIM:]
        z = uc * ux
        conv = (z[HALO - 1:HALO - 1 + TM] * convw_ref[0:1, :]
                + z[HALO:HALO + TM] * convw_ref[1:2, :]
                + z[HALO + 1:HALO + 1 + TM] * convw_ref[2:3, :])
        ya = ub[HALO:HALO + TM] * conv

        variant = left_ok.astype(jnp.int32) + 2 * right_ok.astype(jnp.int32)
        inv_cnt = inv_ref[variant]
        up_hi, up_lo = _split_bf16(up)
        yb_groups = []
        for gi in range(len(POOL_WINDOWS)):
            band = band_ref[variant, gi]
            sl = slice(gi * POOL_GROUP, (gi + 1) * POOL_GROUP)
            tot = _dot(band, up_hi[:, sl]) + _dot(band, up_lo[:, sl])
            p = tot * inv_cnt[:, gi:gi + 1] - up[HALO:HALO + TM, sl]
            yb_groups.append(_dot(p.astype(BF16), poolw_ref[gi]))
        yb = jnp.concatenate(yb_groups, axis=1) * pscale_ref[...]
        mixes.append(jnp.concatenate([ya, yb], axis=1).astype(BF16))

    ys = [_dot(mix, wout_ref[...]) for mix in mixes]
    return [_layer_norm(ALPHA * x_alls[k][HALO:HALO + TM] + gate * ys[k], g_ref[...], b_ref[...])
            for k in range(n)]


EV_TILES = 2
EV_ROWS = EV_TILES * TM


N_CAST_EVEN = 3


def _even_mixer_kernel(xctx_ref, xprev_ref, xlat_ref, xnext_ref, mod_ref, win_ref, convw_ref, poolw_ref,
                       pscale_ref, wout_ref, g_ref, b_ref, band_ref, inv_ref, *rest):
    n = N_CAST_EVEN
    cast_in, o_ref, cast_out, xall_ref = rest[:n], rest[n], rest[n + 1:2 * n + 1], rest[-1]
    _cast_step(cast_in, cast_out)
    s = pl.program_id(0)
    n_ctx = T_P // EV_ROWS
    is_latent = s >= n_ctx
    first_tile = lax.rem(s - n_ctx, DEC_SEQ // EV_ROWS) * EV_TILES

    @pl.when(jnp.logical_not(is_latent))
    def _():
        for k in range(EV_TILES):
            xall_ref[k, 0:HALO, :] = jnp.zeros((HALO, D), F32)
            xall_ref[k, HALO:HALO + TM, :] = xctx_ref[k * TM:(k + 1) * TM, :]
            xall_ref[k, HALO + TM:, :] = jnp.zeros((HALO, D), F32)

    @pl.when(is_latent)
    def _():
        for k in range(EV_TILES):
            lo, hi = k * TM, (k + 1) * TM
            xall_ref[k, 0:HALO, :] = xprev_ref[...] if k == 0 else xlat_ref[lo - HALO:lo, :]
            xall_ref[k, HALO:HALO + TM, :] = xlat_ref[lo:hi, :]
            xall_ref[k, HALO + TM:, :] = xnext_ref[...] if k == EV_TILES - 1 else xlat_ref[hi:hi + HALO, :]

    left_oks = [jnp.logical_and(is_latent, first_tile + k != 0) for k in range(EV_TILES)]
    right_oks = [jnp.logical_and(is_latent, first_tile + k != TPS - 1) for k in range(EV_TILES)]
    outs = _even_tiles([xall_ref[k] for k in range(EV_TILES)], left_oks, right_oks, mod_ref, win_ref,
                       convw_ref, poolw_ref, pscale_ref, wout_ref, g_ref, b_ref, band_ref, inv_ref)
    for k in range(EV_TILES):
        o_ref[k * TM:(k + 1) * TM, :] = outs[k]


def _even_mixer(x_ctx, x_lat, mods, w_in, conv_w, pool_w, pool_scale, w_out, g, b, cast_srcs):
    assert len(cast_srcs) == N_CAST_EVEN
    hb = EV_ROWS // HALO
    n8 = T_S // HALO
    n_ctx = T_P // EV_ROWS
    lat = lambda i: jnp.maximum(i - n_ctx, 0)
    cast_specs, cast_shapes = _cast_stream(cast_srcs)
    bands, inv_cnt = _pool_tables()
    outs = pl.pallas_call(
        _even_mixer_kernel,
        grid=(T // EV_ROWS,),
        in_specs=[
            pl.BlockSpec((EV_ROWS, D), lambda i: (jnp.minimum(i, n_ctx - 1), 0)),
            pl.BlockSpec((HALO, D), lambda i: (jnp.maximum(lat(i) * hb - 1, 0), 0)),
            pl.BlockSpec((EV_ROWS, D), lambda i: (lat(i), 0)),
            pl.BlockSpec((HALO, D), lambda i: (jnp.minimum((lat(i) + 1) * hb, n8 - 1), 0)),
            pl.BlockSpec((None, 6, D), lambda i: (_cond_index(i, EV_ROWS), 0, 0)),
            _const_spec((D, 4 * CONV_DIM)),
            _const_spec((3, CONV_DIM)),
            _const_spec((4, POOL_GROUP, POOL_GROUP)),
            _const_spec((1, 4 * POOL_GROUP)),
            _const_spec((D, D)),
            _const_spec((1, D)),
            _const_spec((1, D)),
            _const_spec(bands.shape),
            _const_spec(inv_cnt.shape),
        ] + cast_specs,
        out_specs=[pl.BlockSpec((EV_ROWS, D), lambda i: (i, 0))] + cast_specs,
        out_shape=[jax.ShapeDtypeStruct((T, D), F32)] + cast_shapes,
        scratch_shapes=[pltpu.VMEM((EV_TILES, TM + 2 * HALO, D), F32)],
        compiler_params=_params(1),
        name="even_mixer",
    )(x_ctx, x_lat, x_lat, x_lat, mods, w_in, conv_w, pool_w, pool_scale, w_out, g, b, bands, inv_cnt, *cast_srcs)
    return outs[0], outs[1:]


FF_CHUNK = D_FF // 2


def _ffn_kernel(x_ref, mod_ref, wg_ref, wu_ref, wd_ref, g_ref, b_ref, cast_ref, o_ref, cast_out_ref):
    _cast_step([cast_ref], [cast_out_ref])
    shift, scale, gate = mod_ref[3:4, :], mod_ref[4:5, :], mod_ref[5:6, :]
    x = x_ref[...]
    h = (x * (1.0 + scale) + shift).astype(BF16)
    f = jnp.zeros((TM_FFN, D), F32)
    for c in range(D_FF // FF_CHUNK):
        sl = slice(c * FF_CHUNK, (c + 1) * FF_CHUNK)
        a = _silu(_dot(h, wg_ref[:, sl])) * _dot(h, wu_ref[:, sl])
        f = f + _dot(a.astype(BF16), wd_ref[sl, :])
    o_ref[...] = _layer_norm(ALPHA * x + gate * f, g_ref[...], b_ref[...])


def _ffn(x, mods, wg, wu, wd, g, b, cast_src):
    cast_specs, cast_shapes = _cast_stream([cast_src])
    return pl.pallas_call(
        _ffn_kernel,
        grid=(T // TM_FFN,),
        in_specs=[
            pl.BlockSpec((TM_FFN, D), lambda i: (i, 0)),
            pl.BlockSpec((None, 6, D), lambda i: (_cond_index(i, TM_FFN), 0, 0)),
            _const_spec((D, D_FF)),
            _const_spec((D, D_FF)),
            _const_spec((D_FF, D)),
            _const_spec((1, D)),
            _const_spec((1, D)),
        ] + cast_specs,
        out_specs=[pl.BlockSpec((TM_FFN, D), lambda i: (i, 0))] + cast_specs,
        out_shape=[jax.ShapeDtypeStruct((T, D), F32)] + cast_shapes,
        compiler_params=_params(1),
        name="dense_swiglu",
    )(x, mods, wg, wu, wd, g, b, cast_src)


W_IN_EXT = Q_RANK + KV_RANK + FNET_DIM + 128 + 128
QH = 256
ATT_SCALE = (QK_NOPE + QK_ROPE) ** -0.5 * float(np.log2(np.e))


def _odd_proj_kernel(x_ref, mod_ref, rope_ref, win_ref, qn_ref, kvn_ref, wqa_ref, wqb_ref,
                     avg_ref, dfth_ref, dftl_ref, cast_ref,
                     q_ref, ckv_ref, kpe_ref, y_ref, cast_out_ref):
    _cast_step([cast_ref], [cast_out_ref])
    shift, scale = mod_ref[0:1, :], mod_ref[1:2, :]
    h = (x_ref[...] * (1.0 + scale) + shift).astype(BF16)
    u = _dot(h, win_ref[...])
    uq = u[:, 0:Q_RANK]
    ukv = u[:, Q_RANK:Q_RANK + KV_RANK]
    uf = u[:, Q_RANK + KV_RANK:Q_RANK + KV_RANK + FNET_DIM]
    o = Q_RANK + KV_RANK + FNET_DIM
    upe, upe_rot = u[:, o:o + 128], u[:, o + 128:o + 256]
    cos, sin = rope_ref[:, 0:128], rope_ref[:, 128:256]

    ckv_ref[...] = ukv * lax.rsqrt(jnp.mean(ukv * ukv, axis=-1, keepdims=True) + RMS_EPS) * kvn_ref[...]
    kpe_ref[...] = upe * cos + upe_rot * sin

    qlat = (uq * lax.rsqrt(jnp.mean(uq * uq, axis=-1, keepdims=True) + RMS_EPS) * qn_ref[...]).astype(BF16)
    qa = _dot(qlat, wqa_ref[...])
    qb = _dot(qlat, wqb_ref[...])
    for hd in range(N_HEADS):
        nope = qa[:, hd * QH:hd * QH + 128]
        pe = qa[:, hd * QH + 128:(hd + 1) * QH] * cos + qb[:, hd * 128:(hd + 1) * 128] * sin
        q_ref[:, hd * QH:hd * QH + 128] = (nope * ATT_SCALE).astype(BF16)
        q_ref[:, hd * QH + 128:(hd + 1) * QH] = (pe * ATT_SCALE).astype(BF16)

    avg = avg_ref[...]
    uf_hi, uf_lo = _split_bf16(uf)
    mu = _dot(uf_hi, avg) + _dot(uf_lo, avg)
    dlt = uf - mu
    sq_hi, sq_lo = _split_bf16(dlt * dlt)
    var = _dot(sq_hi, avg) + _dot(sq_lo, avg)
    xn = dlt * lax.rsqrt(var + LN_EPS)
    xn_hi, xn_lo = _split_bf16(xn)
    y = _dot(xn_hi, dfth_ref[...]) + _dot(xn_lo, dfth_ref[...]) + _dot(xn_hi, dftl_ref[...])
    y_ref[...] = y.astype(BF16)


def _odd_proj(x, mods, rope_tab, w_in_ext, q_norm, kv_norm, wqa, wqb, avg, dft_hi, dft_lo, cast_src):
    cast_specs, cast_shapes = _cast_stream([cast_src])

    def rope_index(i):
        return (jnp.where(i < T_P // TM2, 0, 1 + lax.rem(i - T_P // TM2, DEC_SEQ // TM2)), 0)

    return pl.pallas_call(
        _odd_proj_kernel,
        grid=(T // TM2,),
        in_specs=[
            pl.BlockSpec((TM2, D), lambda i: (i, 0)),
            pl.BlockSpec((None, 6, D), lambda i: (_cond_index(i, TM2), 0, 0)),
            pl.BlockSpec((TM2, 256), rope_index),
            _const_spec((D, W_IN_EXT)),
            _const_spec((1, Q_RANK)),
            _const_spec((1, KV_RANK)),
            _const_spec((Q_RANK, N_HEADS * QH)),
            _const_spec((Q_RANK, N_HEADS * 128)),
            _const_spec((FNET_DIM, FNET_DIM)),
            _const_spec((FNET_DIM, 2 * FNET_DIM)),
            _const_spec((FNET_DIM, 2 * FNET_DIM)),
        ] + cast_specs,
        out_specs=[
            pl.BlockSpec((TM2, N_HEADS * QH), lambda i: (i, 0)),
            pl.BlockSpec((TM2, KV_RANK), lambda i: (i, 0)),
            pl.BlockSpec((TM2, 128), lambda i: (i, 0)),
            pl.BlockSpec((TM2, 2 * FNET_DIM), lambda i: (i, 0)),
        ] + cast_specs,
        out_shape=[
            jax.ShapeDtypeStruct((T, N_HEADS * QH), BF16),
            jax.ShapeDtypeStruct((T, KV_RANK), F32),
            jax.ShapeDtypeStruct((T, 128), F32),
            jax.ShapeDtypeStruct((T, 2 * FNET_DIM), BF16),
        ] + cast_shapes,
        compiler_params=_params(1),
        name="odd_projections",
    )(x, mods, rope_tab, w_in_ext, q_norm, kv_norm, wqa, wqb, avg, dft_hi, dft_lo, cast_src)


V_OFF = N_HEADS * QH
KV_COLS = V_OFF + N_HEADS * V_DIM


def _attn_body(q_ref, kv_ref, o_ref):
    for hd in range(N_HEADS):
        qh = q_ref[:, hd * QH:(hd + 1) * QH]
        kh = kv_ref[:, hd * QH:(hd + 1) * QH]
        s = lax.dot_general(qh, kh, (((1,), (1,)), ((), ())), preferred_element_type=F32)
        p = jnp.exp2(s - jnp.max(s, axis=-1, keepdims=True))
        den = jnp.sum(p, axis=-1, keepdims=True)
        vh = kv_ref[:, V_OFF + hd * V_DIM:V_OFF + (hd + 1) * V_DIM]
        o = _dot(p.astype(BF16), vh)
        o_ref[:, hd * 128:(hd + 1) * 128] = (o / den).astype(BF16)


KV_CHUNK = 512


def _attn_kernel(q_ref, ckvp_ref, kpep_ref, ckvs_ref, kpes_ref, cckv_ref, ckpe_ref, wkv_ref, o_ref,
                 kvp_s, kvs_s):
    i = pl.program_id(0)

    def expand(ckv, kpe, dst, row0):
        n = ckv.shape[0]
        kv = _dot(ckv.astype(BF16), wkv_ref[...]).astype(BF16)
        kpe = kpe.astype(BF16)
        for hd in range(N_HEADS):
            dst[row0:row0 + n, hd * QH:hd * QH + QK_NOPE] = kv[:, hd * QK_NOPE:(hd + 1) * QK_NOPE]
            dst[row0:row0 + n, hd * QH + QK_NOPE:(hd + 1) * QH] = kpe
        dst[row0:row0 + n, V_OFF:] = kv[:, N_HEADS * QK_NOPE:]

    @pl.when(i < NP_TILES)
    def _():
        expand(ckvp_ref[...], kpep_ref[...], kvp_s, 0)
        _attn_body(q_ref, kvp_s, o_ref)

    @pl.when(i >= NP_TILES)
    def _():
        @pl.when(lax.rem(i - NP_TILES, TPS) == 0)
        def _():
            expand(cckv_ref[...], ckpe_ref[...], kvs_s, 0)
            for c in range(DEC_SEQ // KV_CHUNK):
                rows = slice(c * KV_CHUNK, (c + 1) * KV_CHUNK)
                expand(ckvs_ref[rows, :], kpes_ref[rows, :], kvs_s, PAST + c * KV_CHUNK)

        _attn_body(q_ref, kvs_s, o_ref)


def _attention(q, ckv, kpe, cache_ckv, cache_kpe128, w_kv):
    ctx_blk = lambda i: (jnp.minimum(i, NP_TILES - 1), 0)
    lat_b = lambda i: jnp.maximum(i - NP_TILES, 0) // TPS
    lat_blk = lambda i: (T_P // DEC_SEQ + lat_b(i), 0)
    return pl.pallas_call(
        _attn_kernel,
        grid=(N_TILES,),
        in_specs=[
            pl.BlockSpec((TM, N_HEADS * QH), lambda i: (i, 0)),
            pl.BlockSpec((SEQ, KV_RANK), ctx_blk),
            pl.BlockSpec((SEQ, 128), ctx_blk),
            pl.BlockSpec((DEC_SEQ, KV_RANK), lat_blk),
            pl.BlockSpec((DEC_SEQ, 128), lat_blk),
            pl.BlockSpec((None, PAST, KV_RANK), lambda i: (lat_b(i), 0, 0)),
            pl.BlockSpec((None, PAST, 128), lambda i: (lat_b(i), 0, 0)),
            _const_spec((KV_RANK, 2 * N_HEADS * 128)),
        ],
        out_specs=pl.BlockSpec((TM, N_HEADS * V_DIM), lambda i: (i, 0)),
        out_shape=jax.ShapeDtypeStruct((T, N_HEADS * V_DIM), BF16),
        scratch_shapes=[pltpu.VMEM((SEQ, KV_COLS), BF16), pltpu.VMEM((LK_S, KV_COLS), BF16)],
        compiler_params=_params(1),
        name="attention",
    )(q, ckv, kpe, ckv, kpe, cache_ckv, cache_kpe128, w_kv)


def _pos_dft_body(y_ref, c, s, o_ref):
    f = _dot(c, y_ref[:, 0:FNET_DIM]) + _dot(s, y_ref[:, FNET_DIM:])
    o_ref[...] = f.astype(BF16)


def _pos_dft_kernel(yp_ref, cp_ref, sp_ref, ys_ref, c0_ref, s0_ref, cb_ref, sb_ref, o_ref):
    @pl.when(pl.program_id(0) < NP_TILES)
    def _():
        _pos_dft_body(yp_ref, cp_ref[...].astype(BF16), sp_ref[...].astype(BF16), o_ref)

    @pl.when(pl.program_id(0) >= NP_TILES)
    def _():
        j = lax.rem(pl.program_id(0) - NP_TILES, TPS)
        cb, sb = cb_ref[pl.ds(j, 1), :], sb_ref[pl.ds(j, 1), :]
        c0, s0 = c0_ref[...], s0_ref[...]
        _pos_dft_body(ys_ref, (c0 * cb - s0 * sb).astype(BF16), (s0 * cb + c0 * sb).astype(BF16), o_ref)


def _pos_dft(y, tabs_p, base_s, step_s):
    ctx_blk = lambda i: (jnp.minimum(i, NP_TILES - 1), 0)
    lat_seq = lambda i: (T_P // DEC_SEQ + jnp.maximum(i - NP_TILES, 0) // TPS, 0)
    return pl.pallas_call(
        _pos_dft_kernel,
        grid=(N_TILES,),
        in_specs=[pl.BlockSpec((SEQ, 2 * FNET_DIM), ctx_blk)]
        + [_const_spec((SEQ, SEQ))] * 2
        + [pl.BlockSpec((DEC_SEQ, 2 * FNET_DIM), lat_seq)]
        + [_const_spec((TM, DEC_SEQ))] * 2
        + [_const_spec((TPS, DEC_SEQ))] * 2,
        out_specs=pl.BlockSpec((TM, FNET_DIM), lambda i: (i, 0)),
        out_shape=jax.ShapeDtypeStruct((T, FNET_DIM), BF16),
        compiler_params=_params(1),
        name="pos_dft",
    )(y, *tabs_p, y, *base_s, *step_s)


def _odd_merge_kernel(x_ref, attn_ref, f_ref, mod_ref, fw_ref, wo_ref, g_ref, b_ref, rh_ref, rl_ref, tri_ref,
                      cast_ref, xo_ref, info_ref, infot_ref, cnt_ref, cast_out_ref, carry_ref):
    _cast_step([cast_ref], [cast_out_ref])

    @pl.when(pl.program_id(0) == 0)
    def _():
        carry_ref[...] = jnp.zeros_like(carry_ref)

    gate = mod_ref[2:3, :]
    shift2, scale2 = mod_ref[3:4, :], mod_ref[4:5, :]
    fm = _dot(f_ref[...], fw_ref[...]).astype(BF16)
    y = _dot(attn_ref[...], wo_ref[0:N_HEADS * V_DIM, :]) + _dot(fm, wo_ref[N_HEADS * V_DIM:, :])
    x = _layer_norm(ALPHA * x_ref[...] + gate * y, g_ref[...], b_ref[...])
    xo_ref[...] = x
    h = x * (1.0 + scale2) + shift2

    h_hi, h_lo = _split_bf16(h)
    logits = _dot(h_hi, rh_ref[...]) + _dot(h_lo, rh_ref[...]) + _dot(h_hi, rl_ref[...])
    lane = lax.broadcasted_iota(jnp.int32, (TM2, 128), 1)
    neg = jnp.float32(-jnp.inf)
    logits = jnp.where(lane < N_EXPERTS, logits, neg)
    m1 = jnp.max(logits, axis=-1, keepdims=True)
    i1 = jnp.min(jnp.where(logits == m1, lane, 128), axis=-1, keepdims=True)
    rest = jnp.where(lane == i1, neg, logits)
    m2 = jnp.max(rest, axis=-1, keepdims=True)
    i2 = jnp.min(jnp.where(rest == m2, lane, 128), axis=-1, keepdims=True)
    e2 = jnp.exp(m2 - m1)
    w1 = 1.0 / (1.0 + e2)
    w2 = e2 / (1.0 + e2)
    info = jnp.where(lane == 0, w1, 0.0)
    info = jnp.where(lane == 1, w2, info)
    info = jnp.where(lane == 2, i1.astype(F32), info)
    info = jnp.where(lane == 3, i2.astype(F32), info)

    uses = jnp.logical_or(lane == i1, lane == i2)
    seen = _dot(tri_ref[...], jnp.where(uses, 1.0, 0.0).astype(BF16)) + carry_ref[...]
    r1 = jnp.sum(jnp.where(lane == i1, seen, 0.0), axis=-1, keepdims=True)
    r2 = jnp.sum(jnp.where(lane == i2, seen, 0.0), axis=-1, keepdims=True)
    info = jnp.where(lane == 4, r1, info)
    info = jnp.where(lane == 5, r2, info)
    info_ref[...] = info
    infot_ref[...] = info.T
    total = carry_ref[...] + jnp.sum(jnp.where(uses, 1.0, 0.0), axis=0, keepdims=True)
    carry_ref[...] = total
    cnt_ref[...] = jnp.broadcast_to(total, cnt_ref.shape)


def _odd_merge(x, attn, f, mods, fnet_w, w_out, g, b, r_hi, r_lo, cast_src):
    row = lambda i: (i, 0)
    tri = jnp.asarray(np.tril(np.ones((TM2, TM2), np.float32), -1), BF16)
    cast_specs, cast_shapes = _cast_stream([cast_src])
    return pl.pallas_call(
        _odd_merge_kernel,
        grid=(T // TM2,),
        in_specs=[
            pl.BlockSpec((TM2, D), row),
            pl.BlockSpec((TM2, N_HEADS * V_DIM), row),
            pl.BlockSpec((TM2, FNET_DIM), row),
            pl.BlockSpec((None, 6, D), lambda i: (_cond_index(i, TM2), 0, 0)),
            _const_spec((FNET_DIM, FNET_DIM)),
            _const_spec((N_HEADS * V_DIM + FNET_DIM, D)),
            _const_spec((1, D)),
            _const_spec((1, D)),
            _const_spec((D, 128)),
            _const_spec((D, 128)),
            _const_spec((TM2, TM2)),
        ] + cast_specs,
        out_specs=[pl.BlockSpec((TM2, D), row), pl.BlockSpec((TM2, 128), row),
                   pl.BlockSpec((128, TM2), lambda i: (0, i)), pl.BlockSpec((8, 128), lambda i: (i, 0))] + cast_specs,
        out_shape=[jax.ShapeDtypeStruct((T, D), F32), jax.ShapeDtypeStruct((T, 128), F32),
                   jax.ShapeDtypeStruct((128, T), F32), jax.ShapeDtypeStruct((T // TM2 * 8, 128), F32)] + cast_shapes,
        scratch_shapes=[pltpu.VMEM((1, 128), F32)],
        compiler_params=_params(1),
        name="odd_merge_router",
    )(x, attn, f, mods, fnet_w, w_out, g, b, r_hi, r_lo, tri, cast_src)


TM_D = 512
ROW = (8, 128)
ROW_DT = BF16
DMA_UNROLL = 8


def _row_copy(src, s, dst, d, sem):
    return pltpu.make_async_copy(src.at[s], dst.at[d], sem)


CHUNK = 16
LOCAL_ROWS = 2 * TM2 + N_EXPERTS * CHUNK


def _chunk_copies(tile, nch_ref, fn):
    for e in range(N_EXPERTS):
        def body(c, carry, e=e):
            fn(tile * N_EXPERTS + e, c, e % 2)
            return carry

        lax.fori_loop(0, nch_ref[tile * N_EXPERTS + e], body, 0)


def _dispatch_kernel(dest_ref, pad_lo_ref, pad_hi_ref, x_ref, mod_ref, xs_ref, h_ref, zero_ref, sem, zsem):
    i = pl.program_id(0)
    base = i * TM_D
    slot = lax.rem(i, 2)

    @pl.when(i == 0)
    def _():
        zero_ref[...] = jnp.zeros_like(zero_ref)
        for e in range(N_EXPERTS + 1):
            def zissue(r, carry):
                _row_copy(zero_ref, 0, xs_ref, r, zsem).start()
                return carry

            def zdrain(r, carry):
                _row_copy(zero_ref, 0, xs_ref, 0, zsem).wait()
                return carry

            lax.fori_loop(pad_lo_ref[e], pad_hi_ref[e], zissue, 0)
            lax.fori_loop(pad_lo_ref[e], pad_hi_ref[e], zdrain, 0)

    shift2, scale2 = mod_ref[3:4, :], mod_ref[4:5, :]
    h_ref[slot] = (x_ref[...] * (1.0 + scale2) + shift2).astype(ROW_DT).reshape((TM_D,) + ROW)

    def issue(r, carry):
        t = base + r
        _row_copy(h_ref.at[slot], r, xs_ref, dest_ref[t], sem.at[slot]).start(priority=0)
        _row_copy(h_ref.at[slot], r, xs_ref, dest_ref[T + t], sem.at[slot]).start(priority=1)
        return carry

    lax.fori_loop(0, TM_D, issue, 0, unroll=DMA_UNROLL)

    def drain(s):
        def body(r, carry):
            _row_copy(h_ref.at[s], 0, xs_ref, 0, sem.at[s]).wait()
            _row_copy(h_ref.at[s], 0, xs_ref, 0, sem.at[s]).wait()
            return carry

        lax.fori_loop(0, TM_D, body, 0, unroll=DMA_UNROLL)

    @pl.when(i > 0)
    def _():
        drain(1 - slot)

    @pl.when(i == T // TM_D - 1)
    def _():
        drain(slot)


def _dispatch(dest, pad_lo, pad_hi, x, mods):
    return pl.pallas_call(
        _dispatch_kernel,
        grid_spec=pltpu.PrefetchScalarGridSpec(
            num_scalar_prefetch=3,
            grid=(T // TM_D,),
            in_specs=[pl.BlockSpec((TM_D, D), lambda i, *_: (i, 0)),
                      pl.BlockSpec((None, 6, D), lambda i, *_: (_cond_index(i, TM_D), 0, 0))],
            out_specs=pl.BlockSpec(memory_space=pl.ANY),
            scratch_shapes=[pltpu.VMEM((2, TM_D) + ROW, ROW_DT), pltpu.VMEM((1,) + ROW, ROW_DT),
                            pltpu.SemaphoreType.DMA((2,)), pltpu.SemaphoreType.DMA(())],
        ),
        out_shape=jax.ShapeDtypeStruct((R_MAX,) + ROW, ROW_DT),
        compiler_params=pltpu.CompilerParams(dimension_semantics=("arbitrary",), has_side_effects=True),
        name="expert_dispatch",
    )(dest, pad_lo, pad_hi, x, mods)


def _expert_kernel(te_ref, nt_ref, xs_ref, wg_ref, wu_ref, wd_ref, o_ref):
    @pl.when(pl.program_id(0) < nt_ref[0])
    def _():
        h = xs_ref[...].reshape(TM_E, D)
        a = _silu(_dot(h, wg_ref[...])) * _dot(h, wu_ref[...])
        o_ref[...] = _dot(a.astype(BF16), wd_ref[...]).astype(ROW_DT).reshape((TM_E,) + ROW)

    @pl.when(pl.program_id(0) >= nt_ref[0])
    def _():
        o_ref[...] = jnp.zeros_like(o_ref)


def _experts(tile_expert, n_used, xs, wg, wu, wd):
    return pl.pallas_call(
        _expert_kernel,
        grid_spec=pltpu.PrefetchScalarGridSpec(
            num_scalar_prefetch=2,
            grid=(N_ETILES,),
            in_specs=[
                pl.BlockSpec((TM_E,) + ROW, lambda i, te, nt: (jnp.minimum(i, nt[0] - 1), 0, 0)),
                pl.BlockSpec((None, D, D_FF_EXPERT), lambda i, te, nt: (te[i], 0, 0)),
                pl.BlockSpec((None, D, D_FF_EXPERT), lambda i, te, nt: (te[i], 0, 0)),
                pl.BlockSpec((None, D_FF_EXPERT, D), lambda i, te, nt: (te[i], 0, 0)),
            ],
            out_specs=pl.BlockSpec((TM_E,) + ROW, lambda i, te, nt: (i, 0, 0)),
        ),
        out_shape=jax.ShapeDtypeStruct((R_MAX,) + ROW, ROW_DT),
        compiler_params=_params(1),
        name="expert_swiglu",
    )(tile_expert, n_used, xs, wg, wu, wd)


def _combine_kernel(lp_ref, d0_ref, off_ref, nch_ref, x_ref, info_ref, mod_ref, g_ref, b_ref, ys_ref,
                    op_ref, os_ref, local_ref, rows_ref, sem):
    i = pl.program_id(0)
    slot = lax.rem(i, 2)

    def chunk(s, seg, c):
        return pltpu.make_async_copy(ys_ref.at[pl.ds(d0_ref[seg] + c * CHUNK, CHUNK)],
                                     local_ref.at[s, pl.ds(off_ref[seg] + c * CHUNK, CHUNK)], sem.at[s])

    @pl.when(i == 0)
    def _():
        _chunk_copies(0, nch_ref, lambda seg, c, prio: chunk(0, seg, c).start(priority=prio))

    @pl.when(i + 1 < T // TM2)
    def _():
        _chunk_copies(i + 1, nch_ref, lambda seg, c, prio: chunk(1 - slot, seg, c).start(priority=prio))

    _chunk_copies(i, nch_ref, lambda seg, c, prio: chunk(slot, seg, c).wait())

    def pick(r, carry):
        rows_ref[0, r] = local_ref[slot, lp_ref[i * TM2 + r]]
        rows_ref[1, r] = local_ref[slot, lp_ref[T + i * TM2 + r]]
        return carry

    lax.fori_loop(0, TM2, pick, 0, unroll=DMA_UNROLL)

    gate = mod_ref[5:6, :]
    w1, w2 = info_ref[:, 0:1], info_ref[:, 1:2]
    y = w1 * rows_ref[0].reshape(TM2, D).astype(F32) + w2 * rows_ref[1].reshape(TM2, D).astype(F32)
    out = _layer_norm(ALPHA * x_ref[...] + gate * y, g_ref[...], b_ref[...])

    @pl.when(pl.program_id(0) < T_P // TM2)
    def _():
        op_ref[...] = out

    @pl.when(pl.program_id(0) >= T_P // TM2)
    def _():
        os_ref[...] = out


def _combine(lp, d0, off, nch, x, info, mods, g, b, ys):
    return pl.pallas_call(
        _combine_kernel,
        grid_spec=pltpu.PrefetchScalarGridSpec(
            num_scalar_prefetch=4,
            grid=(T // TM2,),
            in_specs=[
                pl.BlockSpec((TM2, D), lambda i, *_: (i, 0)),
                pl.BlockSpec((TM2, 128), lambda i, *_: (i, 0)),
                pl.BlockSpec((None, 6, D), lambda i, *_: (_cond_index(i, TM2), 0, 0)),
                pl.BlockSpec((1, D), lambda i, *_: (0, 0)),
                pl.BlockSpec((1, D), lambda i, *_: (0, 0)),
                pl.BlockSpec(memory_space=pl.ANY),
            ],
            out_specs=[pl.BlockSpec((TM2, D), lambda i, *_: (jnp.minimum(i, T_P // TM2 - 1), 0)),
                       pl.BlockSpec((TM2, D), lambda i, *_: (jnp.maximum(i - T_P // TM2, 0), 0))],
            scratch_shapes=[pltpu.VMEM((2, LOCAL_ROWS) + ROW, ROW_DT), pltpu.VMEM((2, TM2) + ROW, ROW_DT),
                            pltpu.SemaphoreType.DMA((2,))],
        ),
        out_shape=[jax.ShapeDtypeStruct((T_P, D), F32), jax.ShapeDtypeStruct((T_S, D), F32)],
        compiler_params=_params(1),
        name="expert_combine",
    )(lp, d0, off, nch, x, info, mods, g, b, ys)


def _rot_cols(w):
    w4 = w.reshape(w.shape[:-1] + (2, 2, QK_ROPE // 4))
    return jnp.stack([-w4[..., 1, :], w4[..., 0, :]], axis=-2).reshape(w.shape)


def _rope_table():
    rows = DEC_SEQ // GRID_W
    row = np.repeat(np.arange(rows), GRID_W).astype(np.float32)
    col = np.tile(np.arange(GRID_W), rows).astype(np.float32)
    half = QK_ROPE // 2
    inv = (ROPE_THETA ** (-np.arange(0, half, 2, dtype=np.float32) / half)).astype(np.float32)
    ar, ac = row[:, None] * inv, col[:, None] * inv
    ang = np.concatenate([ar, ar, ac, ac], axis=-1)
    cos = np.concatenate([np.ones((TM2, QK_ROPE)), np.cos(ang)], axis=0)
    sin = np.concatenate([np.zeros((TM2, QK_ROPE)), np.sin(ang)], axis=0)
    n = cos.shape[0]
    return jnp.asarray(np.concatenate([cos, np.ones((n, 64)), sin, np.zeros((n, 64))], axis=1), F32)


def _dft_angles(rows, n):
    k = np.arange(n, dtype=np.int64)
    return ((np.asarray(rows, np.int64)[:, None] * k[None, :]) % n) * (2.0 * np.pi / n)


def _dft_tables(n):
    ang = _dft_angles(np.arange(n), n)
    return np.cos(ang) * n ** -0.5, np.sin(ang) * n ** -0.5


def _hi_lo(m):
    m = jnp.asarray(m, F32)
    hi = m.astype(BF16)
    return hi, (m - hi.astype(F32)).astype(BF16)


def _block_diag4(m):
    return np.kron(np.eye(4), m)


def kernel(x_prompt, x_sample, cache_ckv, cache_kpe, c, c_ctx, ada_w, ada_b, ln_g, ln_b, ev_w_in, ev_conv_w, ev_pool_w, ev_pool_scale, ev_w_out, ffn_w_gate, ffn_w_up, ffn_w_down, od_w_in, od_q_norm, od_kv_norm, od_w_q_b, od_w_kv_b, od_fnet_w, od_w_out, moe_router, moe_w_gate, moe_w_up, moe_w_down):
    cond8 = jnp.concatenate([c_ctx[None, :], c, jnp.zeros((8 - N_COND, D), F32)], axis=0)
    mods = _modulation(cond8, ada_w, ada_b)[:, :N_COND].reshape(DEPTH, N_COND, 6, D)

    x, (ffn_wg, ffn_wu, ffn_wd) = _even_mixer(
        x_prompt.reshape(T_P, D), x_sample.reshape(T_S, D), mods[0], ev_w_in[0].astype(BF16), ev_conv_w[0],
        ev_pool_w[0].astype(BF16), ev_pool_scale[0][None, :], ev_w_out[0].astype(BF16),
        ln_g[0, 0][None, :], ln_b[0, 0][None, :], [ffn_w_gate[0], ffn_w_up[0], ffn_w_down[0]])
    x, moe_wd = _ffn(x, mods[0], ffn_wg, ffn_wu, ffn_wd, ln_g[0, 1][None, :], ln_b[0, 1][None, :],
                     moe_w_down[0].reshape(N_EXPERTS * D_FF_EXPERT, D))

    w_in = od_w_in[0]
    w_pe = w_in[:, Q_RANK + KV_RANK:Q_RANK + KV_RANK + QK_ROPE]
    zpad = jnp.zeros((D, 64), F32)
    w_in_ext = jnp.concatenate([w_in[:, :Q_RANK + KV_RANK], w_in[:, Q_RANK + KV_RANK + QK_ROPE:],
                                w_pe, zpad, _rot_cols(w_pe), zpad], axis=1).astype(BF16)
    wq = od_w_q_b[0].reshape(Q_RANK, N_HEADS, QK_NOPE + QK_ROPE)
    zq = jnp.zeros((Q_RANK, N_HEADS, 64), F32)
    wqa = jnp.concatenate([wq, zq], axis=-1).reshape(Q_RANK, N_HEADS * QH).astype(BF16)
    wqb = jnp.concatenate([_rot_cols(wq[..., QK_NOPE:]), zq], axis=-1).reshape(Q_RANK, N_HEADS * 128).astype(BF16)
    wkv = od_w_kv_b[0].reshape(KV_RANK, N_HEADS, QK_NOPE + V_DIM)
    w_kv = jnp.concatenate([wkv[..., :QK_NOPE].reshape(KV_RANK, -1), wkv[..., QK_NOPE:].reshape(KV_RANK, -1)],
                           axis=1).astype(BF16)

    avg = jnp.asarray(_block_diag4(np.full((FNET_GROUP_DIM, FNET_GROUP_DIM), 1.0 / FNET_GROUP_DIM)), BF16)
    cc, sc = _dft_tables(FNET_GROUP_DIM)
    dft_hi, dft_lo = _hi_lo(np.concatenate([_block_diag4(cc), -_block_diag4(sc)], axis=1))

    q, ckv, kpe, y_dft, moe_wg = _odd_proj(x, mods[1], _rope_table(), w_in_ext, od_q_norm[0][None, :],
                                            od_kv_norm[0][None, :], wqa, wqb, avg, dft_hi, dft_lo,
                                            moe_w_gate[0].reshape(N_EXPERTS * D, D_FF_EXPERT))

    cache_kpe128 = jnp.pad(cache_kpe[:, 0], ((0, 0), (0, 0), (0, 128 - QK_ROPE)))
    attn = _attention(q, ckv, kpe, cache_ckv[:, 0], cache_kpe128, w_kv)

    tabs_p = tuple(jnp.asarray(m, F32) for m in _dft_tables(SEQ))
    a_base = _dft_angles(np.arange(TM), DEC_SEQ)
    a_step = _dft_angles(np.arange(TPS) * TM, DEC_SEQ)
    base_s = (jnp.asarray(np.cos(a_base) * DEC_SEQ ** -0.5, F32), jnp.asarray(np.sin(a_base) * DEC_SEQ ** -0.5, F32))
    step_s = (jnp.asarray(np.cos(a_step), F32), jnp.asarray(np.sin(a_step), F32))
    f = _pos_dft(y_dft, tabs_p, base_s, step_s)

    router = jnp.pad(moe_router[0], ((0, 0), (0, 128 - N_EXPERTS)))
    r_hi, r_lo = _hi_lo(router)
    x, info, info_t, cnt, moe_wu = _odd_merge(x, attn, f, mods[1], od_fnet_w[0].astype(BF16),
                                              od_w_out[0].astype(BF16), ln_g[1, 0][None, :], ln_b[1, 0][None, :],
                                              r_hi, r_lo, moe_w_up[0].reshape(N_EXPERTS * D, D_FF_EXPERT))

    n_tiles = T // TM2
    after = cnt.reshape(n_tiles, 8, 128)[:, 0, :N_EXPERTS].astype(jnp.int32)
    before = jnp.concatenate([jnp.zeros((1, N_EXPERTS), jnp.int32), after[:-1]], axis=0)
    counts = after[-1]
    padded = ((counts + TM_E - 1) // TM_E) * TM_E
    g_end = jnp.cumsum(padded)
    g_start = g_end - padded
    n_chunks = (after - before + CHUNK - 1) // CHUNK
    seg_off = CHUNK * (jnp.cumsum(n_chunks, axis=1) - n_chunks)
    seg_dst = g_start[None, :] + before
    choice = info_t[2:4].astype(jnp.int32)
    rank = info_t[4:6].astype(jnp.int32)
    shift_t = jnp.repeat((seg_off - before).T, TM2, axis=1)
    lp, dest = rank, rank
    for e in range(N_EXPERTS):
        lp = lp + jnp.where(choice == e, shift_t[e][None, :], 0)
        dest = dest + jnp.where(choice == e, g_start[e], 0)
    lp = lp.reshape(-1).astype(jnp.int32)
    dest = dest.reshape(-1).astype(jnp.int32)
    d0, off, nch = (a.reshape(-1).astype(jnp.int32) for a in (seg_dst, seg_off, n_chunks))
    tile_row = jnp.arange(N_ETILES, dtype=jnp.int32) * TM_E
    tile_expert = jnp.minimum(jnp.sum((tile_row[:, None] >= g_end[None, :]).astype(jnp.int32), axis=1),
                              N_EXPERTS - 1).astype(jnp.int32)
    n_used = (g_end[-1:] // TM_E).astype(jnp.int32)

    pad_lo = jnp.concatenate([g_start + counts, g_end[-1:]]).astype(jnp.int32)
    pad_hi = jnp.concatenate([g_end, jnp.full((1,), R_MAX, jnp.int32)]).astype(jnp.int32)
    xs = _dispatch(dest, pad_lo, pad_hi, x, mods[1])
    ys = _experts(tile_expert, n_used, xs, moe_wg.reshape(N_EXPERTS, D, D_FF_EXPERT),
                  moe_wu.reshape(N_EXPERTS, D, D_FF_EXPERT), moe_wd.reshape(N_EXPERTS, D_FF_EXPERT, D))
    yp, ysm = _combine(lp, d0, off, nch, x, info, mods[1], ln_g[1, 1][None, :], ln_b[1, 1][None, :], ys)

    y_prompt = yp.reshape(BATCH, SEQ, D)
    y_sample = ysm.reshape(DEC_BATCH, DEC_SEQ, D)
    new_ckv = ckv[:T_P].reshape(BATCH, 1, SEQ, KV_RANK)
    new_kpe = kpe[:T_P, :QK_ROPE].reshape(BATCH, 1, SEQ, QK_ROPE)
    return (y_prompt, y_sample, new_ckv, new_kpe)
```

```python
import functools

import numpy as np
import jax
import jax.numpy as jnp
from jax import lax
from jax.experimental import pallas as pl
from jax.experimental.pallas import tpu as pltpu

F32 = jnp.float32
BF16 = jnp.bfloat16

D = 1024
BATCH, SEQ = 32, 256
DEC_BATCH, DEC_SEQ = 2, 2048
PAST = 512
GRID_W = 64
T_P = BATCH * SEQ
T_S = DEC_BATCH * DEC_SEQ
T = T_P + T_S
N_COND = 1 + DEC_BATCH

CONV_DIM = 512
POOL_WINDOWS = (2, 4, 8, 16)
POOL_GROUP = 128
N_HEADS = 8
QK_NOPE, QK_ROPE, V_DIM = 128, 64, 128
Q_RANK, KV_RANK = 384, 256
FNET_DIM, FNET_GROUP_DIM = 256, 64
D_FF = 2816
N_EXPERTS = 8
D_FF_EXPERT = 1792
DEPTH = 2
ALPHA = (2 * DEPTH) ** 0.25
LN_EPS = 1e-5
RMS_EPS = 1e-6
ROPE_THETA = 10000.0

TM = 256
NP_TILES = T_P // TM
TPS = DEC_SEQ // TM
N_TILES = T // TM
HALO = 8
TM2 = 512
TM_FFN = 512
LK_S = PAST + DEC_SEQ
TM_E = 256
N_ETILES = (2 * T) // TM_E + N_EXPERTS + 1
R_MAX = N_ETILES * TM_E
VMEM_LIMIT = 56 * 1024 * 1024


def _cond_index(i, tm=TM):
    return jnp.where(i < T_P // tm, 0, 1 + (i - T_P // tm) // (DEC_SEQ // tm))


def _const_spec(shape):
    nd = len(shape)
    return pl.BlockSpec(shape, lambda *_: (0,) * nd, pipeline_mode=pl.Buffered(1))


def _params(n_axes=1, vmem=VMEM_LIMIT):
    return pltpu.CompilerParams(dimension_semantics=("arbitrary",) * n_axes, vmem_limit_bytes=vmem)


def _layer_norm(v, g, b):
    mu = jnp.mean(v, axis=-1, keepdims=True)
    d = v - mu
    var = jnp.mean(d * d, axis=-1, keepdims=True)
    return d * lax.rsqrt(var + LN_EPS) * g + b


def _split_bf16(v):
    hi = v.astype(BF16)
    lo = (v - hi.astype(F32)).astype(BF16)
    return hi, lo


def _dot(a, b):
    return jnp.dot(a, b, preferred_element_type=F32)


def _silu(v):
    return v / (1.0 + jnp.exp(-v))


CAST_BLOCKS = 16


def _cast_stream(srcs):
    specs = [pl.BlockSpec((w.shape[0] // CAST_BLOCKS, w.shape[1]),
                          lambda i, *_: (jnp.minimum(i, CAST_BLOCKS - 1), 0)) for w in srcs]
    return specs, [jax.ShapeDtypeStruct(w.shape, BF16) for w in srcs]


def _cast_step(srcs, dsts):
    @pl.when(pl.program_id(0) < CAST_BLOCKS)
    def _():
        for src, dst in zip(srcs, dsts):
            dst[...] = src[...].astype(BF16)


def _mod_kernel(cond_ref, w_ref, b_ref, o_ref):
    s = _silu(cond_ref[...]).astype(BF16)
    o_ref[...] = _dot(s, w_ref[...].astype(BF16)) + b_ref[...]


def _modulation(cond8, ada_w, ada_b):
    nb = 6 * D // 1024
    return pl.pallas_call(
        _mod_kernel,
        grid=(DEPTH, nb),
        in_specs=[
            pl.BlockSpec((8, D), lambda l, j: (0, 0)),
            pl.BlockSpec((None, D, 1024), lambda l, j: (l, 0, j)),
            pl.BlockSpec((None, 1, 1024), lambda l, j: (l, 0, j)),
        ],
        out_specs=pl.BlockSpec((None, 8, 1024), lambda l, j: (l, 0, j)),
        out_shape=jax.ShapeDtypeStruct((DEPTH, 8, 6 * D), F32),
        compiler_params=_params(2),
        name="adaln_modulation",
    )(cond8, ada_w, ada_b.reshape(DEPTH, 1, 6 * D))


def _pool_tables():
    t = np.arange(TM)[:, None]
    r = np.arange(TM + 2 * HALO)[None, :]
    pos = r - HALO
    bands = np.zeros((4, len(POOL_WINDOWS), TM, TM + 2 * HALO), np.float32)
    inv = np.zeros((4, TM, 128), np.float32)
    for variant in range(4):
        left_ok, right_ok = variant & 1, variant >> 1
        col_ok = (r >= (0 if left_ok else HALO)) & (r < (TM + 2 * HALO if right_ok else TM + HALO))
        first, last = (-HALO if left_ok else 0), (TM + HALO if right_ok else TM)
        for gi, w in enumerate(POOL_WINDOWS):
            bands[variant, gi] = (pos >= t - w // 2) & (pos < t + w // 2) & col_ok
            cnt = np.minimum(t[:, 0] + w // 2, last) - np.maximum(t[:, 0] - w // 2, first)
            inv[variant, :, gi] = 1.0 / cnt
    return jnp.asarray(bands, BF16), jnp.asarray(inv, F32)


def _even_tiles(x_alls, left_oks, right_oks, mod_ref, win_ref, convw_ref, poolw_ref, pscale_ref, wout_ref,
                g_ref, b_ref, band_ref, inv_ref):
    shift, scale, gate = mod_ref[0:1, :], mod_ref[1:2, :], mod_ref[2:3, :]
    n = len(x_alls)
    hs = []
    for x_all, left_ok, right_ok in zip(x_alls, left_oks, right_oks):
        h = x_all * (1.0 + scale) + shift
        hs.append(jnp.concatenate([jnp.where(left_ok, h[:HALO], 0.0), h[HALO:HALO + TM],
                                   jnp.where(right_ok, h[HALO + TM:], 0.0)], axis=0).astype(BF16))
    us = [_dot(h, win_ref[...]) for h in hs]

    mixes = []
    for u, left_ok, right_ok in zip(us, left_oks, right_oks):
        ux, ub = u[:, 0:CONV_DIM], u[:, CONV_DIM:2 * CONV_DIM]
        uc, up = u[:, 2 * CONV_DIM:3 * CONV_DIM], u[:, 3 * CONV_DIM:]
        z = uc * ux
        conv = (z[HALO - 1:HALO - 1 + TM] * convw_ref[0:1, :]
                + z[HALO:HALO + TM] * convw_ref[1:2, :]
                + z[HALO + 1:HALO + 1 + TM] * convw_ref[2:3, :])
        ya = ub[HALO:HALO + TM] * conv

        variant = left_ok.astype(jnp.int32) + 2 * right_ok.astype(jnp.int32)
        inv_cnt = inv_ref[variant]
        up_hi, up_lo = _split_bf16(up)
        yb_groups = []
        for gi in range(len(POOL_WINDOWS)):
            band = band_ref[variant, gi]
            sl = slice(gi * POOL_GROUP, (gi + 1) * POOL_GROUP)
            tot = _dot(band, up_hi[:, sl]) + _dot(band, up_lo[:, sl])
            p = tot * inv_cnt[:, gi:gi + 1] - up[HALO:HALO + TM, sl]
            yb_groups.append(_dot(p.astype(BF16), poolw_ref[gi]))
        yb = jnp.concatenate(yb_groups, axis=1) * pscale_ref[...]
        mixes.append(jnp.concatenate([ya, yb], axis=1).astype(BF16))

    ys = [_dot(mix, wout_ref[...]) for mix in mixes]
    return [_layer_norm(ALPHA * x_alls[k][HALO:HALO + TM] + gate * ys[k], g_ref[...], b_ref[...])
            for k in range(n)]


EV_TILES = 2
EV_ROWS = EV_TILES * TM


N_CAST_EVEN = 3


def _even_mixer_kernel(xctx_ref, xprev_ref, xlat_ref, xnext_ref, mod_ref, win_ref, convw_ref, poolw_ref,
                       pscale_ref, wout_ref, g_ref, b_ref, band_ref, inv_ref, *rest):
    n = N_CAST_EVEN
    cast_in, o_ref, cast_out, xall_ref = rest[:n], rest[n], rest[n + 1:2 * n + 1], rest[-1]
    _cast_step(cast_in, cast_out)
    s = pl.program_id(0)
    n_ctx = T_P // EV_ROWS
    is_latent = s >= n_ctx
    first_tile = lax.rem(s - n_ctx, DEC_SEQ // EV_ROWS) * EV_TILES

    @pl.when(jnp.logical_not(is_latent))
    def _():
        for k in range(EV_TILES):
            xall_ref[k, 0:HALO, :] = jnp.zeros((HALO, D), F32)
            xall_ref[k, HALO:HALO + TM, :] = xctx_ref[k * TM:(k + 1) * TM, :]
            xall_ref[k, HALO + TM:, :] = jnp.zeros((HALO, D), F32)

    @pl.when(is_latent)
    def _():
        for k in range(EV_TILES):
            lo, hi = k * TM, (k + 1) * TM
            xall_ref[k, 0:HALO, :] = xprev_ref[...] if k == 0 else xlat_ref[lo - HALO:lo, :]
            xall_ref[k, HALO:HALO + TM, :] = xlat_ref[lo:hi, :]
            xall_ref[k, HALO + TM:, :] = xnext_ref[...] if k == EV_TILES - 1 else xlat_ref[hi:hi + HALO, :]

    left_oks = [jnp.logical_and(is_latent, first_tile + k != 0) for k in range(EV_TILES)]
    right_oks = [jnp.logical_and(is_latent, first_tile + k != TPS - 1) for k in range(EV_TILES)]
    outs = _even_tiles([xall_ref[k] for k in range(EV_TILES)], left_oks, right_oks, mod_ref, win_ref,
                       convw_ref, poolw_ref, pscale_ref, wout_ref, g_ref, b_ref, band_ref, inv_ref)
    for k in range(EV_TILES):
        o_ref[k * TM:(k + 1) * TM, :] = outs[k]


def _even_mixer(x_ctx, x_lat, mods, w_in, conv_w, pool_w, pool_scale, w_out, g, b, cast_srcs):
    assert len(cast_srcs) == N_CAST_EVEN
    hb = EV_ROWS // HALO
    n8 = T_S // HALO
    n_ctx = T_P // EV_ROWS
    lat = lambda i: jnp.maximum(i - n_ctx, 0)
    cast_specs, cast_shapes = _cast_stream(cast_srcs)
    bands, inv_cnt = _pool_tables()
    outs = pl.pallas_call(
        _even_mixer_kernel,
        grid=(T // EV_ROWS,),
        in_specs=[
            pl.BlockSpec((EV_ROWS, D), lambda i: (jnp.minimum(i, n_ctx - 1), 0)),
            pl.BlockSpec((HALO, D), lambda i: (jnp.maximum(lat(i) * hb - 1, 0), 0)),
            pl.BlockSpec((EV_ROWS, D), lambda i: (lat(i), 0)),
            pl.BlockSpec((HALO, D), lambda i: (jnp.minimum((lat(i) + 1) * hb, n8 - 1), 0)),
            pl.BlockSpec((None, 6, D), lambda i: (_cond_index(i, EV_ROWS), 0, 0)),
            _const_spec((D, 4 * CONV_DIM)),
            _const_spec((3, CONV_DIM)),
            _const_spec((4, POOL_GROUP, POOL_GROUP)),
            _const_spec((1, 4 * POOL_GROUP)),
            _const_spec((D, D)),
            _const_spec((1, D)),
            _const_spec((1, D)),
            _const_spec(bands.shape),
            _const_spec(inv_cnt.shape),
        ] + cast_specs,
        out_specs=[pl.BlockSpec((EV_ROWS, D), lambda i: (i, 0))] + cast_specs,
        out_shape=[jax.ShapeDtypeStruct((T, D), F32)] + cast_shapes,
        scratch_shapes=[pltpu.VMEM((EV_TILES, TM + 2 * HALO, D), F32)],
        compiler_params=_params(1),
        name="even_mixer",
    )(x_ctx, x_lat, x_lat, x_lat, mods, w_in, conv_w, pool_w, pool_scale, w_out, g, b, bands, inv_cnt, *cast_srcs)
    return outs[0], outs[1:]


FF_CHUNK = D_FF // 2


def _ffn_kernel(x_ref, mod_ref, wg_ref, wu_ref, wd_ref, g_ref, b_ref, cast_ref, o_ref, cast_out_ref):
    _cast_step([cast_ref], [cast_out_ref])
    shift, scale, gate = mod_ref[3:4, :], mod_ref[4:5, :], mod_ref[5:6, :]
    x = x_ref[...]
    h = (x * (1.0 + scale) + shift).astype(BF16)
    f = jnp.zeros((TM_FFN, D), F32)
    for c in range(D_FF // FF_CHUNK):
        sl = slice(c * FF_CHUNK, (c + 1) * FF_CHUNK)
        a = _silu(_dot(h, wg_ref[:, sl])) * _dot(h, wu_ref[:, sl])
        f = f + _dot(a.astype(BF16), wd_ref[sl, :])
    o_ref[...] = _layer_norm(ALPHA * x + gate * f, g_ref[...], b_ref[...])


def _ffn(x, mods, wg, wu, wd, g, b, cast_src):
    cast_specs, cast_shapes = _cast_stream([cast_src])
    return pl.pallas_call(
        _ffn_kernel,
        grid=(T // TM_FFN,),
        in_specs=[
            pl.BlockSpec((TM_FFN, D), lambda i: (i, 0)),
            pl.BlockSpec((None, 6, D), lambda i: (_cond_index(i, TM_FFN), 0, 0)),
            _const_spec((D, D_FF)),
            _const_spec((D, D_FF)),
            _const_spec((D_FF, D)),
            _const_spec((1, D)),
            _const_spec((1, D)),
        ] + cast_specs,
        out_specs=[pl.BlockSpec((TM_FFN, D), lambda i: (i, 0))] + cast_specs,
        out_shape=[jax.ShapeDtypeStruct((T, D), F32)] + cast_shapes,
        compiler_params=_params(1),
        name="dense_swiglu",
    )(x, mods, wg, wu, wd, g, b, cast_src)


W_IN_EXT = Q_RANK + KV_RANK + FNET_DIM + 128 + 128
QH = 256
ATT_SCALE = (QK_NOPE + QK_ROPE) ** -0.5 * float(np.log2(np.e))


def _odd_proj_kernel(x_ref, mod_ref, rope_ref, win_ref, qn_ref, kvn_ref, wqa_ref, wqb_ref,
                     avg_ref, dfth_ref, dftl_ref, cast_ref,
                     q_ref, ckv_ref, kpe_ref, y_ref, cast_out_ref, new_ckv_ref, new_kpe_ref):
    _cast_step([cast_ref], [cast_out_ref])
    shift, scale = mod_ref[0:1, :], mod_ref[1:2, :]
    h = (x_ref[...] * (1.0 + scale) + shift).astype(BF16)
    u = _dot(h, win_ref[...])
    uq = u[:, 0:Q_RANK]
    ukv = u[:, Q_RANK:Q_RANK + KV_RANK]
    uf = u[:, Q_RANK + KV_RANK:Q_RANK + KV_RANK + FNET_DIM]
    o = Q_RANK + KV_RANK + FNET_DIM
    upe, upe_rot = u[:, o:o + 128], u[:, o + 128:o + 256]
    cos, sin = rope_ref[:, 0:128], rope_ref[:, 128:256]

    ckv = ukv * lax.rsqrt(jnp.mean(ukv * ukv, axis=-1, keepdims=True) + RMS_EPS) * kvn_ref[...]
    kpe = upe * cos + upe_rot * sin
    ckv_ref[...] = ckv
    kpe_ref[...] = kpe

    @pl.when(pl.program_id(0) < T_P // TM2)
    def _():
        new_ckv_ref[...] = ckv
        new_kpe_ref[...] = kpe[:, 0:QK_ROPE]

    qlat = (uq * lax.rsqrt(jnp.mean(uq * uq, axis=-1, keepdims=True) + RMS_EPS) * qn_ref[...]).astype(BF16)
    qa = _dot(qlat, wqa_ref[...])
    qb = _dot(qlat, wqb_ref[...])
    for hd in range(N_HEADS):
        nope = qa[:, hd * QH:hd * QH + 128]
        pe = qa[:, hd * QH + 128:(hd + 1) * QH] * cos + qb[:, hd * 128:(hd + 1) * 128] * sin
        q_ref[:, hd * QH:hd * QH + 128] = (nope * ATT_SCALE).astype(BF16)
        q_ref[:, hd * QH + 128:(hd + 1) * QH] = (pe * ATT_SCALE).astype(BF16)

    avg = avg_ref[...]
    uf_hi, uf_lo = _split_bf16(uf)
    mu = _dot(uf_hi, avg) + _dot(uf_lo, avg)
    dlt = uf - mu
    sq_hi, sq_lo = _split_bf16(dlt * dlt)
    var = _dot(sq_hi, avg) + _dot(sq_lo, avg)
    xn = dlt * lax.rsqrt(var + LN_EPS)
    xn_hi, xn_lo = _split_bf16(xn)
    y = _dot(xn_hi, dfth_ref[...]) + _dot(xn_lo, dfth_ref[...]) + _dot(xn_hi, dftl_ref[...])
    y_ref[...] = y.astype(BF16)


def _odd_proj(x, mods, rope_tab, w_in_ext, q_norm, kv_norm, wqa, wqb, avg, dft_hi, dft_lo, cast_src):
    cast_specs, cast_shapes = _cast_stream([cast_src])

    def rope_index(i):
        return (jnp.where(i < T_P // TM2, 0, 1 + lax.rem(i - T_P // TM2, DEC_SEQ // TM2)), 0)

    return pl.pallas_call(
        _odd_proj_kernel,
        grid=(T // TM2,),
        in_specs=[
            pl.BlockSpec((TM2, D), lambda i: (i, 0)),
            pl.BlockSpec((None, 6, D), lambda i: (_cond_index(i, TM2), 0, 0)),
            pl.BlockSpec((TM2, 256), rope_index),
            _const_spec((D, W_IN_EXT)),
            _const_spec((1, Q_RANK)),
            _const_spec((1, KV_RANK)),
            _const_spec((Q_RANK, N_HEADS * QH)),
            _const_spec((Q_RANK, N_HEADS * 128)),
            _const_spec((FNET_DIM, FNET_DIM)),
            _const_spec((FNET_DIM, 2 * FNET_DIM)),
            _const_spec((FNET_DIM, 2 * FNET_DIM)),
        ] + cast_specs,
        out_specs=[
            pl.BlockSpec((TM2, N_HEADS * QH), lambda i: (i, 0)),
            pl.BlockSpec((TM2, KV_RANK), lambda i: (i, 0)),
            pl.BlockSpec((TM2, 128), lambda i: (i, 0)),
            pl.BlockSpec((TM2, 2 * FNET_DIM), lambda i: (i, 0)),
        ] + cast_specs + [
            pl.BlockSpec((TM2, KV_RANK), lambda i: (jnp.minimum(i, T_P // TM2 - 1), 0)),
            pl.BlockSpec((TM2, QK_ROPE), lambda i: (jnp.minimum(i, T_P // TM2 - 1), 0)),
        ],
        out_shape=[
            jax.ShapeDtypeStruct((T, N_HEADS * QH), BF16),
            jax.ShapeDtypeStruct((T, KV_RANK), F32),
            jax.ShapeDtypeStruct((T, 128), F32),
            jax.ShapeDtypeStruct((T, 2 * FNET_DIM), BF16),
        ] + cast_shapes + [
            jax.ShapeDtypeStruct((T_P, KV_RANK), F32),
            jax.ShapeDtypeStruct((T_P, QK_ROPE), F32),
        ],
        compiler_params=_params(1),
        name="odd_projections",
    )(x, mods, rope_tab, w_in_ext, q_norm, kv_norm, wqa, wqb, avg, dft_hi, dft_lo, cast_src)


V_OFF = N_HEADS * QH
KV_COLS = V_OFF + N_HEADS * V_DIM


def _attn_body(q_ref, kv_ref, o_ref):
    for hd in range(N_HEADS):
        qh = q_ref[:, hd * QH:(hd + 1) * QH]
        kh = kv_ref[:, hd * QH:(hd + 1) * QH]
        s = lax.dot_general(qh, kh, (((1,), (1,)), ((), ())), preferred_element_type=F32)
        p = jnp.exp2(s - jnp.max(s, axis=-1, keepdims=True))
        den = jnp.sum(p, axis=-1, keepdims=True)
        vh = kv_ref[:, V_OFF + hd * V_DIM:V_OFF + (hd + 1) * V_DIM]
        o = _dot(p.astype(BF16), vh)
        o_ref[:, hd * 128:(hd + 1) * 128] = (o / den).astype(BF16)


KV_CHUNK = 512


def _attn_kernel(q_ref, ckvp_ref, kpep_ref, ckvs_ref, kpes_ref, cckv_ref, ckpe_ref, wkv_ref, cast_ref, o_ref,
                 cast_out_ref, kvp_s, kvs_s):
    _cast_step([cast_ref], [cast_out_ref])
    i = pl.program_id(0)

    def expand(ckv, kpe, dst, row0):
        n = ckv.shape[0]
        kv = _dot(ckv.astype(BF16), wkv_ref[...]).astype(BF16)
        kpe = kpe.astype(BF16)
        for hd in range(N_HEADS):
            dst[row0:row0 + n, hd * QH:hd * QH + QK_NOPE] = kv[:, hd * QK_NOPE:(hd + 1) * QK_NOPE]
            dst[row0:row0 + n, hd * QH + QK_NOPE:(hd + 1) * QH] = kpe
        dst[row0:row0 + n, V_OFF:] = kv[:, N_HEADS * QK_NOPE:]

    @pl.when(i < NP_TILES)
    def _():
        expand(ckvp_ref[...], kpep_ref[...], kvp_s, 0)
        _attn_body(q_ref, kvp_s, o_ref)

    @pl.when(i >= NP_TILES)
    def _():
        @pl.when(lax.rem(i - NP_TILES, TPS) == 0)
        def _():
            expand(cckv_ref[...], ckpe_ref[...], kvs_s, 0)
            for c in range(DEC_SEQ // KV_CHUNK):
                rows = slice(c * KV_CHUNK, (c + 1) * KV_CHUNK)
                expand(ckvs_ref[rows, :], kpes_ref[rows, :], kvs_s, PAST + c * KV_CHUNK)

        _attn_body(q_ref, kvs_s, o_ref)


def _attention(q, ckv, kpe, cache_ckv, cache_kpe128, w_kv, cast_src):
    cast_specs, cast_shapes = _cast_stream([cast_src])
    ctx_blk = lambda i: (jnp.minimum(i, NP_TILES - 1), 0)
    lat_b = lambda i: jnp.maximum(i - NP_TILES, 0) // TPS
    lat_blk = lambda i: (T_P // DEC_SEQ + lat_b(i), 0)
    return pl.pallas_call(
        _attn_kernel,
        grid=(N_TILES,),
        in_specs=[
            pl.BlockSpec((TM, N_HEADS * QH), lambda i: (i, 0)),
            pl.BlockSpec((SEQ, KV_RANK), ctx_blk),
            pl.BlockSpec((SEQ, 128), ctx_blk),
            pl.BlockSpec((DEC_SEQ, KV_RANK), lat_blk),
            pl.BlockSpec((DEC_SEQ, 128), lat_blk),
            pl.BlockSpec((None, PAST, KV_RANK), lambda i: (lat_b(i), 0, 0)),
            pl.BlockSpec((None, PAST, 128), lambda i: (lat_b(i), 0, 0)),
            _const_spec((KV_RANK, 2 * N_HEADS * 128)),
        ] + cast_specs,
        out_specs=[pl.BlockSpec((TM, N_HEADS * V_DIM), lambda i: (i, 0))] + cast_specs,
        out_shape=[jax.ShapeDtypeStruct((T, N_HEADS * V_DIM), BF16)] + cast_shapes,
        scratch_shapes=[pltpu.VMEM((SEQ, KV_COLS), BF16), pltpu.VMEM((LK_S, KV_COLS), BF16)],
        compiler_params=_params(1),
        name="attention",
    )(q, ckv, kpe, ckv, kpe, cache_ckv, cache_kpe128, w_kv, cast_src)


def _pos_dft_body(y_ref, c, s, o_ref):
    f = _dot(c, y_ref[:, 0:FNET_DIM]) + _dot(s, y_ref[:, FNET_DIM:])
    o_ref[...] = f.astype(BF16)


def _pos_dft_kernel(yp_ref, cp_ref, sp_ref, ys_ref, c0_ref, s0_ref, cb_ref, sb_ref, o_ref):
    @pl.when(pl.program_id(0) < NP_TILES)
    def _():
        _pos_dft_body(yp_ref, cp_ref[...].astype(BF16), sp_ref[...].astype(BF16), o_ref)

    @pl.when(pl.program_id(0) >= NP_TILES)
    def _():
        j = lax.rem(pl.program_id(0) - NP_TILES, TPS)
        cb, sb = cb_ref[pl.ds(j, 1), :], sb_ref[pl.ds(j, 1), :]
        c0, s0 = c0_ref[...], s0_ref[...]
        _pos_dft_body(ys_ref, (c0 * cb - s0 * sb).astype(BF16), (s0 * cb + c0 * sb).astype(BF16), o_ref)


def _pos_dft(y, tabs_p, base_s, step_s):
    ctx_blk = lambda i: (jnp.minimum(i, NP_TILES - 1), 0)
    lat_seq = lambda i: (T_P // DEC_SEQ + jnp.maximum(i - NP_TILES, 0) // TPS, 0)
    return pl.pallas_call(
        _pos_dft_kernel,
        grid=(N_TILES,),
        in_specs=[pl.BlockSpec((SEQ, 2 * FNET_DIM), ctx_blk)]
        + [_const_spec((SEQ, SEQ))] * 2
        + [pl.BlockSpec((DEC_SEQ, 2 * FNET_DIM), lat_seq)]
        + [_const_spec((TM, DEC_SEQ))] * 2
        + [_const_spec((TPS, DEC_SEQ))] * 2,
        out_specs=pl.BlockSpec((TM, FNET_DIM), lambda i: (i, 0)),
        out_shape=jax.ShapeDtypeStruct((T, FNET_DIM), BF16),
        compiler_params=_params(1),
        name="pos_dft",
    )(y, *tabs_p, y, *base_s, *step_s)


def _odd_merge_kernel(x_ref, attn_ref, f_ref, mod_ref, fw_ref, wo_ref, g_ref, b_ref, rh_ref, rl_ref, tri_ref,
                      cast_ref, xo_ref, info_ref, infot_ref, cnt_ref, cast_out_ref, carry_ref):
    _cast_step([cast_ref], [cast_out_ref])

    @pl.when(pl.program_id(0) == 0)
    def _():
        carry_ref[...] = jnp.zeros_like(carry_ref)

    gate = mod_ref[2:3, :]
    shift2, scale2 = mod_ref[3:4, :], mod_ref[4:5, :]
    fm = _dot(f_ref[...], fw_ref[...]).astype(BF16)
    y = _dot(attn_ref[...], wo_ref[0:N_HEADS * V_DIM, :]) + _dot(fm, wo_ref[N_HEADS * V_DIM:, :])
    x = _layer_norm(ALPHA * x_ref[...] + gate * y, g_ref[...], b_ref[...])
    xo_ref[...] = x
    h = x * (1.0 + scale2) + shift2

    h_hi, h_lo = _split_bf16(h)
    logits = _dot(h_hi, rh_ref[...]) + _dot(h_lo, rh_ref[...]) + _dot(h_hi, rl_ref[...])
    lane = lax.broadcasted_iota(jnp.int32, (TM2, 128), 1)
    neg = jnp.float32(-jnp.inf)
    logits = jnp.where(lane < N_EXPERTS, logits, neg)
    m1 = jnp.max(logits, axis=-1, keepdims=True)
    i1 = jnp.min(jnp.where(logits == m1, lane, 128), axis=-1, keepdims=True)
    rest = jnp.where(lane == i1, neg, logits)
    m2 = jnp.max(rest, axis=-1, keepdims=True)
    i2 = jnp.min(jnp.where(rest == m2, lane, 128), axis=-1, keepdims=True)
    e2 = jnp.exp(m2 - m1)
    w1 = 1.0 / (1.0 + e2)
    w2 = e2 / (1.0 + e2)
    info = jnp.where(lane == 0, w1, 0.0)
    info = jnp.where(lane == 1, w2, info)
    info = jnp.where(lane == 2, i1.astype(F32), info)
    info = jnp.where(lane == 3, i2.astype(F32), info)

    uses = jnp.logical_or(lane == i1, lane == i2)
    seen = _dot(tri_ref[...], jnp.where(uses, 1.0, 0.0).astype(BF16)) + carry_ref[...]
    r1 = jnp.sum(jnp.where(lane == i1, seen, 0.0), axis=-1, keepdims=True)
    r2 = jnp.sum(jnp.where(lane == i2, seen, 0.0), axis=-1, keepdims=True)
    info = jnp.where(lane == 4, r1, info)
    info = jnp.where(lane == 5, r2, info)
    info_ref[...] = info
    infot_ref[...] = info.T
    total = carry_ref[...] + jnp.sum(jnp.where(uses, 1.0, 0.0), axis=0, keepdims=True)
    carry_ref[...] = total
    cnt_ref[...] = jnp.broadcast_to(total, cnt_ref.shape)


def _odd_merge(x, attn, f, mods, fnet_w, w_out, g, b, r_hi, r_lo, cast_src):
    row = lambda i: (i, 0)
    tri = jnp.asarray(np.tril(np.ones((TM2, TM2), np.float32), -1), BF16)
    cast_specs, cast_shapes = _cast_stream([cast_src])
    return pl.pallas_call(
        _odd_merge_kernel,
        grid=(T // TM2,),
        in_specs=[
            pl.BlockSpec((TM2, D), row),
            pl.BlockSpec((TM2, N_HEADS * V_DIM), row),
            pl.BlockSpec((TM2, FNET_DIM), row),
            pl.BlockSpec((None, 6, D), lambda i: (_cond_index(i, TM2), 0, 0)),
            _const_spec((FNET_DIM, FNET_DIM)),
            _const_spec((N_HEADS * V_DIM + FNET_DIM, D)),
            _const_spec((1, D)),
            _const_spec((1, D)),
            _const_spec((D, 128)),
            _const_spec((D, 128)),
            _const_spec((TM2, TM2)),
        ] + cast_specs,
        out_specs=[pl.BlockSpec((TM2, D), row), pl.BlockSpec((TM2, 128), row),
                   pl.BlockSpec((128, TM2), lambda i: (0, i)), pl.BlockSpec((8, 128), lambda i: (i, 0))] + cast_specs,
        out_shape=[jax.ShapeDtypeStruct((T, D), F32), jax.ShapeDtypeStruct((T, 128), F32),
                   jax.ShapeDtypeStruct((128, T), F32), jax.ShapeDtypeStruct((T // TM2 * 8, 128), F32)] + cast_shapes,
        scratch_shapes=[pltpu.VMEM((1, 128), F32)],
        compiler_params=_params(1),
        name="odd_merge_router",
    )(x, attn, f, mods, fnet_w, w_out, g, b, r_hi, r_lo, tri, cast_src)


TM_D = 512
ROW = (8, 128)
ROW_DT = BF16
DMA_UNROLL = 8


def _row_copy(src, s, dst, d, sem):
    return pltpu.make_async_copy(src.at[s], dst.at[d], sem)


CHUNK = 16
LOCAL_ROWS = 2 * TM2 + N_EXPERTS * CHUNK


def _chunk_copies(tile, nch_ref, fn):
    for e in range(N_EXPERTS):
        def body(c, carry, e=e):
            fn(tile * N_EXPERTS + e, c, e % 2)
            return carry

        lax.fori_loop(0, nch_ref[tile * N_EXPERTS + e], body, 0)


def _dispatch_kernel(dest_ref, pad_lo_ref, pad_hi_ref, x_ref, mod_ref, xs_ref, h_ref, zero_ref, sem, zsem):
    i = pl.program_id(0)
    base = i * TM_D
    slot = lax.rem(i, 2)

    @pl.when(i == 0)
    def _():
        zero_ref[...] = jnp.zeros_like(zero_ref)
        for e in range(N_EXPERTS + 1):
            def zissue(r, carry):
                _row_copy(zero_ref, 0, xs_ref, r, zsem).start()
                return carry

            def zdrain(r, carry):
                _row_copy(zero_ref, 0, xs_ref, 0, zsem).wait()
                return carry

            lax.fori_loop(pad_lo_ref[e], pad_hi_ref[e], zissue, 0)
            lax.fori_loop(pad_lo_ref[e], pad_hi_ref[e], zdrain, 0)

    shift2, scale2 = mod_ref[3:4, :], mod_ref[4:5, :]
    h_ref[slot] = (x_ref[...] * (1.0 + scale2) + shift2).astype(ROW_DT).reshape((TM_D,) + ROW)

    def issue(r, carry):
        t = base + r
        _row_copy(h_ref.at[slot], r, xs_ref, dest_ref[t], sem.at[slot]).start(priority=0)
        _row_copy(h_ref.at[slot], r, xs_ref, dest_ref[T + t], sem.at[slot]).start(priority=1)
        return carry

    lax.fori_loop(0, TM_D, issue, 0, unroll=DMA_UNROLL)

    def drain(s):
        def body(r, carry):
            _row_copy(h_ref.at[s], 0, xs_ref, 0, sem.at[s]).wait()
            _row_copy(h_ref.at[s], 0, xs_ref, 0, sem.at[s]).wait()
            return carry

        lax.fori_loop(0, TM_D, body, 0, unroll=DMA_UNROLL)

    @pl.when(i > 0)
    def _():
        drain(1 - slot)

    @pl.when(i == T // TM_D - 1)
    def _():
        drain(slot)


def _dispatch(dest, pad_lo, pad_hi, x, mods):
    return pl.pallas_call(
        _dispatch_kernel,
        grid_spec=pltpu.PrefetchScalarGridSpec(
            num_scalar_prefetch=3,
            grid=(T // TM_D,),
            in_specs=[pl.BlockSpec((TM_D, D), lambda i, *_: (i, 0)),
                      pl.BlockSpec((None, 6, D), lambda i, *_: (_cond_index(i, TM_D), 0, 0))],
            out_specs=pl.BlockSpec(memory_space=pl.ANY),
            scratch_shapes=[pltpu.VMEM((2, TM_D) + ROW, ROW_DT), pltpu.VMEM((1,) + ROW, ROW_DT),
                            pltpu.SemaphoreType.DMA((2,)), pltpu.SemaphoreType.DMA(())],
        ),
        out_shape=jax.ShapeDtypeStruct((R_MAX,) + ROW, ROW_DT),
        compiler_params=pltpu.CompilerParams(dimension_semantics=("arbitrary",), has_side_effects=True),
        name="expert_dispatch",
    )(dest, pad_lo, pad_hi, x, mods)


def _expert_kernel(te_ref, nt_ref, xs_ref, wg_ref, wu_ref, wd_ref, o_ref):
    @pl.when(pl.program_id(0) < nt_ref[0])
    def _():
        h = xs_ref[...].reshape(TM_E, D)
        a = _silu(_dot(h, wg_ref[...])) * _dot(h, wu_ref[...])
        o_ref[...] = _dot(a.astype(BF16), wd_ref[...]).astype(ROW_DT).reshape((TM_E,) + ROW)

    @pl.when(pl.program_id(0) >= nt_ref[0])
    def _():
        o_ref[...] = jnp.zeros_like(o_ref)


def _experts(tile_expert, n_used, xs, wg, wu, wd):
    return pl.pallas_call(
        _expert_kernel,
        grid_spec=pltpu.PrefetchScalarGridSpec(
            num_scalar_prefetch=2,
            grid=(N_ETILES,),
            in_specs=[
                pl.BlockSpec((TM_E,) + ROW, lambda i, te, nt: (jnp.minimum(i, nt[0] - 1), 0, 0)),
                pl.BlockSpec((None, D, D_FF_EXPERT), lambda i, te, nt: (te[i], 0, 0)),
                pl.BlockSpec((None, D, D_FF_EXPERT), lambda i, te, nt: (te[i], 0, 0)),
                pl.BlockSpec((None, D_FF_EXPERT, D), lambda i, te, nt: (te[i], 0, 0)),
            ],
            out_specs=pl.BlockSpec((TM_E,) + ROW, lambda i, te, nt: (i, 0, 0)),
        ),
        out_shape=jax.ShapeDtypeStruct((R_MAX,) + ROW, ROW_DT),
        compiler_params=_params(1),
        name="expert_swiglu",
    )(tile_expert, n_used, xs, wg, wu, wd)


def _combine_kernel(lp_ref, d0_ref, off_ref, nch_ref, x_ref, info_ref, mod_ref, g_ref, b_ref, ys_ref,
                    op_ref, os_ref, local_ref, rows_ref, sem):
    i = pl.program_id(0)
    slot = lax.rem(i, 2)

    def chunk(s, seg, c):
        return pltpu.make_async_copy(ys_ref.at[pl.ds(d0_ref[seg] + c * CHUNK, CHUNK)],
                                     local_ref.at[s, pl.ds(off_ref[seg] + c * CHUNK, CHUNK)], sem.at[s])

    @pl.when(i == 0)
    def _():
        _chunk_copies(0, nch_ref, lambda seg, c, prio: chunk(0, seg, c).start(priority=prio))

    @pl.when(i + 1 < T // TM2)
    def _():
        _chunk_copies(i + 1, nch_ref, lambda seg, c, prio: chunk(1 - slot, seg, c).start(priority=prio))

    _chunk_copies(i, nch_ref, lambda seg, c, prio: chunk(slot, seg, c).wait())

    def pick(r, carry):
        rows_ref[0, r] = local_ref[slot, lp_ref[i * TM2 + r]]
        rows_ref[1, r] = local_ref[slot, lp_ref[T + i * TM2 + r]]
        return carry

    lax.fori_loop(0, TM2, pick, 0, unroll=DMA_UNROLL)

    gate = mod_ref[5:6, :]
    w1, w2 = info_ref[:, 0:1], info_ref[:, 1:2]
    y = w1 * rows_ref[0].reshape(TM2, D).astype(F32) + w2 * rows_ref[1].reshape(TM2, D).astype(F32)
    out = _layer_norm(ALPHA * x_ref[...] + gate * y, g_ref[...], b_ref[...])

    @pl.when(pl.program_id(0) < T_P // TM2)
    def _():
        op_ref[...] = out

    @pl.when(pl.program_id(0) >= T_P // TM2)
    def _():
        os_ref[...] = out


def _combine(lp, d0, off, nch, x, info, mods, g, b, ys):
    return pl.pallas_call(
        _combine_kernel,
        grid_spec=pltpu.PrefetchScalarGridSpec(
            num_scalar_prefetch=4,
            grid=(T // TM2,),
            in_specs=[
                pl.BlockSpec((TM2, D), lambda i, *_: (i, 0)),
                pl.BlockSpec((TM2, 128), lambda i, *_: (i, 0)),
                pl.BlockSpec((None, 6, D), lambda i, *_: (_cond_index(i, TM2), 0, 0)),
                pl.BlockSpec((1, D), lambda i, *_: (0, 0)),
                pl.BlockSpec((1, D), lambda i, *_: (0, 0)),
                pl.BlockSpec(memory_space=pl.ANY),
            ],
            out_specs=[pl.BlockSpec((TM2, D), lambda i, *_: (jnp.minimum(i, T_P // TM2 - 1), 0)),
                       pl.BlockSpec((TM2, D), lambda i, *_: (jnp.maximum(i - T_P // TM2, 0), 0))],
            scratch_shapes=[pltpu.VMEM((2, LOCAL_ROWS) + ROW, ROW_DT), pltpu.VMEM((2, TM2) + ROW, ROW_DT),
                            pltpu.SemaphoreType.DMA((2,))],
        ),
        out_shape=[jax.ShapeDtypeStruct((T_P, D), F32), jax.ShapeDtypeStruct((T_S, D), F32)],
        compiler_params=_params(1),
        name="expert_combine",
    )(lp, d0, off, nch, x, info, mods, g, b, ys)


def _rot_cols(w):
    w4 = w.reshape(w.shape[:-1] + (2, 2, QK_ROPE // 4))
    return jnp.stack([-w4[..., 1, :], w4[..., 0, :]], axis=-2).reshape(w.shape)


def _rope_table():
    rows = DEC_SEQ // GRID_W
    row = np.repeat(np.arange(rows), GRID_W).astype(np.float32)
    col = np.tile(np.arange(GRID_W), rows).astype(np.float32)
    half = QK_ROPE // 2
    inv = (ROPE_THETA ** (-np.arange(0, half, 2, dtype=np.float32) / half)).astype(np.float32)
    ar, ac = row[:, None] * inv, col[:, None] * inv
    ang = np.concatenate([ar, ar, ac, ac], axis=-1)
    cos = np.concatenate([np.ones((TM2, QK_ROPE)), np.cos(ang)], axis=0)
    sin = np.concatenate([np.zeros((TM2, QK_ROPE)), np.sin(ang)], axis=0)
    n = cos.shape[0]
    return jnp.asarray(np.concatenate([cos, np.ones((n, 64)), sin, np.zeros((n, 64))], axis=1), F32)


def _dft_angles(rows, n):
    k = np.arange(n, dtype=np.int64)
    return ((np.asarray(rows, np.int64)[:, None] * k[None, :]) % n) * (2.0 * np.pi / n)


def _dft_tables(n):
    ang = _dft_angles(np.arange(n), n)
    return np.cos(ang) * n ** -0.5, np.sin(ang) * n ** -0.5


def _hi_lo(m):
    m = jnp.asarray(m, F32)
    hi = m.astype(BF16)
    return hi, (m - hi.astype(F32)).astype(BF16)


def _block_diag4(m):
    return np.kron(np.eye(4), m)


def kernel(x_prompt, x_sample, cache_ckv, cache_kpe, c, c_ctx, ada_w, ada_b, ln_g, ln_b, ev_w_in, ev_conv_w, ev_pool_w, ev_pool_scale, ev_w_out, ffn_w_gate, ffn_w_up, ffn_w_down, od_w_in, od_q_norm, od_kv_norm, od_w_q_b, od_w_kv_b, od_fnet_w, od_w_out, moe_router, moe_w_gate, moe_w_up, moe_w_down):
    cond8 = jnp.concatenate([c_ctx[None, :], c, jnp.zeros((8 - N_COND, D), F32)], axis=0)
    mods = _modulation(cond8, ada_w, ada_b)[:, :N_COND].reshape(DEPTH, N_COND, 6, D)

    x, (ffn_wg, ffn_wu, ffn_wd) = _even_mixer(
        x_prompt.reshape(T_P, D), x_sample.reshape(T_S, D), mods[0], ev_w_in[0].astype(BF16), ev_conv_w[0],
        ev_pool_w[0].astype(BF16), ev_pool_scale[0][None, :], ev_w_out[0].astype(BF16),
        ln_g[0, 0][None, :], ln_b[0, 0][None, :], [ffn_w_gate[0], ffn_w_up[0], ffn_w_down[0]])
    x, moe_wd = _ffn(x, mods[0], ffn_wg, ffn_wu, ffn_wd, ln_g[0, 1][None, :], ln_b[0, 1][None, :],
                     moe_w_down[0].reshape(N_EXPERTS * D_FF_EXPERT, D))

    w_in = od_w_in[0]
    w_pe = w_in[:, Q_RANK + KV_RANK:Q_RANK + KV_RANK + QK_ROPE]
    zpad = jnp.zeros((D, 64), F32)
    w_in_ext = jnp.concatenate([w_in[:, :Q_RANK + KV_RANK], w_in[:, Q_RANK + KV_RANK + QK_ROPE:],
                                w_pe, zpad, _rot_cols(w_pe), zpad], axis=1).astype(BF16)
    wq = od_w_q_b[0].reshape(Q_RANK, N_HEADS, QK_NOPE + QK_ROPE)
    zq = jnp.zeros((Q_RANK, N_HEADS, 64), F32)
    wqa = jnp.concatenate([wq, zq], axis=-1).reshape(Q_RANK, N_HEADS * QH).astype(BF16)
    wqb = jnp.concatenate([_rot_cols(wq[..., QK_NOPE:]), zq], axis=-1).reshape(Q_RANK, N_HEADS * 128).astype(BF16)
    wkv = od_w_kv_b[0].reshape(KV_RANK, N_HEADS, QK_NOPE + V_DIM)
    w_kv = jnp.concatenate([wkv[..., :QK_NOPE].reshape(KV_RANK, -1), wkv[..., QK_NOPE:].reshape(KV_RANK, -1)],
                           axis=1).astype(BF16)

    avg = jnp.asarray(_block_diag4(np.full((FNET_GROUP_DIM, FNET_GROUP_DIM), 1.0 / FNET_GROUP_DIM)), BF16)
    cc, sc = _dft_tables(FNET_GROUP_DIM)
    dft_hi, dft_lo = _hi_lo(np.concatenate([_block_diag4(cc), -_block_diag4(sc)], axis=1))

    q, ckv, kpe, y_dft, moe_wg, ckv_ctx, kpe_ctx = _odd_proj(x, mods[1], _rope_table(), w_in_ext, od_q_norm[0][None, :],
                                            od_kv_norm[0][None, :], wqa, wqb, avg, dft_hi, dft_lo,
                                            moe_w_gate[0].reshape(N_EXPERTS * D, D_FF_EXPERT))

    cache_kpe128 = jnp.pad(cache_kpe[:, 0], ((0, 0), (0, 0), (0, 128 - QK_ROPE)))
    attn, w_out_bf = _attention(q, ckv, kpe, cache_ckv[:, 0], cache_kpe128, w_kv, od_w_out[0])

    tabs_p = tuple(jnp.asarray(m, F32) for m in _dft_tables(SEQ))
    a_base = _dft_angles(np.arange(TM), DEC_SEQ)
    a_step = _dft_angles(np.arange(TPS) * TM, DEC_SEQ)
    base_s = (jnp.asarray(np.cos(a_base) * DEC_SEQ ** -0.5, F32), jnp.asarray(np.sin(a_base) * DEC_SEQ ** -0.5, F32))
    step_s = (jnp.asarray(np.cos(a_step), F32), jnp.asarray(np.sin(a_step), F32))
    f = _pos_dft(y_dft, tabs_p, base_s, step_s)

    router = jnp.pad(moe_router[0], ((0, 0), (0, 128 - N_EXPERTS)))
    r_hi, r_lo = _hi_lo(router)
    x, info, info_t, cnt, moe_wu = _odd_merge(x, attn, f, mods[1], od_fnet_w[0].astype(BF16),
                                              w_out_bf, ln_g[1, 0][None, :], ln_b[1, 0][None, :],
                                              r_hi, r_lo, moe_w_up[0].reshape(N_EXPERTS * D, D_FF_EXPERT))

    n_tiles = T // TM2
    after = cnt.reshape(n_tiles, 8, 128)[:, 0, :N_EXPERTS].astype(jnp.int32)
    before = jnp.concatenate([jnp.zeros((1, N_EXPERTS), jnp.int32), after[:-1]], axis=0)
    counts = after[-1]
    padded = ((counts + TM_E - 1) // TM_E) * TM_E
    g_end = jnp.cumsum(padded)
    g_start = g_end - padded
    n_chunks = (after - before + CHUNK - 1) // CHUNK
    seg_off = CHUNK * (jnp.cumsum(n_chunks, axis=1) - n_chunks)
    seg_dst = g_start[None, :] + before
    choice = info_t[2:4].astype(jnp.int32)
    rank = info_t[4:6].astype(jnp.int32)
    shift_t = jnp.repeat((seg_off - before).T, TM2, axis=1)
    lp, dest = rank, rank
    for e in range(N_EXPERTS):
        lp = lp + jnp.where(choice == e, shift_t[e][None, :], 0)
        dest = dest + jnp.where(choice == e, g_start[e], 0)
    lp = lp.reshape(-1).astype(jnp.int32)
    dest = dest.reshape(-1).astype(jnp.int32)
    d0, off, nch = (a.reshape(-1).astype(jnp.int32) for a in (seg_dst, seg_off, n_chunks))
    tile_row = jnp.arange(N_ETILES, dtype=jnp.int32) * TM_E
    tile_expert = jnp.minimum(jnp.sum((tile_row[:, None] >= g_end[None, :]).astype(jnp.int32), axis=1),
                              N_EXPERTS - 1).astype(jnp.int32)
    n_used = (g_end[-1:] // TM_E).astype(jnp.int32)

    pad_lo = jnp.concatenate([g_start + counts, g_end[-1:]]).astype(jnp.int32)
    pad_hi = jnp.concatenate([g_end, jnp.full((1,), R_MAX, jnp.int32)]).astype(jnp.int32)
    xs = _dispatch(dest, pad_lo, pad_hi, x, mods[1])
    ys = _experts(tile_expert, n_used, xs, moe_wg.reshape(N_EXPERTS, D, D_FF_EXPERT),
                  moe_wu.reshape(N_EXPERTS, D, D_FF_EXPERT), moe_wd.reshape(N_EXPERTS, D_FF_EXPERT, D))
    yp, ysm = _combine(lp, d0, off, nch, x, info, mods[1], ln_g[1, 1][None, :], ln_b[1, 1][None, :], ys)

    y_prompt = yp.reshape(BATCH, SEQ, D)
    y_sample = ysm.reshape(DEC_BATCH, DEC_SEQ, D)
    new_ckv = ckv_ctx.reshape(BATCH, 1, SEQ, KV_RANK)
    new_kpe = kpe_ctx.reshape(BATCH, 1, SEQ, QK_ROPE)
    return (y_prompt, y_sample, new_ckv, new_kpe)
```

```python
import functools

import numpy as np
import jax
import jax.numpy as jnp
from jax import lax
from jax.experimental import pallas as pl
from jax.experimental.pallas import tpu as pltpu

F32 = jnp.float32
BF16 = jnp.bfloat16

D = 1024
BATCH, SEQ = 32, 256
DEC_BATCH, DEC_SEQ = 2, 2048
PAST = 512
GRID_W = 64
T_P = BATCH * SEQ
T_S = DEC_BATCH * DEC_SEQ
T = T_P + T_S
N_COND = 1 + DEC_BATCH

CONV_DIM = 512
POOL_WINDOWS = (2, 4, 8, 16)
POOL_GROUP = 128
N_HEADS = 8
QK_NOPE, QK_ROPE, V_DIM = 128, 64, 128
Q_RANK, KV_RANK = 384, 256
FNET_DIM, FNET_GROUP_DIM = 256, 64
D_FF = 2816
N_EXPERTS = 8
D_FF_EXPERT = 1792
DEPTH = 2
ALPHA = (2 * DEPTH) ** 0.25
LN_EPS = 1e-5
RMS_EPS = 1e-6
ROPE_THETA = 10000.0

TM = 256
NP_TILES = T_P // TM
TPS = DEC_SEQ // TM
N_TILES = T // TM
HALO = 8
TM2 = 512
TM_FFN = 512
LK_S = PAST + DEC_SEQ
TM_E = 256
N_ETILES = (2 * T) // TM_E + N_EXPERTS + 1
R_MAX = N_ETILES * TM_E
VMEM_LIMIT = 56 * 1024 * 1024


def _cond_index(i, tm=TM):
    return jnp.where(i < T_P // tm, 0, 1 + (i - T_P // tm) // (DEC_SEQ // tm))


def _const_spec(shape):
    nd = len(shape)
    return pl.BlockSpec(shape, lambda *_: (0,) * nd, pipeline_mode=pl.Buffered(1))


def _params(n_axes=1, vmem=VMEM_LIMIT):
    return pltpu.CompilerParams(dimension_semantics=("arbitrary",) * n_axes, vmem_limit_bytes=vmem)


def _layer_norm(v, g, b):
    mu = jnp.mean(v, axis=-1, keepdims=True)
    d = v - mu
    var = jnp.mean(d * d, axis=-1, keepdims=True)
    return d * lax.rsqrt(var + LN_EPS) * g + b


def _split_bf16(v):
    hi = v.astype(BF16)
    lo = (v - hi.astype(F32)).astype(BF16)
    return hi, lo


def _dot(a, b):
    return jnp.dot(a, b, preferred_element_type=F32)


def _silu(v):
    return v / (1.0 + jnp.exp(-v))


CAST_BLOCKS = 16


def _cast_stream(srcs):
    specs = [pl.BlockSpec((w.shape[0] // CAST_BLOCKS, w.shape[1]),
                          lambda i, *_: (jnp.minimum(i, CAST_BLOCKS - 1), 0)) for w in srcs]
    return specs, [jax.ShapeDtypeStruct(w.shape, BF16) for w in srcs]


def _cast_step(srcs, dsts):
    @pl.when(pl.program_id(0) < CAST_BLOCKS)
    def _():
        for src, dst in zip(srcs, dsts):
            dst[...] = src[...].astype(BF16)


def _mod_kernel(cond_ref, w_ref, b_ref, o_ref):
    s = _silu(cond_ref[...]).astype(BF16)
    o_ref[...] = _dot(s, w_ref[...].astype(BF16)) + b_ref[...]


def _modulation(cond8, ada_w, ada_b):
    nb = 6 * D // 1024
    return pl.pallas_call(
        _mod_kernel,
        grid=(DEPTH, nb),
        in_specs=[
            pl.BlockSpec((8, D), lambda l, j: (0, 0)),
            pl.BlockSpec((None, D, 1024), lambda l, j: (l, 0, j)),
            pl.BlockSpec((None, 1, 1024), lambda l, j: (l, 0, j)),
        ],
        out_specs=pl.BlockSpec((None, 8, 1024), lambda l, j: (l, 0, j)),
        out_shape=jax.ShapeDtypeStruct((DEPTH, 8, 6 * D), F32),
        compiler_params=_params(2),
        name="adaln_modulation",
    )(cond8, ada_w, ada_b.reshape(DEPTH, 1, 6 * D))


def _pool_tables():
    t = np.arange(TM)[:, None]
    r = np.arange(TM + 2 * HALO)[None, :]
    pos = r - HALO
    bands = np.zeros((4, len(POOL_WINDOWS), TM, TM + 2 * HALO), np.float32)
    inv = np.zeros((4, TM, 128), np.float32)
    for variant in range(4):
        left_ok, right_ok = variant & 1, variant >> 1
        col_ok = (r >= (0 if left_ok else HALO)) & (r < (TM + 2 * HALO if right_ok else TM + HALO))
        first, last = (-HALO if left_ok else 0), (TM + HALO if right_ok else TM)
        for gi, w in enumerate(POOL_WINDOWS):
            bands[variant, gi] = (pos >= t - w // 2) & (pos < t + w // 2) & col_ok
            cnt = np.minimum(t[:, 0] + w // 2, last) - np.maximum(t[:, 0] - w // 2, first)
            inv[variant, :, gi] = 1.0 / cnt
    return jnp.asarray(bands, BF16), jnp.asarray(inv, F32)


def _even_tiles(x_alls, left_oks, right_oks, mod_ref, win_ref, convw_ref, poolw_ref, pscale_ref, wout_ref,
                g_ref, b_ref, band_ref, inv_ref):
    shift, scale, gate = mod_ref[0:1, :], mod_ref[1:2, :], mod_ref[2:3, :]
    n = len(x_alls)
    hs = []
    for x_all, left_ok, right_ok in zip(x_alls, left_oks, right_oks):
        h = x_all * (1.0 + scale) + shift
        hs.append(jnp.concatenate([jnp.where(left_ok, h[:HALO], 0.0), h[HALO:HALO + TM],
                                   jnp.where(right_ok, h[HALO + TM:], 0.0)], axis=0).astype(BF16))
    us = [_dot(h, win_ref[...]) for h in hs]

    mixes = []
    for u, left_ok, right_ok in zip(us, left_oks, right_oks):
        ux, ub = u[:, 0:CONV_DIM], u[:, CONV_DIM:2 * CONV_DIM]
        uc, up = u[:, 2 * CONV_DIM:3 * CONV_DIM], u[:, 3 * CONV_DIM:]
        z = uc * ux
        conv = (z[HALO - 1:HALO - 1 + TM] * convw_ref[0:1, :]
                + z[HALO:HALO + TM] * convw_ref[1:2, :]
                + z[HALO + 1:HALO + 1 + TM] * convw_ref[2:3, :])
        ya = ub[HALO:HALO + TM] * conv

        variant = left_ok.astype(jnp.int32) + 2 * right_ok.astype(jnp.int32)
        inv_cnt = inv_ref[variant]
        up_hi, up_lo = _split_bf16(up)
        yb_groups = []
        for gi in range(len(POOL_WINDOWS)):
            band = band_ref[variant, gi]
            sl = slice(gi * POOL_GROUP, (gi + 1) * POOL_GROUP)
            tot = _dot(band, up_hi[:, sl]) + _dot(band, up_lo[:, sl])
            p = tot * inv_cnt[:, gi:gi + 1] - up[HALO:HALO + TM, sl]
            yb_groups.append(_dot(p.astype(BF16), poolw_ref[gi]))
        yb = jnp.concatenate(yb_groups, axis=1) * pscale_ref[...]
        mixes.append(jnp.concatenate([ya, yb], axis=1).astype(BF16))

    ys = [_dot(mix, wout_ref[...]) for mix in mixes]
    return [_layer_norm(ALPHA * x_alls[k][HALO:HALO + TM] + gate * ys[k], g_ref[...], b_ref[...])
            for k in range(n)]


EV_TILES = 2
EV_ROWS = EV_TILES * TM


N_CAST_EVEN = 3


def _even_mixer_kernel(xctx_ref, xprev_ref, xlat_ref, xnext_ref, mod_ref, win_ref, convw_ref, poolw_ref,
                       pscale_ref, wout_ref, g_ref, b_ref, band_ref, inv_ref, *rest):
    n = N_CAST_EVEN
    cast_in, o_ref, cast_out, xall_ref = rest[:n], rest[n], rest[n + 1:2 * n + 1], rest[-1]
    _cast_step(cast_in, cast_out)
    s = pl.program_id(0)
    n_ctx = T_P // EV_ROWS
    is_latent = s >= n_ctx
    first_tile = lax.rem(s - n_ctx, DEC_SEQ // EV_ROWS) * EV_TILES

    @pl.when(jnp.logical_not(is_latent))
    def _():
        for k in range(EV_TILES):
            xall_ref[k, 0:HALO, :] = jnp.zeros((HALO, D), F32)
            xall_ref[k, HALO:HALO + TM, :] = xctx_ref[k * TM:(k + 1) * TM, :]
            xall_ref[k, HALO + TM:, :] = jnp.zeros((HALO, D), F32)

    @pl.when(is_latent)
    def _():
        for k in range(EV_TILES):
            lo, hi = k * TM, (k + 1) * TM
            xall_ref[k, 0:HALO, :] = xprev_ref[...] if k == 0 else xlat_ref[lo - HALO:lo, :]
            xall_ref[k, HALO:HALO + TM, :] = xlat_ref[lo:hi, :]
            xall_ref[k, HALO + TM:, :] = xnext_ref[...] if k == EV_TILES - 1 else xlat_ref[hi:hi + HALO, :]

    left_oks = [jnp.logical_and(is_latent, first_tile + k != 0) for k in range(EV_TILES)]
    right_oks = [jnp.logical_and(is_latent, first_tile + k != TPS - 1) for k in range(EV_TILES)]
    outs = _even_tiles([xall_ref[k] for k in range(EV_TILES)], left_oks, right_oks, mod_ref, win_ref,
                       convw_ref, poolw_ref, pscale_ref, wout_ref, g_ref, b_ref, band_ref, inv_ref)
    for k in range(EV_TILES):
        o_ref[k * TM:(k + 1) * TM, :] = outs[k]


def _even_mixer(x_ctx, x_lat, mods, w_in, conv_w, pool_w, pool_scale, w_out, g, b, cast_srcs):
    assert len(cast_srcs) == N_CAST_EVEN
    hb = EV_ROWS // HALO
    n8 = T_S // HALO
    n_ctx = T_P // EV_ROWS
    lat = lambda i: jnp.maximum(i - n_ctx, 0)
    cast_specs, cast_shapes = _cast_stream(cast_srcs)
    bands, inv_cnt = _pool_tables()
    outs = pl.pallas_call(
        _even_mixer_kernel,
        grid=(T // EV_ROWS,),
        in_specs=[
            pl.BlockSpec((EV_ROWS, D), lambda i: (jnp.minimum(i, n_ctx - 1), 0)),
            pl.BlockSpec((HALO, D), lambda i: (jnp.maximum(lat(i) * hb - 1, 0), 0)),
            pl.BlockSpec((EV_ROWS, D), lambda i: (lat(i), 0)),
            pl.BlockSpec((HALO, D), lambda i: (jnp.minimum((lat(i) + 1) * hb, n8 - 1), 0)),
            pl.BlockSpec((None, 6, D), lambda i: (_cond_index(i, EV_ROWS), 0, 0)),
            _const_spec((D, 4 * CONV_DIM)),
            _const_spec((3, CONV_DIM)),
            _const_spec((4, POOL_GROUP, POOL_GROUP)),
            _const_spec((1, 4 * POOL_GROUP)),
            _const_spec((D, D)),
            _const_spec((1, D)),
            _const_spec((1, D)),
            _const_spec(bands.shape),
            _const_spec(inv_cnt.shape),
        ] + cast_specs,
        out_specs=[pl.BlockSpec((EV_ROWS, D), lambda i: (i, 0))] + cast_specs,
        out_shape=[jax.ShapeDtypeStruct((T, D), F32)] + cast_shapes,
        scratch_shapes=[pltpu.VMEM((EV_TILES, TM + 2 * HALO, D), F32)],
        compiler_params=_params(1),
        name="even_mixer",
    )(x_ctx, x_lat, x_lat, x_lat, mods, w_in, conv_w, pool_w, pool_scale, w_out, g, b, bands, inv_cnt, *cast_srcs)
    return outs[0], outs[1:]


FF_CHUNK = D_FF // 2


def _ffn_kernel(x_ref, mod_ref, wg_ref, wu_ref, wd_ref, g_ref, b_ref, cast_ref, o_ref, cast_out_ref):
    _cast_step([cast_ref], [cast_out_ref])
    shift, scale, gate = mod_ref[3:4, :], mod_ref[4:5, :], mod_ref[5:6, :]
    x = x_ref[...]
    h = (x * (1.0 + scale) + shift).astype(BF16)
    f = jnp.zeros((TM_FFN, D), F32)
    for c in range(D_FF // FF_CHUNK):
        sl = slice(c * FF_CHUNK, (c + 1) * FF_CHUNK)
        a = _silu(_dot(h, wg_ref[:, sl])) * _dot(h, wu_ref[:, sl])
        f = f + _dot(a.astype(BF16), wd_ref[sl, :])
    o_ref[...] = _layer_norm(ALPHA * x + gate * f, g_ref[...], b_ref[...])


def _ffn(x, mods, wg, wu, wd, g, b, cast_src):
    cast_specs, cast_shapes = _cast_stream([cast_src])
    return pl.pallas_call(
        _ffn_kernel,
        grid=(T // TM_FFN,),
        in_specs=[
            pl.BlockSpec((TM_FFN, D), lambda i: (i, 0)),
            pl.BlockSpec((None, 6, D), lambda i: (_cond_index(i, TM_FFN), 0, 0)),
            _const_spec((D, D_FF)),
            _const_spec((D, D_FF)),
            _const_spec((D_FF, D)),
            _const_spec((1, D)),
            _const_spec((1, D)),
        ] + cast_specs,
        out_specs=[pl.BlockSpec((TM_FFN, D), lambda i: (i, 0))] + cast_specs,
        out_shape=[jax.ShapeDtypeStruct((T, D), F32)] + cast_shapes,
        compiler_params=_params(1),
        name="dense_swiglu",
    )(x, mods, wg, wu, wd, g, b, cast_src)


W_IN_EXT = Q_RANK + KV_RANK + FNET_DIM + 128 + 128
QH = 256
ATT_SCALE = (QK_NOPE + QK_ROPE) ** -0.5 * float(np.log2(np.e))


def _odd_proj_kernel(x_ref, mod_ref, rope_ref, win_ref, qn_ref, kvn_ref, wqa_ref, wqb_ref,
                     avg_ref, dfth_ref, dftl_ref, cast_ref,
                     q_ref, ckv_ref, kpe_ref, y_ref, cast_out_ref):
    _cast_step([cast_ref], [cast_out_ref])
    shift, scale = mod_ref[0:1, :], mod_ref[1:2, :]
    h = (x_ref[...] * (1.0 + scale) + shift).astype(BF16)
    u = _dot(h, win_ref[...])
    uq = u[:, 0:Q_RANK]
    ukv = u[:, Q_RANK:Q_RANK + KV_RANK]
    uf = u[:, Q_RANK + KV_RANK:Q_RANK + KV_RANK + FNET_DIM]
    o = Q_RANK + KV_RANK + FNET_DIM
    upe, upe_rot = u[:, o:o + 128], u[:, o + 128:o + 256]
    cos, sin = rope_ref[:, 0:128], rope_ref[:, 128:256]

    ckv_ref[...] = ukv * lax.rsqrt(jnp.mean(ukv * ukv, axis=-1, keepdims=True) + RMS_EPS) * kvn_ref[...]
    kpe_ref[...] = upe * cos + upe_rot * sin

    qlat = (uq * lax.rsqrt(jnp.mean(uq * uq, axis=-1, keepdims=True) + RMS_EPS) * qn_ref[...]).astype(BF16)
    qa = _dot(qlat, wqa_ref[...])
    qb = _dot(qlat, wqb_ref[...])
    for hd in range(N_HEADS):
        nope = qa[:, hd * QH:hd * QH + 128]
        pe = qa[:, hd * QH + 128:(hd + 1) * QH] * cos + qb[:, hd * 128:(hd + 1) * 128] * sin
        q_ref[:, hd * QH:hd * QH + 128] = (nope * ATT_SCALE).astype(BF16)
        q_ref[:, hd * QH + 128:(hd + 1) * QH] = (pe * ATT_SCALE).astype(BF16)

    avg = avg_ref[...]
    uf_hi, uf_lo = _split_bf16(uf)
    mu = _dot(uf_hi, avg) + _dot(uf_lo, avg)
    dlt = uf - mu
    sq_hi, sq_lo = _split_bf16(dlt * dlt)
    var = _dot(sq_hi, avg) + _dot(sq_lo, avg)
    xn = dlt * lax.rsqrt(var + LN_EPS)
    xn_hi, xn_lo = _split_bf16(xn)
    y = _dot(xn_hi, dfth_ref[...]) + _dot(xn_lo, dfth_ref[...]) + _dot(xn_hi, dftl_ref[...])
    y_ref[...] = y.astype(BF16)


def _odd_proj(x, mods, rope_tab, w_in_ext, q_norm, kv_norm, wqa, wqb, avg, dft_hi, dft_lo, cast_src):
    cast_specs, cast_shapes = _cast_stream([cast_src])

    def rope_index(i):
        return (jnp.where(i < T_P // TM2, 0, 1 + lax.rem(i - T_P // TM2, DEC_SEQ // TM2)), 0)

    return pl.pallas_call(
        _odd_proj_kernel,
        grid=(T // TM2,),
        in_specs=[
            pl.BlockSpec((TM2, D), lambda i: (i, 0)),
            pl.BlockSpec((None, 6, D), lambda i: (_cond_index(i, TM2), 0, 0)),
            pl.BlockSpec((TM2, 256), rope_index),
            _const_spec((D, W_IN_EXT)),
            _const_spec((1, Q_RANK)),
            _const_spec((1, KV_RANK)),
            _const_spec((Q_RANK, N_HEADS * QH)),
            _const_spec((Q_RANK, N_HEADS * 128)),
            _const_spec((FNET_DIM, FNET_DIM)),
            _const_spec((FNET_DIM, 2 * FNET_DIM)),
            _const_spec((FNET_DIM, 2 * FNET_DIM)),
        ] + cast_specs,
        out_specs=[
            pl.BlockSpec((TM2, N_HEADS * QH), lambda i: (i, 0)),
            pl.BlockSpec((TM2, KV_RANK), lambda i: (i, 0)),
            pl.BlockSpec((TM2, 128), lambda i: (i, 0)),
            pl.BlockSpec((TM2, 2 * FNET_DIM), lambda i: (i, 0)),
        ] + cast_specs,
        out_shape=[
            jax.ShapeDtypeStruct((T, N_HEADS * QH), BF16),
            jax.ShapeDtypeStruct((T, KV_RANK), F32),
            jax.ShapeDtypeStruct((T, 128), F32),
            jax.ShapeDtypeStruct((T, 2 * FNET_DIM), BF16),
        ] + cast_shapes,
        compiler_params=_params(1),
        name="odd_projections",
    )(x, mods, rope_tab, w_in_ext, q_norm, kv_norm, wqa, wqb, avg, dft_hi, dft_lo, cast_src)


V_OFF = N_HEADS * QH
KV_COLS = V_OFF + N_HEADS * V_DIM


def _attn_body(q_ref, kv_ref, o_refs):
    for hd in range(N_HEADS):
        qh = q_ref[:, hd * QH:(hd + 1) * QH]
        kh = kv_ref[:, hd * QH:(hd + 1) * QH]
        s = lax.dot_general(qh, kh, (((1,), (1,)), ((), ())), preferred_element_type=F32)
        p = jnp.exp2(s - jnp.max(s, axis=-1, keepdims=True))
        den = jnp.sum(p, axis=-1, keepdims=True)
        vh = kv_ref[:, V_OFF + hd * V_DIM:V_OFF + (hd + 1) * V_DIM]
        o = (_dot(p.astype(BF16), vh) / den).astype(BF16)
        for k, o_ref in enumerate(o_refs):
            o_ref[:, hd * 128:(hd + 1) * 128] = o[k * TM:(k + 1) * TM]


KV_CHUNK = 512
LAT_PAIR = 2


def _attn_kernel(q_ref, qpair_ref, ckvp_ref, kpep_ref, ckvs_ref, kpes_ref, cckv_ref, ckpe_ref, wkv_ref, o_ref,
                 kvp_s, kvs_s, held_s):
    i = pl.program_id(0)

    def expand(ckv, kpe, dst, row0):
        n = ckv.shape[0]
        kv = _dot(ckv.astype(BF16), wkv_ref[...]).astype(BF16)
        kpe = kpe.astype(BF16)
        for hd in range(N_HEADS):
            dst[row0:row0 + n, hd * QH:hd * QH + QK_NOPE] = kv[:, hd * QK_NOPE:(hd + 1) * QK_NOPE]
            dst[row0:row0 + n, hd * QH + QK_NOPE:(hd + 1) * QH] = kpe
        dst[row0:row0 + n, V_OFF:] = kv[:, N_HEADS * QK_NOPE:]

    @pl.when(i < NP_TILES)
    def _():
        expand(ckvp_ref[...], kpep_ref[...], kvp_s, 0)
        _attn_body(q_ref, kvp_s, [o_ref])

    @pl.when(jnp.logical_and(i >= NP_TILES, lax.rem(i - NP_TILES, LAT_PAIR) == 0))
    def _():
        @pl.when(lax.rem(i - NP_TILES, TPS) == 0)
        def _():
            expand(cckv_ref[...], ckpe_ref[...], kvs_s, 0)
            for c in range(DEC_SEQ // KV_CHUNK):
                rows = slice(c * KV_CHUNK, (c + 1) * KV_CHUNK)
                expand(ckvs_ref[rows, :], kpes_ref[rows, :], kvs_s, PAST + c * KV_CHUNK)

        _attn_body(qpair_ref, kvs_s, [o_ref, held_s])

    @pl.when(jnp.logical_and(i >= NP_TILES, lax.rem(i - NP_TILES, LAT_PAIR) == 1))
    def _():
        o_ref[...] = held_s[...]


def _attention(q, ckv, kpe, cache_ckv, cache_kpe128, w_kv):
    ctx_blk = lambda i: (jnp.minimum(i, NP_TILES - 1), 0)
    lat_b = lambda i: jnp.maximum(i - NP_TILES, 0) // TPS
    lat_blk = lambda i: (T_P // DEC_SEQ + lat_b(i), 0)
    return pl.pallas_call(
        _attn_kernel,
        grid=(N_TILES,),
        in_specs=[
            pl.BlockSpec((TM, N_HEADS * QH), ctx_blk),
            pl.BlockSpec((LAT_PAIR * TM, N_HEADS * QH),
                         lambda i: (NP_TILES // LAT_PAIR + jnp.maximum(i - NP_TILES, 0) // LAT_PAIR, 0)),
            pl.BlockSpec((SEQ, KV_RANK), ctx_blk),
            pl.BlockSpec((SEQ, 128), ctx_blk),
            pl.BlockSpec((DEC_SEQ, KV_RANK), lat_blk, pipeline_mode=pl.Buffered(1)),
            pl.BlockSpec((DEC_SEQ, 128), lat_blk, pipeline_mode=pl.Buffered(1)),
            pl.BlockSpec((None, PAST, KV_RANK), lambda i: (lat_b(i), 0, 0)),
            pl.BlockSpec((None, PAST, 128), lambda i: (lat_b(i), 0, 0)),
            _const_spec((KV_RANK, 2 * N_HEADS * 128)),
        ],
        out_specs=pl.BlockSpec((TM, N_HEADS * V_DIM), lambda i: (i, 0)),
        out_shape=jax.ShapeDtypeStruct((T, N_HEADS * V_DIM), BF16),
        scratch_shapes=[pltpu.VMEM((SEQ, KV_COLS), BF16), pltpu.VMEM((LK_S, KV_COLS), BF16),
                        pltpu.VMEM((TM, N_HEADS * V_DIM), BF16)],
        compiler_params=_params(1),
        name="attention",
    )(q, q, ckv, kpe, ckv, kpe, cache_ckv, cache_kpe128, w_kv)


def _pos_dft_body(y_ref, c, s, o_ref):
    f = _dot(c, y_ref[:, 0:FNET_DIM]) + _dot(s, y_ref[:, FNET_DIM:])
    o_ref[...] = f.astype(BF16)


def _pos_dft_kernel(yp_ref, cp_ref, sp_ref, ys_ref, c0_ref, s0_ref, cb_ref, sb_ref, o_ref):
    @pl.when(pl.program_id(0) < NP_TILES)
    def _():
        _pos_dft_body(yp_ref, cp_ref[...].astype(BF16), sp_ref[...].astype(BF16), o_ref)

    @pl.when(pl.program_id(0) >= NP_TILES)
    def _():
        j = lax.rem(pl.program_id(0) - NP_TILES, TPS)
        cb, sb = cb_ref[pl.ds(j, 1), :], sb_ref[pl.ds(j, 1), :]
        c0, s0 = c0_ref[...], s0_ref[...]
        _pos_dft_body(ys_ref, (c0 * cb - s0 * sb).astype(BF16), (s0 * cb + c0 * sb).astype(BF16), o_ref)


def _pos_dft(y, tabs_p, base_s, step_s):
    ctx_blk = lambda i: (jnp.minimum(i, NP_TILES - 1), 0)
    lat_seq = lambda i: (T_P // DEC_SEQ + jnp.maximum(i - NP_TILES, 0) // TPS, 0)
    return pl.pallas_call(
        _pos_dft_kernel,
        grid=(N_TILES,),
        in_specs=[pl.BlockSpec((SEQ, 2 * FNET_DIM), ctx_blk)]
        + [_const_spec((SEQ, SEQ))] * 2
        + [pl.BlockSpec((DEC_SEQ, 2 * FNET_DIM), lat_seq)]
        + [_const_spec((TM, DEC_SEQ))] * 2
        + [_const_spec((TPS, DEC_SEQ))] * 2,
        out_specs=pl.BlockSpec((TM, FNET_DIM), lambda i: (i, 0)),
        out_shape=jax.ShapeDtypeStruct((T, FNET_DIM), BF16),
        compiler_params=_params(1),
        name="pos_dft",
    )(y, *tabs_p, y, *base_s, *step_s)


def _odd_merge_kernel(x_ref, attn_ref, f_ref, mod_ref, fw_ref, wo_ref, g_ref, b_ref, rh_ref, rl_ref, tri_ref,
                      cast_ref, xo_ref, info_ref, infot_ref, cnt_ref, cast_out_ref, carry_ref):
    _cast_step([cast_ref], [cast_out_ref])

    @pl.when(pl.program_id(0) == 0)
    def _():
        carry_ref[...] = jnp.zeros_like(carry_ref)

    gate = mod_ref[2:3, :]
    shift2, scale2 = mod_ref[3:4, :], mod_ref[4:5, :]
    fm = _dot(f_ref[...], fw_ref[...]).astype(BF16)
    y = _dot(attn_ref[...], wo_ref[0:N_HEADS * V_DIM, :]) + _dot(fm, wo_ref[N_HEADS * V_DIM:, :])
    x = _layer_norm(ALPHA * x_ref[...] + gate * y, g_ref[...], b_ref[...])
    xo_ref[...] = x
    h = x * (1.0 + scale2) + shift2

    h_hi, h_lo = _split_bf16(h)
    logits = _dot(h_hi, rh_ref[...]) + _dot(h_lo, rh_ref[...]) + _dot(h_hi, rl_ref[...])
    lane = lax.broadcasted_iota(jnp.int32, (TM2, 128), 1)
    neg = jnp.float32(-jnp.inf)
    logits = jnp.where(lane < N_EXPERTS, logits, neg)
    m1 = jnp.max(logits, axis=-1, keepdims=True)
    i1 = jnp.min(jnp.where(logits == m1, lane, 128), axis=-1, keepdims=True)
    rest = jnp.where(lane == i1, neg, logits)
    m2 = jnp.max(rest, axis=-1, keepdims=True)
    i2 = jnp.min(jnp.where(rest == m2, lane, 128), axis=-1, keepdims=True)
    e2 = jnp.exp(m2 - m1)
    w1 = 1.0 / (1.0 + e2)
    w2 = e2 / (1.0 + e2)
    info = jnp.where(lane == 0, w1, 0.0)
    info = jnp.where(lane == 1, w2, info)
    info = jnp.where(lane == 2, i1.astype(F32), info)
    info = jnp.where(lane == 3, i2.astype(F32), info)

    uses = jnp.logical_or(lane == i1, lane == i2)
    seen = _dot(tri_ref[...], jnp.where(uses, 1.0, 0.0).astype(BF16)) + carry_ref[...]
    r1 = jnp.sum(jnp.where(lane == i1, seen, 0.0), axis=-1, keepdims=True)
    r2 = jnp.sum(jnp.where(lane == i2, seen, 0.0), axis=-1, keepdims=True)
    info = jnp.where(lane == 4, r1, info)
    info = jnp.where(lane == 5, r2, info)
    info_ref[...] = info
    infot_ref[...] = info.T
    total = carry_ref[...] + jnp.sum(jnp.where(uses, 1.0, 0.0), axis=0, keepdims=True)
    carry_ref[...] = total
    cnt_ref[...] = jnp.broadcast_to(total, cnt_ref.shape)


def _odd_merge(x, attn, f, mods, fnet_w, w_out, g, b, r_hi, r_lo, cast_src):
    row = lambda i: (i, 0)
    tri = jnp.asarray(np.tril(np.ones((TM2, TM2), np.float32), -1), BF16)
    cast_specs, cast_shapes = _cast_stream([cast_src])
    return pl.pallas_call(
        _odd_merge_kernel,
        grid=(T // TM2,),
        in_specs=[
            pl.BlockSpec((TM2, D), row),
            pl.BlockSpec((TM2, N_HEADS * V_DIM), row),
            pl.BlockSpec((TM2, FNET_DIM), row),
            pl.BlockSpec((None, 6, D), lambda i: (_cond_index(i, TM2), 0, 0)),
            _const_spec((FNET_DIM, FNET_DIM)),
            _const_spec((N_HEADS * V_DIM + FNET_DIM, D)),
            _const_spec((1, D)),
            _const_spec((1, D)),
            _const_spec((D, 128)),
            _const_spec((D, 128)),
            _const_spec((TM2, TM2)),
        ] + cast_specs,
        out_specs=[pl.BlockSpec((TM2, D), row), pl.BlockSpec((TM2, 128), row),
                   pl.BlockSpec((128, TM2), lambda i: (0, i)), pl.BlockSpec((8, 128), lambda i: (i, 0))] + cast_specs,
        out_shape=[jax.ShapeDtypeStruct((T, D), F32), jax.ShapeDtypeStruct((T, 128), F32),
                   jax.ShapeDtypeStruct((128, T), F32), jax.ShapeDtypeStruct((T // TM2 * 8, 128), F32)] + cast_shapes,
        scratch_shapes=[pltpu.VMEM((1, 128), F32)],
        compiler_params=_params(1),
        name="odd_merge_router",
    )(x, attn, f, mods, fnet_w, w_out, g, b, r_hi, r_lo, tri, cast_src)


TM_D = 512
ROW = (8, 128)
ROW_DT = BF16
DMA_UNROLL = 8


def _row_copy(src, s, dst, d, sem):
    return pltpu.make_async_copy(src.at[s], dst.at[d], sem)


CHUNK = 16
LOCAL_ROWS = 2 * TM2 + N_EXPERTS * CHUNK


def _chunk_copies(tile, nch_ref, fn):
    for e in range(N_EXPERTS):
        def body(c, carry, e=e):
            fn(tile * N_EXPERTS + e, c, e % 2)
            return carry

        lax.fori_loop(0, nch_ref[tile * N_EXPERTS + e], body, 0)


def _dispatch_kernel(dest_ref, pad_lo_ref, pad_hi_ref, x_ref, mod_ref, xs_ref, h_ref, zero_ref, sem, zsem):
    i = pl.program_id(0)
    base = i * TM_D
    slot = lax.rem(i, 2)

    @pl.when(i == 0)
    def _():
        zero_ref[...] = jnp.zeros_like(zero_ref)
        for e in range(N_EXPERTS + 1):
            def zissue(r, carry):
                _row_copy(zero_ref, 0, xs_ref, r, zsem).start()
                return carry

            def zdrain(r, carry):
                _row_copy(zero_ref, 0, xs_ref, 0, zsem).wait()
                return carry

            lax.fori_loop(pad_lo_ref[e], pad_hi_ref[e], zissue, 0)
            lax.fori_loop(pad_lo_ref[e], pad_hi_ref[e], zdrain, 0)

    shift2, scale2 = mod_ref[3:4, :], mod_ref[4:5, :]
    h_ref[slot] = (x_ref[...] * (1.0 + scale2) + shift2).astype(ROW_DT).reshape((TM_D,) + ROW)

    def issue(r, carry):
        t = base + r
        _row_copy(h_ref.at[slot], r, xs_ref, dest_ref[t], sem.at[slot]).start(priority=0)
        _row_copy(h_ref.at[slot], r, xs_ref, dest_ref[T + t], sem.at[slot]).start(priority=1)
        return carry

    lax.fori_loop(0, TM_D, issue, 0, unroll=DMA_UNROLL)

    def drain(s):
        def body(r, carry):
            _row_copy(h_ref.at[s], 0, xs_ref, 0, sem.at[s]).wait()
            _row_copy(h_ref.at[s], 0, xs_ref, 0, sem.at[s]).wait()
            return carry

        lax.fori_loop(0, TM_D, body, 0, unroll=DMA_UNROLL)

    @pl.when(i > 0)
    def _():
        drain(1 - slot)

    @pl.when(i == T // TM_D - 1)
    def _():
        drain(slot)


def _dispatch(dest, pad_lo, pad_hi, x, mods):
    return pl.pallas_call(
        _dispatch_kernel,
        grid_spec=pltpu.PrefetchScalarGridSpec(
            num_scalar_prefetch=3,
            grid=(T // TM_D,),
            in_specs=[pl.BlockSpec((TM_D, D), lambda i, *_: (i, 0)),
                      pl.BlockSpec((None, 6, D), lambda i, *_: (_cond_index(i, TM_D), 0, 0))],
            out_specs=pl.BlockSpec(memory_space=pl.ANY),
            scratch_shapes=[pltpu.VMEM((2, TM_D) + ROW, ROW_DT), pltpu.VMEM((1,) + ROW, ROW_DT),
                            pltpu.SemaphoreType.DMA((2,)), pltpu.SemaphoreType.DMA(())],
        ),
        out_shape=jax.ShapeDtypeStruct((R_MAX,) + ROW, ROW_DT),
        compiler_params=pltpu.CompilerParams(dimension_semantics=("arbitrary",), has_side_effects=True),
        name="expert_dispatch",
    )(dest, pad_lo, pad_hi, x, mods)


def _expert_kernel(te_ref, nt_ref, xs_ref, wg_ref, wu_ref, wd_ref, o_ref):
    @pl.when(pl.program_id(0) < nt_ref[0])
    def _():
        h = xs_ref[...].reshape(TM_E, D)
        a = _silu(_dot(h, wg_ref[...])) * _dot(h, wu_ref[...])
        o_ref[...] = _dot(a.astype(BF16), wd_ref[...]).astype(ROW_DT).reshape((TM_E,) + ROW)

    @pl.when(pl.program_id(0) >= nt_ref[0])
    def _():
        o_ref[...] = jnp.zeros_like(o_ref)


def _experts(tile_expert, n_used, xs, wg, wu, wd):
    return pl.pallas_call(
        _expert_kernel,
        grid_spec=pltpu.PrefetchScalarGridSpec(
            num_scalar_prefetch=2,
            grid=(N_ETILES,),
            in_specs=[
                pl.BlockSpec((TM_E,) + ROW, lambda i, te, nt: (jnp.minimum(i, nt[0] - 1), 0, 0)),
                pl.BlockSpec((None, D, D_FF_EXPERT), lambda i, te, nt: (te[i], 0, 0)),
                pl.BlockSpec((None, D, D_FF_EXPERT), lambda i, te, nt: (te[i], 0, 0)),
                pl.BlockSpec((None, D_FF_EXPERT, D), lambda i, te, nt: (te[i], 0, 0)),
            ],
            out_specs=pl.BlockSpec((TM_E,) + ROW, lambda i, te, nt: (i, 0, 0)),
        ),
        out_shape=jax.ShapeDtypeStruct((R_MAX,) + ROW, ROW_DT),
        compiler_params=_params(1),
        name="expert_swiglu",
    )(tile_expert, n_used, xs, wg, wu, wd)


def _combine_kernel(lp_ref, d0_ref, off_ref, nch_ref, x_ref, info_ref, mod_ref, g_ref, b_ref, ys_ref,
                    op_ref, os_ref, local_ref, rows_ref, sem):
    i = pl.program_id(0)
    slot = lax.rem(i, 2)

    def chunk(s, seg, c):
        return pltpu.make_async_copy(ys_ref.at[pl.ds(d0_ref[seg] + c * CHUNK, CHUNK)],
                                     local_ref.at[s, pl.ds(off_ref[seg] + c * CHUNK, CHUNK)], sem.at[s])

    @pl.when(i == 0)
    def _():
        _chunk_copies(0, nch_ref, lambda seg, c, prio: chunk(0, seg, c).start(priority=prio))

    @pl.when(i + 1 < T // TM2)
    def _():
        _chunk_copies(i + 1, nch_ref, lambda seg, c, prio: chunk(1 - slot, seg, c).start(priority=prio))

    _chunk_copies(i, nch_ref, lambda seg, c, prio: chunk(slot, seg, c).wait())

    def pick(r, carry):
        rows_ref[0, r] = local_ref[slot, lp_ref[i * TM2 + r]]
        rows_ref[1, r] = local_ref[slot, lp_ref[T + i * TM2 + r]]
        return carry

    lax.fori_loop(0, TM2, pick, 0, unroll=DMA_UNROLL)

    gate = mod_ref[5:6, :]
    w1, w2 = info_ref[:, 0:1], info_ref[:, 1:2]
    y = w1 * rows_ref[0].reshape(TM2, D).astype(F32) + w2 * rows_ref[1].reshape(TM2, D).astype(F32)
    out = _layer_norm(ALPHA * x_ref[...] + gate * y, g_ref[...], b_ref[...])

    @pl.when(pl.program_id(0) < T_P // TM2)
    def _():
        op_ref[...] = out

    @pl.when(pl.program_id(0) >= T_P // TM2)
    def _():
        os_ref[...] = out


def _combine(lp, d0, off, nch, x, info, mods, g, b, ys):
    return pl.pallas_call(
        _combine_kernel,
        grid_spec=pltpu.PrefetchScalarGridSpec(
            num_scalar_prefetch=4,
            grid=(T // TM2,),
            in_specs=[
                pl.BlockSpec((TM2, D), lambda i, *_: (i, 0)),
                pl.BlockSpec((TM2, 128), lambda i, *_: (i, 0)),
                pl.BlockSpec((None, 6, D), lambda i, *_: (_cond_index(i, TM2), 0, 0)),
                pl.BlockSpec((1, D), lambda i, *_: (0, 0)),
                pl.BlockSpec((1, D), lambda i, *_: (0, 0)),
                pl.BlockSpec(memory_space=pl.ANY),
            ],
            out_specs=[pl.BlockSpec((TM2, D), lambda i, *_: (jnp.minimum(i, T_P // TM2 - 1), 0)),
                       pl.BlockSpec((TM2, D), lambda i, *_: (jnp.maximum(i - T_P // TM2, 0), 0))],
            scratch_shapes=[pltpu.VMEM((2, LOCAL_ROWS) + ROW, ROW_DT), pltpu.VMEM((2, TM2) + ROW, ROW_DT),
                            pltpu.SemaphoreType.DMA((2,))],
        ),
        out_shape=[jax.ShapeDtypeStruct((T_P, D), F32), jax.ShapeDtypeStruct((T_S, D), F32)],
        compiler_params=_params(1),
        name="expert_combine",
    )(lp, d0, off, nch, x, info, mods, g, b, ys)


def _rot_cols(w):
    w4 = w.reshape(w.shape[:-1] + (2, 2, QK_ROPE // 4))
    return jnp.stack([-w4[..., 1, :], w4[..., 0, :]], axis=-2).reshape(w.shape)


def _rope_table():
    rows = DEC_SEQ // GRID_W
    row = np.repeat(np.arange(rows), GRID_W).astype(np.float32)
    col = np.tile(np.arange(GRID_W), rows).astype(np.float32)
    half = QK_ROPE // 2
    inv = (ROPE_THETA ** (-np.arange(0, half, 2, dtype=np.float32) / half)).astype(np.float32)
    ar, ac = row[:, None] * inv, col[:, None] * inv
    ang = np.concatenate([ar, ar, ac, ac], axis=-1)
    cos = np.concatenate([np.ones((TM2, QK_ROPE)), np.cos(ang)], axis=0)
    sin = np.concatenate([np.zeros((TM2, QK_ROPE)), np.sin(ang)], axis=0)
    n = cos.shape[0]
    return jnp.asarray(np.concatenate([cos, np.ones((n, 64)), sin, np.zeros((n, 64))], axis=1), F32)


def _dft_angles(rows, n):
    k = np.arange(n, dtype=np.int64)
    return ((np.asarray(rows, np.int64)[:, None] * k[None, :]) % n) * (2.0 * np.pi / n)


def _dft_tables(n):
    ang = _dft_angles(np.arange(n), n)
    return np.cos(ang) * n ** -0.5, np.sin(ang) * n ** -0.5


def _hi_lo(m):
    m = jnp.asarray(m, F32)
    hi = m.astype(BF16)
    return hi, (m - hi.astype(F32)).astype(BF16)


def _block_diag4(m):
    return np.kron(np.eye(4), m)


def kernel(x_prompt, x_sample, cache_ckv, cache_kpe, c, c_ctx, ada_w, ada_b, ln_g, ln_b, ev_w_in, ev_conv_w, ev_pool_w, ev_pool_scale, ev_w_out, ffn_w_gate, ffn_w_up, ffn_w_down, od_w_in, od_q_norm, od_kv_norm, od_w_q_b, od_w_kv_b, od_fnet_w, od_w_out, moe_router, moe_w_gate, moe_w_up, moe_w_down):
    cond8 = jnp.concatenate([c_ctx[None, :], c, jnp.zeros((8 - N_COND, D), F32)], axis=0)
    mods = _modulation(cond8, ada_w, ada_b)[:, :N_COND].reshape(DEPTH, N_COND, 6, D)

    x, (ffn_wg, ffn_wu, ffn_wd) = _even_mixer(
        x_prompt.reshape(T_P, D), x_sample.reshape(T_S, D), mods[0], ev_w_in[0].astype(BF16), ev_conv_w[0],
        ev_pool_w[0].astype(BF16), ev_pool_scale[0][None, :], ev_w_out[0].astype(BF16),
        ln_g[0, 0][None, :], ln_b[0, 0][None, :], [ffn_w_gate[0], ffn_w_up[0], ffn_w_down[0]])
    x, moe_wd = _ffn(x, mods[0], ffn_wg, ffn_wu, ffn_wd, ln_g[0, 1][None, :], ln_b[0, 1][None, :],
                     moe_w_down[0].reshape(N_EXPERTS * D_FF_EXPERT, D))

    w_in = od_w_in[0]
    w_pe = w_in[:, Q_RANK + KV_RANK:Q_RANK + KV_RANK + QK_ROPE]
    zpad = jnp.zeros((D, 64), F32)
    w_in_ext = jnp.concatenate([w_in[:, :Q_RANK + KV_RANK], w_in[:, Q_RANK + KV_RANK + QK_ROPE:],
                                w_pe, zpad, _rot_cols(w_pe), zpad], axis=1).astype(BF16)
    wq = od_w_q_b[0].reshape(Q_RANK, N_HEADS, QK_NOPE + QK_ROPE)
    zq = jnp.zeros((Q_RANK, N_HEADS, 64), F32)
    wqa = jnp.concatenate([wq, zq], axis=-1).reshape(Q_RANK, N_HEADS * QH).astype(BF16)
    wqb = jnp.concatenate([_rot_cols(wq[..., QK_NOPE:]), zq], axis=-1).reshape(Q_RANK, N_HEADS * 128).astype(BF16)
    wkv = od_w_kv_b[0].reshape(KV_RANK, N_HEADS, QK_NOPE + V_DIM)
    w_kv = jnp.concatenate([wkv[..., :QK_NOPE].reshape(KV_RANK, -1), wkv[..., QK_NOPE:].reshape(KV_RANK, -1)],
                           axis=1).astype(BF16)

    avg = jnp.asarray(_block_diag4(np.full((FNET_GROUP_DIM, FNET_GROUP_DIM), 1.0 / FNET_GROUP_DIM)), BF16)
    cc, sc = _dft_tables(FNET_GROUP_DIM)
    dft_hi, dft_lo = _hi_lo(np.concatenate([_block_diag4(cc), -_block_diag4(sc)], axis=1))

    q, ckv, kpe, y_dft, moe_wg = _odd_proj(x, mods[1], _rope_table(), w_in_ext, od_q_norm[0][None, :],
                                            od_kv_norm[0][None, :], wqa, wqb, avg, dft_hi, dft_lo,
                                            moe_w_gate[0].reshape(N_EXPERTS * D, D_FF_EXPERT))

    cache_kpe128 = jnp.pad(cache_kpe[:, 0], ((0, 0), (0, 0), (0, 128 - QK_ROPE)))
    attn = _attention(q, ckv, kpe, cache_ckv[:, 0], cache_kpe128, w_kv)

    tabs_p = tuple(jnp.asarray(m, F32) for m in _dft_tables(SEQ))
    a_base = _dft_angles(np.arange(TM), DEC_SEQ)
    a_step = _dft_angles(np.arange(TPS) * TM, DEC_SEQ)
    base_s = (jnp.asarray(np.cos(a_base) * DEC_SEQ ** -0.5, F32), jnp.asarray(np.sin(a_base) * DEC_SEQ ** -0.5, F32))
    step_s = (jnp.asarray(np.cos(a_step), F32), jnp.asarray(np.sin(a_step), F32))
    f = _pos_dft(y_dft, tabs_p, base_s, step_s)

    router = jnp.pad(moe_router[0], ((0, 0), (0, 128 - N_EXPERTS)))
    r_hi, r_lo = _hi_lo(router)
    x, info, info_t, cnt, moe_wu = _odd_merge(x, attn, f, mods[1], od_fnet_w[0].astype(BF16),
                                              od_w_out[0].astype(BF16), ln_g[1, 0][None, :], ln_b[1, 0][None, :],
                                              r_hi, r_lo, moe_w_up[0].reshape(N_EXPERTS * D, D_FF_EXPERT))

    n_tiles = T // TM2
    after = cnt.reshape(n_tiles, 8, 128)[:, 0, :N_EXPERTS].astype(jnp.int32)
    before = jnp.concatenate([jnp.zeros((1, N_EXPERTS), jnp.int32), after[:-1]], axis=0)
    counts = after[-1]
    padded = ((counts + TM_E - 1) // TM_E) * TM_E
    g_end = jnp.cumsum(padded)
    g_start = g_end - padded
    n_chunks = (after - before + CHUNK - 1) // CHUNK
    seg_off = CHUNK * (jnp.cumsum(n_chunks, axis=1) - n_chunks)
    seg_dst = g_start[None, :] + before
    choice = info_t[2:4].astype(jnp.int32)
    rank = info_t[4:6].astype(jnp.int32)
    shift_t = jnp.repeat((seg_off - before).T, TM2, axis=1)
    lp, dest = rank, rank
    for e in range(N_EXPERTS):
        lp = lp + jnp.where(choice == e, shift_t[e][None, :], 0)
        dest = dest + jnp.where(choice == e, g_start[e], 0)
    lp = lp.reshape(-1).astype(jnp.int32)
    dest = dest.reshape(-1).astype(jnp.int32)
    d0, off, nch = (a.reshape(-1).astype(jnp.int32) for a in (seg_dst, seg_off, n_chunks))
    tile_row = jnp.arange(N_ETILES, dtype=jnp.int32) * TM_E
    tile_expert = jnp.minimum(jnp.sum((tile_row[:, None] >= g_end[None, :]).astype(jnp.int32), axis=1),
                              N_EXPERTS - 1).astype(jnp.int32)
    n_used = (g_end[-1:] // TM_E).astype(jnp.int32)

    pad_lo = jnp.concatenate([g_start + counts, g_end[-1:]]).astype(jnp.int32)
    pad_hi = jnp.concatenate([g_end, jnp.full((1,), R_MAX, jnp.int32)]).astype(jnp.int32)
    xs = _dispatch(dest, pad_lo, pad_hi, x, mods[1])
    ys = _experts(tile_expert, n_used, xs, moe_wg.reshape(N_EXPERTS, D, D_FF_EXPERT),
                  moe_wu.reshape(N_EXPERTS, D, D_FF_EXPERT), moe_wd.reshape(N_EXPERTS, D_FF_EXPERT, D))
    yp, ysm = _combine(lp, d0, off, nch, x, info, mods[1], ln_g[1, 1][None, :], ln_b[1, 1][None, :], ys)

    y_prompt = yp.reshape(BATCH, SEQ, D)
    y_sample = ysm.reshape(DEC_BATCH, DEC_SEQ, D)
    new_ckv = ckv[:T_P].reshape(BATCH, 1, SEQ, KV_RANK)
    new_kpe = kpe[:T_P, :QK_ROPE].reshape(BATCH, 1, SEQ, QK_ROPE)
    return (y_prompt, y_sample, new_ckv, new_kpe)
```

```python
import functools

import numpy as np
import jax
import jax.numpy as jnp
from jax import lax
from jax.experimental import pallas as pl
from jax.experimental.pallas import tpu as pltpu

F32 = jnp.float32
BF16 = jnp.bfloat16

D = 1024
BATCH, SEQ = 32, 256
DEC_BATCH, DEC_SEQ = 2, 2048
PAST = 512
GRID_W = 64
T_P = BATCH * SEQ
T_S = DEC_BATCH * DEC_SEQ
T = T_P + T_S
N_COND = 1 + DEC_BATCH

CONV_DIM = 512
POOL_WINDOWS = (2, 4, 8, 16)
POOL_GROUP = 128
N_HEADS = 8
QK_NOPE, QK_ROPE, V_DIM = 128, 64, 128
Q_RANK, KV_RANK = 384, 256
FNET_DIM, FNET_GROUP_DIM = 256, 64
D_FF = 2816
N_EXPERTS = 8
D_FF_EXPERT = 1792
DEPTH = 2
ALPHA = (2 * DEPTH) ** 0.25
LN_EPS = 1e-5
RMS_EPS = 1e-6
ROPE_THETA = 10000.0

TM = 256
NP_TILES = T_P // TM
TPS = DEC_SEQ // TM
N_TILES = T // TM
HALO = 8
TM2 = 512
TM_FFN = 512
LK_S = PAST + DEC_SEQ
TM_E = 512
N_ETILES = (2 * T) // TM_E + N_EXPERTS + 1
R_MAX = N_ETILES * TM_E
VMEM_LIMIT = 56 * 1024 * 1024


def _cond_index(i, tm=TM):
    return jnp.where(i < T_P // tm, 0, 1 + (i - T_P // tm) // (DEC_SEQ // tm))


def _const_spec(shape):
    nd = len(shape)
    return pl.BlockSpec(shape, lambda *_: (0,) * nd, pipeline_mode=pl.Buffered(1))


def _params(n_axes=1, vmem=VMEM_LIMIT):
    return pltpu.CompilerParams(dimension_semantics=("arbitrary",) * n_axes, vmem_limit_bytes=vmem)


def _layer_norm(v, g, b):
    mu = jnp.mean(v, axis=-1, keepdims=True)
    d = v - mu
    var = jnp.mean(d * d, axis=-1, keepdims=True)
    return d * lax.rsqrt(var + LN_EPS) * g + b


def _split_bf16(v):
    hi = v.astype(BF16)
    lo = (v - hi.astype(F32)).astype(BF16)
    return hi, lo


def _dot(a, b):
    return jnp.dot(a, b, preferred_element_type=F32)


def _silu(v):
    return v / (1.0 + jnp.exp(-v))


CAST_BLOCKS = 16


def _cast_stream(srcs):
    specs = [pl.BlockSpec((w.shape[0] // CAST_BLOCKS, w.shape[1]),
                          lambda i, *_: (jnp.minimum(i, CAST_BLOCKS - 1), 0)) for w in srcs]
    return specs, [jax.ShapeDtypeStruct(w.shape, BF16) for w in srcs]


def _cast_step(srcs, dsts):
    @pl.when(pl.program_id(0) < CAST_BLOCKS)
    def _():
        for src, dst in zip(srcs, dsts):
            dst[...] = src[...].astype(BF16)


def _mod_kernel(cond_ref, w_ref, b_ref, o_ref):
    s = _silu(cond_ref[...]).astype(BF16)
    o_ref[...] = _dot(s, w_ref[...].astype(BF16)) + b_ref[...]


def _modulation(cond8, ada_w, ada_b):
    nb = 6 * D // 1024
    return pl.pallas_call(
        _mod_kernel,
        grid=(DEPTH, nb),
        in_specs=[
            pl.BlockSpec((8, D), lambda l, j: (0, 0)),
            pl.BlockSpec((None, D, 1024), lambda l, j: (l, 0, j)),
            pl.BlockSpec((None, 1, 1024), lambda l, j: (l, 0, j)),
        ],
        out_specs=pl.BlockSpec((None, 8, 1024), lambda l, j: (l, 0, j)),
        out_shape=jax.ShapeDtypeStruct((DEPTH, 8, 6 * D), F32),
        compiler_params=_params(2),
        name="adaln_modulation",
    )(cond8, ada_w, ada_b.reshape(DEPTH, 1, 6 * D))


def _pool_tables():
    t = np.arange(TM)[:, None]
    r = np.arange(TM + 2 * HALO)[None, :]
    pos = r - HALO
    bands = np.zeros((4, len(POOL_WINDOWS), TM, TM + 2 * HALO), np.float32)
    inv = np.zeros((4, TM, 128), np.float32)
    for variant in range(4):
        left_ok, right_ok = variant & 1, variant >> 1
        col_ok = (r >= (0 if left_ok else HALO)) & (r < (TM + 2 * HALO if right_ok else TM + HALO))
        first, last = (-HALO if left_ok else 0), (TM + HALO if right_ok else TM)
        for gi, w in enumerate(POOL_WINDOWS):
            bands[variant, gi] = (pos >= t - w // 2) & (pos < t + w // 2) & col_ok
            cnt = np.minimum(t[:, 0] + w // 2, last) - np.maximum(t[:, 0] - w // 2, first)
            inv[variant, :, gi] = 1.0 / cnt
    return jnp.asarray(bands, BF16), jnp.asarray(inv, F32)


def _even_tiles(x_alls, left_oks, right_oks, mod_ref, win_ref, convw_ref, poolw_ref, pscale_ref, wout_ref,
                g_ref, b_ref, band_ref, inv_ref):
    shift, scale, gate = mod_ref[0:1, :], mod_ref[1:2, :], mod_ref[2:3, :]
    n = len(x_alls)
    hs = []
    for x_all, left_ok, right_ok in zip(x_alls, left_oks, right_oks):
        h = x_all * (1.0 + scale) + shift
        hs.append(jnp.concatenate([jnp.where(left_ok, h[:HALO], 0.0), h[HALO:HALO + TM],
                                   jnp.where(right_ok, h[HALO + TM:], 0.0)], axis=0).astype(BF16))
    us = [_dot(h, win_ref[...]) for h in hs]

    mixes = []
    for u, left_ok, right_ok in zip(us, left_oks, right_oks):
        ux, ub = u[:, 0:CONV_DIM], u[:, CONV_DIM:2 * CONV_DIM]
        uc, up = u[:, 2 * CONV_DIM:3 * CONV_DIM], u[:, 3 * CONV_DIM:]
        z = uc * ux
        conv = (z[HALO - 1:HALO - 1 + TM] * convw_ref[0:1, :]
                + z[HALO:HALO + TM] * convw_ref[1:2, :]
                + z[HALO + 1:HALO + 1 + TM] * convw_ref[2:3, :])
        ya = ub[HALO:HALO + TM] * conv

        variant = left_ok.astype(jnp.int32) + 2 * right_ok.astype(jnp.int32)
        inv_cnt = inv_ref[variant]
        up_hi, up_lo = _split_bf16(up)
        yb_groups = []
        for gi in range(len(POOL_WINDOWS)):
            band = band_ref[variant, gi]
            sl = slice(gi * POOL_GROUP, (gi + 1) * POOL_GROUP)
            tot = _dot(band, up_hi[:, sl]) + _dot(band, up_lo[:, sl])
            p = tot * inv_cnt[:, gi:gi + 1] - up[HALO:HALO + TM, sl]
            yb_groups.append(_dot(p.astype(BF16), poolw_ref[gi]))
        yb = jnp.concatenate(yb_groups, axis=1) * pscale_ref[...]
        mixes.append(jnp.concatenate([ya, yb], axis=1).astype(BF16))

    ys = [_dot(mix, wout_ref[...]) for mix in mixes]
    return [_layer_norm(ALPHA * x_alls[k][HALO:HALO + TM] + gate * ys[k], g_ref[...], b_ref[...])
            for k in range(n)]


EV_TILES = 2
EV_ROWS = EV_TILES * TM


N_CAST_EVEN = 3


def _even_mixer_kernel(xctx_ref, xprev_ref, xlat_ref, xnext_ref, mod_ref, win_ref, convw_ref, poolw_ref,
                       pscale_ref, wout_ref, g_ref, b_ref, band_ref, inv_ref, *rest):
    n = N_CAST_EVEN
    cast_in, o_ref, cast_out, xall_ref = rest[:n], rest[n], rest[n + 1:2 * n + 1], rest[-1]
    _cast_step(cast_in, cast_out)
    s = pl.program_id(0)
    n_ctx = T_P // EV_ROWS
    is_latent = s >= n_ctx
    first_tile = lax.rem(s - n_ctx, DEC_SEQ // EV_ROWS) * EV_TILES

    @pl.when(jnp.logical_not(is_latent))
    def _():
        for k in range(EV_TILES):
            xall_ref[k, 0:HALO, :] = jnp.zeros((HALO, D), F32)
            xall_ref[k, HALO:HALO + TM, :] = xctx_ref[k * TM:(k + 1) * TM, :]
            xall_ref[k, HALO + TM:, :] = jnp.zeros((HALO, D), F32)

    @pl.when(is_latent)
    def _():
        for k in range(EV_TILES):
            lo, hi = k * TM, (k + 1) * TM
            xall_ref[k, 0:HALO, :] = xprev_ref[...] if k == 0 else xlat_ref[lo - HALO:lo, :]
            xall_ref[k, HALO:HALO + TM, :] = xlat_ref[lo:hi, :]
            xall_ref[k, HALO + TM:, :] = xnext_ref[...] if k == EV_TILES - 1 else xlat_ref[hi:hi + HALO, :]

    left_oks = [jnp.logical_and(is_latent, first_tile + k != 0) for k in range(EV_TILES)]
    right_oks = [jnp.logical_and(is_latent, first_tile + k != TPS - 1) for k in range(EV_TILES)]
    outs = _even_tiles([xall_ref[k] for k in range(EV_TILES)], left_oks, right_oks, mod_ref, win_ref,
                       convw_ref, poolw_ref, pscale_ref, wout_ref, g_ref, b_ref, band_ref, inv_ref)
    for k in range(EV_TILES):
        o_ref[k * TM:(k + 1) * TM, :] = outs[k]


def _even_mixer(x_ctx, x_lat, mods, w_in, conv_w, pool_w, pool_scale, w_out, g, b, cast_srcs):
    assert len(cast_srcs) == N_CAST_EVEN
    hb = EV_ROWS // HALO
    n8 = T_S // HALO
    n_ctx = T_P // EV_ROWS
    lat = lambda i: jnp.maximum(i - n_ctx, 0)
    cast_specs, cast_shapes = _cast_stream(cast_srcs)
    bands, inv_cnt = _pool_tables()
    outs = pl.pallas_call(
        _even_mixer_kernel,
        grid=(T // EV_ROWS,),
        in_specs=[
            pl.BlockSpec((EV_ROWS, D), lambda i: (jnp.minimum(i, n_ctx - 1), 0)),
            pl.BlockSpec((HALO, D), lambda i: (jnp.maximum(lat(i) * hb - 1, 0), 0)),
            pl.BlockSpec((EV_ROWS, D), lambda i: (lat(i), 0)),
            pl.BlockSpec((HALO, D), lambda i: (jnp.minimum((lat(i) + 1) * hb, n8 - 1), 0)),
            pl.BlockSpec((None, 6, D), lambda i: (_cond_index(i, EV_ROWS), 0, 0)),
            _const_spec((D, 4 * CONV_DIM)),
            _const_spec((3, CONV_DIM)),
            _const_spec((4, POOL_GROUP, POOL_GROUP)),
            _const_spec((1, 4 * POOL_GROUP)),
            _const_spec((D, D)),
            _const_spec((1, D)),
            _const_spec((1, D)),
            _const_spec(bands.shape),
            _const_spec(inv_cnt.shape),
        ] + cast_specs,
        out_specs=[pl.BlockSpec((EV_ROWS, D), lambda i: (i, 0))] + cast_specs,
        out_shape=[jax.ShapeDtypeStruct((T, D), F32)] + cast_shapes,
        scratch_shapes=[pltpu.VMEM((EV_TILES, TM + 2 * HALO, D), F32)],
        compiler_params=_params(1),
        name="even_mixer",
    )(x_ctx, x_lat, x_lat, x_lat, mods, w_in, conv_w, pool_w, pool_scale, w_out, g, b, bands, inv_cnt, *cast_srcs)
    return outs[0], outs[1:]


FF_CHUNK = D_FF // 2


def _ffn_kernel(x_ref, mod_ref, wg_ref, wu_ref, wd_ref, g_ref, b_ref, cast_ref, o_ref, cast_out_ref):
    _cast_step([cast_ref], [cast_out_ref])
    shift, scale, gate = mod_ref[3:4, :], mod_ref[4:5, :], mod_ref[5:6, :]
    x = x_ref[...]
    h = (x * (1.0 + scale) + shift).astype(BF16)
    f = jnp.zeros((TM_FFN, D), F32)
    for c in range(D_FF // FF_CHUNK):
        sl = slice(c * FF_CHUNK, (c + 1) * FF_CHUNK)
        a = _silu(_dot(h, wg_ref[:, sl])) * _dot(h, wu_ref[:, sl])
        f = f + _dot(a.astype(BF16), wd_ref[sl, :])
    o_ref[...] = _layer_norm(ALPHA * x + gate * f, g_ref[...], b_ref[...])


def _ffn(x, mods, wg, wu, wd, g, b, cast_src):
    cast_specs, cast_shapes = _cast_stream([cast_src])
    return pl.pallas_call(
        _ffn_kernel,
        grid=(T // TM_FFN,),
        in_specs=[
            pl.BlockSpec((TM_FFN, D), lambda i: (i, 0)),
            pl.BlockSpec((None, 6, D), lambda i: (_cond_index(i, TM_FFN), 0, 0)),
            _const_spec((D, D_FF)),
            _const_spec((D, D_FF)),
            _const_spec((D_FF, D)),
            _const_spec((1, D)),
            _const_spec((1, D)),
        ] + cast_specs,
        out_specs=[pl.BlockSpec((TM_FFN, D), lambda i: (i, 0))] + cast_specs,
        out_shape=[jax.ShapeDtypeStruct((T, D), F32)] + cast_shapes,
        compiler_params=_params(1),
        name="dense_swiglu",
    )(x, mods, wg, wu, wd, g, b, cast_src)


W_IN_EXT = Q_RANK + KV_RANK + FNET_DIM + 128 + 128
QH = 256
ATT_SCALE = (QK_NOPE + QK_ROPE) ** -0.5 * float(np.log2(np.e))


def _odd_proj_kernel(x_ref, mod_ref, rope_ref, win_ref, qn_ref, kvn_ref, wqa_ref, wqb_ref,
                     avg_ref, dfth_ref, dftl_ref, cast_ref,
                     q_ref, ckv_ref, kpe_ref, y_ref, cast_out_ref):
    _cast_step([cast_ref], [cast_out_ref])
    shift, scale = mod_ref[0:1, :], mod_ref[1:2, :]
    h = (x_ref[...] * (1.0 + scale) + shift).astype(BF16)
    u = _dot(h, win_ref[...])
    uq = u[:, 0:Q_RANK]
    ukv = u[:, Q_RANK:Q_RANK + KV_RANK]
    uf = u[:, Q_RANK + KV_RANK:Q_RANK + KV_RANK + FNET_DIM]
    o = Q_RANK + KV_RANK + FNET_DIM
    upe, upe_rot = u[:, o:o + 128], u[:, o + 128:o + 256]
    cos, sin = rope_ref[:, 0:128], rope_ref[:, 128:256]

    ckv_ref[...] = ukv * lax.rsqrt(jnp.mean(ukv * ukv, axis=-1, keepdims=True) + RMS_EPS) * kvn_ref[...]
    kpe_ref[...] = upe * cos + upe_rot * sin

    qlat = (uq * lax.rsqrt(jnp.mean(uq * uq, axis=-1, keepdims=True) + RMS_EPS) * qn_ref[...]).astype(BF16)
    qa = _dot(qlat, wqa_ref[...])
    qb = _dot(qlat, wqb_ref[...])
    for hd in range(N_HEADS):
        nope = qa[:, hd * QH:hd * QH + 128]
        pe = qa[:, hd * QH + 128:(hd + 1) * QH] * cos + qb[:, hd * 128:(hd + 1) * 128] * sin
        q_ref[:, hd * QH:hd * QH + 128] = (nope * ATT_SCALE).astype(BF16)
        q_ref[:, hd * QH + 128:(hd + 1) * QH] = (pe * ATT_SCALE).astype(BF16)

    avg = avg_ref[...]
    uf_hi, uf_lo = _split_bf16(uf)
    mu = _dot(uf_hi, avg) + _dot(uf_lo, avg)
    dlt = uf - mu
    sq_hi, sq_lo = _split_bf16(dlt * dlt)
    var = _dot(sq_hi, avg) + _dot(sq_lo, avg)
    xn = dlt * lax.rsqrt(var + LN_EPS)
    xn_hi, xn_lo = _split_bf16(xn)
    y = _dot(xn_hi, dfth_ref[...]) + _dot(xn_lo, dfth_ref[...]) + _dot(xn_hi, dftl_ref[...])
    y_ref[...] = y.astype(BF16)


def _odd_proj(x, mods, rope_tab, w_in_ext, q_norm, kv_norm, wqa, wqb, avg, dft_hi, dft_lo, cast_src):
    cast_specs, cast_shapes = _cast_stream([cast_src])

    def rope_index(i):
        return (jnp.where(i < T_P // TM2, 0, 1 + lax.rem(i - T_P // TM2, DEC_SEQ // TM2)), 0)

    return pl.pallas_call(
        _odd_proj_kernel,
        grid=(T // TM2,),
        in_specs=[
            pl.BlockSpec((TM2, D), lambda i: (i, 0)),
            pl.BlockSpec((None, 6, D), lambda i: (_cond_index(i, TM2), 0, 0)),
            pl.BlockSpec((TM2, 256), rope_index),
            _const_spec((D, W_IN_EXT)),
            _const_spec((1, Q_RANK)),
            _const_spec((1, KV_RANK)),
            _const_spec((Q_RANK, N_HEADS * QH)),
            _const_spec((Q_RANK, N_HEADS * 128)),
            _const_spec((FNET_DIM, FNET_DIM)),
            _const_spec((FNET_DIM, 2 * FNET_DIM)),
            _const_spec((FNET_DIM, 2 * FNET_DIM)),
        ] + cast_specs,
        out_specs=[
            pl.BlockSpec((TM2, N_HEADS * QH), lambda i: (i, 0)),
            pl.BlockSpec((TM2, KV_RANK), lambda i: (i, 0)),
            pl.BlockSpec((TM2, 128), lambda i: (i, 0)),
            pl.BlockSpec((TM2, 2 * FNET_DIM), lambda i: (i, 0)),
        ] + cast_specs,
        out_shape=[
            jax.ShapeDtypeStruct((T, N_HEADS * QH), BF16),
            jax.ShapeDtypeStruct((T, KV_RANK), F32),
            jax.ShapeDtypeStruct((T, 128), F32),
            jax.ShapeDtypeStruct((T, 2 * FNET_DIM), BF16),
        ] + cast_shapes,
        compiler_params=_params(1),
        name="odd_projections",
    )(x, mods, rope_tab, w_in_ext, q_norm, kv_norm, wqa, wqb, avg, dft_hi, dft_lo, cast_src)


V_OFF = N_HEADS * QH
KV_COLS = V_OFF + N_HEADS * V_DIM


def _attn_body(q_ref, kv_ref, o_refs):
    for hd in range(N_HEADS):
        qh = q_ref[:, hd * QH:(hd + 1) * QH]
        kh = kv_ref[:, hd * QH:(hd + 1) * QH]
        s = lax.dot_general(qh, kh, (((1,), (1,)), ((), ())), preferred_element_type=F32)
        p = jnp.exp2(s - jnp.max(s, axis=-1, keepdims=True))
        den = jnp.sum(p, axis=-1, keepdims=True)
        vh = kv_ref[:, V_OFF + hd * V_DIM:V_OFF + (hd + 1) * V_DIM]
        o = (_dot(p.astype(BF16), vh) / den).astype(BF16)
        for k, o_ref in enumerate(o_refs):
            o_ref[:, hd * 128:(hd + 1) * 128] = o[k * TM:(k + 1) * TM]


KV_CHUNK = 512
LAT_PAIR = 2


def _attn_kernel(q_ref, qpair_ref, ckvp_ref, kpep_ref, ckvs_ref, kpes_ref, cckv_ref, ckpe_ref, wkv_ref, o_ref,
                 kvp_s, kvs_s, held_s):
    i = pl.program_id(0)

    def expand(ckv, kpe, dst, row0):
        n = ckv.shape[0]
        kv = _dot(ckv.astype(BF16), wkv_ref[...]).astype(BF16)
        kpe = kpe.astype(BF16)
        for hd in range(N_HEADS):
            dst[row0:row0 + n, hd * QH:hd * QH + QK_NOPE] = kv[:, hd * QK_NOPE:(hd + 1) * QK_NOPE]
            dst[row0:row0 + n, hd * QH + QK_NOPE:(hd + 1) * QH] = kpe
        dst[row0:row0 + n, V_OFF:] = kv[:, N_HEADS * QK_NOPE:]

    @pl.when(i < NP_TILES)
    def _():
        expand(ckvp_ref[...], kpep_ref[...], kvp_s, 0)
        _attn_body(q_ref, kvp_s, [o_ref])

    @pl.when(jnp.logical_and(i >= NP_TILES, lax.rem(i - NP_TILES, LAT_PAIR) == 0))
    def _():
        @pl.when(lax.rem(i - NP_TILES, TPS) == 0)
        def _():
            expand(cckv_ref[...], ckpe_ref[...], kvs_s, 0)
            for c in range(DEC_SEQ // KV_CHUNK):
                rows = slice(c * KV_CHUNK, (c + 1) * KV_CHUNK)
                expand(ckvs_ref[rows, :], kpes_ref[rows, :], kvs_s, PAST + c * KV_CHUNK)

        _attn_body(qpair_ref, kvs_s, [o_ref, held_s])

    @pl.when(jnp.logical_and(i >= NP_TILES, lax.rem(i - NP_TILES, LAT_PAIR) == 1))
    def _():
        o_ref[...] = held_s[...]


def _attention(q, ckv, kpe, cache_ckv, cache_kpe128, w_kv):
    ctx_blk = lambda i: (jnp.minimum(i, NP_TILES - 1), 0)
    lat_b = lambda i: jnp.maximum(i - NP_TILES, 0) // TPS
    lat_blk = lambda i: (T_P // DEC_SEQ + lat_b(i), 0)
    return pl.pallas_call(
        _attn_kernel,
        grid=(N_TILES,),
        in_specs=[
            pl.BlockSpec((TM, N_HEADS * QH), ctx_blk),
            pl.BlockSpec((LAT_PAIR * TM, N_HEADS * QH),
                         lambda i: (NP_TILES // LAT_PAIR + jnp.maximum(i - NP_TILES, 0) // LAT_PAIR, 0)),
            pl.BlockSpec((SEQ, KV_RANK), ctx_blk),
            pl.BlockSpec((SEQ, 128), ctx_blk),
            pl.BlockSpec((DEC_SEQ, KV_RANK), lat_blk, pipeline_mode=pl.Buffered(1)),
            pl.BlockSpec((DEC_SEQ, 128), lat_blk, pipeline_mode=pl.Buffered(1)),
            pl.BlockSpec((None, PAST, KV_RANK), lambda i: (lat_b(i), 0, 0)),
            pl.BlockSpec((None, PAST, 128), lambda i: (lat_b(i), 0, 0)),
            _const_spec((KV_RANK, 2 * N_HEADS * 128)),
        ],
        out_specs=pl.BlockSpec((TM, N_HEADS * V_DIM), lambda i: (i, 0)),
        out_shape=jax.ShapeDtypeStruct((T, N_HEADS * V_DIM), BF16),
        scratch_shapes=[pltpu.VMEM((SEQ, KV_COLS), BF16), pltpu.VMEM((LK_S, KV_COLS), BF16),
                        pltpu.VMEM((TM, N_HEADS * V_DIM), BF16)],
        compiler_params=_params(1),
        name="attention",
    )(q, q, ckv, kpe, ckv, kpe, cache_ckv, cache_kpe128, w_kv)


def _pos_dft_body(y_ref, c, s, o_ref):
    f = _dot(c, y_ref[:, 0:FNET_DIM]) + _dot(s, y_ref[:, FNET_DIM:])
    o_ref[...] = f.astype(BF16)


def _pos_dft_kernel(yp_ref, cp_ref, sp_ref, ys_ref, c0_ref, s0_ref, cb_ref, sb_ref, o_ref):
    @pl.when(pl.program_id(0) < NP_TILES)
    def _():
        _pos_dft_body(yp_ref, cp_ref[...].astype(BF16), sp_ref[...].astype(BF16), o_ref)

    @pl.when(pl.program_id(0) >= NP_TILES)
    def _():
        j = lax.rem(pl.program_id(0) - NP_TILES, TPS)
        cb, sb = cb_ref[pl.ds(j, 1), :], sb_ref[pl.ds(j, 1), :]
        c0, s0 = c0_ref[...], s0_ref[...]
        _pos_dft_body(ys_ref, (c0 * cb - s0 * sb).astype(BF16), (s0 * cb + c0 * sb).astype(BF16), o_ref)


def _pos_dft(y, tabs_p, base_s, step_s):
    ctx_blk = lambda i: (jnp.minimum(i, NP_TILES - 1), 0)
    lat_seq = lambda i: (T_P // DEC_SEQ + jnp.maximum(i - NP_TILES, 0) // TPS, 0)
    return pl.pallas_call(
        _pos_dft_kernel,
        grid=(N_TILES,),
        in_specs=[pl.BlockSpec((SEQ, 2 * FNET_DIM), ctx_blk)]
        + [_const_spec((SEQ, SEQ))] * 2
        + [pl.BlockSpec((DEC_SEQ, 2 * FNET_DIM), lat_seq)]
        + [_const_spec((TM, DEC_SEQ))] * 2
        + [_const_spec((TPS, DEC_SEQ))] * 2,
        out_specs=pl.BlockSpec((TM, FNET_DIM), lambda i: (i, 0)),
        out_shape=jax.ShapeDtypeStruct((T, FNET_DIM), BF16),
        compiler_params=_params(1),
        name="pos_dft",
    )(y, *tabs_p, y, *base_s, *step_s)


def _odd_merge_kernel(x_ref, attn_ref, f_ref, mod_ref, fw_ref, wo_ref, g_ref, b_ref, rh_ref, rl_ref, tri_ref,
                      cast_ref, xo_ref, info_ref, infot_ref, cnt_ref, cast_out_ref, carry_ref):
    _cast_step([cast_ref], [cast_out_ref])

    @pl.when(pl.program_id(0) == 0)
    def _():
        carry_ref[...] = jnp.zeros_like(carry_ref)

    gate = mod_ref[2:3, :]
    shift2, scale2 = mod_ref[3:4, :], mod_ref[4:5, :]
    fm = _dot(f_ref[...], fw_ref[...]).astype(BF16)
    y = _dot(attn_ref[...], wo_ref[0:N_HEADS * V_DIM, :]) + _dot(fm, wo_ref[N_HEADS * V_DIM:, :])
    x = _layer_norm(ALPHA * x_ref[...] + gate * y, g_ref[...], b_ref[...])
    xo_ref[...] = x
    h = x * (1.0 + scale2) + shift2

    h_hi, h_lo = _split_bf16(h)
    logits = _dot(h_hi, rh_ref[...]) + _dot(h_lo, rh_ref[...]) + _dot(h_hi, rl_ref[...])
    lane = lax.broadcasted_iota(jnp.int32, (TM2, 128), 1)
    neg = jnp.float32(-jnp.inf)
    logits = jnp.where(lane < N_EXPERTS, logits, neg)
    m1 = jnp.max(logits, axis=-1, keepdims=True)
    i1 = jnp.min(jnp.where(logits == m1, lane, 128), axis=-1, keepdims=True)
    rest = jnp.where(lane == i1, neg, logits)
    m2 = jnp.max(rest, axis=-1, keepdims=True)
    i2 = jnp.min(jnp.where(rest == m2, lane, 128), axis=-1, keepdims=True)
    e2 = jnp.exp(m2 - m1)
    w1 = 1.0 / (1.0 + e2)
    w2 = e2 / (1.0 + e2)
    info = jnp.where(lane == 0, w1, 0.0)
    info = jnp.where(lane == 1, w2, info)
    info = jnp.where(lane == 2, i1.astype(F32), info)
    info = jnp.where(lane == 3, i2.astype(F32), info)

    uses = jnp.logical_or(lane == i1, lane == i2)
    seen = _dot(tri_ref[...], jnp.where(uses, 1.0, 0.0).astype(BF16)) + carry_ref[...]
    r1 = jnp.sum(jnp.where(lane == i1, seen, 0.0), axis=-1, keepdims=True)
    r2 = jnp.sum(jnp.where(lane == i2, seen, 0.0), axis=-1, keepdims=True)
    info = jnp.where(lane == 4, r1, info)
    info = jnp.where(lane == 5, r2, info)
    info_ref[...] = info
    infot_ref[...] = info.T
    total = carry_ref[...] + jnp.sum(jnp.where(uses, 1.0, 0.0), axis=0, keepdims=True)
    carry_ref[...] = total
    cnt_ref[...] = jnp.broadcast_to(total, cnt_ref.shape)


def _odd_merge(x, attn, f, mods, fnet_w, w_out, g, b, r_hi, r_lo, cast_src):
    row = lambda i: (i, 0)
    tri = jnp.asarray(np.tril(np.ones((TM2, TM2), np.float32), -1), BF16)
    cast_specs, cast_shapes = _cast_stream([cast_src])
    return pl.pallas_call(
        _odd_merge_kernel,
        grid=(T // TM2,),
        in_specs=[
            pl.BlockSpec((TM2, D), row),
            pl.BlockSpec((TM2, N_HEADS * V_DIM), row),
            pl.BlockSpec((TM2, FNET_DIM), row),
            pl.BlockSpec((None, 6, D), lambda i: (_cond_index(i, TM2), 0, 0)),
            _const_spec((FNET_DIM, FNET_DIM)),
            _const_spec((N_HEADS * V_DIM + FNET_DIM, D)),
            _const_spec((1, D)),
            _const_spec((1, D)),
            _const_spec((D, 128)),
            _const_spec((D, 128)),
            _const_spec((TM2, TM2)),
        ] + cast_specs,
        out_specs=[pl.BlockSpec((TM2, D), row), pl.BlockSpec((TM2, 128), row),
                   pl.BlockSpec((128, TM2), lambda i: (0, i)), pl.BlockSpec((8, 128), lambda i: (i, 0))] + cast_specs,
        out_shape=[jax.ShapeDtypeStruct((T, D), F32), jax.ShapeDtypeStruct((T, 128), F32),
                   jax.ShapeDtypeStruct((128, T), F32), jax.ShapeDtypeStruct((T // TM2 * 8, 128), F32)] + cast_shapes,
        scratch_shapes=[pltpu.VMEM((1, 128), F32)],
        compiler_params=_params(1),
        name="odd_merge_router",
    )(x, attn, f, mods, fnet_w, w_out, g, b, r_hi, r_lo, tri, cast_src)


TM_D = 512
ROW = (8, 128)
ROW_DT = BF16
DMA_UNROLL = 8


def _row_copy(src, s, dst, d, sem):
    return pltpu.make_async_copy(src.at[s], dst.at[d], sem)


CHUNK = 16
LOCAL_ROWS = 2 * TM2 + N_EXPERTS * CHUNK


def _chunk_copies(tile, nch_ref, fn):
    for e in range(N_EXPERTS):
        def body(c, carry, e=e):
            fn(tile * N_EXPERTS + e, c, e % 2)
            return carry

        lax.fori_loop(0, nch_ref[tile * N_EXPERTS + e], body, 0)


def _dispatch_kernel(dest_ref, pad_lo_ref, pad_hi_ref, x_ref, mod_ref, xs_ref, h_ref, zero_ref, sem, zsem):
    i = pl.program_id(0)
    base = i * TM_D
    slot = lax.rem(i, 2)

    @pl.when(i == 0)
    def _():
        zero_ref[...] = jnp.zeros_like(zero_ref)
        for e in range(N_EXPERTS + 1):
            def zissue(r, carry):
                _row_copy(zero_ref, 0, xs_ref, r, zsem).start()
                return carry

            def zdrain(r, carry):
                _row_copy(zero_ref, 0, xs_ref, 0, zsem).wait()
                return carry

            lax.fori_loop(pad_lo_ref[e], pad_hi_ref[e], zissue, 0)
            lax.fori_loop(pad_lo_ref[e], pad_hi_ref[e], zdrain, 0)

    shift2, scale2 = mod_ref[3:4, :], mod_ref[4:5, :]
    h_ref[slot] = (x_ref[...] * (1.0 + scale2) + shift2).astype(ROW_DT).reshape((TM_D,) + ROW)

    def issue(r, carry):
        t = base + r
        _row_copy(h_ref.at[slot], r, xs_ref, dest_ref[t], sem.at[slot]).start(priority=0)
        _row_copy(h_ref.at[slot], r, xs_ref, dest_ref[T + t], sem.at[slot]).start(priority=1)
        return carry

    lax.fori_loop(0, TM_D, issue, 0, unroll=DMA_UNROLL)

    def drain(s):
        def body(r, carry):
            _row_copy(h_ref.at[s], 0, xs_ref, 0, sem.at[s]).wait()
            _row_copy(h_ref.at[s], 0, xs_ref, 0, sem.at[s]).wait()
            return carry

        lax.fori_loop(0, TM_D, body, 0, unroll=DMA_UNROLL)

    @pl.when(i > 0)
    def _():
        drain(1 - slot)

    @pl.when(i == T // TM_D - 1)
    def _():
        drain(slot)


def _dispatch(dest, pad_lo, pad_hi, x, mods):
    return pl.pallas_call(
        _dispatch_kernel,
        grid_spec=pltpu.PrefetchScalarGridSpec(
            num_scalar_prefetch=3,
            grid=(T // TM_D,),
            in_specs=[pl.BlockSpec((TM_D, D), lambda i, *_: (i, 0)),
                      pl.BlockSpec((None, 6, D), lambda i, *_: (_cond_index(i, TM_D), 0, 0))],
            out_specs=pl.BlockSpec(memory_space=pl.ANY),
            scratch_shapes=[pltpu.VMEM((2, TM_D) + ROW, ROW_DT), pltpu.VMEM((1,) + ROW, ROW_DT),
                            pltpu.SemaphoreType.DMA((2,)), pltpu.SemaphoreType.DMA(())],
        ),
        out_shape=jax.ShapeDtypeStruct((R_MAX,) + ROW, ROW_DT),
        compiler_params=pltpu.CompilerParams(dimension_semantics=("arbitrary",), has_side_effects=True),
        name="expert_dispatch",
    )(dest, pad_lo, pad_hi, x, mods)


def _expert_kernel(te_ref, nt_ref, xs_ref, wg_ref, wu_ref, wd_ref, o_ref):
    @pl.when(pl.program_id(0) < nt_ref[0])
    def _():
        h = xs_ref[...].reshape(TM_E, D)
        a = _silu(_dot(h, wg_ref[...])) * _dot(h, wu_ref[...])
        o_ref[...] = _dot(a.astype(BF16), wd_ref[...]).astype(ROW_DT).reshape((TM_E,) + ROW)

    @pl.when(pl.program_id(0) >= nt_ref[0])
    def _():
        o_ref[...] = jnp.zeros_like(o_ref)


def _experts(tile_expert, n_used, xs, wg, wu, wd):
    return pl.pallas_call(
        _expert_kernel,
        grid_spec=pltpu.PrefetchScalarGridSpec(
            num_scalar_prefetch=2,
            grid=(N_ETILES,),
            in_specs=[
                pl.BlockSpec((TM_E,) + ROW, lambda i, te, nt: (jnp.minimum(i, nt[0] - 1), 0, 0)),
                pl.BlockSpec((None, D, D_FF_EXPERT), lambda i, te, nt: (te[i], 0, 0)),
                pl.BlockSpec((None, D, D_FF_EXPERT), lambda i, te, nt: (te[i], 0, 0)),
                pl.BlockSpec((None, D_FF_EXPERT, D), lambda i, te, nt: (te[i], 0, 0)),
            ],
            out_specs=pl.BlockSpec((TM_E,) + ROW, lambda i, te, nt: (i, 0, 0)),
        ),
        out_shape=jax.ShapeDtypeStruct((R_MAX,) + ROW, ROW_DT),
        compiler_params=_params(1),
        name="expert_swiglu",
    )(tile_expert, n_used, xs, wg, wu, wd)


def _combine_kernel(lp_ref, d0_ref, off_ref, nch_ref, x_ref, info_ref, mod_ref, g_ref, b_ref, ys_ref,
                    op_ref, os_ref, local_ref, rows_ref, sem):
    i = pl.program_id(0)
    slot = lax.rem(i, 2)

    def chunk(s, seg, c):
        return pltpu.make_async_copy(ys_ref.at[pl.ds(d0_ref[seg] + c * CHUNK, CHUNK)],
                                     local_ref.at[s, pl.ds(off_ref[seg] + c * CHUNK, CHUNK)], sem.at[s])

    @pl.when(i == 0)
    def _():
        _chunk_copies(0, nch_ref, lambda seg, c, prio: chunk(0, seg, c).start(priority=prio))

    @pl.when(i + 1 < T // TM2)
    def _():
        _chunk_copies(i + 1, nch_ref, lambda seg, c, prio: chunk(1 - slot, seg, c).start(priority=prio))

    _chunk_copies(i, nch_ref, lambda seg, c, prio: chunk(slot, seg, c).wait())

    def pick(r, carry):
        rows_ref[0, r] = local_ref[slot, lp_ref[i * TM2 + r]]
        rows_ref[1, r] = local_ref[slot, lp_ref[T + i * TM2 + r]]
        return carry

    lax.fori_loop(0, TM2, pick, 0, unroll=DMA_UNROLL)

    gate = mod_ref[5:6, :]
    w1, w2 = info_ref[:, 0:1], info_ref[:, 1:2]
    y = w1 * rows_ref[0].reshape(TM2, D).astype(F32) + w2 * rows_ref[1].reshape(TM2, D).astype(F32)
    out = _layer_norm(ALPHA * x_ref[...] + gate * y, g_ref[...], b_ref[...])

    @pl.when(pl.program_id(0) < T_P // TM2)
    def _():
        op_ref[...] = out

    @pl.when(pl.program_id(0) >= T_P // TM2)
    def _():
        os_ref[...] = out


def _combine(lp, d0, off, nch, x, info, mods, g, b, ys):
    return pl.pallas_call(
        _combine_kernel,
        grid_spec=pltpu.PrefetchScalarGridSpec(
            num_scalar_prefetch=4,
            grid=(T // TM2,),
            in_specs=[
                pl.BlockSpec((TM2, D), lambda i, *_: (i, 0)),
                pl.BlockSpec((TM2, 128), lambda i, *_: (i, 0)),
                pl.BlockSpec((None, 6, D), lambda i, *_: (_cond_index(i, TM2), 0, 0)),
                pl.BlockSpec((1, D), lambda i, *_: (0, 0)),
                pl.BlockSpec((1, D), lambda i, *_: (0, 0)),
                pl.BlockSpec(memory_space=pl.ANY),
            ],
            out_specs=[pl.BlockSpec((TM2, D), lambda i, *_: (jnp.minimum(i, T_P // TM2 - 1), 0)),
                       pl.BlockSpec((TM2, D), lambda i, *_: (jnp.maximum(i - T_P // TM2, 0), 0))],
            scratch_shapes=[pltpu.VMEM((2, LOCAL_ROWS) + ROW, ROW_DT), pltpu.VMEM((2, TM2) + ROW, ROW_DT),
                            pltpu.SemaphoreType.DMA((2,))],
        ),
        out_shape=[jax.ShapeDtypeStruct((T_P, D), F32), jax.ShapeDtypeStruct((T_S, D), F32)],
        compiler_params=_params(1),
        name="expert_combine",
    )(lp, d0, off, nch, x, info, mods, g, b, ys)


def _rot_cols(w):
    w4 = w.reshape(w.shape[:-1] + (2, 2, QK_ROPE // 4))
    return jnp.stack([-w4[..., 1, :], w4[..., 0, :]], axis=-2).reshape(w.shape)


def _rope_table():
    rows = DEC_SEQ // GRID_W
    row = np.repeat(np.arange(rows), GRID_W).astype(np.float32)
    col = np.tile(np.arange(GRID_W), rows).astype(np.float32)
    half = QK_ROPE // 2
    inv = (ROPE_THETA ** (-np.arange(0, half, 2, dtype=np.float32) / half)).astype(np.float32)
    ar, ac = row[:, None] * inv, col[:, None] * inv
    ang = np.concatenate([ar, ar, ac, ac], axis=-1)
    cos = np.concatenate([np.ones((TM2, QK_ROPE)), np.cos(ang)], axis=0)
    sin = np.concatenate([np.zeros((TM2, QK_ROPE)), np.sin(ang)], axis=0)
    n = cos.shape[0]
    return jnp.asarray(np.concatenate([cos, np.ones((n, 64)), sin, np.zeros((n, 64))], axis=1), F32)


def _dft_angles(rows, n):
    k = np.arange(n, dtype=np.int64)
    return ((np.asarray(rows, np.int64)[:, None] * k[None, :]) % n) * (2.0 * np.pi / n)


def _dft_tables(n):
    ang = _dft_angles(np.arange(n), n)
    return np.cos(ang) * n ** -0.5, np.sin(ang) * n ** -0.5


def _hi_lo(m):
    m = jnp.asarray(m, F32)
    hi = m.astype(BF16)
    return hi, (m - hi.astype(F32)).astype(BF16)


def _block_diag4(m):
    return np.kron(np.eye(4), m)


def kernel(x_prompt, x_sample, cache_ckv, cache_kpe, c, c_ctx, ada_w, ada_b, ln_g, ln_b, ev_w_in, ev_conv_w, ev_pool_w, ev_pool_scale, ev_w_out, ffn_w_gate, ffn_w_up, ffn_w_down, od_w_in, od_q_norm, od_kv_norm, od_w_q_b, od_w_kv_b, od_fnet_w, od_w_out, moe_router, moe_w_gate, moe_w_up, moe_w_down):
    cond8 = jnp.concatenate([c_ctx[None, :], c, jnp.zeros((8 - N_COND, D), F32)], axis=0)
    mods = _modulation(cond8, ada_w, ada_b)[:, :N_COND].reshape(DEPTH, N_COND, 6, D)

    x, (ffn_wg, ffn_wu, ffn_wd) = _even_mixer(
        x_prompt.reshape(T_P, D), x_sample.reshape(T_S, D), mods[0], ev_w_in[0].astype(BF16), ev_conv_w[0],
        ev_pool_w[0].astype(BF16), ev_pool_scale[0][None, :], ev_w_out[0].astype(BF16),
        ln_g[0, 0][None, :], ln_b[0, 0][None, :], [ffn_w_gate[0], ffn_w_up[0], ffn_w_down[0]])
    x, moe_wd = _ffn(x, mods[0], ffn_wg, ffn_wu, ffn_wd, ln_g[0, 1][None, :], ln_b[0, 1][None, :],
                     moe_w_down[0].reshape(N_EXPERTS * D_FF_EXPERT, D))

    w_in = od_w_in[0]
    w_pe = w_in[:, Q_RANK + KV_RANK:Q_RANK + KV_RANK + QK_ROPE]
    zpad = jnp.zeros((D, 64), F32)
    w_in_ext = jnp.concatenate([w_in[:, :Q_RANK + KV_RANK], w_in[:, Q_RANK + KV_RANK + QK_ROPE:],
                                w_pe, zpad, _rot_cols(w_pe), zpad], axis=1).astype(BF16)
    wq = od_w_q_b[0].reshape(Q_RANK, N_HEADS, QK_NOPE + QK_ROPE)
    zq = jnp.zeros((Q_RANK, N_HEADS, 64), F32)
    wqa = jnp.concatenate([wq, zq], axis=-1).reshape(Q_RANK, N_HEADS * QH).astype(BF16)
    wqb = jnp.concatenate([_rot_cols(wq[..., QK_NOPE:]), zq], axis=-1).reshape(Q_RANK, N_HEADS * 128).astype(BF16)
    wkv = od_w_kv_b[0].reshape(KV_RANK, N_HEADS, QK_NOPE + V_DIM)
    w_kv = jnp.concatenate([wkv[..., :QK_NOPE].reshape(KV_RANK, -1), wkv[..., QK_NOPE:].reshape(KV_RANK, -1)],
                           axis=1).astype(BF16)

    avg = jnp.asarray(_block_diag4(np.full((FNET_GROUP_DIM, FNET_GROUP_DIM), 1.0 / FNET_GROUP_DIM)), BF16)
    cc, sc = _dft_tables(FNET_GROUP_DIM)
    dft_hi, dft_lo = _hi_lo(np.concatenate([_block_diag4(cc), -_block_diag4(sc)], axis=1))

    q, ckv, kpe, y_dft, moe_wg = _odd_proj(x, mods[1], _rope_table(), w_in_ext, od_q_norm[0][None, :],
                                            od_kv_norm[0][None, :], wqa, wqb, avg, dft_hi, dft_lo,
                                            moe_w_gate[0].reshape(N_EXPERTS * D, D_FF_EXPERT))

    cache_kpe128 = jnp.pad(cache_kpe[:, 0], ((0, 0), (0, 0), (0, 128 - QK_ROPE)))
    attn = _attention(q, ckv, kpe, cache_ckv[:, 0], cache_kpe128, w_kv)

    tabs_p = tuple(jnp.asarray(m, F32) for m in _dft_tables(SEQ))
    a_base = _dft_angles(np.arange(TM), DEC_SEQ)
    a_step = _dft_angles(np.arange(TPS) * TM, DEC_SEQ)
    base_s = (jnp.asarray(np.cos(a_base) * DEC_SEQ ** -0.5, F32), jnp.asarray(np.sin(a_base) * DEC_SEQ ** -0.5, F32))
    step_s = (jnp.asarray(np.cos(a_step), F32), jnp.asarray(np.sin(a_step), F32))
    f = _pos_dft(y_dft, tabs_p, base_s, step_s)

    router = jnp.pad(moe_router[0], ((0, 0), (0, 128 - N_EXPERTS)))
    r_hi, r_lo = _hi_lo(router)
    x, info, info_t, cnt, moe_wu = _odd_merge(x, attn, f, mods[1], od_fnet_w[0].astype(BF16),
                                              od_w_out[0].astype(BF16), ln_g[1, 0][None, :], ln_b[1, 0][None, :],
                                              r_hi, r_lo, moe_w_up[0].reshape(N_EXPERTS * D, D_FF_EXPERT))

    n_tiles = T // TM2
    after = cnt.reshape(n_tiles, 8, 128)[:, 0, :N_EXPERTS].astype(jnp.int32)
    before = jnp.concatenate([jnp.zeros((1, N_EXPERTS), jnp.int32), after[:-1]], axis=0)
    counts = after[-1]
    padded = ((counts + TM_E - 1) // TM_E) * TM_E
    g_end = jnp.cumsum(padded)
    g_start = g_end - padded
    n_chunks = (after - before + CHUNK - 1) // CHUNK
    seg_off = CHUNK * (jnp.cumsum(n_chunks, axis=1) - n_chunks)
    seg_dst = g_start[None, :] + before
    choice = info_t[2:4].astype(jnp.int32)
    rank = info_t[4:6].astype(jnp.int32)
    shift_t = jnp.repeat((seg_off - before).T, TM2, axis=1)
    lp, dest = rank, rank
    for e in range(N_EXPERTS):
        lp = lp + jnp.where(choice == e, shift_t[e][None, :], 0)
        dest = dest + jnp.where(choice == e, g_start[e], 0)
    lp = lp.reshape(-1).astype(jnp.int32)
    dest = dest.reshape(-1).astype(jnp.int32)
    d0, off, nch = (a.reshape(-1).astype(jnp.int32) for a in (seg_dst, seg_off, n_chunks))
    tile_row = jnp.arange(N_ETILES, dtype=jnp.int32) * TM_E
    tile_expert = jnp.minimum(jnp.sum((tile_row[:, None] >= g_end[None, :]).astype(jnp.int32), axis=1),
                              N_EXPERTS - 1).astype(jnp.int32)
    n_used = (g_end[-1:] // TM_E).astype(jnp.int32)

    pad_lo = jnp.concatenate([g_start + counts, g_end[-1:]]).astype(jnp.int32)
    pad_hi = jnp.concatenate([g_end, jnp.full((1,), R_MAX, jnp.int32)]).astype(jnp.int32)
    xs = _dispatch(dest, pad_lo, pad_hi, x, mods[1])
    ys = _experts(tile_expert, n_used, xs, moe_wg.reshape(N_EXPERTS, D, D_FF_EXPERT),
                  moe_wu.reshape(N_EXPERTS, D, D_FF_EXPERT), moe_wd.reshape(N_EXPERTS, D_FF_EXPERT, D))
    yp, ysm = _combine(lp, d0, off, nch, x, info, mods[1], ln_g[1, 1][None, :], ln_b[1, 1][None, :], ys)

    y_prompt = yp.reshape(BATCH, SEQ, D)
    y_sample = ysm.reshape(DEC_BATCH, DEC_SEQ, D)
    new_ckv = ckv[:T_P].reshape(BATCH, 1, SEQ, KV_RANK)
    new_kpe = kpe[:T_P, :QK_ROPE].reshape(BATCH, 1, SEQ, QK_ROPE)
    return (y_prompt, y_sample, new_ckv, new_kpe)
```

```python
import functools

import numpy as np
import jax
import jax.numpy as jnp
from jax import lax
from jax.experimental import pallas as pl
from jax.experimental.pallas import tpu as pltpu

F32 = jnp.float32
BF16 = jnp.bfloat16

D = 1024
BATCH, SEQ = 32, 256
DEC_BATCH, DEC_SEQ = 2, 2048
PAST = 512
GRID_W = 64
T_P = BATCH * SEQ
T_S = DEC_BATCH * DEC_SEQ
T = T_P + T_S
N_COND = 1 + DEC_BATCH

CONV_DIM = 512
POOL_WINDOWS = (2, 4, 8, 16)
POOL_GROUP = 128
N_HEADS = 8
QK_NOPE, QK_ROPE, V_DIM = 128, 64, 128
Q_RANK, KV_RANK = 384, 256
FNET_DIM, FNET_GROUP_DIM = 256, 64
D_FF = 2816
N_EXPERTS = 8
D_FF_EXPERT = 1792
DEPTH = 2
ALPHA = (2 * DEPTH) ** 0.25
LN_EPS = 1e-5
RMS_EPS = 1e-6
ROPE_THETA = 10000.0

TM = 256
NP_TILES = T_P // TM
TPS = DEC_SEQ // TM
N_TILES = T // TM
HALO = 8
TM2 = 512
TM_FFN = 512
LK_S = PAST + DEC_SEQ
TM_E = 512
N_ETILES = (2 * T) // TM_E + N_EXPERTS + 1
R_MAX = N_ETILES * TM_E
VMEM_LIMIT = 56 * 1024 * 1024


def _cond_index(i, tm=TM):
    return jnp.where(i < T_P // tm, 0, 1 + (i - T_P // tm) // (DEC_SEQ // tm))


def _const_spec(shape):
    nd = len(shape)
    return pl.BlockSpec(shape, lambda *_: (0,) * nd, pipeline_mode=pl.Buffered(1))


def _params(n_axes=1, vmem=VMEM_LIMIT):
    return pltpu.CompilerParams(dimension_semantics=("arbitrary",) * n_axes, vmem_limit_bytes=vmem)


def _layer_norm(v, g, b):
    mu = jnp.mean(v, axis=-1, keepdims=True)
    d = v - mu
    var = jnp.mean(d * d, axis=-1, keepdims=True)
    return d * lax.rsqrt(var + LN_EPS) * g + b


def _split_bf16(v):
    hi = v.astype(BF16)
    lo = (v - hi.astype(F32)).astype(BF16)
    return hi, lo


def _dot(a, b):
    return jnp.dot(a, b, preferred_element_type=F32)


def _silu(v):
    return v / (1.0 + jnp.exp(-v))


CAST_BLOCKS = 16


def _cast_stream(srcs):
    specs = [pl.BlockSpec((w.shape[0] // CAST_BLOCKS, w.shape[1]),
                          lambda i, *_: (jnp.minimum(i, CAST_BLOCKS - 1), 0)) for w in srcs]
    return specs, [jax.ShapeDtypeStruct(w.shape, BF16) for w in srcs]


def _cast_step(srcs, dsts):
    @pl.when(pl.program_id(0) < CAST_BLOCKS)
    def _():
        for src, dst in zip(srcs, dsts):
            dst[...] = src[...].astype(BF16)


def _mod_kernel(cond_ref, w_ref, b_ref, o_ref):
    s = _silu(cond_ref[...]).astype(BF16)
    o_ref[...] = _dot(s, w_ref[...].astype(BF16)) + b_ref[...]


def _modulation(cond8, ada_w, ada_b):
    nb = 6 * D // 1024
    return pl.pallas_call(
        _mod_kernel,
        grid=(DEPTH, nb),
        in_specs=[
            pl.BlockSpec((8, D), lambda l, j: (0, 0)),
            pl.BlockSpec((None, D, 1024), lambda l, j: (l, 0, j)),
            pl.BlockSpec((None, 1, 1024), lambda l, j: (l, 0, j)),
        ],
        out_specs=pl.BlockSpec((None, 8, 1024), lambda l, j: (l, 0, j)),
        out_shape=jax.ShapeDtypeStruct((DEPTH, 8, 6 * D), F32),
        compiler_params=_params(2),
        name="adaln_modulation",
    )(cond8, ada_w, ada_b.reshape(DEPTH, 1, 6 * D))


def _pool_tables():
    t = np.arange(TM)[:, None]
    r = np.arange(TM + 2 * HALO)[None, :]
    pos = r - HALO
    bands = np.zeros((4, len(POOL_WINDOWS), TM, TM + 2 * HALO), np.float32)
    inv = np.zeros((4, TM, 128), np.float32)
    for variant in range(4):
        left_ok, right_ok = variant & 1, variant >> 1
        col_ok = (r >= (0 if left_ok else HALO)) & (r < (TM + 2 * HALO if right_ok else TM + HALO))
        first, last = (-HALO if left_ok else 0), (TM + HALO if right_ok else TM)
        for gi, w in enumerate(POOL_WINDOWS):
            bands[variant, gi] = (pos >= t - w // 2) & (pos < t + w // 2) & col_ok
            cnt = np.minimum(t[:, 0] + w // 2, last) - np.maximum(t[:, 0] - w // 2, first)
            inv[variant, :, gi] = 1.0 / cnt
    return jnp.asarray(bands, BF16), jnp.asarray(inv, F32)


def _even_tiles(x_alls, left_oks, right_oks, mod_ref, win_ref, convw_ref, poolw_ref, pscale_ref, wout_ref,
                g_ref, b_ref, band_ref, inv_ref):
    shift, scale, gate = mod_ref[0:1, :], mod_ref[1:2, :], mod_ref[2:3, :]
    n = len(x_alls)
    hs = []
    for x_all, left_ok, right_ok in zip(x_alls, left_oks, right_oks):
        h = x_all * (1.0 + scale) + shift
        hs.append(jnp.concatenate([jnp.where(left_ok, h[:HALO], 0.0), h[HALO:HALO + TM],
                                   jnp.where(right_ok, h[HALO + TM:], 0.0)], axis=0).astype(BF16))
    us = [_dot(h, win_ref[...]) for h in hs]

    mixes = []
    for u, left_ok, right_ok in zip(us, left_oks, right_oks):
        ux, ub = u[:, 0:CONV_DIM], u[:, CONV_DIM:2 * CONV_DIM]
        uc, up = u[:, 2 * CONV_DIM:3 * CONV_DIM], u[:, 3 * CONV_DIM:]
        z = uc * ux
        conv = (z[HALO - 1:HALO - 1 + TM] * convw_ref[0:1, :]
                + z[HALO:HALO + TM] * convw_ref[1:2, :]
                + z[HALO + 1:HALO + 1 + TM] * convw_ref[2:3, :])
        ya = ub[HALO:HALO + TM] * conv

        variant = left_ok.astype(jnp.int32) + 2 * right_ok.astype(jnp.int32)
        inv_cnt = inv_ref[variant]
        up_hi, up_lo = _split_bf16(up)
        yb_groups = []
        for gi in range(len(POOL_WINDOWS)):
            band = band_ref[variant, gi]
            sl = slice(gi * POOL_GROUP, (gi + 1) * POOL_GROUP)
            tot = _dot(band, up_hi[:, sl]) + _dot(band, up_lo[:, sl])
            p = tot * inv_cnt[:, gi:gi + 1] - up[HALO:HALO + TM, sl]
            yb_groups.append(_dot(p.astype(BF16), poolw_ref[gi]))
        yb = jnp.concatenate(yb_groups, axis=1) * pscale_ref[...]
        mixes.append(jnp.concatenate([ya, yb], axis=1).astype(BF16))

    ys = [_dot(mix, wout_ref[...]) for mix in mixes]
    return [_layer_norm(ALPHA * x_alls[k][HALO:HALO + TM] + gate * ys[k], g_ref[...], b_ref[...])
            for k in range(n)]


EV_TILES = 2
EV_ROWS = EV_TILES * TM


N_CAST_EVEN = 3


def _even_mixer_kernel(xctx_ref, xprev_ref, xlat_ref, xnext_ref, mod_ref, win_ref, convw_ref, poolw_ref,
                       pscale_ref, wout_ref, g_ref, b_ref, band_ref, inv_ref, *rest):
    n = N_CAST_EVEN
    cast_in, o_ref, cast_out, xall_ref = rest[:n], rest[n], rest[n + 1:2 * n + 1], rest[-1]
    _cast_step(cast_in, cast_out)
    s = pl.program_id(0)
    n_ctx = T_P // EV_ROWS
    is_latent = s >= n_ctx
    first_tile = lax.rem(s - n_ctx, DEC_SEQ // EV_ROWS) * EV_TILES

    @pl.when(jnp.logical_not(is_latent))
    def _():
        for k in range(EV_TILES):
            xall_ref[k, 0:HALO, :] = jnp.zeros((HALO, D), F32)
            xall_ref[k, HALO:HALO + TM, :] = xctx_ref[k * TM:(k + 1) * TM, :]
            xall_ref[k, HALO + TM:, :] = jnp.zeros((HALO, D), F32)

    @pl.when(is_latent)
    def _():
        for k in range(EV_TILES):
            lo, hi = k * TM, (k + 1) * TM
            xall_ref[k, 0:HALO, :] = xprev_ref[...] if k == 0 else xlat_ref[lo - HALO:lo, :]
            xall_ref[k, HALO:HALO + TM, :] = xlat_ref[lo:hi, :]
            xall_ref[k, HALO + TM:, :] = xnext_ref[...] if k == EV_TILES - 1 else xlat_ref[hi:hi + HALO, :]

    left_oks = [jnp.logical_and(is_latent, first_tile + k != 0) for k in range(EV_TILES)]
    right_oks = [jnp.logical_and(is_latent, first_tile + k != TPS - 1) for k in range(EV_TILES)]
    outs = _even_tiles([xall_ref[k] for k in range(EV_TILES)], left_oks, right_oks, mod_ref, win_ref,
                       convw_ref, poolw_ref, pscale_ref, wout_ref, g_ref, b_ref, band_ref, inv_ref)
    for k in range(EV_TILES):
        o_ref[k * TM:(k + 1) * TM, :] = outs[k]


def _even_mixer(x_ctx, x_lat, mods, w_in, conv_w, pool_w, pool_scale, w_out, g, b, cast_srcs):
    assert len(cast_srcs) == N_CAST_EVEN
    hb = EV_ROWS // HALO
    n8 = T_S // HALO
    n_ctx = T_P // EV_ROWS
    lat = lambda i: jnp.maximum(i - n_ctx, 0)
    cast_specs, cast_shapes = _cast_stream(cast_srcs)
    bands, inv_cnt = _pool_tables()
    outs = pl.pallas_call(
        _even_mixer_kernel,
        grid=(T // EV_ROWS,),
        in_specs=[
            pl.BlockSpec((EV_ROWS, D), lambda i: (jnp.minimum(i, n_ctx - 1), 0)),
            pl.BlockSpec((HALO, D), lambda i: (jnp.maximum(lat(i) * hb - 1, 0), 0)),
            pl.BlockSpec((EV_ROWS, D), lambda i: (lat(i), 0)),
            pl.BlockSpec((HALO, D), lambda i: (jnp.minimum((lat(i) + 1) * hb, n8 - 1), 0)),
            pl.BlockSpec((None, 6, D), lambda i: (_cond_index(i, EV_ROWS), 0, 0)),
            _const_spec((D, 4 * CONV_DIM)),
            _const_spec((3, CONV_DIM)),
            _const_spec((4, POOL_GROUP, POOL_GROUP)),
            _const_spec((1, 4 * POOL_GROUP)),
            _const_spec((D, D)),
            _const_spec((1, D)),
            _const_spec((1, D)),
            _const_spec(bands.shape),
            _const_spec(inv_cnt.shape),
        ] + cast_specs,
        out_specs=[pl.BlockSpec((EV_ROWS, D), lambda i: (i, 0))] + cast_specs,
        out_shape=[jax.ShapeDtypeStruct((T, D), F32)] + cast_shapes,
        scratch_shapes=[pltpu.VMEM((EV_TILES, TM + 2 * HALO, D), F32)],
        compiler_params=_params(1),
        name="even_mixer",
    )(x_ctx, x_lat, x_lat, x_lat, mods, w_in, conv_w, pool_w, pool_scale, w_out, g, b, bands, inv_cnt, *cast_srcs)
    return outs[0], outs[1:]


FF_CHUNK = D_FF // 2


def _ffn_kernel(x_ref, mod_ref, wg_ref, wu_ref, wd_ref, g_ref, b_ref, cast_ref, o_ref, cast_out_ref):
    _cast_step([cast_ref], [cast_out_ref])
    shift, scale, gate = mod_ref[3:4, :], mod_ref[4:5, :], mod_ref[5:6, :]
    x = x_ref[...]
    h = (x * (1.0 + scale) + shift).astype(BF16)
    f = jnp.zeros((TM_FFN, D), F32)
    for c in range(D_FF // FF_CHUNK):
        sl = slice(c * FF_CHUNK, (c + 1) * FF_CHUNK)
        a = _silu(_dot(h, wg_ref[:, sl])) * _dot(h, wu_ref[:, sl])
        f = f + _dot(a.astype(BF16), wd_ref[sl, :])
    o_ref[...] = _layer_norm(ALPHA * x + gate * f, g_ref[...], b_ref[...])


def _ffn(x, mods, wg, wu, wd, g, b, cast_src):
    cast_specs, cast_shapes = _cast_stream([cast_src])
    return pl.pallas_call(
        _ffn_kernel,
        grid=(T // TM_FFN,),
        in_specs=[
            pl.BlockSpec((TM_FFN, D), lambda i: (i, 0)),
            pl.BlockSpec((None, 6, D), lambda i: (_cond_index(i, TM_FFN), 0, 0)),
            _const_spec((D, D_FF)),
            _const_spec((D, D_FF)),
            _const_spec((D_FF, D)),
            _const_spec((1, D)),
            _const_spec((1, D)),
        ] + cast_specs,
        out_specs=[pl.BlockSpec((TM_FFN, D), lambda i: (i, 0))] + cast_specs,
        out_shape=[jax.ShapeDtypeStruct((T, D), F32)] + cast_shapes,
        compiler_params=_params(1),
        name="dense_swiglu",
    )(x, mods, wg, wu, wd, g, b, cast_src)


W_IN_EXT = Q_RANK + KV_RANK + FNET_DIM + 128 + 128
QH = 256
ATT_SCALE = (QK_NOPE + QK_ROPE) ** -0.5 * float(np.log2(np.e))


def _odd_proj_kernel(x_ref, mod_ref, rope_ref, win_ref, qn_ref, kvn_ref, wqa_ref, wqb_ref,
                     avg_ref, dfth_ref, dftl_ref, cast_ref,
                     q_ref, ckv_ref, kpe_ref, y_ref, cast_out_ref):
    _cast_step([cast_ref], [cast_out_ref])
    shift, scale = mod_ref[0:1, :], mod_ref[1:2, :]
    h = (x_ref[...] * (1.0 + scale) + shift).astype(BF16)
    u = _dot(h, win_ref[...])
    uq = u[:, 0:Q_RANK]
    ukv = u[:, Q_RANK:Q_RANK + KV_RANK]
    uf = u[:, Q_RANK + KV_RANK:Q_RANK + KV_RANK + FNET_DIM]
    o = Q_RANK + KV_RANK + FNET_DIM
    upe, upe_rot = u[:, o:o + 128], u[:, o + 128:o + 256]
    cos, sin = rope_ref[:, 0:128], rope_ref[:, 128:256]

    ckv_ref[...] = ukv * lax.rsqrt(jnp.mean(ukv * ukv, axis=-1, keepdims=True) + RMS_EPS) * kvn_ref[...]
    kpe_ref[...] = upe * cos + upe_rot * sin

    qlat = (uq * lax.rsqrt(jnp.mean(uq * uq, axis=-1, keepdims=True) + RMS_EPS) * qn_ref[...]).astype(BF16)
    qa = _dot(qlat, wqa_ref[...])
    qb = _dot(qlat, wqb_ref[...])
    for hd in range(N_HEADS):
        nope = qa[:, hd * QH:hd * QH + 128]
        pe = qa[:, hd * QH + 128:(hd + 1) * QH] * cos + qb[:, hd * 128:(hd + 1) * 128] * sin
        q_ref[:, hd * QH:hd * QH + 128] = (nope * ATT_SCALE).astype(BF16)
        q_ref[:, hd * QH + 128:(hd + 1) * QH] = (pe * ATT_SCALE).astype(BF16)

    avg = avg_ref[...]
    uf_hi, uf_lo = _split_bf16(uf)
    mu = _dot(uf_hi, avg) + _dot(uf_lo, avg)
    dlt = uf - mu
    sq_hi, sq_lo = _split_bf16(dlt * dlt)
    var = _dot(sq_hi, avg) + _dot(sq_lo, avg)
    xn = dlt * lax.rsqrt(var + LN_EPS)
    xn_hi, xn_lo = _split_bf16(xn)
    y = _dot(xn_hi, dfth_ref[...]) + _dot(xn_lo, dfth_ref[...]) + _dot(xn_hi, dftl_ref[...])
    y_ref[...] = y.astype(BF16)


def _odd_proj(x, mods, rope_tab, w_in_ext, q_norm, kv_norm, wqa, wqb, avg, dft_hi, dft_lo, cast_src):
    cast_specs, cast_shapes = _cast_stream([cast_src])

    def rope_index(i):
        return (jnp.where(i < T_P // TM2, 0, 1 + lax.rem(i - T_P // TM2, DEC_SEQ // TM2)), 0)

    return pl.pallas_call(
        _odd_proj_kernel,
        grid=(T // TM2,),
        in_specs=[
            pl.BlockSpec((TM2, D), lambda i: (i, 0)),
            pl.BlockSpec((None, 6, D), lambda i: (_cond_index(i, TM2), 0, 0)),
            pl.BlockSpec((TM2, 256), rope_index),
            _const_spec((D, W_IN_EXT)),
            _const_spec((1, Q_RANK)),
            _const_spec((1, KV_RANK)),
            _const_spec((Q_RANK, N_HEADS * QH)),
            _const_spec((Q_RANK, N_HEADS * 128)),
            _const_spec((FNET_DIM, FNET_DIM)),
            _const_spec((FNET_DIM, 2 * FNET_DIM)),
            _const_spec((FNET_DIM, 2 * FNET_DIM)),
        ] + cast_specs,
        out_specs=[
            pl.BlockSpec((TM2, N_HEADS * QH), lambda i: (i, 0)),
            pl.BlockSpec((TM2, KV_RANK), lambda i: (i, 0)),
            pl.BlockSpec((TM2, 128), lambda i: (i, 0)),
            pl.BlockSpec((TM2, 2 * FNET_DIM), lambda i: (i, 0)),
        ] + cast_specs,
        out_shape=[
            jax.ShapeDtypeStruct((T, N_HEADS * QH), BF16),
            jax.ShapeDtypeStruct((T, KV_RANK), F32),
            jax.ShapeDtypeStruct((T, 128), F32),
            jax.ShapeDtypeStruct((T, 2 * FNET_DIM), BF16),
        ] + cast_shapes,
        compiler_params=_params(1),
        name="odd_projections",
    )(x, mods, rope_tab, w_in_ext, q_norm, kv_norm, wqa, wqb, avg, dft_hi, dft_lo, cast_src)


V_OFF = N_HEADS * QH
KV_COLS = V_OFF + N_HEADS * V_DIM


def _attn_body(q_ref, kv_ref, o_refs):
    for hd in range(N_HEADS):
        qh = q_ref[:, hd * QH:(hd + 1) * QH]
        kh = kv_ref[:, hd * QH:(hd + 1) * QH]
        s = lax.dot_general(qh, kh, (((1,), (1,)), ((), ())), preferred_element_type=F32)
        p = jnp.exp2(s - jnp.max(s, axis=-1, keepdims=True))
        den = jnp.sum(p, axis=-1, keepdims=True)
        vh = kv_ref[:, V_OFF + hd * V_DIM:V_OFF + (hd + 1) * V_DIM]
        o = (_dot(p.astype(BF16), vh) / den).astype(BF16)
        for k, o_ref in enumerate(o_refs):
            o_ref[:, hd * 128:(hd + 1) * 128] = o[k * TM:(k + 1) * TM]


KV_CHUNK = 512
LAT_PAIR = 2


def _attn_kernel(q_ref, qpair_ref, ckvp_ref, kpep_ref, ckvs_ref, kpes_ref, cckv_ref, ckpe_ref, wkv_ref, o_ref,
                 kvp_s, kvs_s, held_s):
    i = pl.program_id(0)

    def expand(ckv, kpe, dst, row0):
        n = ckv.shape[0]
        kv = _dot(ckv.astype(BF16), wkv_ref[...]).astype(BF16)
        kpe = kpe.astype(BF16)
        for hd in range(N_HEADS):
            dst[row0:row0 + n, hd * QH:hd * QH + QK_NOPE] = kv[:, hd * QK_NOPE:(hd + 1) * QK_NOPE]
            dst[row0:row0 + n, hd * QH + QK_NOPE:(hd + 1) * QH] = kpe
        dst[row0:row0 + n, V_OFF:] = kv[:, N_HEADS * QK_NOPE:]

    @pl.when(i < NP_TILES)
    def _():
        expand(ckvp_ref[...], kpep_ref[...], kvp_s, 0)
        _attn_body(q_ref, kvp_s, [o_ref])

    @pl.when(jnp.logical_and(i >= NP_TILES, lax.rem(i - NP_TILES, LAT_PAIR) == 0))
    def _():
        @pl.when(lax.rem(i - NP_TILES, TPS) == 0)
        def _():
            expand(cckv_ref[...], ckpe_ref[...], kvs_s, 0)
            for c in range(DEC_SEQ // KV_CHUNK):
                rows = slice(c * KV_CHUNK, (c + 1) * KV_CHUNK)
                expand(ckvs_ref[rows, :], kpes_ref[rows, :], kvs_s, PAST + c * KV_CHUNK)

        _attn_body(qpair_ref, kvs_s, [o_ref, held_s])

    @pl.when(jnp.logical_and(i >= NP_TILES, lax.rem(i - NP_TILES, LAT_PAIR) == 1))
    def _():
        o_ref[...] = held_s[...]


def _attention(q, ckv, kpe, cache_ckv, cache_kpe128, w_kv):
    ctx_blk = lambda i: (jnp.minimum(i, NP_TILES - 1), 0)
    lat_b = lambda i: jnp.maximum(i - NP_TILES, 0) // TPS
    lat_blk = lambda i: (T_P // DEC_SEQ + lat_b(i), 0)
    return pl.pallas_call(
        _attn_kernel,
        grid=(N_TILES,),
        in_specs=[
            pl.BlockSpec((TM, N_HEADS * QH), ctx_blk),
            pl.BlockSpec((LAT_PAIR * TM, N_HEADS * QH),
                         lambda i: (NP_TILES // LAT_PAIR + jnp.maximum(i - NP_TILES, 0) // LAT_PAIR, 0)),
            pl.BlockSpec((SEQ, KV_RANK), ctx_blk),
            pl.BlockSpec((SEQ, 128), ctx_blk),
            pl.BlockSpec((DEC_SEQ, KV_RANK), lat_blk, pipeline_mode=pl.Buffered(1)),
            pl.BlockSpec((DEC_SEQ, 128), lat_blk, pipeline_mode=pl.Buffered(1)),
            pl.BlockSpec((None, PAST, KV_RANK), lambda i: (lat_b(i), 0, 0)),
            pl.BlockSpec((None, PAST, 128), lambda i: (lat_b(i), 0, 0)),
            _const_spec((KV_RANK, 2 * N_HEADS * 128)),
        ],
        out_specs=pl.BlockSpec((TM, N_HEADS * V_DIM), lambda i: (i, 0)),
        out_shape=jax.ShapeDtypeStruct((T, N_HEADS * V_DIM), BF16),
        scratch_shapes=[pltpu.VMEM((SEQ, KV_COLS), BF16), pltpu.VMEM((LK_S, KV_COLS), BF16),
                        pltpu.VMEM((TM, N_HEADS * V_DIM), BF16)],
        compiler_params=_params(1),
        name="attention",
    )(q, q, ckv, kpe, ckv, kpe, cache_ckv, cache_kpe128, w_kv)


def _pos_dft_body(y_ref, c, s, o_ref):
    f = _dot(c, y_ref[:, 0:FNET_DIM]) + _dot(s, y_ref[:, FNET_DIM:])
    o_ref[...] = f.astype(BF16)


def _pos_dft_kernel(yp_ref, cp_ref, sp_ref, ys_ref, c0_ref, s0_ref, cb_ref, sb_ref, o_ref):
    @pl.when(pl.program_id(0) < NP_TILES)
    def _():
        _pos_dft_body(yp_ref, cp_ref[...].astype(BF16), sp_ref[...].astype(BF16), o_ref)

    @pl.when(pl.program_id(0) >= NP_TILES)
    def _():
        j = lax.rem(pl.program_id(0) - NP_TILES, TPS)
        cb, sb = cb_ref[pl.ds(j, 1), :], sb_ref[pl.ds(j, 1), :]
        c0, s0 = c0_ref[...], s0_ref[...]
        _pos_dft_body(ys_ref, (c0 * cb - s0 * sb).astype(BF16), (s0 * cb + c0 * sb).astype(BF16), o_ref)


def _pos_dft(y, tabs_p, base_s, step_s):
    ctx_blk = lambda i: (jnp.minimum(i, NP_TILES - 1), 0)
    lat_seq = lambda i: (T_P // DEC_SEQ + jnp.maximum(i - NP_TILES, 0) // TPS, 0)
    return pl.pallas_call(
        _pos_dft_kernel,
        grid=(N_TILES,),
        in_specs=[pl.BlockSpec((SEQ, 2 * FNET_DIM), ctx_blk)]
        + [_const_spec((SEQ, SEQ))] * 2
        + [pl.BlockSpec((DEC_SEQ, 2 * FNET_DIM), lat_seq)]
        + [_const_spec((TM, DEC_SEQ))] * 2
        + [_const_spec((TPS, DEC_SEQ))] * 2,
        out_specs=pl.BlockSpec((TM, FNET_DIM), lambda i: (i, 0)),
        out_shape=jax.ShapeDtypeStruct((T, FNET_DIM), BF16),
        compiler_params=_params(1),
        name="pos_dft",
    )(y, *tabs_p, y, *base_s, *step_s)


def _odd_merge_kernel(x_ref, attn_ref, f_ref, mod_ref, fw_ref, wo_ref, g_ref, b_ref, rh_ref, rl_ref, tri_ref,
                      cast_ref, xo_ref, info_ref, infot_ref, cnt_ref, cast_out_ref, carry_ref):
    _cast_step([cast_ref], [cast_out_ref])

    @pl.when(pl.program_id(0) == 0)
    def _():
        carry_ref[...] = jnp.zeros_like(carry_ref)

    gate = mod_ref[2:3, :]
    shift2, scale2 = mod_ref[3:4, :], mod_ref[4:5, :]
    fm = _dot(f_ref[...], fw_ref[...]).astype(BF16)
    y = _dot(attn_ref[...], wo_ref[0:N_HEADS * V_DIM, :]) + _dot(fm, wo_ref[N_HEADS * V_DIM:, :])
    x = _layer_norm(ALPHA * x_ref[...] + gate * y, g_ref[...], b_ref[...])
    xo_ref[...] = x
    h = x * (1.0 + scale2) + shift2

    h_hi, h_lo = _split_bf16(h)
    logits = _dot(h_hi, rh_ref[...]) + _dot(h_lo, rh_ref[...]) + _dot(h_hi, rl_ref[...])
    lane = lax.broadcasted_iota(jnp.int32, (TM2, 128), 1)
    neg = jnp.float32(-jnp.inf)
    logits = jnp.where(lane < N_EXPERTS, logits, neg)
    m1 = jnp.max(logits, axis=-1, keepdims=True)
    i1 = jnp.min(jnp.where(logits == m1, lane, 128), axis=-1, keepdims=True)
    rest = jnp.where(lane == i1, neg, logits)
    m2 = jnp.max(rest, axis=-1, keepdims=True)
    i2 = jnp.min(jnp.where(rest == m2, lane, 128), axis=-1, keepdims=True)
    e2 = jnp.exp(m2 - m1)
    w1 = 1.0 / (1.0 + e2)
    w2 = e2 / (1.0 + e2)
    info = jnp.where(lane == 0, w1, 0.0)
    info = jnp.where(lane == 1, w2, info)
    info = jnp.where(lane == 2, i1.astype(F32), info)
    info = jnp.where(lane == 3, i2.astype(F32), info)

    uses = jnp.logical_or(lane == i1, lane == i2)
    seen = _dot(tri_ref[...], jnp.where(uses, 1.0, 0.0).astype(BF16)) + carry_ref[...]
    r1 = jnp.sum(jnp.where(lane == i1, seen, 0.0), axis=-1, keepdims=True)
    r2 = jnp.sum(jnp.where(lane == i2, seen, 0.0), axis=-1, keepdims=True)
    info = jnp.where(lane == 4, r1, info)
    info = jnp.where(lane == 5, r2, info)
    info_ref[...] = info
    infot_ref[...] = info.T
    total = carry_ref[...] + jnp.sum(jnp.where(uses, 1.0, 0.0), axis=0, keepdims=True)
    carry_ref[...] = total
    cnt_ref[...] = jnp.broadcast_to(total, cnt_ref.shape)


def _odd_merge(x, attn, f, mods, fnet_w, w_out, g, b, r_hi, r_lo, cast_src):
    row = lambda i: (i, 0)
    tri = jnp.asarray(np.tril(np.ones((TM2, TM2), np.float32), -1), BF16)
    cast_specs, cast_shapes = _cast_stream([cast_src])
    return pl.pallas_call(
        _odd_merge_kernel,
        grid=(T // TM2,),
        in_specs=[
            pl.BlockSpec((TM2, D), row),
            pl.BlockSpec((TM2, N_HEADS * V_DIM), row),
            pl.BlockSpec((TM2, FNET_DIM), row),
            pl.BlockSpec((None, 6, D), lambda i: (_cond_index(i, TM2), 0, 0)),
            _const_spec((FNET_DIM, FNET_DIM)),
            _const_spec((N_HEADS * V_DIM + FNET_DIM, D)),
            _const_spec((1, D)),
            _const_spec((1, D)),
            _const_spec((D, 128)),
            _const_spec((D, 128)),
            _const_spec((TM2, TM2)),
        ] + cast_specs,
        out_specs=[pl.BlockSpec((TM2, D), row), pl.BlockSpec((TM2, 128), row),
                   pl.BlockSpec((128, TM2), lambda i: (0, i)), pl.BlockSpec((8, 128), lambda i: (i, 0))] + cast_specs,
        out_shape=[jax.ShapeDtypeStruct((T, D), F32), jax.ShapeDtypeStruct((T, 128), F32),
                   jax.ShapeDtypeStruct((128, T), F32), jax.ShapeDtypeStruct((T // TM2 * 8, 128), F32)] + cast_shapes,
        scratch_shapes=[pltpu.VMEM((1, 128), F32)],
        compiler_params=_params(1),
        name="odd_merge_router",
    )(x, attn, f, mods, fnet_w, w_out, g, b, r_hi, r_lo, tri, cast_src)


TM_D = 512
ROW = (8, 128)
ROW_DT = BF16
DMA_UNROLL = 8
ZERO_ROWS = 32


def _row_copy(src, s, dst, d, sem):
    return pltpu.make_async_copy(src.at[s], dst.at[d], sem)


CHUNK = 16
LOCAL_ROWS = 2 * TM2 + N_EXPERTS * CHUNK


def _chunk_copies(tile, nch_ref, fn):
    for e in range(N_EXPERTS):
        def body(c, carry, e=e):
            fn(tile * N_EXPERTS + e, c, e % 2)
            return carry

        lax.fori_loop(0, nch_ref[tile * N_EXPERTS + e], body, 0)


def _dispatch_kernel(dest_ref, pad_lo_ref, pad_hi_ref, x_ref, mod_ref, xs_ref, h_ref, zero_ref, sem, zsem):
    i = pl.program_id(0)
    base = i * TM_D
    slot = lax.rem(i, 2)

    @pl.when(i == 0)
    def _():
        zero_ref[...] = jnp.zeros_like(zero_ref)
        for e in range(N_EXPERTS + 1):
            lo, hi = pad_lo_ref[e], pad_hi_ref[e]
            n_blocks = (hi - lo) // ZERO_ROWS
            tail = lo + n_blocks * ZERO_ROWS

            def zblock(c, lo=lo):
                return pltpu.make_async_copy(zero_ref, xs_ref.at[pl.ds(lo + c * ZERO_ROWS, ZERO_ROWS)], zsem)

            def zissue(c, carry):
                zblock(c).start()
                return carry

            def zdrain(c, carry):
                zblock(c).wait()
                return carry

            def rissue(r, carry):
                _row_copy(zero_ref, 0, xs_ref, r, zsem).start()
                return carry

            def rdrain(r, carry):
                _row_copy(zero_ref, 0, xs_ref, 0, zsem).wait()
                return carry

            lax.fori_loop(0, n_blocks, zissue, 0)
            lax.fori_loop(tail, hi, rissue, 0)
            lax.fori_loop(0, n_blocks, zdrain, 0)
            lax.fori_loop(tail, hi, rdrain, 0)

    shift2, scale2 = mod_ref[3:4, :], mod_ref[4:5, :]
    h_ref[slot] = (x_ref[...] * (1.0 + scale2) + shift2).astype(ROW_DT).reshape((TM_D,) + ROW)

    def issue(r, carry):
        t = base + r
        _row_copy(h_ref.at[slot], r, xs_ref, dest_ref[t], sem.at[slot]).start(priority=0)
        _row_copy(h_ref.at[slot], r, xs_ref, dest_ref[T + t], sem.at[slot]).start(priority=1)
        return carry

    lax.fori_loop(0, TM_D, issue, 0, unroll=DMA_UNROLL)

    def drain(s):
        def body(r, carry):
            _row_copy(h_ref.at[s], 0, xs_ref, 0, sem.at[s]).wait()
            _row_copy(h_ref.at[s], 0, xs_ref, 0, sem.at[s]).wait()
            return carry

        lax.fori_loop(0, TM_D, body, 0, unroll=DMA_UNROLL)

    @pl.when(i > 0)
    def _():
        drain(1 - slot)

    @pl.when(i == T // TM_D - 1)
    def _():
        drain(slot)


def _dispatch(dest, pad_lo, pad_hi, x, mods):
    return pl.pallas_call(
        _dispatch_kernel,
        grid_spec=pltpu.PrefetchScalarGridSpec(
            num_scalar_prefetch=3,
            grid=(T // TM_D,),
            in_specs=[pl.BlockSpec((TM_D, D), lambda i, *_: (i, 0)),
                      pl.BlockSpec((None, 6, D), lambda i, *_: (_cond_index(i, TM_D), 0, 0))],
            out_specs=pl.BlockSpec(memory_space=pl.ANY),
            scratch_shapes=[pltpu.VMEM((2, TM_D) + ROW, ROW_DT), pltpu.VMEM((ZERO_ROWS,) + ROW, ROW_DT),
                            pltpu.SemaphoreType.DMA((2,)), pltpu.SemaphoreType.DMA(())],
        ),
        out_shape=jax.ShapeDtypeStruct((R_MAX,) + ROW, ROW_DT),
        compiler_params=pltpu.CompilerParams(dimension_semantics=("arbitrary",), has_side_effects=True),
        name="expert_dispatch",
    )(dest, pad_lo, pad_hi, x, mods)


def _expert_kernel(te_ref, nt_ref, xs_ref, wg_ref, wu_ref, wd_ref, o_ref):
    @pl.when(pl.program_id(0) < nt_ref[0])
    def _():
        h = xs_ref[...].reshape(TM_E, D)
        a = _silu(_dot(h, wg_ref[...])) * _dot(h, wu_ref[...])
        o_ref[...] = _dot(a.astype(BF16), wd_ref[...]).astype(ROW_DT).reshape((TM_E,) + ROW)

    @pl.when(pl.program_id(0) >= nt_ref[0])
    def _():
        o_ref[...] = jnp.zeros_like(o_ref)


def _experts(tile_expert, n_used, xs, wg, wu, wd):
    return pl.pallas_call(
        _expert_kernel,
        grid_spec=pltpu.PrefetchScalarGridSpec(
            num_scalar_prefetch=2,
            grid=(N_ETILES,),
            in_specs=[
                pl.BlockSpec((TM_E,) + ROW, lambda i, te, nt: (jnp.minimum(i, nt[0] - 1), 0, 0)),
                pl.BlockSpec((None, D, D_FF_EXPERT), lambda i, te, nt: (te[i], 0, 0)),
                pl.BlockSpec((None, D, D_FF_EXPERT), lambda i, te, nt: (te[i], 0, 0)),
                pl.BlockSpec((None, D_FF_EXPERT, D), lambda i, te, nt: (te[i], 0, 0)),
            ],
            out_specs=pl.BlockSpec((TM_E,) + ROW, lambda i, te, nt: (i, 0, 0)),
        ),
        out_shape=jax.ShapeDtypeStruct((R_MAX,) + ROW, ROW_DT),
        compiler_params=_params(1),
        name="expert_swiglu",
    )(tile_expert, n_used, xs, wg, wu, wd)


def _combine_kernel(lp_ref, d0_ref, off_ref, nch_ref, x_ref, info_ref, mod_ref, g_ref, b_ref, ys_ref,
                    op_ref, os_ref, local_ref, rows_ref, sem):
    i = pl.program_id(0)
    slot = lax.rem(i, 2)

    def chunk(s, seg, c):
        return pltpu.make_async_copy(ys_ref.at[pl.ds(d0_ref[seg] + c * CHUNK, CHUNK)],
                                     local_ref.at[s, pl.ds(off_ref[seg] + c * CHUNK, CHUNK)], sem.at[s])

    @pl.when(i == 0)
    def _():
        _chunk_copies(0, nch_ref, lambda seg, c, prio: chunk(0, seg, c).start(priority=prio))

    @pl.when(i + 1 < T // TM2)
    def _():
        _chunk_copies(i + 1, nch_ref, lambda seg, c, prio: chunk(1 - slot, seg, c).start(priority=prio))

    _chunk_copies(i, nch_ref, lambda seg, c, prio: chunk(slot, seg, c).wait())

    def pick(r, carry):
        rows_ref[0, r] = local_ref[slot, lp_ref[i * TM2 + r]]
        rows_ref[1, r] = local_ref[slot, lp_ref[T + i * TM2 + r]]
        return carry

    lax.fori_loop(0, TM2, pick, 0, unroll=DMA_UNROLL)

    gate = mod_ref[5:6, :]
    w1, w2 = info_ref[:, 0:1], info_ref[:, 1:2]
    y = w1 * rows_ref[0].reshape(TM2, D).astype(F32) + w2 * rows_ref[1].reshape(TM2, D).astype(F32)
    out = _layer_norm(ALPHA * x_ref[...] + gate * y, g_ref[...], b_ref[...])

    @pl.when(pl.program_id(0) < T_P // TM2)
    def _():
        op_ref[...] = out

    @pl.when(pl.program_id(0) >= T_P // TM2)
    def _():
        os_ref[...] = out


def _combine(lp, d0, off, nch, x, info, mods, g, b, ys):
    return pl.pallas_call(
        _combine_kernel,
        grid_spec=pltpu.PrefetchScalarGridSpec(
            num_scalar_prefetch=4,
            grid=(T // TM2,),
            in_specs=[
                pl.BlockSpec((TM2, D), lambda i, *_: (i, 0)),
                pl.BlockSpec((TM2, 128), lambda i, *_: (i, 0)),
                pl.BlockSpec((None, 6, D), lambda i, *_: (_cond_index(i, TM2), 0, 0)),
                pl.BlockSpec((1, D), lambda i, *_: (0, 0)),
                pl.BlockSpec((1, D), lambda i, *_: (0, 0)),
                pl.BlockSpec(memory_space=pl.ANY),
            ],
            out_specs=[pl.BlockSpec((TM2, D), lambda i, *_: (jnp.minimum(i, T_P // TM2 - 1), 0)),
                       pl.BlockSpec((TM2, D), lambda i, *_: (jnp.maximum(i - T_P // TM2, 0), 0))],
            scratch_shapes=[pltpu.VMEM((2, LOCAL_ROWS) + ROW, ROW_DT), pltpu.VMEM((2, TM2) + ROW, ROW_DT),
                            pltpu.SemaphoreType.DMA((2,))],
        ),
        out_shape=[jax.ShapeDtypeStruct((T_P, D), F32), jax.ShapeDtypeStruct((T_S, D), F32)],
        compiler_params=_params(1),
        name="expert_combine",
    )(lp, d0, off, nch, x, info, mods, g, b, ys)


def _rot_cols(w):
    w4 = w.reshape(w.shape[:-1] + (2, 2, QK_ROPE // 4))
    return jnp.stack([-w4[..., 1, :], w4[..., 0, :]], axis=-2).reshape(w.shape)


def _rope_table():
    rows = DEC_SEQ // GRID_W
    row = np.repeat(np.arange(rows), GRID_W).astype(np.float32)
    col = np.tile(np.arange(GRID_W), rows).astype(np.float32)
    half = QK_ROPE // 2
    inv = (ROPE_THETA ** (-np.arange(0, half, 2, dtype=np.float32) / half)).astype(np.float32)
    ar, ac = row[:, None] * inv, col[:, None] * inv
    ang = np.concatenate([ar, ar, ac, ac], axis=-1)
    cos = np.concatenate([np.ones((TM2, QK_ROPE)), np.cos(ang)], axis=0)
    sin = np.concatenate([np.zeros((TM2, QK_ROPE)), np.sin(ang)], axis=0)
    n = cos.shape[0]
    return jnp.asarray(np.concatenate([cos, np.ones((n, 64)), sin, np.zeros((n, 64))], axis=1), F32)


def _dft_angles(rows, n):
    k = np.arange(n, dtype=np.int64)
    return ((np.asarray(rows, np.int64)[:, None] * k[None, :]) % n) * (2.0 * np.pi / n)


def _dft_tables(n):
    ang = _dft_angles(np.arange(n), n)
    return np.cos(ang) * n ** -0.5, np.sin(ang) * n ** -0.5


def _hi_lo(m):
    m = jnp.asarray(m, F32)
    hi = m.astype(BF16)
    return hi, (m - hi.astype(F32)).astype(BF16)


def _block_diag4(m):
    return np.kron(np.eye(4), m)


def kernel(x_prompt, x_sample, cache_ckv, cache_kpe, c, c_ctx, ada_w, ada_b, ln_g, ln_b, ev_w_in, ev_conv_w, ev_pool_w, ev_pool_scale, ev_w_out, ffn_w_gate, ffn_w_up, ffn_w_down, od_w_in, od_q_norm, od_kv_norm, od_w_q_b, od_w_kv_b, od_fnet_w, od_w_out, moe_router, moe_w_gate, moe_w_up, moe_w_down):
    cond8 = jnp.concatenate([c_ctx[None, :], c, jnp.zeros((8 - N_COND, D), F32)], axis=0)
    mods = _modulation(cond8, ada_w, ada_b)[:, :N_COND].reshape(DEPTH, N_COND, 6, D)

    x, (ffn_wg, ffn_wu, ffn_wd) = _even_mixer(
        x_prompt.reshape(T_P, D), x_sample.reshape(T_S, D), mods[0], ev_w_in[0].astype(BF16), ev_conv_w[0],
        ev_pool_w[0].astype(BF16), ev_pool_scale[0][None, :], ev_w_out[0].astype(BF16),
        ln_g[0, 0][None, :], ln_b[0, 0][None, :], [ffn_w_gate[0], ffn_w_up[0], ffn_w_down[0]])
    x, moe_wd = _ffn(x, mods[0], ffn_wg, ffn_wu, ffn_wd, ln_g[0, 1][None, :], ln_b[0, 1][None, :],
                     moe_w_down[0].reshape(N_EXPERTS * D_FF_EXPERT, D))

    w_in = od_w_in[0]
    w_pe = w_in[:, Q_RANK + KV_RANK:Q_RANK + KV_RANK + QK_ROPE]
    zpad = jnp.zeros((D, 64), F32)
    w_in_ext = jnp.concatenate([w_in[:, :Q_RANK + KV_RANK], w_in[:, Q_RANK + KV_RANK + QK_ROPE:],
                                w_pe, zpad, _rot_cols(w_pe), zpad], axis=1).astype(BF16)
    wq = od_w_q_b[0].reshape(Q_RANK, N_HEADS, QK_NOPE + QK_ROPE)
    zq = jnp.zeros((Q_RANK, N_HEADS, 64), F32)
    wqa = jnp.concatenate([wq, zq], axis=-1).reshape(Q_RANK, N_HEADS * QH).astype(BF16)
    wqb = jnp.concatenate([_rot_cols(wq[..., QK_NOPE:]), zq], axis=-1).reshape(Q_RANK, N_HEADS * 128).astype(BF16)
    wkv = od_w_kv_b[0].reshape(KV_RANK, N_HEADS, QK_NOPE + V_DIM)
    w_kv = jnp.concatenate([wkv[..., :QK_NOPE].reshape(KV_RANK, -1), wkv[..., QK_NOPE:].reshape(KV_RANK, -1)],
                           axis=1).astype(BF16)

    avg = jnp.asarray(_block_diag4(np.full((FNET_GROUP_DIM, FNET_GROUP_DIM), 1.0 / FNET_GROUP_DIM)), BF16)
    cc, sc = _dft_tables(FNET_GROUP_DIM)
    dft_hi, dft_lo = _hi_lo(np.concatenate([_block_diag4(cc), -_block_diag4(sc)], axis=1))

    q, ckv, kpe, y_dft, moe_wg = _odd_proj(x, mods[1], _rope_table(), w_in_ext, od_q_norm[0][None, :],
                                            od_kv_norm[0][None, :], wqa, wqb, avg, dft_hi, dft_lo,
                                            moe_w_gate[0].reshape(N_EXPERTS * D, D_FF_EXPERT))

    cache_kpe128 = jnp.pad(cache_kpe[:, 0], ((0, 0), (0, 0), (0, 128 - QK_ROPE)))
    attn = _attention(q, ckv, kpe, cache_ckv[:, 0], cache_kpe128, w_kv)

    tabs_p = tuple(jnp.asarray(m, F32) for m in _dft_tables(SEQ))
    a_base = _dft_angles(np.arange(TM), DEC_SEQ)
    a_step = _dft_angles(np.arange(TPS) * TM, DEC_SEQ)
    base_s = (jnp.asarray(np.cos(a_base) * DEC_SEQ ** -0.5, F32), jnp.asarray(np.sin(a_base) * DEC_SEQ ** -0.5, F32))
    step_s = (jnp.asarray(np.cos(a_step), F32), jnp.asarray(np.sin(a_step), F32))
    f = _pos_dft(y_dft, tabs_p, base_s, step_s)

    router = jnp.pad(moe_router[0], ((0, 0), (0, 128 - N_EXPERTS)))
    r_hi, r_lo = _hi_lo(router)
    x, info, info_t, cnt, moe_wu = _odd_merge(x, attn, f, mods[1], od_fnet_w[0].astype(BF16),
                                              od_w_out[0].astype(BF16), ln_g[1, 0][None, :], ln_b[1, 0][None, :],
                                              r_hi, r_lo, moe_w_up[0].reshape(N_EXPERTS * D, D_FF_EXPERT))

    n_tiles = T // TM2
    after = cnt.reshape(n_tiles, 8, 128)[:, 0, :N_EXPERTS].astype(jnp.int32)
    before = jnp.concatenate([jnp.zeros((1, N_EXPERTS), jnp.int32), after[:-1]], axis=0)
    counts = after[-1]
    padded = ((counts + TM_E - 1) // TM_E) * TM_E
    g_end = jnp.cumsum(padded)
    g_start = g_end - padded
    n_chunks = (after - before + CHUNK - 1) // CHUNK
    seg_off = CHUNK * (jnp.cumsum(n_chunks, axis=1) - n_chunks)
    seg_dst = g_start[None, :] + before
    choice = info_t[2:4].astype(jnp.int32)
    rank = info_t[4:6].astype(jnp.int32)
    shift_t = jnp.repeat((seg_off - before).T, TM2, axis=1)
    lp, dest = rank, rank
    for e in range(N_EXPERTS):
        lp = lp + jnp.where(choice == e, shift_t[e][None, :], 0)
        dest = dest + jnp.where(choice == e, g_start[e], 0)
    lp = lp.reshape(-1).astype(jnp.int32)
    dest = dest.reshape(-1).astype(jnp.int32)
    d0, off, nch = (a.reshape(-1).astype(jnp.int32) for a in (seg_dst, seg_off, n_chunks))
    tile_row = jnp.arange(N_ETILES, dtype=jnp.int32) * TM_E
    tile_expert = jnp.minimum(jnp.sum((tile_row[:, None] >= g_end[None, :]).astype(jnp.int32), axis=1),
                              N_EXPERTS - 1).astype(jnp.int32)
    n_used = (g_end[-1:] // TM_E).astype(jnp.int32)

    pad_lo = jnp.concatenate([g_start + counts, g_end[-1:]]).astype(jnp.int32)
    pad_hi = jnp.concatenate([g_end, jnp.full((1,), R_MAX, jnp.int32)]).astype(jnp.int32)
    xs = _dispatch(dest, pad_lo, pad_hi, x, mods[1])
    ys = _experts(tile_expert, n_used, xs, moe_wg.reshape(N_EXPERTS, D, D_FF_EXPERT),
                  moe_wu.reshape(N_EXPERTS, D, D_FF_EXPERT), moe_wd.reshape(N_EXPERTS, D_FF_EXPERT, D))
    yp, ysm = _combine(lp, d0, off, nch, x, info, mods[1], ln_g[1, 1][None, :], ln_b[1, 1][None, :], ys)

    y_prompt = yp.reshape(BATCH, SEQ, D)
    y_sample = ysm.reshape(DEC_BATCH, DEC_SEQ, D)
    new_ckv = ckv[:T_P].reshape(BATCH, 1, SEQ, KV_RANK)
    new_kpe = kpe[:T_P, :QK_ROPE].reshape(BATCH, 1, SEQ, QK_ROPE)
    return (y_prompt, y_sample, new_ckv, new_kpe)
```

```python
import functools

import numpy as np
import jax
import jax.numpy as jnp
from jax import lax
from jax.experimental import pallas as pl
from jax.experimental.pallas import tpu as pltpu

F32 = jnp.float32
BF16 = jnp.bfloat16

D = 1024
BATCH, SEQ = 32, 256
DEC_BATCH, DEC_SEQ = 2, 2048
PAST = 512
GRID_W = 64
T_P = BATCH * SEQ
T_S = DEC_BATCH * DEC_SEQ
T = T_P + T_S
N_COND = 1 + DEC_BATCH

CONV_DIM = 512
POOL_WINDOWS = (2, 4, 8, 16)
POOL_GROUP = 128
N_HEADS = 8
QK_NOPE, QK_ROPE, V_DIM = 128, 64, 128
Q_RANK, KV_RANK = 384, 256
FNET_DIM, FNET_GROUP_DIM = 256, 64
D_FF = 2816
N_EXPERTS = 8
D_FF_EXPERT = 1792
DEPTH = 2
ALPHA = (2 * DEPTH) ** 0.25
LN_EPS = 1e-5
RMS_EPS = 1e-6
ROPE_THETA = 10000.0

TM = 256
NP_TILES = T_P // TM
TPS = DEC_SEQ // TM
N_TILES = T // TM
HALO = 8
TM2 = 512
TM_FFN = 512
LK_S = PAST + DEC_SEQ
TM_E = 512
N_ETILES = (2 * T) // TM_E + N_EXPERTS + 1
R_MAX = N_ETILES * TM_E
VMEM_LIMIT = 56 * 1024 * 1024


def _cond_index(i, tm=TM):
    return jnp.where(i < T_P // tm, 0, 1 + (i - T_P // tm) // (DEC_SEQ // tm))


def _const_spec(shape):
    nd = len(shape)
    return pl.BlockSpec(shape, lambda *_: (0,) * nd, pipeline_mode=pl.Buffered(1))


def _params(n_axes=1, vmem=VMEM_LIMIT):
    return pltpu.CompilerParams(dimension_semantics=("arbitrary",) * n_axes, vmem_limit_bytes=vmem)


def _layer_norm(v, g, b):
    mu = jnp.mean(v, axis=-1, keepdims=True)
    d = v - mu
    var = jnp.mean(d * d, axis=-1, keepdims=True)
    return d * lax.rsqrt(var + LN_EPS) * g + b


def _split_bf16(v):
    hi = v.astype(BF16)
    lo = (v - hi.astype(F32)).astype(BF16)
    return hi, lo


def _dot(a, b):
    return jnp.dot(a, b, preferred_element_type=F32)


def _silu(v):
    return v / (1.0 + jnp.exp(-v))


CAST_BLOCKS = 16


def _cast_stream(srcs):
    specs = [pl.BlockSpec((w.shape[0] // CAST_BLOCKS, w.shape[1]),
                          lambda i, *_: (jnp.minimum(i, CAST_BLOCKS - 1), 0)) for w in srcs]
    return specs, [jax.ShapeDtypeStruct(w.shape, BF16) for w in srcs]


def _cast_step(srcs, dsts):
    @pl.when(pl.program_id(0) < CAST_BLOCKS)
    def _():
        for src, dst in zip(srcs, dsts):
            dst[...] = src[...].astype(BF16)


def _mod_kernel(cond_ref, w_ref, b_ref, o_ref):
    s = _silu(cond_ref[...]).astype(BF16)
    o_ref[...] = _dot(s, w_ref[...].astype(BF16)) + b_ref[...]


def _modulation(cond8, ada_w, ada_b):
    nb = 6 * D // 1024
    return pl.pallas_call(
        _mod_kernel,
        grid=(DEPTH, nb),
        in_specs=[
            pl.BlockSpec((8, D), lambda l, j: (0, 0)),
            pl.BlockSpec((None, D, 1024), lambda l, j: (l, 0, j)),
            pl.BlockSpec((None, 1, 1024), lambda l, j: (l, 0, j)),
        ],
        out_specs=pl.BlockSpec((None, 8, 1024), lambda l, j: (l, 0, j)),
        out_shape=jax.ShapeDtypeStruct((DEPTH, 8, 6 * D), F32),
        compiler_params=_params(2),
        name="adaln_modulation",
    )(cond8, ada_w, ada_b.reshape(DEPTH, 1, 6 * D))


def _pool_tables():
    t = np.arange(TM)[:, None]
    r = np.arange(TM + 2 * HALO)[None, :]
    pos = r - HALO
    bands = np.zeros((4, len(POOL_WINDOWS), TM, TM + 2 * HALO), np.float32)
    inv = np.zeros((4, TM, 128), np.float32)
    for variant in range(4):
        left_ok, right_ok = variant & 1, variant >> 1
        col_ok = (r >= (0 if left_ok else HALO)) & (r < (TM + 2 * HALO if right_ok else TM + HALO))
        first, last = (-HALO if left_ok else 0), (TM + HALO if right_ok else TM)
        for gi, w in enumerate(POOL_WINDOWS):
            bands[variant, gi] = (pos >= t - w // 2) & (pos < t + w // 2) & col_ok
            cnt = np.minimum(t[:, 0] + w // 2, last) - np.maximum(t[:, 0] - w // 2, first)
            inv[variant, :, gi] = 1.0 / cnt
    return jnp.asarray(bands, BF16), jnp.asarray(inv, F32)


def _even_tiles(x_alls, left_oks, right_oks, mod_ref, win_ref, convw_ref, poolw_ref, pscale_ref, wout_ref,
                g_ref, b_ref, band_ref, inv_ref):
    shift, scale, gate = mod_ref[0:1, :], mod_ref[1:2, :], mod_ref[2:3, :]
    n = len(x_alls)
    hs = []
    for x_all, left_ok, right_ok in zip(x_alls, left_oks, right_oks):
        h = x_all * (1.0 + scale) + shift
        hs.append(jnp.concatenate([jnp.where(left_ok, h[:HALO], 0.0), h[HALO:HALO + TM],
                                   jnp.where(right_ok, h[HALO + TM:], 0.0)], axis=0).astype(BF16))
    us = [_dot(h, win_ref[...]) for h in hs]

    mixes = []
    for u, left_ok, right_ok in zip(us, left_oks, right_oks):
        ux, ub = u[:, 0:CONV_DIM], u[:, CONV_DIM:2 * CONV_DIM]
        uc, up = u[:, 2 * CONV_DIM:3 * CONV_DIM], u[:, 3 * CONV_DIM:]
        z = uc * ux
        conv = (z[HALO - 1:HALO - 1 + TM] * convw_ref[0:1, :]
                + z[HALO:HALO + TM] * convw_ref[1:2, :]
                + z[HALO + 1:HALO + 1 + TM] * convw_ref[2:3, :])
        ya = ub[HALO:HALO + TM] * conv

        variant = left_ok.astype(jnp.int32) + 2 * right_ok.astype(jnp.int32)
        inv_cnt = inv_ref[variant]
        up_hi, up_lo = _split_bf16(up)
        yb_groups = []
        for gi in range(len(POOL_WINDOWS)):
            band = band_ref[variant, gi]
            sl = slice(gi * POOL_GROUP, (gi + 1) * POOL_GROUP)
            tot = _dot(band, up_hi[:, sl]) + _dot(band, up_lo[:, sl])
            p = tot * inv_cnt[:, gi:gi + 1] - up[HALO:HALO + TM, sl]
            yb_groups.append(_dot(p.astype(BF16), poolw_ref[gi]))
        yb = jnp.concatenate(yb_groups, axis=1) * pscale_ref[...]
        mixes.append(jnp.concatenate([ya, yb], axis=1).astype(BF16))

    ys = [_dot(mix, wout_ref[...]) for mix in mixes]
    return [_layer_norm(ALPHA * x_alls[k][HALO:HALO + TM] + gate * ys[k], g_ref[...], b_ref[...])
            for k in range(n)]


EV_TILES = 2
EV_ROWS = EV_TILES * TM


N_CAST_EVEN = 3


def _even_mixer_kernel(xctx_ref, xprev_ref, xlat_ref, xnext_ref, mod_ref, win_ref, convw_ref, poolw_ref,
                       pscale_ref, wout_ref, g_ref, b_ref, band_ref, inv_ref, *rest):
    n = N_CAST_EVEN
    cast_in, o_ref, cast_out, xall_ref = rest[:n], rest[n], rest[n + 1:2 * n + 1], rest[-1]
    _cast_step(cast_in, cast_out)
    s = pl.program_id(0)
    n_ctx = T_P // EV_ROWS
    is_latent = s >= n_ctx
    first_tile = lax.rem(s - n_ctx, DEC_SEQ // EV_ROWS) * EV_TILES

    @pl.when(jnp.logical_not(is_latent))
    def _():
        for k in range(EV_TILES):
            xall_ref[k, 0:HALO, :] = jnp.zeros((HALO, D), F32)
            xall_ref[k, HALO:HALO + TM, :] = xctx_ref[k * TM:(k + 1) * TM, :]
            xall_ref[k, HALO + TM:, :] = jnp.zeros((HALO, D), F32)

    @pl.when(is_latent)
    def _():
        for k in range(EV_TILES):
            lo, hi = k * TM, (k + 1) * TM
            xall_ref[k, 0:HALO, :] = xprev_ref[...] if k == 0 else xlat_ref[lo - HALO:lo, :]
            xall_ref[k, HALO:HALO + TM, :] = xlat_ref[lo:hi, :]
            xall_ref[k, HALO + TM:, :] = xnext_ref[...] if k == EV_TILES - 1 else xlat_ref[hi:hi + HALO, :]

    left_oks = [jnp.logical_and(is_latent, first_tile + k != 0) for k in range(EV_TILES)]
    right_oks = [jnp.logical_and(is_latent, first_tile + k != TPS - 1) for k in range(EV_TILES)]
    outs = _even_tiles([xall_ref[k] for k in range(EV_TILES)], left_oks, right_oks, mod_ref, win_ref,
                       convw_ref, poolw_ref, pscale_ref, wout_ref, g_ref, b_ref, band_ref, inv_ref)
    for k in range(EV_TILES):
        o_ref[k * TM:(k + 1) * TM, :] = outs[k]


def _even_mixer(x_ctx, x_lat, mods, w_in, conv_w, pool_w, pool_scale, w_out, g, b, cast_srcs):
    assert len(cast_srcs) == N_CAST_EVEN
    hb = EV_ROWS // HALO
    n8 = T_S // HALO
    n_ctx = T_P // EV_ROWS
    lat = lambda i: jnp.maximum(i - n_ctx, 0)
    cast_specs, cast_shapes = _cast_stream(cast_srcs)
    bands, inv_cnt = _pool_tables()
    outs = pl.pallas_call(
        _even_mixer_kernel,
        grid=(T // EV_ROWS,),
        in_specs=[
            pl.BlockSpec((EV_ROWS, D), lambda i: (jnp.minimum(i, n_ctx - 1), 0)),
            pl.BlockSpec((HALO, D), lambda i: (jnp.maximum(lat(i) * hb - 1, 0), 0)),
            pl.BlockSpec((EV_ROWS, D), lambda i: (lat(i), 0)),
            pl.BlockSpec((HALO, D), lambda i: (jnp.minimum((lat(i) + 1) * hb, n8 - 1), 0)),
            pl.BlockSpec((None, 6, D), lambda i: (_cond_index(i, EV_ROWS), 0, 0)),
            _const_spec((D, 4 * CONV_DIM)),
            _const_spec((3, CONV_DIM)),
            _const_spec((4, POOL_GROUP, POOL_GROUP)),
            _const_spec((1, 4 * POOL_GROUP)),
            _const_spec((D, D)),
            _const_spec((1, D)),
            _const_spec((1, D)),
            _const_spec(bands.shape),
            _const_spec(inv_cnt.shape),
        ] + cast_specs,
        out_specs=[pl.BlockSpec((EV_ROWS, D), lambda i: (i, 0))] + cast_specs,
        out_shape=[jax.ShapeDtypeStruct((T, D), F32)] + cast_shapes,
        scratch_shapes=[pltpu.VMEM((EV_TILES, TM + 2 * HALO, D), F32)],
        compiler_params=_params(1),
        name="even_mixer",
    )(x_ctx, x_lat, x_lat, x_lat, mods, w_in, conv_w, pool_w, pool_scale, w_out, g, b, bands, inv_cnt, *cast_srcs)
    return outs[0], outs[1:]


FF_CHUNK = D_FF // 2


def _ffn_kernel(x_ref, mod_ref, wg_ref, wu_ref, wd_ref, g_ref, b_ref, cast_ref, o_ref, cast_out_ref):
    _cast_step([cast_ref], [cast_out_ref])
    shift, scale, gate = mod_ref[3:4, :], mod_ref[4:5, :], mod_ref[5:6, :]
    x = x_ref[...]
    h = (x * (1.0 + scale) + shift).astype(BF16)
    f = jnp.zeros((TM_FFN, D), F32)
    for c in range(D_FF // FF_CHUNK):
        sl = slice(c * FF_CHUNK, (c + 1) * FF_CHUNK)
        a = _silu(_dot(h, wg_ref[:, sl])) * _dot(h, wu_ref[:, sl])
        f = f + _dot(a.astype(BF16), wd_ref[sl, :])
    o_ref[...] = _layer_norm(ALPHA * x + gate * f, g_ref[...], b_ref[...])


def _ffn(x, mods, wg, wu, wd, g, b, cast_src):
    cast_specs, cast_shapes = _cast_stream([cast_src])
    return pl.pallas_call(
        _ffn_kernel,
        grid=(T // TM_FFN,),
        in_specs=[
            pl.BlockSpec((TM_FFN, D), lambda i: (i, 0)),
            pl.BlockSpec((None, 6, D), lambda i: (_cond_index(i, TM_FFN), 0, 0)),
            _const_spec((D, D_FF)),
            _const_spec((D, D_FF)),
            _const_spec((D_FF, D)),
            _const_spec((1, D)),
            _const_spec((1, D)),
        ] + cast_specs,
        out_specs=[pl.BlockSpec((TM_FFN, D), lambda i: (i, 0))] + cast_specs,
        out_shape=[jax.ShapeDtypeStruct((T, D), F32)] + cast_shapes,
        compiler_params=_params(1),
        name="dense_swiglu",
    )(x, mods, wg, wu, wd, g, b, cast_src)


W_IN_EXT = Q_RANK + KV_RANK + FNET_DIM + 128 + 128
QH = 256
ATT_SCALE = (QK_NOPE + QK_ROPE) ** -0.5 * float(np.log2(np.e))


def _odd_proj_kernel(x_ref, mod_ref, rope_ref, win_ref, qn_ref, kvn_ref, wqa_ref, wqb_ref,
                     avg_ref, dfth_ref, dftl_ref, cast_ref,
                     q_ref, ckv_ref, kpe_ref, y_ref, cast_out_ref):
    _cast_step([cast_ref], [cast_out_ref])
    shift, scale = mod_ref[0:1, :], mod_ref[1:2, :]
    h = (x_ref[...] * (1.0 + scale) + shift).astype(BF16)
    u = _dot(h, win_ref[...])
    uq = u[:, 0:Q_RANK]
    ukv = u[:, Q_RANK:Q_RANK + KV_RANK]
    uf = u[:, Q_RANK + KV_RANK:Q_RANK + KV_RANK + FNET_DIM]
    o = Q_RANK + KV_RANK + FNET_DIM
    upe, upe_rot = u[:, o:o + 128], u[:, o + 128:o + 256]
    cos, sin = rope_ref[:, 0:128], rope_ref[:, 128:256]

    ckv_ref[...] = ukv * lax.rsqrt(jnp.mean(ukv * ukv, axis=-1, keepdims=True) + RMS_EPS) * kvn_ref[...]
    kpe_ref[...] = upe * cos + upe_rot * sin

    qlat = (uq * lax.rsqrt(jnp.mean(uq * uq, axis=-1, keepdims=True) + RMS_EPS) * qn_ref[...]).astype(BF16)
    qa = _dot(qlat, wqa_ref[...])
    qb = _dot(qlat, wqb_ref[...])
    for hd in range(N_HEADS):
        nope = qa[:, hd * QH:hd * QH + 128]
        pe = qa[:, hd * QH + 128:(hd + 1) * QH] * cos + qb[:, hd * 128:(hd + 1) * 128] * sin
        q_ref[:, hd * QH:hd * QH + 128] = (nope * ATT_SCALE).astype(BF16)
        q_ref[:, hd * QH + 128:(hd + 1) * QH] = (pe * ATT_SCALE).astype(BF16)

    avg = avg_ref[...]
    uf_hi, uf_lo = _split_bf16(uf)
    mu = _dot(uf_hi, avg) + _dot(uf_lo, avg)
    dlt = uf - mu
    sq_hi, sq_lo = _split_bf16(dlt * dlt)
    var = _dot(sq_hi, avg) + _dot(sq_lo, avg)
    xn = dlt * lax.rsqrt(var + LN_EPS)
    xn_hi, xn_lo = _split_bf16(xn)
    y = _dot(xn_hi, dfth_ref[...]) + _dot(xn_lo, dfth_ref[...]) + _dot(xn_hi, dftl_ref[...])
    y_ref[...] = y.astype(BF16)


def _odd_proj(x, mods, rope_tab, w_in_ext, q_norm, kv_norm, wqa, wqb, avg, dft_hi, dft_lo, cast_src):
    cast_specs, cast_shapes = _cast_stream([cast_src])

    def rope_index(i):
        return (jnp.where(i < T_P // TM2, 0, 1 + lax.rem(i - T_P // TM2, DEC_SEQ // TM2)), 0)

    return pl.pallas_call(
        _odd_proj_kernel,
        grid=(T // TM2,),
        in_specs=[
            pl.BlockSpec((TM2, D), lambda i: (i, 0)),
            pl.BlockSpec((None, 6, D), lambda i: (_cond_index(i, TM2), 0, 0)),
            pl.BlockSpec((TM2, 256), rope_index),
            _const_spec((D, W_IN_EXT)),
            _const_spec((1, Q_RANK)),
            _const_spec((1, KV_RANK)),
            _const_spec((Q_RANK, N_HEADS * QH)),
            _const_spec((Q_RANK, N_HEADS * 128)),
            _const_spec((FNET_DIM, FNET_DIM)),
            _const_spec((FNET_DIM, 2 * FNET_DIM)),
            _const_spec((FNET_DIM, 2 * FNET_DIM)),
        ] + cast_specs,
        out_specs=[
            pl.BlockSpec((TM2, N_HEADS * QH), lambda i: (i, 0)),
            pl.BlockSpec((TM2, KV_RANK), lambda i: (i, 0)),
            pl.BlockSpec((TM2, 128), lambda i: (i, 0)),
            pl.BlockSpec((TM2, 2 * FNET_DIM), lambda i: (i, 0)),
        ] + cast_specs,
        out_shape=[
            jax.ShapeDtypeStruct((T, N_HEADS * QH), BF16),
            jax.ShapeDtypeStruct((T, KV_RANK), F32),
            jax.ShapeDtypeStruct((T, 128), F32),
            jax.ShapeDtypeStruct((T, 2 * FNET_DIM), BF16),
        ] + cast_shapes,
        compiler_params=_params(1),
        name="odd_projections",
    )(x, mods, rope_tab, w_in_ext, q_norm, kv_norm, wqa, wqb, avg, dft_hi, dft_lo, cast_src)


V_OFF = N_HEADS * QH
KV_COLS = V_OFF + N_HEADS * V_DIM


def _attn_body(q_ref, kv_ref, o_refs):
    for hd in range(N_HEADS):
        qh = q_ref[:, hd * QH:(hd + 1) * QH]
        kh = kv_ref[:, hd * QH:(hd + 1) * QH]
        s = lax.dot_general(qh, kh, (((1,), (1,)), ((), ())), preferred_element_type=F32)
        p = jnp.exp2(s - jnp.max(s, axis=-1, keepdims=True))
        den = jnp.sum(p, axis=-1, keepdims=True)
        vh = kv_ref[:, V_OFF + hd * V_DIM:V_OFF + (hd + 1) * V_DIM]
        o = (_dot(p.astype(BF16), vh) / den).astype(BF16)
        for k, o_ref in enumerate(o_refs):
            o_ref[:, hd * 128:(hd + 1) * 128] = o[k * TM:(k + 1) * TM]


KV_CHUNK = 512
LAT_PAIR = 2


def _attn_kernel(q_ref, qpair_ref, ckvp_ref, kpep_ref, ckvs_ref, kpes_ref, cckv_ref, ckpe_ref, wkv_ref, o_ref,
                 kvp_s, kvs_s, held_s):
    i = pl.program_id(0)

    def expand(ckv, kpe, dst, row0):
        n = ckv.shape[0]
        kv = _dot(ckv.astype(BF16), wkv_ref[...]).astype(BF16)
        kpe = kpe.astype(BF16)
        for hd in range(N_HEADS):
            dst[row0:row0 + n, hd * QH:hd * QH + QK_NOPE] = kv[:, hd * QK_NOPE:(hd + 1) * QK_NOPE]
            dst[row0:row0 + n, hd * QH + QK_NOPE:(hd + 1) * QH] = kpe
        dst[row0:row0 + n, V_OFF:] = kv[:, N_HEADS * QK_NOPE:]

    @pl.when(i < NP_TILES)
    def _():
        expand(ckvp_ref[...], kpep_ref[...], kvp_s, 0)
        _attn_body(q_ref, kvp_s, [o_ref])

    @pl.when(jnp.logical_and(i >= NP_TILES, lax.rem(i - NP_TILES, LAT_PAIR) == 0))
    def _():
        @pl.when(lax.rem(i - NP_TILES, TPS) == 0)
        def _():
            expand(cckv_ref[...], ckpe_ref[...], kvs_s, 0)
            for c in range(DEC_SEQ // KV_CHUNK):
                rows = slice(c * KV_CHUNK, (c + 1) * KV_CHUNK)
                expand(ckvs_ref[rows, :], kpes_ref[rows, :], kvs_s, PAST + c * KV_CHUNK)

        _attn_body(qpair_ref, kvs_s, [o_ref, held_s])

    @pl.when(jnp.logical_and(i >= NP_TILES, lax.rem(i - NP_TILES, LAT_PAIR) == 1))
    def _():
        o_ref[...] = held_s[...]


def _attention(q, ckv, kpe, cache_ckv, cache_kpe128, w_kv):
    ctx_blk = lambda i: (jnp.minimum(i, NP_TILES - 1), 0)
    lat_b = lambda i: jnp.maximum(i - NP_TILES, 0) // TPS
    lat_blk = lambda i: (T_P // DEC_SEQ + lat_b(i), 0)
    return pl.pallas_call(
        _attn_kernel,
        grid=(N_TILES,),
        in_specs=[
            pl.BlockSpec((TM, N_HEADS * QH), ctx_blk),
            pl.BlockSpec((LAT_PAIR * TM, N_HEADS * QH),
                         lambda i: (NP_TILES // LAT_PAIR + jnp.maximum(i - NP_TILES, 0) // LAT_PAIR, 0)),
            pl.BlockSpec((SEQ, KV_RANK), ctx_blk),
            pl.BlockSpec((SEQ, 128), ctx_blk),
            pl.BlockSpec((DEC_SEQ, KV_RANK), lat_blk, pipeline_mode=pl.Buffered(1)),
            pl.BlockSpec((DEC_SEQ, 128), lat_blk, pipeline_mode=pl.Buffered(1)),
            pl.BlockSpec((None, PAST, KV_RANK), lambda i: (lat_b(i), 0, 0)),
            pl.BlockSpec((None, PAST, 128), lambda i: (lat_b(i), 0, 0)),
            _const_spec((KV_RANK, 2 * N_HEADS * 128)),
        ],
        out_specs=pl.BlockSpec((TM, N_HEADS * V_DIM), lambda i: (i, 0)),
        out_shape=jax.ShapeDtypeStruct((T, N_HEADS * V_DIM), BF16),
        scratch_shapes=[pltpu.VMEM((SEQ, KV_COLS), BF16), pltpu.VMEM((LK_S, KV_COLS), BF16),
                        pltpu.VMEM((TM, N_HEADS * V_DIM), BF16)],
        compiler_params=_params(1),
        name="attention",
    )(q, q, ckv, kpe, ckv, kpe, cache_ckv, cache_kpe128, w_kv)


def _pos_dft_body(y_ref, c, s, o_ref):
    f = _dot(c, y_ref[:, 0:FNET_DIM]) + _dot(s, y_ref[:, FNET_DIM:])
    o_ref[...] = f.astype(BF16)


def _pos_dft_kernel(yp_ref, cp_ref, sp_ref, ys_ref, c0_ref, s0_ref, cb_ref, sb_ref, o_ref):
    @pl.when(pl.program_id(0) < NP_TILES)
    def _():
        _pos_dft_body(yp_ref, cp_ref[...].astype(BF16), sp_ref[...].astype(BF16), o_ref)

    @pl.when(pl.program_id(0) >= NP_TILES)
    def _():
        j = lax.rem(pl.program_id(0) - NP_TILES, TPS)
        cb, sb = cb_ref[pl.ds(j, 1), :], sb_ref[pl.ds(j, 1), :]
        c0, s0 = c0_ref[...], s0_ref[...]
        _pos_dft_body(ys_ref, (c0 * cb - s0 * sb).astype(BF16), (s0 * cb + c0 * sb).astype(BF16), o_ref)


def _pos_dft(y, tabs_p, base_s, step_s):
    ctx_blk = lambda i: (jnp.minimum(i, NP_TILES - 1), 0)
    lat_seq = lambda i: (T_P // DEC_SEQ + jnp.maximum(i - NP_TILES, 0) // TPS, 0)
    return pl.pallas_call(
        _pos_dft_kernel,
        grid=(N_TILES,),
        in_specs=[pl.BlockSpec((SEQ, 2 * FNET_DIM), ctx_blk)]
        + [_const_spec((SEQ, SEQ))] * 2
        + [pl.BlockSpec((DEC_SEQ, 2 * FNET_DIM), lat_seq)]
        + [_const_spec((TM, DEC_SEQ))] * 2
        + [_const_spec((TPS, DEC_SEQ))] * 2,
        out_specs=pl.BlockSpec((TM, FNET_DIM), lambda i: (i, 0)),
        out_shape=jax.ShapeDtypeStruct((T, FNET_DIM), BF16),
        compiler_params=_params(1),
        name="pos_dft",
    )(y, *tabs_p, y, *base_s, *step_s)


def _odd_merge_kernel(x_ref, attn_ref, f_ref, mod_ref, fw_ref, wo_ref, g_ref, b_ref, rh_ref, rl_ref, tri_ref,
                      cast_ref, xo_ref, info_ref, infot_ref, cnt_ref, cast_out_ref, carry_ref):
    _cast_step([cast_ref], [cast_out_ref])

    @pl.when(pl.program_id(0) == 0)
    def _():
        carry_ref[...] = jnp.zeros_like(carry_ref)

    gate = mod_ref[2:3, :]
    shift2, scale2 = mod_ref[3:4, :], mod_ref[4:5, :]
    fm = _dot(f_ref[...], fw_ref[...]).astype(BF16)
    y = _dot(attn_ref[...], wo_ref[0:N_HEADS * V_DIM, :]) + _dot(fm, wo_ref[N_HEADS * V_DIM:, :])
    x = _layer_norm(ALPHA * x_ref[...] + gate * y, g_ref[...], b_ref[...])
    xo_ref[...] = x
    h = x * (1.0 + scale2) + shift2

    h_hi, h_lo = _split_bf16(h)
    logits = _dot(h_hi, rh_ref[...]) + _dot(h_lo, rh_ref[...]) + _dot(h_hi, rl_ref[...])
    lane = lax.broadcasted_iota(jnp.int32, (TM2, 128), 1)
    neg = jnp.float32(-jnp.inf)
    logits = jnp.where(lane < N_EXPERTS, logits, neg)
    m1 = jnp.max(logits, axis=-1, keepdims=True)
    i1 = jnp.min(jnp.where(logits == m1, lane, 128), axis=-1, keepdims=True)
    rest = jnp.where(lane == i1, neg, logits)
    m2 = jnp.max(rest, axis=-1, keepdims=True)
    i2 = jnp.min(jnp.where(rest == m2, lane, 128), axis=-1, keepdims=True)
    e2 = jnp.exp(m2 - m1)
    w1 = 1.0 / (1.0 + e2)
    w2 = e2 / (1.0 + e2)
    info = jnp.where(lane == 0, w1, 0.0)
    info = jnp.where(lane == 1, w2, info)
    info = jnp.where(lane == 2, i1.astype(F32), info)
    info = jnp.where(lane == 3, i2.astype(F32), info)

    uses = jnp.logical_or(lane == i1, lane == i2)
    seen = _dot(tri_ref[...], jnp.where(uses, 1.0, 0.0).astype(BF16)) + carry_ref[...]
    r1 = jnp.sum(jnp.where(lane == i1, seen, 0.0), axis=-1, keepdims=True)
    r2 = jnp.sum(jnp.where(lane == i2, seen, 0.0), axis=-1, keepdims=True)
    info = jnp.where(lane == 4, r1, info)
    info = jnp.where(lane == 5, r2, info)
    info_ref[...] = info
    infot_ref[...] = info.T
    total = carry_ref[...] + jnp.sum(jnp.where(uses, 1.0, 0.0), axis=0, keepdims=True)
    carry_ref[...] = total
    cnt_ref[...] = jnp.broadcast_to(total, cnt_ref.shape)


def _odd_merge(x, attn, f, mods, fnet_w, w_out, g, b, r_hi, r_lo, cast_src):
    row = lambda i: (i, 0)
    tri = jnp.asarray(np.tril(np.ones((TM2, TM2), np.float32), -1), BF16)
    cast_specs, cast_shapes = _cast_stream([cast_src])
    return pl.pallas_call(
        _odd_merge_kernel,
        grid=(T // TM2,),
        in_specs=[
            pl.BlockSpec((TM2, D), row),
            pl.BlockSpec((TM2, N_HEADS * V_DIM), row),
            pl.BlockSpec((TM2, FNET_DIM), row),
            pl.BlockSpec((None, 6, D), lambda i: (_cond_index(i, TM2), 0, 0)),
            _const_spec((FNET_DIM, FNET_DIM)),
            _const_spec((N_HEADS * V_DIM + FNET_DIM, D)),
            _const_spec((1, D)),
            _const_spec((1, D)),
            _const_spec((D, 128)),
            _const_spec((D, 128)),
            _const_spec((TM2, TM2)),
        ] + cast_specs,
        out_specs=[pl.BlockSpec((TM2, D), row), pl.BlockSpec((TM2, 128), row),
                   pl.BlockSpec((128, TM2), lambda i: (0, i)), pl.BlockSpec((8, 128), lambda i: (i, 0))] + cast_specs,
        out_shape=[jax.ShapeDtypeStruct((T, D), F32), jax.ShapeDtypeStruct((T, 128), F32),
                   jax.ShapeDtypeStruct((128, T), F32), jax.ShapeDtypeStruct((T // TM2 * 8, 128), F32)] + cast_shapes,
        scratch_shapes=[pltpu.VMEM((1, 128), F32)],
        compiler_params=_params(1),
        name="odd_merge_router",
    )(x, attn, f, mods, fnet_w, w_out, g, b, r_hi, r_lo, tri, cast_src)


TM_D = 512
ROW = (8, 128)
ROW_DT = BF16
DMA_UNROLL = 16
ZERO_ROWS = 32


def _row_copy(src, s, dst, d, sem):
    return pltpu.make_async_copy(src.at[s], dst.at[d], sem)


CHUNK = 16
LOCAL_ROWS = 2 * TM2 + N_EXPERTS * CHUNK


def _chunk_copies(tile, nch_ref, fn):
    for e in range(N_EXPERTS):
        def body(c, carry, e=e):
            fn(tile * N_EXPERTS + e, c, e % 2)
            return carry

        lax.fori_loop(0, nch_ref[tile * N_EXPERTS + e], body, 0)


def _dispatch_kernel(dest_ref, pad_lo_ref, pad_hi_ref, x_ref, mod_ref, xs_ref, h_ref, zero_ref, sem, zsem):
    i = pl.program_id(0)
    base = i * TM_D
    slot = lax.rem(i, 2)

    @pl.when(i == 0)
    def _():
        zero_ref[...] = jnp.zeros_like(zero_ref)
        for e in range(N_EXPERTS + 1):
            lo, hi = pad_lo_ref[e], pad_hi_ref[e]
            n_blocks = (hi - lo) // ZERO_ROWS
            tail = lo + n_blocks * ZERO_ROWS

            def zblock(c, lo=lo):
                return pltpu.make_async_copy(zero_ref, xs_ref.at[pl.ds(lo + c * ZERO_ROWS, ZERO_ROWS)], zsem)

            def zissue(c, carry):
                zblock(c).start()
                return carry

            def zdrain(c, carry):
                zblock(c).wait()
                return carry

            def rissue(r, carry):
                _row_copy(zero_ref, 0, xs_ref, r, zsem).start()
                return carry

            def rdrain(r, carry):
                _row_copy(zero_ref, 0, xs_ref, 0, zsem).wait()
                return carry

            lax.fori_loop(0, n_blocks, zissue, 0)
            lax.fori_loop(tail, hi, rissue, 0)
            lax.fori_loop(0, n_blocks, zdrain, 0)
            lax.fori_loop(tail, hi, rdrain, 0)

    shift2, scale2 = mod_ref[3:4, :], mod_ref[4:5, :]
    h_ref[slot] = (x_ref[...] * (1.0 + scale2) + shift2).astype(ROW_DT).reshape((TM_D,) + ROW)

    def issue(r, carry):
        t = base + r
        _row_copy(h_ref.at[slot], r, xs_ref, dest_ref[t], sem.at[slot]).start(priority=0)
        _row_copy(h_ref.at[slot], r, xs_ref, dest_ref[T + t], sem.at[slot]).start(priority=1)
        return carry

    lax.fori_loop(0, TM_D, issue, 0, unroll=DMA_UNROLL)

    def drain(s):
        def body(r, carry):
            _row_copy(h_ref.at[s], 0, xs_ref, 0, sem.at[s]).wait()
            _row_copy(h_ref.at[s], 0, xs_ref, 0, sem.at[s]).wait()
            return carry

        lax.fori_loop(0, TM_D, body, 0, unroll=DMA_UNROLL)

    @pl.when(i > 0)
    def _():
        drain(1 - slot)

    @pl.when(i == T // TM_D - 1)
    def _():
        drain(slot)


def _dispatch(dest, pad_lo, pad_hi, x, mods):
    return pl.pallas_call(
        _dispatch_kernel,
        grid_spec=pltpu.PrefetchScalarGridSpec(
            num_scalar_prefetch=3,
            grid=(T // TM_D,),
            in_specs=[pl.BlockSpec((TM_D, D), lambda i, *_: (i, 0)),
                      pl.BlockSpec((None, 6, D), lambda i, *_: (_cond_index(i, TM_D), 0, 0))],
            out_specs=pl.BlockSpec(memory_space=pl.ANY),
            scratch_shapes=[pltpu.VMEM((2, TM_D) + ROW, ROW_DT), pltpu.VMEM((ZERO_ROWS,) + ROW, ROW_DT),
                            pltpu.SemaphoreType.DMA((2,)), pltpu.SemaphoreType.DMA(())],
        ),
        out_shape=jax.ShapeDtypeStruct((R_MAX,) + ROW, ROW_DT),
        compiler_params=pltpu.CompilerParams(dimension_semantics=("arbitrary",), has_side_effects=True),
        name="expert_dispatch",
    )(dest, pad_lo, pad_hi, x, mods)


def _expert_kernel(te_ref, nt_ref, xs_ref, wg_ref, wu_ref, wd_ref, o_ref):
    @pl.when(pl.program_id(0) < nt_ref[0])
    def _():
        h = xs_ref[...].reshape(TM_E, D)
        a = _silu(_dot(h, wg_ref[...])) * _dot(h, wu_ref[...])
        o_ref[...] = _dot(a.astype(BF16), wd_ref[...]).astype(ROW_DT).reshape((TM_E,) + ROW)

    @pl.when(pl.program_id(0) >= nt_ref[0])
    def _():
        o_ref[...] = jnp.zeros_like(o_ref)


def _experts(tile_expert, n_used, xs, wg, wu, wd):
    return pl.pallas_call(
        _expert_kernel,
        grid_spec=pltpu.PrefetchScalarGridSpec(
            num_scalar_prefetch=2,
            grid=(N_ETILES,),
            in_specs=[
                pl.BlockSpec((TM_E,) + ROW, lambda i, te, nt: (jnp.minimum(i, nt[0] - 1), 0, 0)),
                pl.BlockSpec((None, D, D_FF_EXPERT), lambda i, te, nt: (te[i], 0, 0)),
                pl.BlockSpec((None, D, D_FF_EXPERT), lambda i, te, nt: (te[i], 0, 0)),
                pl.BlockSpec((None, D_FF_EXPERT, D), lambda i, te, nt: (te[i], 0, 0)),
            ],
            out_specs=pl.BlockSpec((TM_E,) + ROW, lambda i, te, nt: (i, 0, 0)),
        ),
        out_shape=jax.ShapeDtypeStruct((R_MAX,) + ROW, ROW_DT),
        compiler_params=_params(1),
        name="expert_swiglu",
    )(tile_expert, n_used, xs, wg, wu, wd)


def _combine_kernel(lp_ref, d0_ref, off_ref, nch_ref, x_ref, info_ref, mod_ref, g_ref, b_ref, ys_ref,
                    op_ref, os_ref, local_ref, rows_ref, sem):
    i = pl.program_id(0)
    slot = lax.rem(i, 2)

    def chunk(s, seg, c):
        return pltpu.make_async_copy(ys_ref.at[pl.ds(d0_ref[seg] + c * CHUNK, CHUNK)],
                                     local_ref.at[s, pl.ds(off_ref[seg] + c * CHUNK, CHUNK)], sem.at[s])

    @pl.when(i == 0)
    def _():
        _chunk_copies(0, nch_ref, lambda seg, c, prio: chunk(0, seg, c).start(priority=prio))

    @pl.when(i + 1 < T // TM2)
    def _():
        _chunk_copies(i + 1, nch_ref, lambda seg, c, prio: chunk(1 - slot, seg, c).start(priority=prio))

    _chunk_copies(i, nch_ref, lambda seg, c, prio: chunk(slot, seg, c).wait())

    def pick(r, carry):
        rows_ref[0, r] = local_ref[slot, lp_ref[i * TM2 + r]]
        rows_ref[1, r] = local_ref[slot, lp_ref[T + i * TM2 + r]]
        return carry

    lax.fori_loop(0, TM2, pick, 0, unroll=DMA_UNROLL)

    gate = mod_ref[5:6, :]
    w1, w2 = info_ref[:, 0:1], info_ref[:, 1:2]
    y = w1 * rows_ref[0].reshape(TM2, D).astype(F32) + w2 * rows_ref[1].reshape(TM2, D).astype(F32)
    out = _layer_norm(ALPHA * x_ref[...] + gate * y, g_ref[...], b_ref[...])

    @pl.when(pl.program_id(0) < T_P // TM2)
    def _():
        op_ref[...] = out

    @pl.when(pl.program_id(0) >= T_P // TM2)
    def _():
        os_ref[...] = out


def _combine(lp, d0, off, nch, x, info, mods, g, b, ys):
    return pl.pallas_call(
        _combine_kernel,
        grid_spec=pltpu.PrefetchScalarGridSpec(
            num_scalar_prefetch=4,
            grid=(T // TM2,),
            in_specs=[
                pl.BlockSpec((TM2, D), lambda i, *_: (i, 0)),
                pl.BlockSpec((TM2, 128), lambda i, *_: (i, 0)),
                pl.BlockSpec((None, 6, D), lambda i, *_: (_cond_index(i, TM2), 0, 0)),
                pl.BlockSpec((1, D), lambda i, *_: (0, 0)),
                pl.BlockSpec((1, D), lambda i, *_: (0, 0)),
                pl.BlockSpec(memory_space=pl.ANY),
            ],
            out_specs=[pl.BlockSpec((TM2, D), lambda i, *_: (jnp.minimum(i, T_P // TM2 - 1), 0)),
                       pl.BlockSpec((TM2, D), lambda i, *_: (jnp.maximum(i - T_P // TM2, 0), 0))],
            scratch_shapes=[pltpu.VMEM((2, LOCAL_ROWS) + ROW, ROW_DT), pltpu.VMEM((2, TM2) + ROW, ROW_DT),
                            pltpu.SemaphoreType.DMA((2,))],
        ),
        out_shape=[jax.ShapeDtypeStruct((T_P, D), F32), jax.ShapeDtypeStruct((T_S, D), F32)],
        compiler_params=_params(1),
        name="expert_combine",
    )(lp, d0, off, nch, x, info, mods, g, b, ys)


def _rot_cols(w):
    w4 = w.reshape(w.shape[:-1] + (2, 2, QK_ROPE // 4))
    return jnp.stack([-w4[..., 1, :], w4[..., 0, :]], axis=-2).reshape(w.shape)


def _rope_table():
    rows = DEC_SEQ // GRID_W
    row = np.repeat(np.arange(rows), GRID_W).astype(np.float32)
    col = np.tile(np.arange(GRID_W), rows).astype(np.float32)
    half = QK_ROPE // 2
    inv = (ROPE_THETA ** (-np.arange(0, half, 2, dtype=np.float32) / half)).astype(np.float32)
    ar, ac = row[:, None] * inv, col[:, None] * inv
    ang = np.concatenate([ar, ar, ac, ac], axis=-1)
    cos = np.concatenate([np.ones((TM2, QK_ROPE)), np.cos(ang)], axis=0)
    sin = np.concatenate([np.zeros((TM2, QK_ROPE)), np.sin(ang)], axis=0)
    n = cos.shape[0]
    return jnp.asarray(np.concatenate([cos, np.ones((n, 64)), sin, np.zeros((n, 64))], axis=1), F32)


def _dft_angles(rows, n):
    k = np.arange(n, dtype=np.int64)
    return ((np.asarray(rows, np.int64)[:, None] * k[None, :]) % n) * (2.0 * np.pi / n)


def _dft_tables(n):
    ang = _dft_angles(np.arange(n), n)
    return np.cos(ang) * n ** -0.5, np.sin(ang) * n ** -0.5


def _hi_lo(m):
    m = jnp.asarray(m, F32)
    hi = m.astype(BF16)
    return hi, (m - hi.astype(F32)).astype(BF16)


def _block_diag4(m):
    return np.kron(np.eye(4), m)


def kernel(x_prompt, x_sample, cache_ckv, cache_kpe, c, c_ctx, ada_w, ada_b, ln_g, ln_b, ev_w_in, ev_conv_w, ev_pool_w, ev_pool_scale, ev_w_out, ffn_w_gate, ffn_w_up, ffn_w_down, od_w_in, od_q_norm, od_kv_norm, od_w_q_b, od_w_kv_b, od_fnet_w, od_w_out, moe_router, moe_w_gate, moe_w_up, moe_w_down):
    cond8 = jnp.concatenate([c_ctx[None, :], c, jnp.zeros((8 - N_COND, D), F32)], axis=0)
    mods = _modulation(cond8, ada_w, ada_b)[:, :N_COND].reshape(DEPTH, N_COND, 6, D)

    x, (ffn_wg, ffn_wu, ffn_wd) = _even_mixer(
        x_prompt.reshape(T_P, D), x_sample.reshape(T_S, D), mods[0], ev_w_in[0].astype(BF16), ev_conv_w[0],
        ev_pool_w[0].astype(BF16), ev_pool_scale[0][None, :], ev_w_out[0].astype(BF16),
        ln_g[0, 0][None, :], ln_b[0, 0][None, :], [ffn_w_gate[0], ffn_w_up[0], ffn_w_down[0]])
    x, moe_wd = _ffn(x, mods[0], ffn_wg, ffn_wu, ffn_wd, ln_g[0, 1][None, :], ln_b[0, 1][None, :],
                     moe_w_down[0].reshape(N_EXPERTS * D_FF_EXPERT, D))

    w_in = od_w_in[0]
    w_pe = w_in[:, Q_RANK + KV_RANK:Q_RANK + KV_RANK + QK_ROPE]
    zpad = jnp.zeros((D, 64), F32)
    w_in_ext = jnp.concatenate([w_in[:, :Q_RANK + KV_RANK], w_in[:, Q_RANK + KV_RANK + QK_ROPE:],
                                w_pe, zpad, _rot_cols(w_pe), zpad], axis=1).astype(BF16)
    wq = od_w_q_b[0].reshape(Q_RANK, N_HEADS, QK_NOPE + QK_ROPE)
    zq = jnp.zeros((Q_RANK, N_HEADS, 64), F32)
    wqa = jnp.concatenate([wq, zq], axis=-1).reshape(Q_RANK, N_HEADS * QH).astype(BF16)
    wqb = jnp.concatenate([_rot_cols(wq[..., QK_NOPE:]), zq], axis=-1).reshape(Q_RANK, N_HEADS * 128).astype(BF16)
    wkv = od_w_kv_b[0].reshape(KV_RANK, N_HEADS, QK_NOPE + V_DIM)
    w_kv = jnp.concatenate([wkv[..., :QK_NOPE].reshape(KV_RANK, -1), wkv[..., QK_NOPE:].reshape(KV_RANK, -1)],
                           axis=1).astype(BF16)

    avg = jnp.asarray(_block_diag4(np.full((FNET_GROUP_DIM, FNET_GROUP_DIM), 1.0 / FNET_GROUP_DIM)), BF16)
    cc, sc = _dft_tables(FNET_GROUP_DIM)
    dft_hi, dft_lo = _hi_lo(np.concatenate([_block_diag4(cc), -_block_diag4(sc)], axis=1))

    q, ckv, kpe, y_dft, moe_wg = _odd_proj(x, mods[1], _rope_table(), w_in_ext, od_q_norm[0][None, :],
                                            od_kv_norm[0][None, :], wqa, wqb, avg, dft_hi, dft_lo,
                                            moe_w_gate[0].reshape(N_EXPERTS * D, D_FF_EXPERT))

    cache_kpe128 = jnp.pad(cache_kpe[:, 0], ((0, 0), (0, 0), (0, 128 - QK_ROPE)))
    attn = _attention(q, ckv, kpe, cache_ckv[:, 0], cache_kpe128, w_kv)

    tabs_p = tuple(jnp.asarray(m, F32) for m in _dft_tables(SEQ))
    a_base = _dft_angles(np.arange(TM), DEC_SEQ)
    a_step = _dft_angles(np.arange(TPS) * TM, DEC_SEQ)
    base_s = (jnp.asarray(np.cos(a_base) * DEC_SEQ ** -0.5, F32), jnp.asarray(np.sin(a_base) * DEC_SEQ ** -0.5, F32))
    step_s = (jnp.asarray(np.cos(a_step), F32), jnp.asarray(np.sin(a_step), F32))
    f = _pos_dft(y_dft, tabs_p, base_s, step_s)

    router = jnp.pad(moe_router[0], ((0, 0), (0, 128 - N_EXPERTS)))
    r_hi, r_lo = _hi_lo(router)
    x, info, info_t, cnt, moe_wu = _odd_merge(x, attn, f, mods[1], od_fnet_w[0].astype(BF16),
                                              od_w_out[0].astype(BF16), ln_g[1, 0][None, :], ln_b[1, 0][None, :],
                                              r_hi, r_lo, moe_w_up[0].reshape(N_EXPERTS * D, D_FF_EXPERT))

    n_tiles = T // TM2
    after = cnt.reshape(n_tiles, 8, 128)[:, 0, :N_EXPERTS].astype(jnp.int32)
    before = jnp.concatenate([jnp.zeros((1, N_EXPERTS), jnp.int32), after[:-1]], axis=0)
    counts = after[-1]
    padded = ((counts + TM_E - 1) // TM_E) * TM_E
    g_end = jnp.cumsum(padded)
    g_start = g_end - padded
    n_chunks = (after - before + CHUNK - 1) // CHUNK
    seg_off = CHUNK * (jnp.cumsum(n_chunks, axis=1) - n_chunks)
    seg_dst = g_start[None, :] + before
    choice = info_t[2:4].astype(jnp.int32)
    rank = info_t[4:6].astype(jnp.int32)
    shift_t = jnp.repeat((seg_off - before).T, TM2, axis=1)
    lp, dest = rank, rank
    for e in range(N_EXPERTS):
        lp = lp + jnp.where(choice == e, shift_t[e][None, :], 0)
        dest = dest + jnp.where(choice == e, g_start[e], 0)
    lp = lp.reshape(-1).astype(jnp.int32)
    dest = dest.reshape(-1).astype(jnp.int32)
    d0, off, nch = (a.reshape(-1).astype(jnp.int32) for a in (seg_dst, seg_off, n_chunks))
    tile_row = jnp.arange(N_ETILES, dtype=jnp.int32) * TM_E
    tile_expert = jnp.minimum(jnp.sum((tile_row[:, None] >= g_end[None, :]).astype(jnp.int32), axis=1),
                              N_EXPERTS - 1).astype(jnp.int32)
    n_used = (g_end[-1:] // TM_E).astype(jnp.int32)

    pad_lo = jnp.concatenate([g_start + counts, g_end[-1:]]).astype(jnp.int32)
    pad_hi = jnp.concatenate([g_end, jnp.full((1,), R_MAX, jnp.int32)]).astype(jnp.int32)
    xs = _dispatch(dest, pad_lo, pad_hi, x, mods[1])
    ys = _experts(tile_expert, n_used, xs, moe_wg.reshape(N_EXPERTS, D, D_FF_EXPERT),
                  moe_wu.reshape(N_EXPERTS, D, D_FF_EXPERT), moe_wd.reshape(N_EXPERTS, D_FF_EXPERT, D))
    yp, ysm = _combine(lp, d0, off, nch, x, info, mods[1], ln_g[1, 1][None, :], ln_b[1, 1][None, :], ys)

    y_prompt = yp.reshape(BATCH, SEQ, D)
    y_sample = ysm.reshape(DEC_BATCH, DEC_SEQ, D)
    new_ckv = ckv[:T_P].reshape(BATCH, 1, SEQ, KV_RANK)
    new_kpe = kpe[:T_P, :QK_ROPE].reshape(BATCH, 1, SEQ, QK_ROPE)
    return (y_prompt, y_sample, new_ckv, new_kpe)
```

```python
import functools

import numpy as np
import jax
import jax.numpy as jnp
from jax import lax
from jax.experimental import pallas as pl
from jax.experimental.pallas import tpu as pltpu

F32 = jnp.float32
BF16 = jnp.bfloat16

D = 1024
BATCH, SEQ = 32, 256
DEC_BATCH, DEC_SEQ = 2, 2048
PAST = 512
GRID_W = 64
T_P = BATCH * SEQ
T_S = DEC_BATCH * DEC_SEQ
T = T_P + T_S
N_COND = 1 + DEC_BATCH

CONV_DIM = 512
POOL_WINDOWS = (2, 4, 8, 16)
POOL_GROUP = 128
N_HEADS = 8
QK_NOPE, QK_ROPE, V_DIM = 128, 64, 128
Q_RANK, KV_RANK = 384, 256
FNET_DIM, FNET_GROUP_DIM = 256, 64
D_FF = 2816
N_EXPERTS = 8
D_FF_EXPERT = 1792
DEPTH = 2
ALPHA = (2 * DEPTH) ** 0.25
LN_EPS = 1e-5
RMS_EPS = 1e-6
ROPE_THETA = 10000.0

TM = 256
NP_TILES = T_P // TM
TPS = DEC_SEQ // TM
N_TILES = T // TM
HALO = 8
TM2 = 512
TM_FFN = 512
LK_S = PAST + DEC_SEQ
TM_E = 512
N_ETILES = (2 * T) // TM_E + N_EXPERTS + 1
R_MAX = N_ETILES * TM_E
VMEM_LIMIT = 56 * 1024 * 1024


def _cond_index(i, tm=TM):
    return jnp.where(i < T_P // tm, 0, 1 + (i - T_P // tm) // (DEC_SEQ // tm))


def _const_spec(shape):
    nd = len(shape)
    return pl.BlockSpec(shape, lambda *_: (0,) * nd, pipeline_mode=pl.Buffered(1))


def _params(n_axes=1, vmem=VMEM_LIMIT):
    return pltpu.CompilerParams(dimension_semantics=("arbitrary",) * n_axes, vmem_limit_bytes=vmem)


def _layer_norm(v, g, b):
    mu = jnp.mean(v, axis=-1, keepdims=True)
    d = v - mu
    var = jnp.mean(d * d, axis=-1, keepdims=True)
    return d * lax.rsqrt(var + LN_EPS) * g + b


def _split_bf16(v):
    hi = v.astype(BF16)
    lo = (v - hi.astype(F32)).astype(BF16)
    return hi, lo


def _dot(a, b):
    return jnp.dot(a, b, preferred_element_type=F32)


def _silu(v):
    return v / (1.0 + jnp.exp(-v))


CAST_BLOCKS = 16


def _cast_stream(srcs):
    specs = [pl.BlockSpec((w.shape[0] // CAST_BLOCKS, w.shape[1]),
                          lambda i, *_: (jnp.minimum(i, CAST_BLOCKS - 1), 0)) for w in srcs]
    return specs, [jax.ShapeDtypeStruct(w.shape, BF16) for w in srcs]


def _cast_step(srcs, dsts):
    @pl.when(pl.program_id(0) < CAST_BLOCKS)
    def _():
        for src, dst in zip(srcs, dsts):
            dst[...] = src[...].astype(BF16)


def _mod_kernel(cond_ref, w_ref, b_ref, o_ref):
    s = _silu(cond_ref[...]).astype(BF16)
    o_ref[...] = _dot(s, w_ref[...].astype(BF16)) + b_ref[...]


def _modulation(cond8, ada_w, ada_b):
    nb = 6 * D // 1024
    return pl.pallas_call(
        _mod_kernel,
        grid=(DEPTH, nb),
        in_specs=[
            pl.BlockSpec((8, D), lambda l, j: (0, 0)),
            pl.BlockSpec((None, D, 1024), lambda l, j: (l, 0, j)),
            pl.BlockSpec((None, 1, 1024), lambda l, j: (l, 0, j)),
        ],
        out_specs=pl.BlockSpec((None, 8, 1024), lambda l, j: (l, 0, j)),
        out_shape=jax.ShapeDtypeStruct((DEPTH, 8, 6 * D), F32),
        compiler_params=_params(2),
        name="adaln_modulation",
    )(cond8, ada_w, ada_b.reshape(DEPTH, 1, 6 * D))


def _pool_tables():
    t = np.arange(TM)[:, None]
    r = np.arange(TM + 2 * HALO)[None, :]
    pos = r - HALO
    bands = np.zeros((4, len(POOL_WINDOWS), TM, TM + 2 * HALO), np.float32)
    inv = np.zeros((4, TM, 128), np.float32)
    for variant in range(4):
        left_ok, right_ok = variant & 1, variant >> 1
        col_ok = (r >= (0 if left_ok else HALO)) & (r < (TM + 2 * HALO if right_ok else TM + HALO))
        first, last = (-HALO if left_ok else 0), (TM + HALO if right_ok else TM)
        for gi, w in enumerate(POOL_WINDOWS):
            bands[variant, gi] = (pos >= t - w // 2) & (pos < t + w // 2) & col_ok
            cnt = np.minimum(t[:, 0] + w // 2, last) - np.maximum(t[:, 0] - w // 2, first)
            inv[variant, :, gi] = 1.0 / cnt
    return jnp.asarray(bands, BF16), jnp.asarray(inv, F32)


def _even_tiles(x_alls, left_oks, right_oks, mod_ref, win_ref, convw_ref, poolw_ref, pscale_ref, wout_ref,
                g_ref, b_ref, band_ref, inv_ref):
    shift, scale, gate = mod_ref[0:1, :], mod_ref[1:2, :], mod_ref[2:3, :]
    n = len(x_alls)
    hs = []
    for x_all, left_ok, right_ok in zip(x_alls, left_oks, right_oks):
        h = x_all * (1.0 + scale) + shift
        hs.append(jnp.concatenate([jnp.where(left_ok, h[:HALO], 0.0), h[HALO:HALO + TM],
                                   jnp.where(right_ok, h[HALO + TM:], 0.0)], axis=0).astype(BF16))
    us = [_dot(h, win_ref[...]) for h in hs]

    mixes = []
    for u, left_ok, right_ok in zip(us, left_oks, right_oks):
        ux, ub = u[:, 0:CONV_DIM], u[:, CONV_DIM:2 * CONV_DIM]
        uc, up = u[:, 2 * CONV_DIM:3 * CONV_DIM], u[:, 3 * CONV_DIM:]
        z = uc * ux
        conv = (z[HALO - 1:HALO - 1 + TM] * convw_ref[0:1, :]
                + z[HALO:HALO + TM] * convw_ref[1:2, :]
                + z[HALO + 1:HALO + 1 + TM] * convw_ref[2:3, :])
        ya = ub[HALO:HALO + TM] * conv

        variant = left_ok.astype(jnp.int32) + 2 * right_ok.astype(jnp.int32)
        inv_cnt = inv_ref[variant]
        up_hi, up_lo = _split_bf16(up)
        yb_groups = []
        for gi in range(len(POOL_WINDOWS)):
            band = band_ref[variant, gi]
            sl = slice(gi * POOL_GROUP, (gi + 1) * POOL_GROUP)
            tot = _dot(band, up_hi[:, sl]) + _dot(band, up_lo[:, sl])
            p = tot * inv_cnt[:, gi:gi + 1] - up[HALO:HALO + TM, sl]
            yb_groups.append(_dot(p.astype(BF16), poolw_ref[gi]))
        yb = jnp.concatenate(yb_groups, axis=1) * pscale_ref[...]
        mixes.append(jnp.concatenate([ya, yb], axis=1).astype(BF16))

    ys = [_dot(mix, wout_ref[...]) for mix in mixes]
    return [_layer_norm(ALPHA * x_alls[k][HALO:HALO + TM] + gate * ys[k], g_ref[...], b_ref[...])
            for k in range(n)]


EV_TILES = 2
EV_ROWS = EV_TILES * TM


N_CAST_EVEN = 3


def _even_mixer_kernel(xctx_ref, xprev_ref, xlat_ref, xnext_ref, mod_ref, win_ref, convw_ref, poolw_ref,
                       pscale_ref, wout_ref, g_ref, b_ref, band_ref, inv_ref, *rest):
    n = N_CAST_EVEN
    cast_in, o_ref, cast_out, xall_ref = rest[:n], rest[n], rest[n + 1:2 * n + 1], rest[-1]
    _cast_step(cast_in, cast_out)
    s = pl.program_id(0)
    n_ctx = T_P // EV_ROWS
    is_latent = s >= n_ctx
    first_tile = lax.rem(s - n_ctx, DEC_SEQ // EV_ROWS) * EV_TILES

    @pl.when(jnp.logical_not(is_latent))
    def _():
        for k in range(EV_TILES):
            xall_ref[k, 0:HALO, :] = jnp.zeros((HALO, D), F32)
            xall_ref[k, HALO:HALO + TM, :] = xctx_ref[k * TM:(k + 1) * TM, :]
            xall_ref[k, HALO + TM:, :] = jnp.zeros((HALO, D), F32)

    @pl.when(is_latent)
    def _():
        for k in range(EV_TILES):
            lo, hi = k * TM, (k + 1) * TM
            xall_ref[k, 0:HALO, :] = xprev_ref[...] if k == 0 else xlat_ref[lo - HALO:lo, :]
            xall_ref[k, HALO:HALO + TM, :] = xlat_ref[lo:hi, :]
            xall_ref[k, HALO + TM:, :] = xnext_ref[...] if k == EV_TILES - 1 else xlat_ref[hi:hi + HALO, :]

    left_oks = [jnp.logical_and(is_latent, first_tile + k != 0) for k in range(EV_TILES)]
    right_oks = [jnp.logical_and(is_latent, first_tile + k != TPS - 1) for k in range(EV_TILES)]
    outs = _even_tiles([xall_ref[k] for k in range(EV_TILES)], left_oks, right_oks, mod_ref, win_ref,
                       convw_ref, poolw_ref, pscale_ref, wout_ref, g_ref, b_ref, band_ref, inv_ref)
    for k in range(EV_TILES):
        o_ref[k * TM:(k + 1) * TM, :] = outs[k]


def _even_mixer(x_ctx, x_lat, mods, w_in, conv_w, pool_w, pool_scale, w_out, g, b, cast_srcs):
    assert len(cast_srcs) == N_CAST_EVEN
    hb = EV_ROWS // HALO
    n8 = T_S // HALO
    n_ctx = T_P // EV_ROWS
    lat = lambda i: jnp.maximum(i - n_ctx, 0)
    cast_specs, cast_shapes = _cast_stream(cast_srcs)
    bands, inv_cnt = _pool_tables()
    outs = pl.pallas_call(
        _even_mixer_kernel,
        grid=(T // EV_ROWS,),
        in_specs=[
            pl.BlockSpec((EV_ROWS, D), lambda i: (jnp.minimum(i, n_ctx - 1), 0)),
            pl.BlockSpec((HALO, D), lambda i: (jnp.maximum(lat(i) * hb - 1, 0), 0)),
            pl.BlockSpec((EV_ROWS, D), lambda i: (lat(i), 0)),
            pl.BlockSpec((HALO, D), lambda i: (jnp.minimum((lat(i) + 1) * hb, n8 - 1), 0)),
            pl.BlockSpec((None, 6, D), lambda i: (_cond_index(i, EV_ROWS), 0, 0)),
            _const_spec((D, 4 * CONV_DIM)),
            _const_spec((3, CONV_DIM)),
            _const_spec((4, POOL_GROUP, POOL_GROUP)),
            _const_spec((1, 4 * POOL_GROUP)),
            _const_spec((D, D)),
            _const_spec((1, D)),
            _const_spec((1, D)),
            _const_spec(bands.shape),
            _const_spec(inv_cnt.shape),
        ] + cast_specs,
        out_specs=[pl.BlockSpec((EV_ROWS, D), lambda i: (i, 0))] + cast_specs,
        out_shape=[jax.ShapeDtypeStruct((T, D), F32)] + cast_shapes,
        scratch_shapes=[pltpu.VMEM((EV_TILES, TM + 2 * HALO, D), F32)],
        compiler_params=_params(1),
        name="even_mixer",
    )(x_ctx, x_lat, x_lat, x_lat, mods, w_in, conv_w, pool_w, pool_scale, w_out, g, b, bands, inv_cnt, *cast_srcs)
    return outs[0], outs[1:]


FF_CHUNK = D_FF // 2


def _ffn_kernel(x_ref, mod_ref, wg_ref, wu_ref, wd_ref, g_ref, b_ref, cast_ref, o_ref, cast_out_ref):
    _cast_step([cast_ref], [cast_out_ref])
    shift, scale, gate = mod_ref[3:4, :], mod_ref[4:5, :], mod_ref[5:6, :]
    x = x_ref[...]
    h = (x * (1.0 + scale) + shift).astype(BF16)
    f = jnp.zeros((TM_FFN, D), F32)
    for c in range(D_FF // FF_CHUNK):
        sl = slice(c * FF_CHUNK, (c + 1) * FF_CHUNK)
        a = _silu(_dot(h, wg_ref[:, sl])) * _dot(h, wu_ref[:, sl])
        f = f + _dot(a.astype(BF16), wd_ref[sl, :])
    o_ref[...] = _layer_norm(ALPHA * x + gate * f, g_ref[...], b_ref[...])


def _ffn(x, mods, wg, wu, wd, g, b, cast_src):
    cast_specs, cast_shapes = _cast_stream([cast_src])
    return pl.pallas_call(
        _ffn_kernel,
        grid=(T // TM_FFN,),
        in_specs=[
            pl.BlockSpec((TM_FFN, D), lambda i: (i, 0)),
            pl.BlockSpec((None, 6, D), lambda i: (_cond_index(i, TM_FFN), 0, 0)),
            _const_spec((D, D_FF)),
            _const_spec((D, D_FF)),
            _const_spec((D_FF, D)),
            _const_spec((1, D)),
            _const_spec((1, D)),
        ] + cast_specs,
        out_specs=[pl.BlockSpec((TM_FFN, D), lambda i: (i, 0))] + cast_specs,
        out_shape=[jax.ShapeDtypeStruct((T, D), F32)] + cast_shapes,
        compiler_params=_params(1),
        name="dense_swiglu",
    )(x, mods, wg, wu, wd, g, b, cast_src)


W_IN_EXT = Q_RANK + KV_RANK + FNET_DIM + 128 + 128
QH = 256
ATT_SCALE = (QK_NOPE + QK_ROPE) ** -0.5 * float(np.log2(np.e))


def _odd_proj_kernel(x_ref, mod_ref, rope_ref, win_ref, qn_ref, kvn_ref, wqa_ref, wqb_ref,
                     avg_ref, dfth_ref, dftl_ref, cast_ref,
                     q_ref, ckv_ref, kpe_ref, y_ref, cast_out_ref):
    _cast_step([cast_ref], [cast_out_ref])
    shift, scale = mod_ref[0:1, :], mod_ref[1:2, :]
    h = (x_ref[...] * (1.0 + scale) + shift).astype(BF16)
    u = _dot(h, win_ref[...])
    uq = u[:, 0:Q_RANK]
    ukv = u[:, Q_RANK:Q_RANK + KV_RANK]
    uf = u[:, Q_RANK + KV_RANK:Q_RANK + KV_RANK + FNET_DIM]
    o = Q_RANK + KV_RANK + FNET_DIM
    upe, upe_rot = u[:, o:o + 128], u[:, o + 128:o + 256]
    cos, sin = rope_ref[:, 0:128], rope_ref[:, 128:256]

    ckv_ref[...] = ukv * lax.rsqrt(jnp.mean(ukv * ukv, axis=-1, keepdims=True) + RMS_EPS) * kvn_ref[...]
    kpe_ref[...] = upe * cos + upe_rot * sin

    qlat = (uq * lax.rsqrt(jnp.mean(uq * uq, axis=-1, keepdims=True) + RMS_EPS) * qn_ref[...]).astype(BF16)
    qa = _dot(qlat, wqa_ref[...])
    qb = _dot(qlat, wqb_ref[...])
    for hd in range(N_HEADS):
        nope = qa[:, hd * QH:hd * QH + 128]
        pe = qa[:, hd * QH + 128:(hd + 1) * QH] * cos + qb[:, hd * 128:(hd + 1) * 128] * sin
        q_ref[:, hd * QH:hd * QH + 128] = (nope * ATT_SCALE).astype(BF16)
        q_ref[:, hd * QH + 128:(hd + 1) * QH] = (pe * ATT_SCALE).astype(BF16)

    avg = avg_ref[...]
    uf_hi, uf_lo = _split_bf16(uf)
    mu = _dot(uf_hi, avg) + _dot(uf_lo, avg)
    dlt = uf - mu
    sq_hi, sq_lo = _split_bf16(dlt * dlt)
    var = _dot(sq_hi, avg) + _dot(sq_lo, avg)
    xn = dlt * lax.rsqrt(var + LN_EPS)
    xn_hi, xn_lo = _split_bf16(xn)
    y = _dot(xn_hi, dfth_ref[...]) + _dot(xn_lo, dfth_ref[...]) + _dot(xn_hi, dftl_ref[...])
    y_ref[...] = y.astype(BF16)


def _odd_proj(x, mods, rope_tab, w_in_ext, q_norm, kv_norm, wqa, wqb, avg, dft_hi, dft_lo, cast_src):
    cast_specs, cast_shapes = _cast_stream([cast_src])

    def rope_index(i):
        return (jnp.where(i < T_P // TM2, 0, 1 + lax.rem(i - T_P // TM2, DEC_SEQ // TM2)), 0)

    return pl.pallas_call(
        _odd_proj_kernel,
        grid=(T // TM2,),
        in_specs=[
            pl.BlockSpec((TM2, D), lambda i: (i, 0)),
            pl.BlockSpec((None, 6, D), lambda i: (_cond_index(i, TM2), 0, 0)),
            pl.BlockSpec((TM2, 256), rope_index),
            _const_spec((D, W_IN_EXT)),
            _const_spec((1, Q_RANK)),
            _const_spec((1, KV_RANK)),
            _const_spec((Q_RANK, N_HEADS * QH)),
            _const_spec((Q_RANK, N_HEADS * 128)),
            _const_spec((FNET_DIM, FNET_DIM)),
            _const_spec((FNET_DIM, 2 * FNET_DIM)),
            _const_spec((FNET_DIM, 2 * FNET_DIM)),
        ] + cast_specs,
        out_specs=[
            pl.BlockSpec((TM2, N_HEADS * QH), lambda i: (i, 0)),
            pl.BlockSpec((TM2, KV_RANK), lambda i: (i, 0)),
            pl.BlockSpec((TM2, 128), lambda i: (i, 0)),
            pl.BlockSpec((TM2, 2 * FNET_DIM), lambda i: (i, 0)),
        ] + cast_specs,
        out_shape=[
            jax.ShapeDtypeStruct((T, N_HEADS * QH), BF16),
            jax.ShapeDtypeStruct((T, KV_RANK), F32),
            jax.ShapeDtypeStruct((T, 128), F32),
            jax.ShapeDtypeStruct((T, 2 * FNET_DIM), BF16),
        ] + cast_shapes,
        compiler_params=_params(1),
        name="odd_projections",
    )(x, mods, rope_tab, w_in_ext, q_norm, kv_norm, wqa, wqb, avg, dft_hi, dft_lo, cast_src)


V_OFF = N_HEADS * QH
KV_COLS = V_OFF + N_HEADS * V_DIM


def _attn_body(q_ref, kv_ref, o_refs):
    for hd in range(N_HEADS):
        qh = q_ref[:, hd * QH:(hd + 1) * QH]
        kh = kv_ref[:, hd * QH:(hd + 1) * QH]
        s = lax.dot_general(qh, kh, (((1,), (1,)), ((), ())), preferred_element_type=F32)
        p = jnp.exp2(s - jnp.max(s, axis=-1, keepdims=True))
        den = jnp.sum(p, axis=-1, keepdims=True)
        vh = kv_ref[:, V_OFF + hd * V_DIM:V_OFF + (hd + 1) * V_DIM]
        o = (_dot(p.astype(BF16), vh) / den).astype(BF16)
        for k, o_ref in enumerate(o_refs):
            o_ref[:, hd * 128:(hd + 1) * 128] = o[k * TM:(k + 1) * TM]


KV_CHUNK = 512
LAT_PAIR = 2


def _attn_kernel(q_ref, qpair_ref, ckvp_ref, kpep_ref, ckvs_ref, kpes_ref, cckv_ref, ckpe_ref, wkv_ref, o_ref,
                 kvp_s, kvs_s, held_s):
    i = pl.program_id(0)

    def expand(ckv, kpe, dst, row0):
        n = ckv.shape[0]
        kv = _dot(ckv.astype(BF16), wkv_ref[...]).astype(BF16)
        kpe = kpe.astype(BF16)
        for hd in range(N_HEADS):
            dst[row0:row0 + n, hd * QH:hd * QH + QK_NOPE] = kv[:, hd * QK_NOPE:(hd + 1) * QK_NOPE]
            dst[row0:row0 + n, hd * QH + QK_NOPE:(hd + 1) * QH] = kpe
        dst[row0:row0 + n, V_OFF:] = kv[:, N_HEADS * QK_NOPE:]

    @pl.when(i < NP_TILES)
    def _():
        expand(ckvp_ref[...], kpep_ref[...], kvp_s, 0)
        _attn_body(q_ref, kvp_s, [o_ref])

    @pl.when(jnp.logical_and(i >= NP_TILES, lax.rem(i - NP_TILES, LAT_PAIR) == 0))
    def _():
        @pl.when(lax.rem(i - NP_TILES, TPS) == 0)
        def _():
            expand(cckv_ref[...], ckpe_ref[...], kvs_s, 0)
            for c in range(DEC_SEQ // KV_CHUNK):
                rows = slice(c * KV_CHUNK, (c + 1) * KV_CHUNK)
                expand(ckvs_ref[rows, :], kpes_ref[rows, :], kvs_s, PAST + c * KV_CHUNK)

        _attn_body(qpair_ref, kvs_s, [o_ref, held_s])

    @pl.when(jnp.logical_and(i >= NP_TILES, lax.rem(i - NP_TILES, LAT_PAIR) == 1))
    def _():
        o_ref[...] = held_s[...]


def _attention(q, ckv, kpe, cache_ckv, cache_kpe128, w_kv):
    ctx_blk = lambda i: (jnp.minimum(i, NP_TILES - 1), 0)
    lat_b = lambda i: jnp.maximum(i - NP_TILES, 0) // TPS
    lat_blk = lambda i: (T_P // DEC_SEQ + lat_b(i), 0)
    return pl.pallas_call(
        _attn_kernel,
        grid=(N_TILES,),
        in_specs=[
            pl.BlockSpec((TM, N_HEADS * QH), ctx_blk),
            pl.BlockSpec((LAT_PAIR * TM, N_HEADS * QH),
                         lambda i: (NP_TILES // LAT_PAIR + jnp.maximum(i - NP_TILES, 0) // LAT_PAIR, 0)),
            pl.BlockSpec((SEQ, KV_RANK), ctx_blk),
            pl.BlockSpec((SEQ, 128), ctx_blk),
            pl.BlockSpec((DEC_SEQ, KV_RANK), lat_blk, pipeline_mode=pl.Buffered(1)),
            pl.BlockSpec((DEC_SEQ, 128), lat_blk, pipeline_mode=pl.Buffered(1)),
            pl.BlockSpec((None, PAST, KV_RANK), lambda i: (lat_b(i), 0, 0)),
            pl.BlockSpec((None, PAST, 128), lambda i: (lat_b(i), 0, 0)),
            _const_spec((KV_RANK, 2 * N_HEADS * 128)),
        ],
        out_specs=pl.BlockSpec((TM, N_HEADS * V_DIM), lambda i: (i, 0)),
        out_shape=jax.ShapeDtypeStruct((T, N_HEADS * V_DIM), BF16),
        scratch_shapes=[pltpu.VMEM((SEQ, KV_COLS), BF16), pltpu.VMEM((LK_S, KV_COLS), BF16),
                        pltpu.VMEM((TM, N_HEADS * V_DIM), BF16)],
        compiler_params=_params(1),
        name="attention",
    )(q, q, ckv, kpe, ckv, kpe, cache_ckv, cache_kpe128, w_kv)


def _pos_dft_body(y_ref, c, s, o_ref):
    f = _dot(c, y_ref[:, 0:FNET_DIM]) + _dot(s, y_ref[:, FNET_DIM:])
    o_ref[...] = f.astype(BF16)


def _pos_dft_kernel(yp_ref, cp_ref, sp_ref, ys_ref, c0_ref, s0_ref, cb_ref, sb_ref, o_ref):
    @pl.when(pl.program_id(0) < NP_TILES)
    def _():
        _pos_dft_body(yp_ref, cp_ref[...].astype(BF16), sp_ref[...].astype(BF16), o_ref)

    @pl.when(pl.program_id(0) >= NP_TILES)
    def _():
        j = lax.rem(pl.program_id(0) - NP_TILES, TPS)
        cb, sb = cb_ref[pl.ds(j, 1), :], sb_ref[pl.ds(j, 1), :]
        c0, s0 = c0_ref[...], s0_ref[...]
        _pos_dft_body(ys_ref, (c0 * cb - s0 * sb).astype(BF16), (s0 * cb + c0 * sb).astype(BF16), o_ref)


def _pos_dft(y, tabs_p, base_s, step_s):
    ctx_blk = lambda i: (jnp.minimum(i, NP_TILES - 1), 0)
    lat_seq = lambda i: (T_P // DEC_SEQ + jnp.maximum(i - NP_TILES, 0) // TPS, 0)
    return pl.pallas_call(
        _pos_dft_kernel,
        grid=(N_TILES,),
        in_specs=[pl.BlockSpec((SEQ, 2 * FNET_DIM), ctx_blk)]
        + [_const_spec((SEQ, SEQ))] * 2
        + [pl.BlockSpec((DEC_SEQ, 2 * FNET_DIM), lat_seq)]
        + [_const_spec((TM, DEC_SEQ))] * 2
        + [_const_spec((TPS, DEC_SEQ))] * 2,
        out_specs=pl.BlockSpec((TM, FNET_DIM), lambda i: (i, 0)),
        out_shape=jax.ShapeDtypeStruct((T, FNET_DIM), BF16),
        compiler_params=_params(1),
        name="pos_dft",
    )(y, *tabs_p, y, *base_s, *step_s)


def _odd_merge_kernel(x_ref, attn_ref, f_ref, mod_ref, fw_ref, wo_ref, g_ref, b_ref, rh_ref, rl_ref, tri_ref,
                      cast_ref, xo_ref, info_ref, infot_ref, cnt_ref, cast_out_ref, carry_ref):
    _cast_step([cast_ref], [cast_out_ref])

    @pl.when(pl.program_id(0) == 0)
    def _():
        carry_ref[...] = jnp.zeros_like(carry_ref)

    gate = mod_ref[2:3, :]
    shift2, scale2 = mod_ref[3:4, :], mod_ref[4:5, :]
    fm = _dot(f_ref[...], fw_ref[...]).astype(BF16)
    y = _dot(attn_ref[...], wo_ref[0:N_HEADS * V_DIM, :]) + _dot(fm, wo_ref[N_HEADS * V_DIM:, :])
    x = _layer_norm(ALPHA * x_ref[...] + gate * y, g_ref[...], b_ref[...])
    xo_ref[...] = x
    h = x * (1.0 + scale2) + shift2

    h_hi, h_lo = _split_bf16(h)
    logits = _dot(h_hi, rh_ref[...]) + _dot(h_lo, rh_ref[...]) + _dot(h_hi, rl_ref[...])
    lane = lax.broadcasted_iota(jnp.int32, (TM2, 128), 1)
    neg = jnp.float32(-jnp.inf)
    logits = jnp.where(lane < N_EXPERTS, logits, neg)
    m1 = jnp.max(logits, axis=-1, keepdims=True)
    i1 = jnp.min(jnp.where(logits == m1, lane, 128), axis=-1, keepdims=True)
    rest = jnp.where(lane == i1, neg, logits)
    m2 = jnp.max(rest, axis=-1, keepdims=True)
    i2 = jnp.min(jnp.where(rest == m2, lane, 128), axis=-1, keepdims=True)
    e2 = jnp.exp(m2 - m1)
    w1 = 1.0 / (1.0 + e2)
    w2 = e2 / (1.0 + e2)
    info = jnp.where(lane == 0, w1, 0.0)
    info = jnp.where(lane == 1, w2, info)
    info = jnp.where(lane == 2, i1.astype(F32), info)
    info = jnp.where(lane == 3, i2.astype(F32), info)

    uses = jnp.logical_or(lane == i1, lane == i2)
    seen = _dot(tri_ref[...], jnp.where(uses, 1.0, 0.0).astype(BF16)) + carry_ref[...]
    r1 = jnp.sum(jnp.where(lane == i1, seen, 0.0), axis=-1, keepdims=True)
    r2 = jnp.sum(jnp.where(lane == i2, seen, 0.0), axis=-1, keepdims=True)
    info = jnp.where(lane == 4, r1, info)
    info = jnp.where(lane == 5, r2, info)
    info_ref[...] = info
    infot_ref[...] = info.T
    total = carry_ref[...] + jnp.sum(jnp.where(uses, 1.0, 0.0), axis=0, keepdims=True)
    carry_ref[...] = total
    cnt_ref[...] = jnp.broadcast_to(total, cnt_ref.shape)


def _odd_merge(x, attn, f, mods, fnet_w, w_out, g, b, r_hi, r_lo, cast_src):
    row = lambda i: (i, 0)
    tri = jnp.asarray(np.tril(np.ones((TM2, TM2), np.float32), -1), BF16)
    cast_specs, cast_shapes = _cast_stream([cast_src])
    return pl.pallas_call(
        _odd_merge_kernel,
        grid=(T // TM2,),
        in_specs=[
            pl.BlockSpec((TM2, D), row),
            pl.BlockSpec((TM2, N_HEADS * V_DIM), row),
            pl.BlockSpec((TM2, FNET_DIM), row),
            pl.BlockSpec((None, 6, D), lambda i: (_cond_index(i, TM2), 0, 0)),
            _const_spec((FNET_DIM, FNET_DIM)),
            _const_spec((N_HEADS * V_DIM + FNET_DIM, D)),
            _const_spec((1, D)),
            _const_spec((1, D)),
            _const_spec((D, 128)),
            _const_spec((D, 128)),
            _const_spec((TM2, TM2)),
        ] + cast_specs,
        out_specs=[pl.BlockSpec((TM2, D), row), pl.BlockSpec((TM2, 128), row),
                   pl.BlockSpec((128, TM2), lambda i: (0, i)), pl.BlockSpec((8, 128), lambda i: (i, 0))] + cast_specs,
        out_shape=[jax.ShapeDtypeStruct((T, D), F32), jax.ShapeDtypeStruct((T, 128), F32),
                   jax.ShapeDtypeStruct((128, T), F32), jax.ShapeDtypeStruct((T // TM2 * 8, 128), F32)] + cast_shapes,
        scratch_shapes=[pltpu.VMEM((1, 128), F32)],
        compiler_params=_params(1),
        name="odd_merge_router",
    )(x, attn, f, mods, fnet_w, w_out, g, b, r_hi, r_lo, tri, cast_src)


TM_D = 512
ROW = (8, 128)
ROW_DT = BF16
DMA_UNROLL = 32
ZERO_ROWS = 32


def _row_copy(src, s, dst, d, sem):
    return pltpu.make_async_copy(src.at[s], dst.at[d], sem)


CHUNK = 16
LOCAL_ROWS = 2 * TM2 + N_EXPERTS * CHUNK


def _chunk_copies(tile, nch_ref, fn):
    for e in range(N_EXPERTS):
        def body(c, carry, e=e):
            fn(tile * N_EXPERTS + e, c, e % 2)
            return carry

        lax.fori_loop(0, nch_ref[tile * N_EXPERTS + e], body, 0)


def _dispatch_kernel(dest_ref, pad_lo_ref, pad_hi_ref, x_ref, mod_ref, xs_ref, h_ref, zero_ref, sem, zsem):
    i = pl.program_id(0)
    base = i * TM_D
    slot = lax.rem(i, 2)

    @pl.when(i == 0)
    def _():
        zero_ref[...] = jnp.zeros_like(zero_ref)
        for e in range(N_EXPERTS + 1):
            lo, hi = pad_lo_ref[e], pad_hi_ref[e]
            n_blocks = (hi - lo) // ZERO_ROWS
            tail = lo + n_blocks * ZERO_ROWS

            def zblock(c, lo=lo):
                return pltpu.make_async_copy(zero_ref, xs_ref.at[pl.ds(lo + c * ZERO_ROWS, ZERO_ROWS)], zsem)

            def zissue(c, carry):
                zblock(c).start()
                return carry

            def zdrain(c, carry):
                zblock(c).wait()
                return carry

            def rissue(r, carry):
                _row_copy(zero_ref, 0, xs_ref, r, zsem).start()
                return carry

            def rdrain(r, carry):
                _row_copy(zero_ref, 0, xs_ref, 0, zsem).wait()
                return carry

            lax.fori_loop(0, n_blocks, zissue, 0)
            lax.fori_loop(tail, hi, rissue, 0)
            lax.fori_loop(0, n_blocks, zdrain, 0)
            lax.fori_loop(tail, hi, rdrain, 0)

    shift2, scale2 = mod_ref[3:4, :], mod_ref[4:5, :]
    h_ref[slot] = (x_ref[...] * (1.0 + scale2) + shift2).astype(ROW_DT).reshape((TM_D,) + ROW)

    def issue(r, carry):
        t = base + r
        _row_copy(h_ref.at[slot], r, xs_ref, dest_ref[t], sem.at[slot]).start(priority=0)
        _row_copy(h_ref.at[slot], r, xs_ref, dest_ref[T + t], sem.at[slot]).start(priority=1)
        return carry

    lax.fori_loop(0, TM_D, issue, 0, unroll=DMA_UNROLL)

    def drain(s):
        def body(r, carry):
            _row_copy(h_ref.at[s], 0, xs_ref, 0, sem.at[s]).wait()
            _row_copy(h_ref.at[s], 0, xs_ref, 0, sem.at[s]).wait()
            return carry

        lax.fori_loop(0, TM_D, body, 0, unroll=DMA_UNROLL)

    @pl.when(i > 0)
    def _():
        drain(1 - slot)

    @pl.when(i == T // TM_D - 1)
    def _():
        drain(slot)


def _dispatch(dest, pad_lo, pad_hi, x, mods):
    return pl.pallas_call(
        _dispatch_kernel,
        grid_spec=pltpu.PrefetchScalarGridSpec(
            num_scalar_prefetch=3,
            grid=(T // TM_D,),
            in_specs=[pl.BlockSpec((TM_D, D), lambda i, *_: (i, 0)),
                      pl.BlockSpec((None, 6, D), lambda i, *_: (_cond_index(i, TM_D), 0, 0))],
            out_specs=pl.BlockSpec(memory_space=pl.ANY),
            scratch_shapes=[pltpu.VMEM((2, TM_D) + ROW, ROW_DT), pltpu.VMEM((ZERO_ROWS,) + ROW, ROW_DT),
                            pltpu.SemaphoreType.DMA((2,)), pltpu.SemaphoreType.DMA(())],
        ),
        out_shape=jax.ShapeDtypeStruct((R_MAX,) + ROW, ROW_DT),
        compiler_params=pltpu.CompilerParams(dimension_semantics=("arbitrary",), has_side_effects=True),
        name="expert_dispatch",
    )(dest, pad_lo, pad_hi, x, mods)


def _expert_kernel(te_ref, nt_ref, xs_ref, wg_ref, wu_ref, wd_ref, o_ref):
    @pl.when(pl.program_id(0) < nt_ref[0])
    def _():
        h = xs_ref[...].reshape(TM_E, D)
        a = _silu(_dot(h, wg_ref[...])) * _dot(h, wu_ref[...])
        o_ref[...] = _dot(a.astype(BF16), wd_ref[...]).astype(ROW_DT).reshape((TM_E,) + ROW)

    @pl.when(pl.program_id(0) >= nt_ref[0])
    def _():
        o_ref[...] = jnp.zeros_like(o_ref)


def _experts(tile_expert, n_used, xs, wg, wu, wd):
    return pl.pallas_call(
        _expert_kernel,
        grid_spec=pltpu.PrefetchScalarGridSpec(
            num_scalar_prefetch=2,
            grid=(N_ETILES,),
            in_specs=[
                pl.BlockSpec((TM_E,) + ROW, lambda i, te, nt: (jnp.minimum(i, nt[0] - 1), 0, 0)),
                pl.BlockSpec((None, D, D_FF_EXPERT), lambda i, te, nt: (te[i], 0, 0)),
                pl.BlockSpec((None, D, D_FF_EXPERT), lambda i, te, nt: (te[i], 0, 0)),
                pl.BlockSpec((None, D_FF_EXPERT, D), lambda i, te, nt: (te[i], 0, 0)),
            ],
            out_specs=pl.BlockSpec((TM_E,) + ROW, lambda i, te, nt: (i, 0, 0)),
        ),
        out_shape=jax.ShapeDtypeStruct((R_MAX,) + ROW, ROW_DT),
        compiler_params=_params(1),
        name="expert_swiglu",
    )(tile_expert, n_used, xs, wg, wu, wd)


def _combine_kernel(lp_ref, d0_ref, off_ref, nch_ref, x_ref, info_ref, mod_ref, g_ref, b_ref, ys_ref,
                    op_ref, os_ref, local_ref, rows_ref, sem):
    i = pl.program_id(0)
    slot = lax.rem(i, 2)

    def chunk(s, seg, c):
        return pltpu.make_async_copy(ys_ref.at[pl.ds(d0_ref[seg] + c * CHUNK, CHUNK)],
                                     local_ref.at[s, pl.ds(off_ref[seg] + c * CHUNK, CHUNK)], sem.at[s])

    @pl.when(i == 0)
    def _():
        _chunk_copies(0, nch_ref, lambda seg, c, prio: chunk(0, seg, c).start(priority=prio))

    @pl.when(i + 1 < T // TM2)
    def _():
        _chunk_copies(i + 1, nch_ref, lambda seg, c, prio: chunk(1 - slot, seg, c).start(priority=prio))

    _chunk_copies(i, nch_ref, lambda seg, c, prio: chunk(slot, seg, c).wait())

    def pick(r, carry):
        rows_ref[0, r] = local_ref[slot, lp_ref[i * TM2 + r]]
        rows_ref[1, r] = local_ref[slot, lp_ref[T + i * TM2 + r]]
        return carry

    lax.fori_loop(0, TM2, pick, 0, unroll=DMA_UNROLL)

    gate = mod_ref[5:6, :]
    w1, w2 = info_ref[:, 0:1], info_ref[:, 1:2]
    y = w1 * rows_ref[0].reshape(TM2, D).astype(F32) + w2 * rows_ref[1].reshape(TM2, D).astype(F32)
    out = _layer_norm(ALPHA * x_ref[...] + gate * y, g_ref[...], b_ref[...])

    @pl.when(pl.program_id(0) < T_P // TM2)
    def _():
        op_ref[...] = out

    @pl.when(pl.program_id(0) >= T_P // TM2)
    def _():
        os_ref[...] = out


def _combine(lp, d0, off, nch, x, info, mods, g, b, ys):
    return pl.pallas_call(
        _combine_kernel,
        grid_spec=pltpu.PrefetchScalarGridSpec(
            num_scalar_prefetch=4,
            grid=(T // TM2,),
            in_specs=[
                pl.BlockSpec((TM2, D), lambda i, *_: (i, 0)),
                pl.BlockSpec((TM2, 128), lambda i, *_: (i, 0)),
                pl.BlockSpec((None, 6, D), lambda i, *_: (_cond_index(i, TM2), 0, 0)),
                pl.BlockSpec((1, D), lambda i, *_: (0, 0)),
                pl.BlockSpec((1, D), lambda i, *_: (0, 0)),
                pl.BlockSpec(memory_space=pl.ANY),
            ],
            out_specs=[pl.BlockSpec((TM2, D), lambda i, *_: (jnp.minimum(i, T_P // TM2 - 1), 0)),
                       pl.BlockSpec((TM2, D), lambda i, *_: (jnp.maximum(i - T_P // TM2, 0), 0))],
            scratch_shapes=[pltpu.VMEM((2, LOCAL_ROWS) + ROW, ROW_DT), pltpu.VMEM((2, TM2) + ROW, ROW_DT),
                            pltpu.SemaphoreType.DMA((2,))],
        ),
        out_shape=[jax.ShapeDtypeStruct((T_P, D), F32), jax.ShapeDtypeStruct((T_S, D), F32)],
        compiler_params=_params(1),
        name="expert_combine",
    )(lp, d0, off, nch, x, info, mods, g, b, ys)


def _rot_cols(w):
    w4 = w.reshape(w.shape[:-1] + (2, 2, QK_ROPE // 4))
    return jnp.stack([-w4[..., 1, :], w4[..., 0, :]], axis=-2).reshape(w.shape)


def _rope_table():
    rows = DEC_SEQ // GRID_W
    row = np.repeat(np.arange(rows), GRID_W).astype(np.float32)
    col = np.tile(np.arange(GRID_W), rows).astype(np.float32)
    half = QK_ROPE // 2
    inv = (ROPE_THETA ** (-np.arange(0, half, 2, dtype=np.float32) / half)).astype(np.float32)
    ar, ac = row[:, None] * inv, col[:, None] * inv
    ang = np.concatenate([ar, ar, ac, ac], axis=-1)
    cos = np.concatenate([np.ones((TM2, QK_ROPE)), np.cos(ang)], axis=0)
    sin = np.concatenate([np.zeros((TM2, QK_ROPE)), np.sin(ang)], axis=0)
    n = cos.shape[0]
    return jnp.asarray(np.concatenate([cos, np.ones((n, 64)), sin, np.zeros((n, 64))], axis=1), F32)


def _dft_angles(rows, n):
    k = np.arange(n, dtype=np.int64)
    return ((np.asarray(rows, np.int64)[:, None] * k[None, :]) % n) * (2.0 * np.pi / n)


def _dft_tables(n):
    ang = _dft_angles(np.arange(n), n)
    return np.cos(ang) * n ** -0.5, np.sin(ang) * n ** -0.5


def _hi_lo(m):
    m = jnp.asarray(m, F32)
    hi = m.astype(BF16)
    return hi, (m - hi.astype(F32)).astype(BF16)


def _block_diag4(m):
    return np.kron(np.eye(4), m)


def kernel(x_prompt, x_sample, cache_ckv, cache_kpe, c, c_ctx, ada_w, ada_b, ln_g, ln_b, ev_w_in, ev_conv_w, ev_pool_w, ev_pool_scale, ev_w_out, ffn_w_gate, ffn_w_up, ffn_w_down, od_w_in, od_q_norm, od_kv_norm, od_w_q_b, od_w_kv_b, od_fnet_w, od_w_out, moe_router, moe_w_gate, moe_w_up, moe_w_down):
    cond8 = jnp.concatenate([c_ctx[None, :], c, jnp.zeros((8 - N_COND, D), F32)], axis=0)
    mods = _modulation(cond8, ada_w, ada_b)[:, :N_COND].reshape(DEPTH, N_COND, 6, D)

    x, (ffn_wg, ffn_wu, ffn_wd) = _even_mixer(
        x_prompt.reshape(T_P, D), x_sample.reshape(T_S, D), mods[0], ev_w_in[0].astype(BF16), ev_conv_w[0],
        ev_pool_w[0].astype(BF16), ev_pool_scale[0][None, :], ev_w_out[0].astype(BF16),
        ln_g[0, 0][None, :], ln_b[0, 0][None, :], [ffn_w_gate[0], ffn_w_up[0], ffn_w_down[0]])
    x, moe_wd = _ffn(x, mods[0], ffn_wg, ffn_wu, ffn_wd, ln_g[0, 1][None, :], ln_b[0, 1][None, :],
                     moe_w_down[0].reshape(N_EXPERTS * D_FF_EXPERT, D))

    w_in = od_w_in[0]
    w_pe = w_in[:, Q_RANK + KV_RANK:Q_RANK + KV_RANK + QK_ROPE]
    zpad = jnp.zeros((D, 64), F32)
    w_in_ext = jnp.concatenate([w_in[:, :Q_RANK + KV_RANK], w_in[:, Q_RANK + KV_RANK + QK_ROPE:],
                                w_pe, zpad, _rot_cols(w_pe), zpad], axis=1).astype(BF16)
    wq = od_w_q_b[0].reshape(Q_RANK, N_HEADS, QK_NOPE + QK_ROPE)
    zq = jnp.zeros((Q_RANK, N_HEADS, 64), F32)
    wqa = jnp.concatenate([wq, zq], axis=-1).reshape(Q_RANK, N_HEADS * QH).astype(BF16)
    wqb = jnp.concatenate([_rot_cols(wq[..., QK_NOPE:]), zq], axis=-1).reshape(Q_RANK, N_HEADS * 128).astype(BF16)
    wkv = od_w_kv_b[0].reshape(KV_RANK, N_HEADS, QK_NOPE + V_DIM)
    w_kv = jnp.concatenate([wkv[..., :QK_NOPE].reshape(KV_RANK, -1), wkv[..., QK_NOPE:].reshape(KV_RANK, -1)],
                           axis=1).astype(BF16)

    avg = jnp.asarray(_block_diag4(np.full((FNET_GROUP_DIM, FNET_GROUP_DIM), 1.0 / FNET_GROUP_DIM)), BF16)
    cc, sc = _dft_tables(FNET_GROUP_DIM)
    dft_hi, dft_lo = _hi_lo(np.concatenate([_block_diag4(cc), -_block_diag4(sc)], axis=1))

    q, ckv, kpe, y_dft, moe_wg = _odd_proj(x, mods[1], _rope_table(), w_in_ext, od_q_norm[0][None, :],
                                            od_kv_norm[0][None, :], wqa, wqb, avg, dft_hi, dft_lo,
                                            moe_w_gate[0].reshape(N_EXPERTS * D, D_FF_EXPERT))

    cache_kpe128 = jnp.pad(cache_kpe[:, 0], ((0, 0), (0, 0), (0, 128 - QK_ROPE)))
    attn = _attention(q, ckv, kpe, cache_ckv[:, 0], cache_kpe128, w_kv)

    tabs_p = tuple(jnp.asarray(m, F32) for m in _dft_tables(SEQ))
    a_base = _dft_angles(np.arange(TM), DEC_SEQ)
    a_step = _dft_angles(np.arange(TPS) * TM, DEC_SEQ)
    base_s = (jnp.asarray(np.cos(a_base) * DEC_SEQ ** -0.5, F32), jnp.asarray(np.sin(a_base) * DEC_SEQ ** -0.5, F32))
    step_s = (jnp.asarray(np.cos(a_step), F32), jnp.asarray(np.sin(a_step), F32))
    f = _pos_dft(y_dft, tabs_p, base_s, step_s)

    router = jnp.pad(moe_router[0], ((0, 0), (0, 128 - N_EXPERTS)))
    r_hi, r_lo = _hi_lo(router)
    x, info, info_t, cnt, moe_wu = _odd_merge(x, attn, f, mods[1], od_fnet_w[0].astype(BF16),
                                              od_w_out[0].astype(BF16), ln_g[1, 0][None, :], ln_b[1, 0][None, :],
                                              r_hi, r_lo, moe_w_up[0].reshape(N_EXPERTS * D, D_FF_EXPERT))

    n_tiles = T // TM2
    after = cnt.reshape(n_tiles, 8, 128)[:, 0, :N_EXPERTS].astype(jnp.int32)
    before = jnp.concatenate([jnp.zeros((1, N_EXPERTS), jnp.int32), after[:-1]], axis=0)
    counts = after[-1]
    padded = ((counts + TM_E - 1) // TM_E) * TM_E
    g_end = jnp.cumsum(padded)
    g_start = g_end - padded
    n_chunks = (after - before + CHUNK - 1) // CHUNK
    seg_off = CHUNK * (jnp.cumsum(n_chunks, axis=1) - n_chunks)
    seg_dst = g_start[None, :] + before
    choice = info_t[2:4].astype(jnp.int32)
    rank = info_t[4:6].astype(jnp.int32)
    shift_t = jnp.repeat((seg_off - before).T, TM2, axis=1)
    lp, dest = rank, rank
    for e in range(N_EXPERTS):
        lp = lp + jnp.where(choice == e, shift_t[e][None, :], 0)
        dest = dest + jnp.where(choice == e, g_start[e], 0)
    lp = lp.reshape(-1).astype(jnp.int32)
    dest = dest.reshape(-1).astype(jnp.int32)
    d0, off, nch = (a.reshape(-1).astype(jnp.int32) for a in (seg_dst, seg_off, n_chunks))
    tile_row = jnp.arange(N_ETILES, dtype=jnp.int32) * TM_E
    tile_expert = jnp.minimum(jnp.sum((tile_row[:, None] >= g_end[None, :]).astype(jnp.int32), axis=1),
                              N_EXPERTS - 1).astype(jnp.int32)
    n_used = (g_end[-1:] // TM_E).astype(jnp.int32)

    pad_lo = jnp.concatenate([g_start + counts, g_end[-1:]]).astype(jnp.int32)
    pad_hi = jnp.concatenate([g_end, jnp.full((1,), R_MAX, jnp.int32)]).astype(jnp.int32)
    xs = _dispatch(dest, pad_lo, pad_hi, x, mods[1])
    ys = _experts(tile_expert, n_used, xs, moe_wg.reshape(N_EXPERTS, D, D_FF_EXPERT),
                  moe_wu.reshape(N_EXPERTS, D, D_FF_EXPERT), moe_wd.reshape(N_EXPERTS, D_FF_EXPERT, D))
    yp, ysm = _combine(lp, d0, off, nch, x, info, mods[1], ln_g[1, 1][None, :], ln_b[1, 1][None, :], ys)

    y_prompt = yp.reshape(BATCH, SEQ, D)
    y_sample = ysm.reshape(DEC_BATCH, DEC_SEQ, D)
    new_ckv = ckv[:T_P].reshape(BATCH, 1, SEQ, KV_RANK)
    new_kpe = kpe[:T_P, :QK_ROPE].reshape(BATCH, 1, SEQ, QK_ROPE)
    return (y_prompt, y_sample, new_ckv, new_kpe)
```
